```python
import jax, jax.numpy as jnp
from jax import lax
import numpy as np

D_MODEL = 2048
BATCH = 8
SEQ = 8192
DEPTH = 4

HEAD_DIM = 64
D_MIX = D_MODEL
D_ATTN = D_MIX // 2
D_GMLP = D_MIX - D_ATTN
N_Q_HEADS = D_ATTN // HEAD_DIM
N_KV_HEADS = 4
N_GMLP_HEADS = D_GMLP // HEAD_DIM
WINDOW = 128
CHUNK = 128
RMS_EPS = 1e-6
D_KV = N_KV_HEADS * HEAD_DIM
D_IN = D_ATTN + 2 * D_KV + D_ATTN + 3 * D_GMLP

kernel_name = "hybrid_swa_sink_gmlp_parallel_heads"


def _rmsnorm(x, g):
    xf = x.astype(jnp.float32)
    y = xf * lax.rsqrt(jnp.mean(xf * xf, axis=-1, keepdims=True) + RMS_EPS)
    return (y * g.astype(jnp.float32)).astype(x.dtype)


def _alibi_slopes(n):
    return jnp.asarray(2.0 ** (-8.0 * np.arange(1, n + 1) / n), dtype=jnp.float32)


def _band(t, nb):
    B, S, H, D = t.shape
    tb = t.reshape(B, nb, WINDOW, H, D)
    prev = jnp.pad(tb, ((0, 0), (1, 0), (0, 0), (0, 0), (0, 0)))[:, :-1]
    return jnp.concatenate([prev, tb], axis=2)


def _swa_gqa_sinks(q, k, v, sinks, slopes):
    B, S, Hq, Dh = q.shape
    Hkv = k.shape[2]
    G = Hq // Hkv
    nb = S // WINDOW
    qb = q.reshape(B, nb, WINDOW, Hkv, G, Dh)
    kb = _band(k, nb)
    vb = _band(v, nb)
    scores = jnp.einsum('bnqhgd,bnkhd->bnhgqk', qb, kb).astype(jnp.float32) * (Dh ** -0.5)
    qpos = jnp.arange(WINDOW)[:, None] + WINDOW
    kpos = jnp.arange(2 * WINDOW)[None, :]
    dist = qpos - kpos
    in_window = (dist >= 0) & (dist < WINDOW)
    not_pad = (jnp.arange(nb)[:, None] > 0) | (kpos >= WINDOW)
    mask = in_window[None] & not_pad[:, None, :]
    sl = slopes.reshape(Hkv, G)
    alibi = -sl[:, :, None, None] * dist.astype(jnp.float32)[None, None]
    scores = jnp.where(mask[None, :, None, None], scores + alibi[None, None], -jnp.inf)
    sink = sinks.astype(jnp.float32).reshape(Hkv, G)[None, None, :, :, None, None]
    m = jnp.maximum(jnp.max(scores, axis=-1, keepdims=True), sink)
    p = jnp.exp(scores - m)
    p = p / (jnp.sum(p, axis=-1, keepdims=True) + jnp.exp(sink - m))
    out = jnp.einsum('bnhgqk,bnkhd->bnqhgd', p.astype(v.dtype), vb)
    return out.reshape(B, S, Hq * Dh)


def _chunked_sgu(u, v, w_s, b_s):
    B, S, H, C = v.shape
    nc = S // CHUNK
    vc = v.reshape(B, nc, CHUNK, H, C)
    w = jnp.tril(w_s)
    mixed = jnp.einsum('hts,bnshc->bnthc', w, vc) + jnp.transpose(b_s)[None, None, :, :, None]
    return u * mixed.reshape(B, S, H, C)


def _fwd_setup_inputs(seed: int = 0) -> dict:
    key = jax.random.key(seed)
    ks = jax.random.split(key, 9)
    f32 = jnp.float32
    x = jax.random.normal(ks[0], (BATCH, SEQ, D_MODEL), f32)
    norm_g = 1.0 + 0.02 * jax.random.normal(ks[1], (DEPTH, D_MODEL), f32)
    w_in = jax.random.normal(ks[2], (DEPTH, D_MODEL, D_IN), f32) * (D_MODEL ** -0.5)
    q_norm = 1.0 + 0.02 * jax.random.normal(ks[3], (DEPTH, HEAD_DIM), f32)
    k_norm = 1.0 + 0.02 * jax.random.normal(ks[4], (DEPTH, HEAD_DIM), f32)
    sinks = 0.5 * jax.random.normal(ks[5], (DEPTH, N_Q_HEADS), f32)
    w_s = jax.random.normal(ks[6], (DEPTH, N_GMLP_HEADS, CHUNK, CHUNK), f32) * (0.5 * CHUNK ** -0.5)
    b_s = 1.0 + 0.02 * jax.random.normal(ks[7], (DEPTH, N_GMLP_HEADS, CHUNK), f32)
    w_out = jax.random.normal(ks[8], (DEPTH, D_MIX, D_MODEL), f32) * (0.5 * D_MIX ** -0.5)
    return {"x": x, "norm_g": norm_g, "w_in": w_in, "q_norm": q_norm, "k_norm": k_norm,
            "sinks": sinks, "w_s": w_s, "b_s": b_s, "w_out": w_out}


def _fwd_reference(x, norm_g, w_in, q_norm, k_norm, sinks, w_s, b_s, w_out):
    B, S, _ = x.shape
    slopes = _alibi_slopes(N_Q_HEADS)
    sizes = [D_ATTN, D_KV, D_KV, D_ATTN, D_GMLP, D_GMLP, D_GMLP]
    cuts = [int(c) for c in np.cumsum(sizes)[:-1]]
    for l in range(DEPTH):
        h = _rmsnorm(x, norm_g[l])
        proj = jnp.einsum('bsd,de->bse', h, w_in[l])
        q, k, v, g_a, z_u, z_v, g_b = jnp.split(proj, cuts, axis=-1)
        q = _rmsnorm(q.reshape(B, S, N_Q_HEADS, HEAD_DIM), q_norm[l])
        k = _rmsnorm(k.reshape(B, S, N_KV_HEADS, HEAD_DIM), k_norm[l])
        v = v.reshape(B, S, N_KV_HEADS, HEAD_DIM)
        attn = _swa_gqa_sinks(q, k, v, sinks[l], slopes) * jax.nn.silu(g_a)
        z_u = jax.nn.gelu(z_u, approximate=False).reshape(B, S, N_GMLP_HEADS, HEAD_DIM)
        z_v = jax.nn.gelu(z_v, approximate=False).reshape(B, S, N_GMLP_HEADS, HEAD_DIM)
        sgu = _chunked_sgu(z_u, z_v, w_s[l], b_s[l]).reshape(B, S, D_GMLP) * jax.nn.silu(g_b)
        mix = jnp.concatenate([attn, sgu], axis=-1)
        x = x + jnp.einsum('bse,ed->bsd', mix, w_out[l])
    return x


import jax as _jax
import jax.numpy as _jnp

TWIN_FORMAT = 'train_step'
FWD_PARAMS = ['x', 'norm_g', 'w_in', 'q_norm', 'k_norm', 'sinks', 'w_s', 'b_s', 'w_out']
TWIN_WEIGHTS = ['norm_g', 'w_in', 'q_norm', 'k_norm', 'sinks', 'w_s', 'b_s', 'w_out']
TWIN_DIFF_INPUT = 'x'
TWIN_INPUTS = ['x', 'norm_g', 'w_in', 'q_norm', 'k_norm', 'sinks', 'w_s', 'b_s', 'w_out', 'loss_target', 'm_norm_g', 'm_w_in', 'm_q_norm', 'm_k_norm', 'm_sinks', 'm_w_s', 'm_b_s', 'm_w_out', 'v_norm_g', 'v_w_in', 'v_q_norm', 'v_k_norm', 'v_sinks', 'v_w_s', 'v_b_s', 'v_w_out']
TWIN_OUTPUTS = ['loss', 'grad_x', 'grad_norm_g', 'grad_w_in', 'grad_q_norm', 'grad_k_norm', 'grad_sinks', 'grad_w_s', 'grad_b_s', 'grad_w_out', 'delta_norm_g', 'delta_w_in', 'delta_q_norm', 'delta_k_norm', 'delta_sinks', 'delta_w_s', 'delta_b_s', 'delta_w_out', 'new_m_norm_g', 'new_m_w_in', 'new_m_q_norm', 'new_m_k_norm', 'new_m_sinks', 'new_m_w_s', 'new_m_b_s', 'new_m_w_out', 'new_v_norm_g', 'new_v_w_in', 'new_v_q_norm', 'new_v_k_norm', 'new_v_sinks', 'new_v_w_s', 'new_v_b_s', 'new_v_w_out']
TWIN_LEAF_KINDS = {'loss': 'loss', 'grad_x': 'grad_x', 'grad_norm_g': 'grad_w', 'grad_w_in': 'grad_w', 'grad_q_norm': 'grad_w', 'grad_k_norm': 'grad_w', 'grad_sinks': 'grad_w', 'grad_w_s': 'grad_w', 'grad_b_s': 'grad_w', 'grad_w_out': 'grad_w', 'delta_norm_g': 'delta_w', 'delta_w_in': 'delta_w', 'delta_q_norm': 'delta_w', 'delta_k_norm': 'delta_w', 'delta_sinks': 'delta_w', 'delta_w_s': 'delta_w', 'delta_b_s': 'delta_w', 'delta_w_out': 'delta_w', 'new_m_norm_g': 'new_m', 'new_m_w_in': 'new_m', 'new_m_q_norm': 'new_m', 'new_m_k_norm': 'new_m', 'new_m_sinks': 'new_m', 'new_m_w_s': 'new_m', 'new_m_b_s': 'new_m', 'new_m_w_out': 'new_m', 'new_v_norm_g': 'new_v', 'new_v_w_in': 'new_v', 'new_v_q_norm': 'new_v', 'new_v_k_norm': 'new_v', 'new_v_sinks': 'new_v', 'new_v_w_s': 'new_v', 'new_v_b_s': 'new_v', 'new_v_w_out': 'new_v'}


def _forward(args):
    return _fwd_reference(*[args[k] for k in FWD_PARAMS])


def _output_shape():
    def fwd():
        inp = _fwd_setup_inputs(0)
        return _fwd_reference(*[inp[k] for k in FWD_PARAMS])
    out = _jax.eval_shape(fwd)
    return out.shape, out.dtype

N_MICROBATCH = 1
ADAM_LR = 0.001
ADAM_B1 = 0.9
ADAM_B2 = 0.999
ADAM_EPS = 1e-08
ADAM_WD = 0.01
ADAM_STEP = 10
PER_EXAMPLE_BATCH_AXIS = {'x': 0, 'loss_target': 0}
SHARED_INPUTS = []
_WEIGHT_DTYPES = {'norm_g': _jnp.float32, 'w_in': _jnp.float32, 'q_norm': _jnp.float32, 'k_norm': _jnp.float32, 'sinks': _jnp.float32, 'w_s': _jnp.float32, 'b_s': _jnp.float32, 'w_out': _jnp.float32}
MOMENT_SCALE = {'norm_g': 1.713607e+00, 'w_in': 3.924897e-02, 'q_norm': 1.675648e+00, 'k_norm': 1.674551e+00, 'sinks': 3.493009e+00, 'w_s': 1.023826e-01, 'b_s': 5.813650e-01, 'w_out': 9.329204e-02}


def _to_microbatches(a, axis):
    t = _jnp.moveaxis(a, axis, 0)
    t = t.reshape((N_MICROBATCH, t.shape[0] // N_MICROBATCH) + t.shape[1:])
    return _jnp.moveaxis(t, 1, axis + 1)


def setup_inputs(seed: int = 0) -> dict:
    inp = _fwd_setup_inputs(seed)
    key = _jax.random.fold_in(_jax.random.key(seed), 7919)
    shape, _ = _output_shape()
    out = dict(inp)
    out["loss_target"] = _jax.random.normal(_jax.random.fold_in(key, 0), shape, _jnp.float32)
    for i, name in enumerate(TWIN_WEIGHTS):
        w = inp[name].astype(_jnp.float32)
        if MOMENT_SCALE is None:
            s = _jnp.sqrt(_jnp.mean(_jnp.square(w)) + 1e-30)
        else:
            s = MOMENT_SCALE[name]
        km, kv = _jax.random.split(_jax.random.fold_in(key, i + 1))
        out[name] = w
        out["m_" + name] = s * _jax.random.normal(km, w.shape, _jnp.float32)
        out["v_" + name] = (s * s) * _jax.random.uniform(kv, w.shape, _jnp.float32, 0.5, 1.5)
    if N_MICROBATCH > 1:
        for name, axis in PER_EXAMPLE_BATCH_AXIS.items():
            out[name] = _to_microbatches(out[name], axis)
    return {'x': out['x'], 'norm_g': out['norm_g'], 'w_in': out['w_in'], 'q_norm': out['q_norm'], 'k_norm': out['k_norm'], 'sinks': out['sinks'], 'w_s': out['w_s'], 'b_s': out['b_s'], 'w_out': out['w_out'], 'loss_target': out['loss_target'], 'm_norm_g': out['m_norm_g'], 'm_w_in': out['m_w_in'], 'm_q_norm': out['m_q_norm'], 'm_k_norm': out['m_k_norm'], 'm_sinks': out['m_sinks'], 'm_w_s': out['m_w_s'], 'm_b_s': out['m_b_s'], 'm_w_out': out['m_w_out'], 'v_norm_g': out['v_norm_g'], 'v_w_in': out['v_w_in'], 'v_q_norm': out['v_q_norm'], 'v_k_norm': out['v_k_norm'], 'v_sinks': out['v_sinks'], 'v_w_s': out['v_w_s'], 'v_b_s': out['v_b_s'], 'v_w_out': out['v_w_out']}


def _loss(weights, diff, rest, loss_target):
    with _jax.named_scope("forward"):
        args = {**rest, TWIN_DIFF_INPUT: diff, **{k: w.astype(_WEIGHT_DTYPES[k]) for k, w in weights.items()}}
        y = _forward(args)
    with _jax.named_scope("loss_head"):
        err = _jnp.square(y.astype(_jnp.float32) - loss_target)
        return 0.5 * _jnp.sum(_jnp.mean(err, axis=-1)) if err.ndim else 0.5 * err


def _adamw(w, g, m, v):
    m = ADAM_B1 * m + (1.0 - ADAM_B1) * g
    v = ADAM_B2 * v + (1.0 - ADAM_B2) * _jnp.square(g)
    m_hat = m / (1.0 - ADAM_B1 ** ADAM_STEP)
    v_hat = v / (1.0 - ADAM_B2 ** ADAM_STEP)
    delta = -ADAM_LR * (m_hat / (_jnp.sqrt(v_hat) + ADAM_EPS) + ADAM_WD * w)
    return delta, m, v


def reference(x, norm_g, w_in, q_norm, k_norm, sinks, w_s, b_s, w_out, loss_target, m_norm_g, m_w_in, m_q_norm, m_k_norm, m_sinks, m_w_s, m_b_s, m_w_out, v_norm_g, v_w_in, v_q_norm, v_k_norm, v_sinks, v_w_s, v_b_s, v_w_out):
    given = dict(x=x, norm_g=norm_g, w_in=w_in, q_norm=q_norm, k_norm=k_norm, sinks=sinks, w_s=w_s, b_s=b_s, w_out=w_out, loss_target=loss_target, m_norm_g=m_norm_g, m_w_in=m_w_in, m_q_norm=m_q_norm, m_k_norm=m_k_norm, m_sinks=m_sinks, m_w_s=m_w_s, m_b_s=m_b_s, m_w_out=m_w_out, v_norm_g=v_norm_g, v_w_in=v_w_in, v_q_norm=v_q_norm, v_k_norm=v_k_norm, v_sinks=v_sinks, v_w_s=v_w_s, v_b_s=v_b_s, v_w_out=v_w_out)
    weights = {n: given[n] for n in TWIN_WEIGHTS}
    shared = {n: given[n] for n in SHARED_INPUTS}
    per_example = {n: given[n] for n in ['x']}
    grad_fn = _jax.value_and_grad(_loss, argnums=(0, 1))

    def one_microbatch(ex, loss_target):
        ex = dict(ex)
        diff = ex.pop(TWIN_DIFF_INPUT)
        return grad_fn(weights, diff, {**shared, **ex}, loss_target)

    if N_MICROBATCH == 1:
        loss, (grad_w, grad_x) = one_microbatch(per_example, given["loss_target"])
    else:
        def body(carry, xs):
            loss_sum, grad_sum = carry
            l_k, (gw_k, gx_k) = one_microbatch(xs[0], xs[1])
            with _jax.named_scope("update"):
                return (loss_sum + l_k, _jax.tree.map(_jnp.add, grad_sum, gw_k)), gx_k

        init = (_jnp.zeros((), _jnp.float32), _jax.tree.map(_jnp.zeros_like, weights))
        (loss, grad_w), grad_x = _jax.lax.scan(body, init, (per_example, given["loss_target"]))
    with _jax.named_scope("update"):
        delta_w, new_m, new_v = {}, {}, {}
        for n in TWIN_WEIGHTS:
            delta_w[n], new_m[n], new_v[n] = _adamw(weights[n], grad_w[n], given["m_" + n], given["v_" + n])
    return (loss, grad_x, *[grad_w[n] for n in TWIN_WEIGHTS], *[delta_w[n] for n in TWIN_WEIGHTS],
            *[new_m[n] for n in TWIN_WEIGHTS], *[new_v[n] for n in TWIN_WEIGHTS])
```

```python
import functools

import numpy as np
import jax
import jax.numpy as jnp
from jax import lax
from jax.experimental import pallas as pl
from jax.experimental.pallas import tpu as pltpu

F32 = jnp.float32
BF16 = jnp.bfloat16

D_MODEL = 2048
D_ATTN = 1024
D_IN = 5632
N_HEADS = 16
DEPTH = 4
BLK = 128
HALF = 64
RMS_EPS = 1e-6
C_Q, C_K, C_V, C_GA, C_ZU, C_ZV, C_GB = 0, 1024, 1280, 1536, 2560, 3584, 4608
NEG = -1e30
N_DEV = 8
W_IN_SHARD = D_IN // N_DEV
W_OUT_SHARD = D_MODEL // N_DEV
INV_SQRT2 = 0.7071067811865476
INV_SQRT_2PI = 0.3989422804014327

ADAM_LR = 0.001
ADAM_B1 = 0.9
ADAM_B2 = 0.999
ADAM_EPS = 1e-08
ADAM_WD = 0.01
ADAM_STEP = 10

NT_DIMS = (((1,), (1,)), ((), ()))
TN_DIMS = (((0,), (0,)), ((), ()))
MESH_AXES = ("x", "y", "c")


def _params(*sem):
    return pltpu.CompilerParams(dimension_semantics=sem)


def _sigmoid(v):
    return 1.0 / (1.0 + jnp.exp(-v))


def _gelu_cdf(z):
    return 0.5 * (1.0 + lax.erf(z * INV_SQRT2))


def _gelu_grad(z, cdf):
    return cdf + z * (jnp.exp(-0.5 * z * z) * INV_SQRT_2PI)


def _lane_halves(rows):
    lane = lax.broadcasted_iota(jnp.int32, (rows, BLK), 1)
    return lane < HALF, lane >= HALF


def _half_sum(v, h_a, h_b):
    s_a = jnp.sum(jnp.where(h_a, v, 0.0), axis=-1, keepdims=True)
    s_b = jnp.sum(jnp.where(h_b, v, 0.0), axis=-1, keepdims=True)
    return jnp.where(h_a, s_a, s_b)


def _half_rms(v, w, h_a, h_b):
    r = lax.rsqrt(_half_sum(v * v, h_a, h_b) * (1.0 / HALF) + RMS_EPS)
    yhat = v * r
    return yhat, r, yhat * w


def _half_rms_bwd(dy, yhat, r, w, h_a, h_b):
    dyh = dy * w
    c = _half_sum(dyh * yhat, h_a, h_b) * (1.0 / HALF)
    return r * (dyh - yhat * c)


def _band_mask(n):
    t = lax.broadcasted_iota(jnp.int32, (BLK, 2 * BLK), 0)
    kk = lax.broadcasted_iota(jnp.int32, (BLK, 2 * BLK), 1)
    dist = t + BLK - kk
    first_key = jnp.where(n > 0, 0, BLK)
    ok = (dist >= 0) & (dist < BLK) & (kk >= first_key)
    return ok, dist.astype(F32)


def _softmax_sink(s_raw, slope, sink, ok, distf):
    s = s_raw * 0.125 - slope * distf
    s = jnp.where(ok, s, NEG)
    m = jnp.maximum(jnp.max(s, axis=-1, keepdims=True), sink)
    p = jnp.exp(s - m)
    es = jnp.exp(sink - m)
    inv = 1.0 / (jnp.sum(p, axis=-1, keepdims=True) + es)
    return p * inv, es * inv


def _rows(i):
    return slice(BLK * i, BLK * (i + 1))


def _cols(base, j):
    return slice(base + BLK * j, base + BLK * (j + 1))


def _to_half(v, have, want):
    return v if have == want else pltpu.roll(v, HALF, 1)


def _tril_mask():
    row = lax.broadcasted_iota(jnp.int32, (BLK, BLK), 0)
    col = lax.broadcasted_iota(jnp.int32, (BLK, BLK), 1)
    return row >= col


def _norm_proj(x, g, w_t):
    s = x.shape[0]
    tm, tn = min(512, s), 512

    def body(x_ref, g_ref, w_ref, h_ref, p_ref):
        @pl.when(pl.program_id(1) == 0)
        def _():
            xf = x_ref[...]
            r = lax.rsqrt(jnp.mean(xf * xf, axis=-1, keepdims=True) + RMS_EPS)
            h_ref[...] = ((xf * r) * g_ref[...]).astype(BF16)

        p_ref[...] = lax.dot_general(h_ref[...], w_ref[...], NT_DIMS, preferred_element_type=F32)

    return pl.pallas_call(
        body, name="norm_proj",
        grid=(s // tm, D_IN // tn),
        in_specs=[pl.BlockSpec((tm, D_MODEL), lambda i, j: (i, 0)),
                  pl.BlockSpec((1, D_MODEL), lambda i, j: (0, 0)),
                  pl.BlockSpec((tn, D_MODEL), lambda i, j: (j, 0))],
        out_specs=[pl.BlockSpec((tm, D_MODEL), lambda i, j: (i, 0)),
                   pl.BlockSpec((tm, tn), lambda i, j: (i, j))],
        out_shape=[jax.ShapeDtypeStruct((s, D_MODEL), BF16), jax.ShapeDtypeStruct((s, D_IN), F32)],
        compiler_params=_params("parallel", "arbitrary"),
    )(x, g, w_t)


def _mixer_fwd(proj, qw, kw, sinks, slopes, w_s, bmap):
    s = proj.shape[0]
    nb = s // BLK

    def body(sinks_ref, slopes_ref, p_ref, kvp_ref, qw_ref, kw_ref, ws_ref, bmap_ref, mix_ref,
             qs_scr, pb_scr):
        n = pl.program_id(0)
        h_a, h_b = _lane_halves(BLK)
        h_a2, h_b2 = _lane_halves(2 * BLK)
        halves = (h_a, h_b)
        ok, distf = _band_mask(n)
        qw_v = qw_ref[...]
        kw_v = kw_ref[...]
        for kt in range(2):
            kband = jnp.concatenate([kvp_ref[:, _cols(0, kt)], p_ref[:, _cols(C_K, kt)]], axis=0)
            _, _, kn = _half_rms(kband, kw_v, h_a2, h_b2)
            knb = kn.astype(BF16)
            vband = jnp.concatenate([kvp_ref[:, _cols(256, kt)], p_ref[:, _cols(C_V, kt)]], axis=0).astype(BF16)
            for gh in range(2):
                g = 2 * kt + gh
                for t in range(2):
                    _, _, qn = _half_rms(p_ref[:, _cols(C_Q, 2 * g + t)], qw_v, h_a, h_b)
                    for qh in range(2):
                        qm = _to_half(jnp.where(halves[qh], qn, 0.0), qh, gh)
                        qs_scr[_rows(2 * t + qh), :] = qm.astype(BF16)
                sc = lax.dot_general(qs_scr[...], knb, NT_DIMS, preferred_element_type=F32)
                for i in range(4):
                    hd = 4 * g + i
                    p, _ = _softmax_sink(sc[_rows(i)], slopes_ref[hd], sinks_ref[hd], ok, distf)
                    pb_scr[_rows(i), :] = p.astype(BF16)
                o = jnp.dot(pb_scr[...], vband, preferred_element_type=F32)
                for t in range(2):
                    j = 2 * g + t
                    out = jnp.where(h_a, _to_half(o[_rows(2 * t)], gh, 0), _to_half(o[_rows(2 * t + 1)], gh, 1))
                    ga = p_ref[:, _cols(C_GA, j)]
                    mix_ref[:, _cols(0, j)] = (out * (ga * _sigmoid(ga))).astype(BF16)

        tril = _tril_mask()
        for j in range(8):
            zu_pre = p_ref[:, _cols(C_ZU, j)]
            zv_pre = p_ref[:, _cols(C_ZV, j)]
            gb = p_ref[:, _cols(C_GB, j)]
            zu = zu_pre * _gelu_cdf(zu_pre)
            zvb = (zv_pre * _gelu_cdf(zv_pre)).astype(BF16)
            w_a = jnp.where(tril, ws_ref[2 * j], 0.0).astype(BF16)
            w_b = jnp.where(tril, ws_ref[2 * j + 1], 0.0).astype(BF16)
            mixed = jnp.where(h_a, jnp.dot(w_a, zvb, preferred_element_type=F32),
                              jnp.dot(w_b, zvb, preferred_element_type=F32)) + bmap_ref[:, _cols(0, j)]
            mix_ref[:, _cols(D_ATTN, j)] = ((zu * mixed) * (gb * _sigmoid(gb))).astype(BF16)

    smem = pl.BlockSpec(memory_space=pltpu.SMEM)
    return pl.pallas_call(
        body, name="mixer_fwd",
        grid=(nb,),
        in_specs=[smem, smem,
                  pl.BlockSpec((BLK, D_IN), lambda n: (n, 0)),
                  pl.BlockSpec((BLK, 512), lambda n: (jnp.maximum(n - 1, 0), 2)),
                  pl.BlockSpec((1, BLK), lambda n: (0, 0)),
                  pl.BlockSpec((1, BLK), lambda n: (0, 0)),
                  pl.BlockSpec((N_HEADS, BLK, BLK), lambda n: (0, 0, 0)),
                  pl.BlockSpec((BLK, D_ATTN), lambda n: (0, 0))],
        out_specs=pl.BlockSpec((BLK, D_MODEL), lambda n: (n, 0)),
        out_shape=jax.ShapeDtypeStruct((s, D_MODEL), BF16),
        scratch_shapes=[pltpu.VMEM((4 * BLK, BLK), BF16), pltpu.VMEM((4 * BLK, 2 * BLK), BF16)],
        compiler_params=_params("arbitrary"),
    )(sinks, slopes, proj, proj, qw, kw, w_s, bmap)


def _out_proj(mix, w_o, x):
    s = x.shape[0]
    tm = min(512, s)

    def body(m_ref, w_ref, x_ref, o_ref):
        o_ref[...] = x_ref[...] + jnp.dot(m_ref[...], w_ref[...], preferred_element_type=F32)

    return pl.pallas_call(
        body, name="out_proj",
        grid=(s // tm,),
        in_specs=[pl.BlockSpec((tm, D_MODEL), lambda i: (i, 0)),
                  pl.BlockSpec((D_MODEL, D_MODEL), lambda i: (0, 0)),
                  pl.BlockSpec((tm, D_MODEL), lambda i: (i, 0))],
        out_specs=pl.BlockSpec((tm, D_MODEL), lambda i: (i, 0)),
        out_shape=jax.ShapeDtypeStruct((s, D_MODEL), F32),
        compiler_params=_params("parallel"),
    )(mix, w_o, x)


def _loss_grad(y, tgt):
    s = y.shape[0]
    tm = min(512, s)

    def body(y_ref, t_ref, dy_ref, sq_ref):
        @pl.when(pl.program_id(0) == 0)
        def _():
            sq_ref[...] = jnp.zeros_like(sq_ref)

        e = y_ref[...] - t_ref[...]
        dy_ref[...] = e * (1.0 / D_MODEL)
        sq_ref[...] += jnp.sum(e * e, axis=0, keepdims=True)

    return pl.pallas_call(
        body, name="loss_grad",
        grid=(s // tm,),
        in_specs=[pl.BlockSpec((tm, D_MODEL), lambda i: (i, 0)), pl.BlockSpec((tm, D_MODEL), lambda i: (i, 0))],
        out_specs=[pl.BlockSpec((tm, D_MODEL), lambda i: (i, 0)), pl.BlockSpec((1, D_MODEL), lambda i: (0, 0))],
        out_shape=[jax.ShapeDtypeStruct((s, D_MODEL), F32), jax.ShapeDtypeStruct((1, D_MODEL), F32)],
        compiler_params=_params("arbitrary"),
    )(y, tgt)


def _dmix(dx, w_o):
    s = dx.shape[0]
    tm = min(512, s)

    def body(d_ref, w_ref, o_ref):
        o_ref[...] = lax.dot_general(d_ref[...].astype(BF16), w_ref[...], NT_DIMS, preferred_element_type=F32)

    return pl.pallas_call(
        body, name="dmix",
        grid=(s // tm,),
        in_specs=[pl.BlockSpec((tm, D_MODEL), lambda i: (i, 0)),
                  pl.BlockSpec((D_MODEL, D_MODEL), lambda i: (0, 0))],
        out_specs=pl.BlockSpec((tm, D_MODEL), lambda i: (i, 0)),
        out_shape=jax.ShapeDtypeStruct((s, D_MODEL), F32),
        compiler_params=_params("parallel"),
    )(dx, w_o)


def _dw_out(mix, dx):
    s = dx.shape[0]
    tk, tn = min(512, s), 512
    nk = s // tk

    def body(m_ref, d_ref, o_ref, acc):
        k = pl.program_id(1)

        @pl.when(k == 0)
        def _():
            acc[...] = jnp.zeros_like(acc)

        acc[...] += lax.dot_general(m_ref[...], d_ref[...].astype(BF16), TN_DIMS, preferred_element_type=F32)

        @pl.when(k == nk - 1)
        def _():
            o_ref[...] = acc[...].astype(BF16)

    return pl.pallas_call(
        body, name="dw_out",
        grid=(D_MODEL // tn, nk),
        in_specs=[pl.BlockSpec((tk, D_MODEL), lambda j, k: (k, 0)),
                  pl.BlockSpec((tk, tn), lambda j, k: (k, j))],
        out_specs=pl.BlockSpec((D_MODEL, tn), lambda j, k: (0, j)),
        out_shape=jax.ShapeDtypeStruct((D_MODEL, D_MODEL), BF16),
        scratch_shapes=[pltpu.VMEM((D_MODEL, tn), F32)],
        compiler_params=_params("parallel", "arbitrary"),
    )(mix, dx)


def _mixer_bwd(proj, dmix, qw, kw, sinks, slopes, w_s, bmap):
    s = proj.shape[0]
    nb = s // BLK

    def body(sinks_ref, slopes_ref, p_ref, kvp_ref, dm_ref, qw_ref, kw_ref, ws_ref, bmap_ref,
             dp_ref, dqw_ref, dkw_ref, dsk_ref, dws_ref, dbs_ref,
             pend, accq, acck, accs, accb, qs_scr, pb_scr, dos_scr, dqk_scr):
        n = pl.program_id(0)
        h_a, h_b = _lane_halves(BLK)
        h_a2, h_b2 = _lane_halves(2 * BLK)
        halves = (h_a, h_b)
        lane = lax.broadcasted_iota(jnp.int32, (BLK, BLK), 1)
        tril = _tril_mask()

        @pl.when(n == 0)
        def _():
            accq[...] = jnp.zeros_like(accq)
            acck[...] = jnp.zeros_like(acck)
            accs[...] = jnp.zeros_like(accs)
            accb[...] = jnp.zeros_like(accb)
            dws_ref[...] = jnp.zeros_like(dws_ref)

        @pl.when(n >= 1)
        def _():
            dp_ref[:, 0:C_K] = pend[:, 0:C_K].astype(BF16)
            dp_ref[:, C_GA:D_IN] = pend[:, C_GA:D_IN].astype(BF16)

        @pl.when(n < nb)
        def _():
            ok, distf = _band_mask(n)
            qw_v = qw_ref[...]
            kw_v = kw_ref[...]
            for kt in range(2):
                kband = jnp.concatenate([kvp_ref[:, _cols(0, kt)], p_ref[:, _cols(C_K, kt)]], axis=0)
                kyhat, kr, kn = _half_rms(kband, kw_v, h_a2, h_b2)
                knb = kn.astype(BF16)
                vband = jnp.concatenate([kvp_ref[:, _cols(256, kt)], p_ref[:, _cols(C_V, kt)]],
                                        axis=0).astype(BF16)
                dkn = jnp.zeros((2 * BLK, BLK), F32)
                dvb = jnp.zeros((2 * BLK, BLK), F32)
                for gh in range(2):
                    g = 2 * kt + gh
                    tiles = []
                    for t in range(2):
                        j = 2 * g + t
                        qyhat, qr, qn = _half_rms(p_ref[:, _cols(C_Q, j)], qw_v, h_a, h_b)
                        ga = p_ref[:, _cols(C_GA, j)]
                        sg = _sigmoid(ga)
                        dma = dm_ref[:, _cols(0, j)]
                        d_o = dma * (ga * sg)
                        tiles.append((qyhat, qr, ga, sg, dma))
                        for qh in range(2):
                            qm = _to_half(jnp.where(halves[qh], qn, 0.0), qh, gh)
                            qs_scr[_rows(2 * t + qh), :] = qm.astype(BF16)
                            dom = _to_half(jnp.where(halves[qh], d_o, 0.0), qh, gh)
                            dos_scr[_rows(2 * t + qh), :] = dom.astype(BF16)
                    sc = lax.dot_general(qs_scr[...], knb, NT_DIMS, preferred_element_type=F32)
                    dpm = lax.dot_general(dos_scr[...], vband, NT_DIMS, preferred_element_type=F32)
                    for i in range(4):
                        hd = 4 * g + i
                        p, p_sink = _softmax_sink(sc[_rows(i)], slopes_ref[hd], sinks_ref[hd], ok, distf)
                        dp_i = dpm[_rows(i)]
                        dsum = jnp.sum(p * dp_i, axis=-1, keepdims=True)
                        ds = p * (dp_i - dsum)
                        accs[...] += jnp.where(lane == hd, -(p_sink * dsum), 0.0)
                        pb_scr[_rows(i), :] = p.astype(BF16)
                        dqk_scr[_rows(i), :] = (ds * 0.125).astype(BF16)
                    o = jnp.dot(pb_scr[...], vband, preferred_element_type=F32)
                    dvb = dvb + lax.dot_general(pb_scr[...], dos_scr[...], TN_DIMS, preferred_element_type=F32)
                    dqn_all = jnp.dot(dqk_scr[...], knb, preferred_element_type=F32)
                    dkn = dkn + lax.dot_general(dqk_scr[...], qs_scr[...], TN_DIMS, preferred_element_type=F32)
                    for t in range(2):
                        j = 2 * g + t
                        qyhat, qr, ga, sg, dma = tiles[t]
                        out = jnp.where(h_a, _to_half(o[_rows(2 * t)], gh, 0), _to_half(o[_rows(2 * t + 1)], gh, 1))
                        dqn = jnp.where(h_a, _to_half(dqn_all[_rows(2 * t)], gh, 0),
                                        _to_half(dqn_all[_rows(2 * t + 1)], gh, 1))
                        pend[:, _cols(C_GA, j)] = (dma * out) * (sg * (1.0 + ga * (1.0 - sg)))
                        pend[:, _cols(C_Q, j)] = _half_rms_bwd(dqn, qyhat, qr, qw_v, h_a, h_b)
                        accq[...] += dqn * qyhat
                dk = _half_rms_bwd(dkn, kyhat, kr, kw_v, h_a2, h_b2)
                dkw_part = dkn * kyhat
                acck[...] += dkw_part[0:BLK] + dkw_part[BLK:2 * BLK]

                @pl.when(n >= 1)
                def _():
                    dp_ref[:, _cols(C_K, kt)] = (pend[:, _cols(C_K, kt)] + dk[0:BLK]).astype(BF16)
                    dp_ref[:, _cols(C_V, kt)] = (pend[:, _cols(C_V, kt)] + dvb[0:BLK]).astype(BF16)

                pend[:, _cols(C_K, kt)] = dk[BLK:2 * BLK]
                pend[:, _cols(C_V, kt)] = dvb[BLK:2 * BLK]

            for j in range(8):
                zu_pre = p_ref[:, _cols(C_ZU, j)]
                zv_pre = p_ref[:, _cols(C_ZV, j)]
                gb = p_ref[:, _cols(C_GB, j)]
                cu = _gelu_cdf(zu_pre)
                cv = _gelu_cdf(zv_pre)
                zu = zu_pre * cu
                zvb = (zv_pre * cv).astype(BF16)
                sgb = _sigmoid(gb)
                silub = gb * sgb
                w_a = jnp.where(tril, ws_ref[2 * j], 0.0).astype(BF16)
                w_b = jnp.where(tril, ws_ref[2 * j + 1], 0.0).astype(BF16)
                mixed = jnp.where(h_a, jnp.dot(w_a, zvb, preferred_element_type=F32),
                                  jnp.dot(w_b, zvb, preferred_element_type=F32)) + bmap_ref[:, _cols(0, j)]
                dmb = dm_ref[:, _cols(D_ATTN, j)]
                dgate = dmb * silub
                pend[:, _cols(C_ZU, j)] = (dgate * mixed) * _gelu_grad(zu_pre, cu)
                pend[:, _cols(C_GB, j)] = (dmb * (zu * mixed)) * (sgb * (1.0 + gb * (1.0 - sgb)))
                dmixed = dgate * zu
                accb[:, _cols(0, j)] += dmixed
                dm_a = jnp.where(h_a, dmixed, 0.0).astype(BF16)
                dm_b = jnp.where(h_b, dmixed, 0.0).astype(BF16)
                dzv = (lax.dot_general(w_a, dm_a, TN_DIMS, preferred_element_type=F32)
                       + lax.dot_general(w_b, dm_b, TN_DIMS, preferred_element_type=F32))
                pend[:, _cols(C_ZV, j)] = dzv * _gelu_grad(zv_pre, cv)
                dws_ref[2 * j] += lax.dot_general(dm_a, zvb, NT_DIMS, preferred_element_type=F32)
                dws_ref[2 * j + 1] += lax.dot_general(dm_b, zvb, NT_DIMS, preferred_element_type=F32)

        @pl.when(n == nb)
        def _():
            dp_ref[:, C_K:C_GA] = pend[:, C_K:C_GA].astype(BF16)
            aq = accq[...]
            ak = acck[...]
            dqw_ref[...] = jnp.sum(aq + pltpu.roll(aq, HALF, 1), axis=0, keepdims=True)
            dkw_ref[...] = jnp.sum(ak + pltpu.roll(ak, HALF, 1), axis=0, keepdims=True)
            dsk_ref[...] = jnp.sum(accs[...], axis=0, keepdims=True)
            for hd in range(N_HEADS):
                dws_ref[hd] = jnp.where(tril, dws_ref[hd], 0.0)
            hrow = lax.broadcasted_iota(jnp.int32, (N_HEADS, D_ATTN), 0)
            hcol = lax.broadcasted_iota(jnp.int32, (N_HEADS, D_ATTN), 1)
            sel = jnp.where((hcol >= hrow * HALF) & (hcol < (hrow + 1) * HALF), 1.0, 0.0).astype(BF16)
            rem = accb[...]
            tot = jnp.zeros((N_HEADS, BLK), F32)
            for _ in range(3):
                part = rem.astype(BF16)
                tot = tot + lax.dot_general(sel, part, NT_DIMS, preferred_element_type=F32)
                rem = rem - part.astype(F32)
            dbs_ref[...] = tot

    smem = pl.BlockSpec(memory_space=pltpu.SMEM)
    last = nb - 1
    return pl.pallas_call(
        body, name="mixer_bwd",
        grid=(nb + 1,),
        in_specs=[smem, smem,
                  pl.BlockSpec((BLK, D_IN), lambda n: (jnp.minimum(n, last), 0)),
                  pl.BlockSpec((BLK, 512), lambda n: (jnp.maximum(jnp.minimum(n, last) - 1, 0), 2)),
                  pl.BlockSpec((BLK, D_MODEL), lambda n: (jnp.minimum(n, last), 0)),
                  pl.BlockSpec((1, BLK), lambda n: (0, 0)),
                  pl.BlockSpec((1, BLK), lambda n: (0, 0)),
                  pl.BlockSpec((N_HEADS, BLK, BLK), lambda n: (0, 0, 0)),
                  pl.BlockSpec((BLK, D_ATTN), lambda n: (0, 0))],
        out_specs=[pl.BlockSpec((BLK, D_IN), lambda n: (jnp.maximum(n - 1, 0), 0)),
                   pl.BlockSpec((1, BLK), lambda n: (0, 0)),
                   pl.BlockSpec((1, BLK), lambda n: (0, 0)),
                   pl.BlockSpec((1, BLK), lambda n: (0, 0)),
                   pl.BlockSpec((N_HEADS, BLK, BLK), lambda n: (0, 0, 0)),
                   pl.BlockSpec((N_HEADS, BLK), lambda n: (0, 0))],
        out_shape=[jax.ShapeDtypeStruct((s, D_IN), BF16),
                   jax.ShapeDtypeStruct((1, BLK), F32),
                   jax.ShapeDtypeStruct((1, BLK), F32),
                   jax.ShapeDtypeStruct((1, BLK), F32),
                   jax.ShapeDtypeStruct((N_HEADS, BLK, BLK), F32),
                   jax.ShapeDtypeStruct((N_HEADS, BLK), F32)],
        scratch_shapes=[pltpu.VMEM((BLK, D_IN), F32),
                        pltpu.VMEM((BLK, BLK), F32), pltpu.VMEM((BLK, BLK), F32), pltpu.VMEM((BLK, BLK), F32),
                        pltpu.VMEM((BLK, D_ATTN), F32),
                        pltpu.VMEM((4 * BLK, BLK), BF16), pltpu.VMEM((4 * BLK, 2 * BLK), BF16),
                        pltpu.VMEM((4 * BLK, BLK), BF16), pltpu.VMEM((4 * BLK, 2 * BLK), BF16)],
        compiler_params=_params("arbitrary"),
    )(sinks, slopes, proj, proj, dmix, qw, kw, w_s, bmap)


def _dh_norm_bwd(dproj, w_t, x, dx_out, g):
    s = x.shape[0]
    tm, tk = min(512, s), 512
    nk = D_IN // tk

    def body(dp_ref, w_ref, x_ref, dxo_ref, g_ref, dx_ref, dg_ref, acc):
        i = pl.program_id(0)
        k = pl.program_id(1)

        @pl.when(k == 0)
        def _():
            acc[...] = jnp.zeros_like(acc)

        @pl.when((i == 0) & (k == 0))
        def _():
            dg_ref[...] = jnp.zeros_like(dg_ref)

        acc[...] += jnp.dot(dp_ref[...], w_ref[...], preferred_element_type=F32)

        @pl.when(k == nk - 1)
        def _():
            xf = x_ref[...]
            dh = acc[...]
            r = lax.rsqrt(jnp.mean(xf * xf, axis=-1, keepdims=True) + RMS_EPS)
            yhat = xf * r
            dyh = dh * g_ref[...]
            c = jnp.mean(dyh * yhat, axis=-1, keepdims=True)
            dx_ref[...] = dxo_ref[...] + r * (dyh - yhat * c)
            dg_ref[...] += jnp.sum(dh * yhat, axis=0, keepdims=True)

    return pl.pallas_call(
        body, name="dh_norm_bwd",
        grid=(s // tm, nk),
        in_specs=[pl.BlockSpec((tm, tk), lambda i, k: (i, k)),
                  pl.BlockSpec((tk, D_MODEL), lambda i, k: (k, 0)),
                  pl.BlockSpec((tm, D_MODEL), lambda i, k: (i, 0)),
                  pl.BlockSpec((tm, D_MODEL), lambda i, k: (i, 0)),
                  pl.BlockSpec((1, D_MODEL), lambda i, k: (0, 0))],
        out_specs=[pl.BlockSpec((tm, D_MODEL), lambda i, k: (i, 0)),
                   pl.BlockSpec((1, D_MODEL), lambda i, k: (0, 0))],
        out_shape=[jax.ShapeDtypeStruct((s, D_MODEL), F32), jax.ShapeDtypeStruct((1, D_MODEL), F32)],
        scratch_shapes=[pltpu.VMEM((tm, D_MODEL), F32)],
        compiler_params=_params("arbitrary", "arbitrary"),
    )(dproj, w_t, x, dx_out, g)


def _dw_in(dproj, h):
    s = h.shape[0]
    tk, tn = min(512, s), 512
    nk = s // tk

    def body(dp_ref, h_ref, o_ref, acc):
        k = pl.program_id(1)

        @pl.when(k == 0)
        def _():
            acc[...] = jnp.zeros_like(acc)

        acc[...] += lax.dot_general(dp_ref[...], h_ref[...], TN_DIMS, preferred_element_type=F32)

        @pl.when(k == nk - 1)
        def _():
            o_ref[...] = acc[...].astype(BF16)

    return pl.pallas_call(
        body, name="dw_in",
        grid=(D_IN // tn, nk),
        in_specs=[pl.BlockSpec((tk, tn), lambda j, k: (k, j)),
                  pl.BlockSpec((tk, D_MODEL), lambda j, k: (k, 0))],
        out_specs=pl.BlockSpec((tn, D_MODEL), lambda j, k: (j, 0)),
        out_shape=jax.ShapeDtypeStruct((D_IN, D_MODEL), BF16),
        scratch_shapes=[pltpu.VMEM((tn, D_MODEL), F32)],
        compiler_params=_params("parallel", "arbitrary"),
    )(dproj, h)


def _mesh_place():
    x, y, c = lax.axis_index("x"), lax.axis_index("y"), lax.axis_index("c")
    return x, y, c, 4 * x + 2 * y + c


def _peer(x, y, c, k):
    px = 1 - x if k & 4 else x
    py = 1 - y if k & 2 else y
    pc = 1 - c if k & 1 else c
    return (px, py, pc), 4 * px + 2 * py + pc


def _exchange(name, srcs, out_shapes, src_view, dst_view):
    na = len(srcs)
    hbm = pl.BlockSpec(memory_space=pltpu.HBM)

    def body(*refs):
        src_refs, dst_refs = refs[:na], refs[na:2 * na]
        send_sems, recv_sems, local_sems = refs[2 * na:]
        x, y, c, me = _mesh_place()
        local = []
        remote = []
        for a in range(na):
            cp = pltpu.make_async_copy(src_view(a, src_refs[a], me), dst_view(a, dst_refs[a], me), local_sems.at[a])
            cp.start()
            local.append(cp)
        for k in range(1, N_DEV):
            peer, pidx = _peer(x, y, c, k)
            for a in range(na):
                cp = pltpu.make_async_remote_copy(
                    src_ref=src_view(a, src_refs[a], pidx), dst_ref=dst_view(a, dst_refs[a], me),
                    send_sem=send_sems.at[a * (N_DEV - 1) + k - 1], recv_sem=recv_sems.at[a * (N_DEV - 1) + k - 1],
                    device_id=peer, device_id_type=pl.DeviceIdType.MESH)
                cp.start()
                remote.append(cp)
        for cp in remote:
            cp.wait()
        for cp in local:
            cp.wait()

    return pl.pallas_call(
        body, name=name,
        in_specs=[hbm] * na, out_specs=[hbm] * na,
        out_shape=out_shapes,
        scratch_shapes=[pltpu.SemaphoreType.DMA((na * (N_DEV - 1),)),
                        pltpu.SemaphoreType.DMA((na * (N_DEV - 1),)),
                        pltpu.SemaphoreType.DMA((na,))],
    )(*srcs)


def _all_gather_weights(wt_shard, wo_shard):
    shard_rows = (W_IN_SHARD, W_OUT_SHARD)

    def src_view(a, ref, idx):
        return ref

    def dst_view(a, ref, idx):
        r = shard_rows[a]
        return ref.at[:, pl.ds(pl.multiple_of(idx * r, 64), r), :]

    return _exchange(
        "all_gather_weights", [wt_shard, wo_shard],
        [jax.ShapeDtypeStruct((DEPTH, D_IN, D_MODEL), BF16), jax.ShapeDtypeStruct((DEPTH, D_MODEL, D_MODEL), BF16)],
        src_view, dst_view)


def _scatter_partials(dwt_part, dwo_part):
    shard_rows = (W_IN_SHARD, W_OUT_SHARD)

    def src_view(a, ref, idx):
        r = shard_rows[a]
        return ref.at[pl.ds(pl.multiple_of(idx * r, 64), r), :]

    def dst_view(a, ref, idx):
        return ref.at[idx]

    return _exchange(
        "scatter_partials", [dwt_part, dwo_part],
        [jax.ShapeDtypeStruct((N_DEV, W_IN_SHARD, D_MODEL), BF16),
         jax.ShapeDtypeStruct((N_DEV, W_OUT_SHARD, D_MODEL), BF16)],
        src_view, dst_view)


def _gather_small(packed):
    def src_view(a, ref, idx):
        return ref

    def dst_view(a, ref, idx):
        return ref.at[idx]

    return _exchange("gather_small", [packed], [jax.ShapeDtypeStruct((N_DEV,) + packed.shape, F32)],
                     src_view, dst_view)[0]


def _sum_slots(slots, tr):
    _, r, c = slots.shape

    def body(s_ref, o_ref):
        tot = s_ref[0].astype(F32)
        for d in range(1, N_DEV):
            tot = tot + s_ref[d].astype(F32)
        o_ref[...] = tot

    return pl.pallas_call(
        body, name="sum_slots",
        grid=(r // tr,),
        in_specs=[pl.BlockSpec((N_DEV, tr, c), lambda i: (0, i, 0))],
        out_specs=pl.BlockSpec((tr, c), lambda i: (i, 0)),
        out_shape=jax.ShapeDtypeStruct((r, c), F32),
        compiler_params=_params("parallel"),
    )(slots)


def _adamw(name, w, g, m, v, tr):
    r, c = w.shape

    def body(w_ref, g_ref, m_ref, v_ref, d_ref, mo_ref, vo_ref):
        gv = g_ref[...]
        m_new = ADAM_B1 * m_ref[...] + (1.0 - ADAM_B1) * gv
        v_new = ADAM_B2 * v_ref[...] + (1.0 - ADAM_B2) * jnp.square(gv)
        m_hat = m_new / (1.0 - ADAM_B1 ** ADAM_STEP)
        v_hat = v_new / (1.0 - ADAM_B2 ** ADAM_STEP)
        d_ref[...] = -ADAM_LR * (m_hat / (jnp.sqrt(v_hat) + ADAM_EPS) + ADAM_WD * w_ref[...])
        mo_ref[...] = m_new
        vo_ref[...] = v_new

    spec = pl.BlockSpec((tr, c), lambda i: (i, 0))
    shape = jax.ShapeDtypeStruct((r, c), F32)
    return pl.pallas_call(
        body, name=name,
        grid=(r // tr,),
        in_specs=[spec] * 4, out_specs=[spec] * 3, out_shape=[shape] * 3,
        compiler_params=_params("parallel"),
    )(w, g, m, v)


def _adamw_nd(name, w, g, m, v, tr=None):
    shape = w.shape
    c = shape[-1]
    flat = [a.reshape(-1, c) for a in (w, g, m, v)]
    rows = flat[0].shape[0]
    outs = _adamw(name, *flat, rows if tr is None else tr)
    return [o.reshape(shape) for o in outs]


SMALL_ROWS = (64, 8, 8, 8, 8192, 64)


def _pack_rows(a, rows):
    flat = a.reshape(-1)
    flat = jnp.pad(flat, (0, rows * BLK - flat.shape[0]))
    return flat.reshape(rows, BLK)


def kernel(x, norm_g, w_in, q_norm, k_norm, sinks, w_s, b_s, w_out, loss_target, m_norm_g, m_w_in, m_q_norm, m_k_norm, m_sinks, m_w_s, m_b_s, m_w_out, v_norm_g, v_w_in, v_q_norm, v_k_norm, v_sinks, v_w_s, v_b_s, v_w_out):
    xs = x[0]
    tgt = loss_target[0]
    slopes = jnp.asarray(2.0 ** (-8.0 * np.arange(1, N_HEADS + 1) / N_HEADS), dtype=F32)

    wt_full, wo_full = _all_gather_weights(jnp.swapaxes(w_in, 1, 2).astype(BF16), w_out.astype(BF16))

    layer_par = []
    for l in range(DEPTH):
        layer_par.append((jnp.tile(q_norm[l], 2)[None, :], jnp.tile(k_norm[l], 2)[None, :],
                          jnp.repeat(b_s[l].T, HALF, axis=1)))

    saved = []
    cur = xs
    for l in range(DEPTH):
        qw, kw, bmap = layer_par[l]
        h, proj = _norm_proj(cur, norm_g[l][None, :], wt_full[l])
        mix = _mixer_fwd(proj, qw, kw, sinks[l], slopes, w_s[l], bmap)
        nxt = _out_proj(mix, wo_full[l], cur)
        saved.append((cur, h, proj, mix))
        cur = nxt

    dx, sq = _loss_grad(cur, tgt)
    loss = lax.psum(0.5 * jnp.sum(sq) / D_MODEL, MESH_AXES)

    g_norm, g_qn, g_kn, g_sk, g_ws, g_bs = ([None] * DEPTH for _ in range(6))
    g_wt, g_wo = [None] * DEPTH, [None] * DEPTH
    for l in reversed(range(DEPTH)):
        x_l, h, proj, mix = saved[l]
        qw, kw, bmap = layer_par[l]
        dmix = _dmix(dx, wo_full[l])
        dwo_part = _dw_out(mix, dx)
        dproj, dqw, dkw, dsk, dws, dbs = _mixer_bwd(proj, dmix, qw, kw, sinks[l], slopes, w_s[l], bmap)
        dx, dng = _dh_norm_bwd(dproj, wt_full[l], x_l, dx, norm_g[l][None, :])
        dwt_part = _dw_in(dproj, h)
        wt_slots, wo_slots = _scatter_partials(dwt_part, dwo_part)
        g_wt[l] = _sum_slots(wt_slots, W_IN_SHARD // 2)
        g_wo[l] = _sum_slots(wo_slots, W_OUT_SHARD)
        g_norm[l], g_qn[l], g_kn[l], g_sk[l], g_ws[l], g_bs[l] = (
            dng[0], dqw[0, :HALF], dkw[0, :HALF], dsk[0, :N_HEADS], dws, dbs)

    small = [jnp.stack(g) for g in (g_norm, g_qn, g_kn, g_sk, g_ws, g_bs)]
    packed = jnp.concatenate([_pack_rows(a, r) for a, r in zip(small, SMALL_ROWS)], axis=0)
    total = _sum_slots(_gather_small(packed), packed.shape[0] // 7)
    grads_small = []
    row = 0
    for a, r in zip(small, SMALL_ROWS):
        grads_small.append(total[row:row + r].reshape(-1)[:a.size].reshape(a.shape))
        row += r
    gr_norm, gr_qn, gr_kn, gr_sk, gr_ws, gr_bs = grads_small
    gr_w_in = jnp.swapaxes(jnp.stack(g_wt), 1, 2)
    gr_w_out = jnp.stack(g_wo)

    grads = [gr_norm, gr_w_in, gr_qn, gr_kn, gr_sk, gr_ws, gr_bs, gr_w_out]
    weights = [norm_g, w_in, q_norm, k_norm, sinks, w_s, b_s, w_out]
    moms = [m_norm_g, m_w_in, m_q_norm, m_k_norm, m_sinks, m_w_s, m_b_s, m_w_out]
    vels = [v_norm_g, v_w_in, v_q_norm, v_k_norm, v_sinks, v_w_s, v_b_s, v_w_out]
    tiles = [None, 512, None, None, None, 1024, None, 256]
    names = ["norm_g", "w_in", "q_norm", "k_norm", "sinks", "w_s", "b_s", "w_out"]
    deltas, new_m, new_v = [], [], []
    for nm, w, g, m, v, tr in zip(names, weights, grads, moms, vels, tiles):
        d, mo, vo = _adamw_nd("adamw_" + nm, w, g, m, v, tr)
        deltas.append(d)
        new_m.append(mo)
        new_v.append(vo)

    return (loss, dx[None], *grads, *deltas, *new_m, *new_v)
```

```python
import numpy as np
import jax
import jax.numpy as jnp
from jax import lax
from jax.experimental import pallas as pl
from jax.experimental.pallas import tpu as pltpu

F32 = jnp.float32
BF16 = jnp.bfloat16

D_MODEL = 2048
D_ATTN = 1024
D_IN = 5632
N_HEADS = 16
DEPTH = 4
BLK = 128
HALF = 64
RMS_EPS = 1e-6
C_Q, C_K, C_V, C_GA, C_ZU, C_ZV, C_GB = 0, 1024, 1280, 1536, 2560, 3584, 4608
NEG = -1e30
N_DEV = 8
W_IN_SHARD = D_IN // N_DEV
W_OUT_SHARD = D_MODEL // N_DEV
INV_SQRT2 = 0.7071067811865476
INV_SQRT_2PI = 0.3989422804014327

TM_RESIDENT = 256
TM_STREAM = 512
TN_PROJ = 512
DW_IN_ROWS = D_IN // 4
DW_IN_TOK = 1024

ADAM_LR = 0.001
ADAM_B1 = 0.9
ADAM_B2 = 0.999
ADAM_EPS = 1e-08
ADAM_WD = 0.01
ADAM_STEP = 10

NT_DIMS = (((1,), (1,)), ((), ()))
TN_DIMS = (((0,), (0,)), ((), ()))
MESH_AXES = ("x", "y", "c")


def _sigmoid(v):
    return 1.0 / (1.0 + jnp.exp(-v))


def _gelu_cdf(z):
    return 0.5 * (1.0 + lax.erf(z * INV_SQRT2))


def _gelu_grad(z, cdf):
    return cdf + z * (jnp.exp(-0.5 * z * z) * INV_SQRT_2PI)


def _lane_halves(rows):
    lane = lax.broadcasted_iota(jnp.int32, (rows, BLK), 1)
    return lane < HALF, lane >= HALF


def _half_sum(v, h_a, h_b):
    s_a = jnp.sum(jnp.where(h_a, v, 0.0), axis=-1, keepdims=True)
    s_b = jnp.sum(jnp.where(h_b, v, 0.0), axis=-1, keepdims=True)
    return jnp.where(h_a, s_a, s_b)


def _half_rms(v, w, h_a, h_b):
    r = lax.rsqrt(_half_sum(v * v, h_a, h_b) * (1.0 / HALF) + RMS_EPS)
    yhat = v * r
    return yhat, r, yhat * w


def _half_rms_bwd(dy, yhat, r, w, h_a, h_b):
    dyh = dy * w
    c = _half_sum(dyh * yhat, h_a, h_b) * (1.0 / HALF)
    return r * (dyh - yhat * c)


def _band_mask(n):
    t = lax.broadcasted_iota(jnp.int32, (BLK, 2 * BLK), 0)
    kk = lax.broadcasted_iota(jnp.int32, (BLK, 2 * BLK), 1)
    dist = t + BLK - kk
    first_key = jnp.where(n > 0, 0, BLK)
    ok = (dist >= 0) & (dist < BLK) & (kk >= first_key)
    return ok, dist.astype(F32)


def _softmax_sink(s_raw, slope, sink, ok, distf):
    s = s_raw * 0.125 - slope * distf
    s = jnp.where(ok, s, NEG)
    m = jnp.maximum(jnp.max(s, axis=-1, keepdims=True), sink)
    p = jnp.exp(s - m)
    es = jnp.exp(sink - m)
    inv = 1.0 / (jnp.sum(p, axis=-1, keepdims=True) + es)
    return p * inv, es * inv


def _rows(i):
    return slice(BLK * i, BLK * (i + 1))


def _cols(base, j):
    return slice(base + BLK * j, base + BLK * (j + 1))


def _to_half(v, have, want):
    return v if have == want else pltpu.roll(v, HALF, 1)


def _tril_mask():
    row = lax.broadcasted_iota(jnp.int32, (BLK, BLK), 0)
    col = lax.broadcasted_iota(jnp.int32, (BLK, BLK), 1)
    return row >= col


def _tile(limit, s):
    t = min(limit, s)
    assert s % t == 0, (s, t)
    return t


def _mesh_place():
    x, y, c = lax.axis_index("x"), lax.axis_index("y"), lax.axis_index("c")
    return x, y, c, 4 * x + 2 * y + c


def _peer(x, y, c, k):
    px = 1 - x if k & 4 else x
    py = 1 - y if k & 2 else y
    pc = 1 - c if k & 1 else c
    return (px, py, pc), 4 * px + 2 * py + pc


class _Pushes:
    def __init__(self, n_arrays, src_view, dst_view):
        self.na = n_arrays
        self.src_view = src_view
        self.dst_view = dst_view

    def scratch(self):
        n = self.na * (N_DEV - 1)
        return [pltpu.SemaphoreType.DMA((n,)), pltpu.SemaphoreType.DMA((n,)), pltpu.SemaphoreType.DMA((self.na,))]

    def copies(self, src_refs, dst_refs, send_sems, recv_sems, local_sems):
        x, y, c, me = _mesh_place()
        cps = []
        for a in range(self.na):
            cps.append(pltpu.make_async_copy(self.src_view(a, src_refs[a], me), self.dst_view(a, dst_refs[a], me),
                                             local_sems.at[a]))
        for k in range(1, N_DEV):
            peer, pidx = _peer(x, y, c, k)
            for a in range(self.na):
                sem = a * (N_DEV - 1) + k - 1
                cps.append(pltpu.make_async_remote_copy(
                    src_ref=self.src_view(a, src_refs[a], pidx), dst_ref=self.dst_view(a, dst_refs[a], me),
                    send_sem=send_sems.at[sem], recv_sem=recv_sems.at[sem],
                    device_id=peer, device_id_type=pl.DeviceIdType.MESH))
        return cps


def _call(name, compute, grid, in_specs, out_specs, out_shape, scratch, args, push=None):
    sem = pltpu.CompilerParams(dimension_semantics=("arbitrary",) * len(grid))
    if push is None:
        return pl.pallas_call(compute, name=name, grid=grid, in_specs=in_specs, out_specs=out_specs,
                              out_shape=out_shape, scratch_shapes=scratch, compiler_params=sem)(*args)
    pushes, srcs, xshapes = push
    n_in, n_out, n_scr, na = len(args), len(out_shape), len(scratch), pushes.na
    hbm = pl.BlockSpec(memory_space=pltpu.HBM)

    def body(*refs):
        ins, refs = refs[:n_in], refs[n_in:]
        xin, refs = refs[:na], refs[na:]
        outs, refs = refs[:n_out], refs[n_out:]
        xout, refs = refs[:na], refs[na:]
        scr, sems = refs[:n_scr], refs[n_scr:]
        cps = pushes.copies(xin, xout, *sems)
        first = pl.program_id(0) == 0
        last = pl.program_id(0) == grid[0] - 1
        for d in range(1, len(grid)):
            first = first & (pl.program_id(d) == 0)
            last = last & (pl.program_id(d) == grid[d] - 1)

        @pl.when(first)
        def _():
            for cp in cps:
                cp.start()

        compute(*ins, *outs, *scr)

        @pl.when(last)
        def _():
            for cp in cps:
                cp.wait()

    return pl.pallas_call(
        body, name=name, grid=grid,
        in_specs=list(in_specs) + [hbm] * na, out_specs=list(out_specs) + [hbm] * na,
        out_shape=list(out_shape) + list(xshapes),
        scratch_shapes=list(scratch) + pushes.scratch(), compiler_params=sem)(*args, *srcs)


def _exchange(name, pushes, srcs, out_shapes):
    na = pushes.na
    hbm = pl.BlockSpec(memory_space=pltpu.HBM)

    def body(*refs):
        cps = pushes.copies(refs[:na], refs[na:2 * na], *refs[2 * na:])
        for cp in cps:
            cp.start()
        for cp in cps:
            cp.wait()

    return pl.pallas_call(body, name=name, in_specs=[hbm] * na, out_specs=[hbm] * na, out_shape=out_shapes,
                          scratch_shapes=pushes.scratch())(*srcs)


def _gather_rows(shard_rows, layer):
    def src_view(a, ref, idx):
        return ref.at[layer]

    def dst_view(a, ref, idx):
        r = shard_rows[a]
        return ref.at[pl.ds(pl.multiple_of(idx * r, 64), r), :]

    return _Pushes(len(shard_rows), src_view, dst_view)


def _scatter_rows(shard_rows):
    def src_view(a, ref, idx):
        r = shard_rows[a]
        return ref.at[pl.ds(pl.multiple_of(idx * r, 64), r), :]

    def dst_view(a, ref, idx):
        return ref.at[idx]

    return _Pushes(len(shard_rows), src_view, dst_view)


def _gather_slots():
    return _Pushes(1, lambda a, ref, idx: ref, lambda a, ref, idx: ref.at[idx])


W_FULL = {W_IN_SHARD: jax.ShapeDtypeStruct((D_IN, D_MODEL), BF16),
          W_OUT_SHARD: jax.ShapeDtypeStruct((D_MODEL, D_MODEL), BF16)}


def _slots_shape(rows, cols, dtype):
    return jax.ShapeDtypeStruct((N_DEV, rows, cols), dtype)


def _sum_slots(slots, tr):
    _, r, c = slots.shape

    def compute(s_ref, o_ref):
        tot = s_ref[0].astype(F32)
        for d in range(1, N_DEV):
            tot = tot + s_ref[d].astype(F32)
        o_ref[...] = tot

    return _call("sum_slots", compute, (r // tr,),
                 [pl.BlockSpec((N_DEV, tr, c), lambda i: (0, i, 0))], [pl.BlockSpec((tr, c), lambda i: (i, 0))],
                 [jax.ShapeDtypeStruct((r, c), F32)], [], [slots])[0]


def _resident(shape):
    return pl.BlockSpec(shape, lambda *_: (0,) * len(shape), pipeline_mode=pl.Buffered(1))


def _norm_proj(x, g, w_t, push=None):
    s = x.shape[0]
    tm = _tile(TM_RESIDENT, s)

    def compute(x_ref, g_ref, w_ref, h_ref, p_ref):
        xf = x_ref[...]
        r = lax.rsqrt(jnp.mean(xf * xf, axis=-1, keepdims=True) + RMS_EPS)
        h = ((xf * r) * g_ref[...]).astype(BF16)
        h_ref[...] = h
        for j in range(D_IN // TN_PROJ):
            cols = slice(j * TN_PROJ, (j + 1) * TN_PROJ)
            p_ref[:, cols] = lax.dot_general(h, w_ref[cols, :], NT_DIMS, preferred_element_type=F32)

    return _call(
        "norm_proj", compute, (s // tm,),
        [pl.BlockSpec((tm, D_MODEL), lambda i: (i, 0)), pl.BlockSpec((1, D_MODEL), lambda i: (0, 0)),
         _resident((D_IN, D_MODEL))],
        [pl.BlockSpec((tm, D_MODEL), lambda i: (i, 0)), pl.BlockSpec((tm, D_IN), lambda i: (i, 0))],
        [jax.ShapeDtypeStruct((s, D_MODEL), BF16), jax.ShapeDtypeStruct((s, D_IN), F32)],
        [], [x, g, w_t], push)


def _mixer_fwd(proj, qw, kw, sinks, slopes, w_s, bmap, push=None):
    s = proj.shape[0]
    nb = s // BLK

    def compute(sinks_ref, slopes_ref, p_ref, kvp_ref, qw_ref, kw_ref, ws_ref, bmap_ref, mix_ref,
                qs_scr, pb_scr):
        n = pl.program_id(0)
        h_a, h_b = _lane_halves(BLK)
        h_a2, h_b2 = _lane_halves(2 * BLK)
        halves = (h_a, h_b)
        ok, distf = _band_mask(n)
        qw_v = qw_ref[...]
        kw_v = kw_ref[...]
        for kt in range(2):
            kband = jnp.concatenate([kvp_ref[:, _cols(0, kt)], p_ref[:, _cols(C_K, kt)]], axis=0)
            _, _, kn = _half_rms(kband, kw_v, h_a2, h_b2)
            knb = kn.astype(BF16)
            vband = jnp.concatenate([kvp_ref[:, _cols(256, kt)], p_ref[:, _cols(C_V, kt)]], axis=0).astype(BF16)
            for gh in range(2):
                g = 2 * kt + gh
                for t in range(2):
                    _, _, qn = _half_rms(p_ref[:, _cols(C_Q, 2 * g + t)], qw_v, h_a, h_b)
                    for qh in range(2):
                        qm = _to_half(jnp.where(halves[qh], qn, 0.0), qh, gh)
                        qs_scr[_rows(2 * t + qh), :] = qm.astype(BF16)
                sc = lax.dot_general(qs_scr[...], knb, NT_DIMS, preferred_element_type=F32)
                for i in range(4):
                    hd = 4 * g + i
                    p, _ = _softmax_sink(sc[_rows(i)], slopes_ref[hd], sinks_ref[hd], ok, distf)
                    pb_scr[_rows(i), :] = p.astype(BF16)
                o = jnp.dot(pb_scr[...], vband, preferred_element_type=F32)
                for t in range(2):
                    j = 2 * g + t
                    out = jnp.where(h_a, _to_half(o[_rows(2 * t)], gh, 0), _to_half(o[_rows(2 * t + 1)], gh, 1))
                    ga = p_ref[:, _cols(C_GA, j)]
                    mix_ref[:, _cols(0, j)] = (out * (ga * _sigmoid(ga))).astype(BF16)

        tril = _tril_mask()
        for j in range(8):
            zu_pre = p_ref[:, _cols(C_ZU, j)]
            zv_pre = p_ref[:, _cols(C_ZV, j)]
            gb = p_ref[:, _cols(C_GB, j)]
            zu = zu_pre * _gelu_cdf(zu_pre)
            zvb = (zv_pre * _gelu_cdf(zv_pre)).astype(BF16)
            w_a = jnp.where(tril, ws_ref[2 * j], 0.0).astype(BF16)
            w_b = jnp.where(tril, ws_ref[2 * j + 1], 0.0).astype(BF16)
            mixed = jnp.where(h_a, jnp.dot(w_a, zvb, preferred_element_type=F32),
                              jnp.dot(w_b, zvb, preferred_element_type=F32)) + bmap_ref[:, _cols(0, j)]
            mix_ref[:, _cols(D_ATTN, j)] = ((zu * mixed) * (gb * _sigmoid(gb))).astype(BF16)

    smem = pl.BlockSpec(memory_space=pltpu.SMEM)
    return _call(
        "mixer_fwd", compute, (nb,),
        [smem, smem,
         pl.BlockSpec((BLK, D_IN), lambda n: (n, 0)),
         pl.BlockSpec((BLK, 512), lambda n: (jnp.maximum(n - 1, 0), 2)),
         pl.BlockSpec((1, BLK), lambda n: (0, 0)),
         pl.BlockSpec((1, BLK), lambda n: (0, 0)),
         pl.BlockSpec((N_HEADS, BLK, BLK), lambda n: (0, 0, 0)),
         pl.BlockSpec((BLK, D_ATTN), lambda n: (0, 0))],
        [pl.BlockSpec((BLK, D_MODEL), lambda n: (n, 0))],
        [jax.ShapeDtypeStruct((s, D_MODEL), BF16)],
        [pltpu.VMEM((4 * BLK, BLK), BF16), pltpu.VMEM((4 * BLK, 2 * BLK), BF16)],
        [sinks, slopes, proj, proj, qw, kw, w_s, bmap], push)


def _out_proj(mix, w_o, x):
    s = x.shape[0]
    tm = _tile(TM_STREAM, s)

    def compute(m_ref, w_ref, x_ref, o_ref):
        o_ref[...] = x_ref[...] + jnp.dot(m_ref[...], w_ref[...], preferred_element_type=F32)

    return _call(
        "out_proj", compute, (s // tm,),
        [pl.BlockSpec((tm, D_MODEL), lambda i: (i, 0)), _resident((D_MODEL, D_MODEL)),
         pl.BlockSpec((tm, D_MODEL), lambda i: (i, 0))],
        [pl.BlockSpec((tm, D_MODEL), lambda i: (i, 0))],
        [jax.ShapeDtypeStruct((s, D_MODEL), F32)], [], [mix, w_o, x])[0]


def _loss_grad(y, tgt):
    s = y.shape[0]
    tm = _tile(TM_STREAM, s)

    def compute(y_ref, t_ref, dy_ref, sq_ref):
        @pl.when(pl.program_id(0) == 0)
        def _():
            sq_ref[...] = jnp.zeros_like(sq_ref)

        e = y_ref[...] - t_ref[...]
        dy_ref[...] = e * (1.0 / D_MODEL)
        sq_ref[...] += jnp.sum(e * e, axis=0, keepdims=True)

    return _call(
        "loss_grad", compute, (s // tm,),
        [pl.BlockSpec((tm, D_MODEL), lambda i: (i, 0)), pl.BlockSpec((tm, D_MODEL), lambda i: (i, 0))],
        [pl.BlockSpec((tm, D_MODEL), lambda i: (i, 0)), pl.BlockSpec((1, D_MODEL), lambda i: (0, 0))],
        [jax.ShapeDtypeStruct((s, D_MODEL), F32), jax.ShapeDtypeStruct((1, D_MODEL), F32)], [], [y, tgt])


def _dmix(dx, w_o):
    s = dx.shape[0]
    tm = _tile(TM_STREAM, s)

    def compute(d_ref, w_ref, o_ref):
        o_ref[...] = lax.dot_general(d_ref[...].astype(BF16), w_ref[...], NT_DIMS, preferred_element_type=F32)

    return _call(
        "dmix", compute, (s // tm,),
        [pl.BlockSpec((tm, D_MODEL), lambda i: (i, 0)), _resident((D_MODEL, D_MODEL))],
        [pl.BlockSpec((tm, D_MODEL), lambda i: (i, 0))],
        [jax.ShapeDtypeStruct((s, D_MODEL), F32)], [], [dx, w_o])[0]


def _dw_out(mix, dx):
    s = dx.shape[0]
    tk = _tile(TM_STREAM, s)
    nk = s // tk

    def compute(m_ref, d_ref, o_ref, acc):
        k = pl.program_id(0)

        @pl.when(k == 0)
        def _():
            acc[...] = jnp.zeros_like(acc)

        acc[...] += lax.dot_general(m_ref[...], d_ref[...].astype(BF16), TN_DIMS, preferred_element_type=F32)

        @pl.when(k == nk - 1)
        def _():
            o_ref[...] = acc[...].astype(BF16)

    return _call(
        "dw_out", compute, (nk,),
        [pl.BlockSpec((tk, D_MODEL), lambda k: (k, 0)), pl.BlockSpec((tk, D_MODEL), lambda k: (k, 0))],
        [pl.BlockSpec((D_MODEL, D_MODEL), lambda k: (0, 0))],
        [jax.ShapeDtypeStruct((D_MODEL, D_MODEL), BF16)],
        [pltpu.VMEM((D_MODEL, D_MODEL), F32)], [mix, dx])[0]


def _mixer_bwd(proj, dmix, qw, kw, sinks, slopes, w_s, bmap, push=None):
    s = proj.shape[0]
    nb = s // BLK

    def compute(sinks_ref, slopes_ref, p_ref, kvp_ref, dm_ref, qw_ref, kw_ref, ws_ref, bmap_ref,
                dp_ref, dqw_ref, dkw_ref, dsk_ref, dws_ref, dbs_ref,
                pend, accq, acck, accs, accb, qs_scr, pb_scr, dos_scr, dqk_scr):
        n = pl.program_id(0)
        h_a, h_b = _lane_halves(BLK)
        h_a2, h_b2 = _lane_halves(2 * BLK)
        halves = (h_a, h_b)
        lane = lax.broadcasted_iota(jnp.int32, (BLK, BLK), 1)
        tril = _tril_mask()

        @pl.when(n == 0)
        def _():
            accq[...] = jnp.zeros_like(accq)
            acck[...] = jnp.zeros_like(acck)
            accs[...] = jnp.zeros_like(accs)
            accb[...] = jnp.zeros_like(accb)
            dws_ref[...] = jnp.zeros_like(dws_ref)

        @pl.when(n >= 1)
        def _():
            dp_ref[:, 0:C_K] = pend[:, 0:C_K].astype(BF16)
            dp_ref[:, C_GA:D_IN] = pend[:, C_GA:D_IN].astype(BF16)

        @pl.when(n < nb)
        def _():
            ok, distf = _band_mask(n)
            qw_v = qw_ref[...]
            kw_v = kw_ref[...]
            for kt in range(2):
                kband = jnp.concatenate([kvp_ref[:, _cols(0, kt)], p_ref[:, _cols(C_K, kt)]], axis=0)
                kyhat, kr, kn = _half_rms(kband, kw_v, h_a2, h_b2)
                knb = kn.astype(BF16)
                vband = jnp.concatenate([kvp_ref[:, _cols(256, kt)], p_ref[:, _cols(C_V, kt)]],
                                        axis=0).astype(BF16)
                dkn = jnp.zeros((2 * BLK, BLK), F32)
                dvb = jnp.zeros((2 * BLK, BLK), F32)
                for gh in range(2):
                    g = 2 * kt + gh
                    tiles = []
                    for t in range(2):
                        j = 2 * g + t
                        qyhat, qr, qn = _half_rms(p_ref[:, _cols(C_Q, j)], qw_v, h_a, h_b)
                        ga = p_ref[:, _cols(C_GA, j)]
                        sg = _sigmoid(ga)
                        dma = dm_ref[:, _cols(0, j)]
                        d_o = dma * (ga * sg)
                        tiles.append((qyhat, qr, ga, sg, dma))
                        for qh in range(2):
                            qm = _to_half(jnp.where(halves[qh], qn, 0.0), qh, gh)
                            qs_scr[_rows(2 * t + qh), :] = qm.astype(BF16)
                            dom = _to_half(jnp.where(halves[qh], d_o, 0.0), qh, gh)
                            dos_scr[_rows(2 * t + qh), :] = dom.astype(BF16)
                    sc = lax.dot_general(qs_scr[...], knb, NT_DIMS, preferred_element_type=F32)
                    dpm = lax.dot_general(dos_scr[...], vband, NT_DIMS, preferred_element_type=F32)
                    for i in range(4):
                        hd = 4 * g + i
                        p, p_sink = _softmax_sink(sc[_rows(i)], slopes_ref[hd], sinks_ref[hd], ok, distf)
                        dp_i = dpm[_rows(i)]
                        dsum = jnp.sum(p * dp_i, axis=-1, keepdims=True)
                        ds = p * (dp_i - dsum)
                        accs[...] += jnp.where(lane == hd, -(p_sink * dsum), 0.0)
                        pb_scr[_rows(i), :] = p.astype(BF16)
                        dqk_scr[_rows(i), :] = (ds * 0.125).astype(BF16)
                    o = jnp.dot(pb_scr[...], vband, preferred_element_type=F32)
                    dvb = dvb + lax.dot_general(pb_scr[...], dos_scr[...], TN_DIMS, preferred_element_type=F32)
                    dqn_all = jnp.dot(dqk_scr[...], knb, preferred_element_type=F32)
                    dkn = dkn + lax.dot_general(dqk_scr[...], qs_scr[...], TN_DIMS, preferred_element_type=F32)
                    for t in range(2):
                        j = 2 * g + t
                        qyhat, qr, ga, sg, dma = tiles[t]
                        out = jnp.where(h_a, _to_half(o[_rows(2 * t)], gh, 0), _to_half(o[_rows(2 * t + 1)], gh, 1))
                        dqn = jnp.where(h_a, _to_half(dqn_all[_rows(2 * t)], gh, 0),
                                        _to_half(dqn_all[_rows(2 * t + 1)], gh, 1))
                        pend[:, _cols(C_GA, j)] = (dma * out) * (sg * (1.0 + ga * (1.0 - sg)))
                        pend[:, _cols(C_Q, j)] = _half_rms_bwd(dqn, qyhat, qr, qw_v, h_a, h_b)
                        accq[...] += dqn * qyhat
                dk = _half_rms_bwd(dkn, kyhat, kr, kw_v, h_a2, h_b2)
                dkw_part = dkn * kyhat
                acck[...] += dkw_part[0:BLK] + dkw_part[BLK:2 * BLK]

                @pl.when(n >= 1)
                def _():
                    dp_ref[:, _cols(C_K, kt)] = (pend[:, _cols(C_K, kt)] + dk[0:BLK]).astype(BF16)
                    dp_ref[:, _cols(C_V, kt)] = (pend[:, _cols(C_V, kt)] + dvb[0:BLK]).astype(BF16)

                pend[:, _cols(C_K, kt)] = dk[BLK:2 * BLK]
                pend[:, _cols(C_V, kt)] = dvb[BLK:2 * BLK]

            for j in range(8):
                zu_pre = p_ref[:, _cols(C_ZU, j)]
                zv_pre = p_ref[:, _cols(C_ZV, j)]
                gb = p_ref[:, _cols(C_GB, j)]
                cu = _gelu_cdf(zu_pre)
                cv = _gelu_cdf(zv_pre)
                zu = zu_pre * cu
                zvb = (zv_pre * cv).astype(BF16)
                sgb = _sigmoid(gb)
                silub = gb * sgb
                w_a = jnp.where(tril, ws_ref[2 * j], 0.0).astype(BF16)
                w_b = jnp.where(tril, ws_ref[2 * j + 1], 0.0).astype(BF16)
                mixed = jnp.where(h_a, jnp.dot(w_a, zvb, preferred_element_type=F32),
                                  jnp.dot(w_b, zvb, preferred_element_type=F32)) + bmap_ref[:, _cols(0, j)]
                dmb = dm_ref[:, _cols(D_ATTN, j)]
                dgate = dmb * silub
                pend[:, _cols(C_ZU, j)] = (dgate * mixed) * _gelu_grad(zu_pre, cu)
                pend[:, _cols(C_GB, j)] = (dmb * (zu * mixed)) * (sgb * (1.0 + gb * (1.0 - sgb)))
                dmixed = dgate * zu
                accb[:, _cols(0, j)] += dmixed
                dm_a = jnp.where(h_a, dmixed, 0.0).astype(BF16)
                dm_b = jnp.where(h_b, dmixed, 0.0).astype(BF16)
                dzv = (lax.dot_general(w_a, dm_a, TN_DIMS, preferred_element_type=F32)
                       + lax.dot_general(w_b, dm_b, TN_DIMS, preferred_element_type=F32))
                pend[:, _cols(C_ZV, j)] = dzv * _gelu_grad(zv_pre, cv)
                dws_ref[2 * j] += lax.dot_general(dm_a, zvb, NT_DIMS, preferred_element_type=F32)
                dws_ref[2 * j + 1] += lax.dot_general(dm_b, zvb, NT_DIMS, preferred_element_type=F32)

        @pl.when(n == nb)
        def _():
            dp_ref[:, C_K:C_GA] = pend[:, C_K:C_GA].astype(BF16)
            aq = accq[...]
            ak = acck[...]
            dqw_ref[...] = jnp.sum(aq + pltpu.roll(aq, HALF, 1), axis=0, keepdims=True)
            dkw_ref[...] = jnp.sum(ak + pltpu.roll(ak, HALF, 1), axis=0, keepdims=True)
            dsk_ref[...] = jnp.sum(accs[...], axis=0, keepdims=True)
            for hd in range(N_HEADS):
                dws_ref[hd] = jnp.where(tril, dws_ref[hd], 0.0)
            hrow = lax.broadcasted_iota(jnp.int32, (N_HEADS, D_ATTN), 0)
            hcol = lax.broadcasted_iota(jnp.int32, (N_HEADS, D_ATTN), 1)
            sel = jnp.where((hcol >= hrow * HALF) & (hcol < (hrow + 1) * HALF), 1.0, 0.0).astype(BF16)
            rem = accb[...]
            tot = jnp.zeros((N_HEADS, BLK), F32)
            for _ in range(3):
                part = rem.astype(BF16)
                tot = tot + lax.dot_general(sel, part, NT_DIMS, preferred_element_type=F32)
                rem = rem - part.astype(F32)
            dbs_ref[...] = tot

    smem = pl.BlockSpec(memory_space=pltpu.SMEM)
    last = nb - 1
    return _call(
        "mixer_bwd", compute, (nb + 1,),
        [smem, smem,
         pl.BlockSpec((BLK, D_IN), lambda n: (jnp.minimum(n, last), 0)),
         pl.BlockSpec((BLK, 512), lambda n: (jnp.maximum(jnp.minimum(n, last) - 1, 0), 2)),
         pl.BlockSpec((BLK, D_MODEL), lambda n: (jnp.minimum(n, last), 0)),
         pl.BlockSpec((1, BLK), lambda n: (0, 0)),
         pl.BlockSpec((1, BLK), lambda n: (0, 0)),
         pl.BlockSpec((N_HEADS, BLK, BLK), lambda n: (0, 0, 0)),
         pl.BlockSpec((BLK, D_ATTN), lambda n: (0, 0))],
        [pl.BlockSpec((BLK, D_IN), lambda n: (jnp.maximum(n - 1, 0), 0)),
         pl.BlockSpec((1, BLK), lambda n: (0, 0)),
         pl.BlockSpec((1, BLK), lambda n: (0, 0)),
         pl.BlockSpec((1, BLK), lambda n: (0, 0)),
         pl.BlockSpec((N_HEADS, BLK, BLK), lambda n: (0, 0, 0)),
         pl.BlockSpec((N_HEADS, BLK), lambda n: (0, 0))],
        [jax.ShapeDtypeStruct((s, D_IN), BF16),
         jax.ShapeDtypeStruct((1, BLK), F32),
         jax.ShapeDtypeStruct((1, BLK), F32),
         jax.ShapeDtypeStruct((1, BLK), F32),
         jax.ShapeDtypeStruct((N_HEADS, BLK, BLK), F32),
         jax.ShapeDtypeStruct((N_HEADS, BLK), F32)],
        [pltpu.VMEM((BLK, D_IN), F32),
         pltpu.VMEM((BLK, BLK), F32), pltpu.VMEM((BLK, BLK), F32), pltpu.VMEM((BLK, BLK), F32),
         pltpu.VMEM((BLK, D_ATTN), F32),
         pltpu.VMEM((4 * BLK, BLK), BF16), pltpu.VMEM((4 * BLK, 2 * BLK), BF16),
         pltpu.VMEM((4 * BLK, BLK), BF16), pltpu.VMEM((4 * BLK, 2 * BLK), BF16)],
        [sinks, slopes, proj, proj, dmix, qw, kw, w_s, bmap], push)


def _dh_norm_bwd(dproj, w_t, x, dx_out, g, push=None):
    s = x.shape[0]
    tm = _tile(TM_RESIDENT, s)

    def compute(dp_ref, w_ref, x_ref, dxo_ref, g_ref, dx_ref, dg_ref):
        @pl.when(pl.program_id(0) == 0)
        def _():
            dg_ref[...] = jnp.zeros_like(dg_ref)

        dh = jnp.dot(dp_ref[...], w_ref[...], preferred_element_type=F32)
        xf = x_ref[...]
        r = lax.rsqrt(jnp.mean(xf * xf, axis=-1, keepdims=True) + RMS_EPS)
        yhat = xf * r
        dyh = dh * g_ref[...]
        c = jnp.mean(dyh * yhat, axis=-1, keepdims=True)
        dx_ref[...] = dxo_ref[...] + r * (dyh - yhat * c)
        dg_ref[...] += jnp.sum(dh * yhat, axis=0, keepdims=True)

    return _call(
        "dh_norm_bwd", compute, (s // tm,),
        [pl.BlockSpec((tm, D_IN), lambda i: (i, 0)), _resident((D_IN, D_MODEL)),
         pl.BlockSpec((tm, D_MODEL), lambda i: (i, 0)), pl.BlockSpec((tm, D_MODEL), lambda i: (i, 0)),
         pl.BlockSpec((1, D_MODEL), lambda i: (0, 0))],
        [pl.BlockSpec((tm, D_MODEL), lambda i: (i, 0)), pl.BlockSpec((1, D_MODEL), lambda i: (0, 0))],
        [jax.ShapeDtypeStruct((s, D_MODEL), F32), jax.ShapeDtypeStruct((1, D_MODEL), F32)],
        [], [dproj, w_t, x, dx_out, g], push)


def _dw_in(dproj, h, push=None):
    s = h.shape[0]
    tk = _tile(DW_IN_TOK, s)
    nk = s // tk

    def compute(dp_ref, h_ref, o_ref, acc):
        k = pl.program_id(1)

        @pl.when(k == 0)
        def _():
            acc[...] = jnp.zeros_like(acc)

        acc[...] += lax.dot_general(dp_ref[...], h_ref[...], TN_DIMS, preferred_element_type=F32)

        @pl.when(k == nk - 1)
        def _():
            o_ref[...] = acc[...].astype(BF16)

    return _call(
        "dw_in", compute, (D_IN // DW_IN_ROWS, nk),
        [pl.BlockSpec((tk, DW_IN_ROWS), lambda j, k: (k, j)), pl.BlockSpec((tk, D_MODEL), lambda j, k: (k, 0))],
        [pl.BlockSpec((DW_IN_ROWS, D_MODEL), lambda j, k: (j, 0))],
        [jax.ShapeDtypeStruct((D_IN, D_MODEL), BF16)],
        [pltpu.VMEM((DW_IN_ROWS, D_MODEL), F32)], [dproj, h], push)


def _adamw(name, w, g, m, v, tr=None):
    shape = w.shape
    c = shape[-1]
    flat = [a.reshape(-1, c) for a in (w, g, m, v)]
    r = flat[0].shape[0]
    tr = r if tr is None else tr

    def compute(w_ref, g_ref, m_ref, v_ref, d_ref, mo_ref, vo_ref):
        gv = g_ref[...]
        m_new = ADAM_B1 * m_ref[...] + (1.0 - ADAM_B1) * gv
        v_new = ADAM_B2 * v_ref[...] + (1.0 - ADAM_B2) * jnp.square(gv)
        m_hat = m_new / (1.0 - ADAM_B1 ** ADAM_STEP)
        v_hat = v_new / (1.0 - ADAM_B2 ** ADAM_STEP)
        d_ref[...] = -ADAM_LR * (m_hat / (jnp.sqrt(v_hat) + ADAM_EPS) + ADAM_WD * w_ref[...])
        mo_ref[...] = m_new
        vo_ref[...] = v_new

    spec = pl.BlockSpec((tr, c), lambda i: (i, 0))
    outs = _call(name, compute, (r // tr,), [spec] * 4, [spec] * 3, [jax.ShapeDtypeStruct((r, c), F32)] * 3,
                 [], flat)
    return [o.reshape(shape) for o in outs]


def _pack_rows(parts):
    rows = []
    for a in parts:
        flat = a.reshape(-1)
        n = -(-flat.shape[0] // (8 * BLK)) * 8
        rows.append(jnp.pad(flat, (0, n * BLK - flat.shape[0])).reshape(n, BLK))
    return jnp.concatenate(rows, axis=0)


def _unpack_rows(packed, like):
    out = []
    row = 0
    for a in like:
        n = -(-a.size // (8 * BLK)) * 8
        out.append(packed[row:row + n].reshape(-1)[:a.size].reshape(a.shape))
        row += n
    return out


def kernel(x, norm_g, w_in, q_norm, k_norm, sinks, w_s, b_s, w_out, loss_target, m_norm_g, m_w_in, m_q_norm, m_k_norm, m_sinks, m_w_s, m_b_s, m_w_out, v_norm_g, v_w_in, v_q_norm, v_k_norm, v_sinks, v_w_s, v_b_s, v_w_out):
    xs = x[0]
    tgt = loss_target[0]
    slopes = jnp.asarray(2.0 ** (-8.0 * np.arange(1, N_HEADS + 1) / N_HEADS), dtype=F32)
    wt_sh = jnp.swapaxes(w_in, 1, 2).astype(BF16)
    wo_sh = w_out.astype(BF16)

    layer_par = []
    for l in range(DEPTH):
        layer_par.append((jnp.tile(q_norm[l], 2)[None, :], jnp.tile(k_norm[l], 2)[None, :],
                          jnp.repeat(b_s[l].T, HALF, axis=1)))

    wt_full, wo_full = _exchange("gather_w0", _gather_rows((W_IN_SHARD, W_OUT_SHARD), 0), [wt_sh, wo_sh],
                                 [W_FULL[W_IN_SHARD], W_FULL[W_OUT_SHARD]])
    saved = []
    cur = xs
    for l in range(DEPTH):
        qw, kw, bmap = layer_par[l]
        more = l + 1 < DEPTH
        res = _norm_proj(cur, norm_g[l][None, :], wt_full,
                         (_gather_rows((W_IN_SHARD,), l + 1), [wt_sh], [W_FULL[W_IN_SHARD]]) if more else None)
        h, proj = res[0], res[1]
        res2 = _mixer_fwd(proj, qw, kw, sinks[l], slopes, w_s[l], bmap,
                          (_gather_rows((W_OUT_SHARD,), l + 1), [wo_sh], [W_FULL[W_OUT_SHARD]]) if more else None)
        mix = res2[0]
        nxt = _out_proj(mix, wo_full, cur)
        saved.append((cur, h, proj, mix, wt_full, wo_full))
        cur = nxt
        if more:
            wt_full, wo_full = res[2], res2[1]

    dx, sq = _loss_grad(cur, tgt)
    loss = lax.psum(0.5 * jnp.sum(sq) / D_MODEL, MESH_AXES)

    g_wt, g_wo, g_small, g_norm = ([None] * DEPTH for _ in range(4))
    small_like = None
    for l in reversed(range(DEPTH)):
        x_l, h, proj, mix, wt_l, wo_l = saved[l]
        qw, kw, bmap = layer_par[l]
        dmix = _dmix(dx, wo_l)
        dwo_part = _dw_out(mix, dx)
        dproj, dqw, dkw, dsk, dws, dbs, wo_slots = _mixer_bwd(
            proj, dmix, qw, kw, sinks[l], slopes, w_s[l], bmap,
            (_scatter_rows((W_OUT_SHARD,)), [dwo_part], [_slots_shape(W_OUT_SHARD, D_MODEL, BF16)]))
        small_like = [dqw[0, :HALF], dkw[0, :HALF], dsk[0, :N_HEADS], dws, dbs]
        packed = _pack_rows(small_like)
        dwt_part, small_slots = _dw_in(dproj, h, (_gather_slots(), [packed], [_slots_shape(*packed.shape, F32)]))
        dx, dng, wt_slots = _dh_norm_bwd(
            dproj, wt_l, x_l, dx, norm_g[l][None, :],
            (_scatter_rows((W_IN_SHARD,)), [dwt_part], [_slots_shape(W_IN_SHARD, D_MODEL, BF16)]))
        g_wt[l] = _sum_slots(wt_slots, W_IN_SHARD // 2)
        g_wo[l] = _sum_slots(wo_slots, W_OUT_SHARD)
        g_small[l] = _unpack_rows(_sum_slots(small_slots, packed.shape[0]), small_like)
        g_norm[l] = dng[0]

    dng_all = _pack_rows([jnp.stack(g_norm)])
    dng_slots = _exchange("gather_dnorm", _gather_slots(), [dng_all], [_slots_shape(*dng_all.shape, F32)])[0]
    gr_norm = _unpack_rows(_sum_slots(dng_slots, dng_all.shape[0]), [norm_g])[0]
    gr_qn, gr_kn, gr_sk, gr_ws, gr_bs = (jnp.stack([g_small[l][i] for l in range(DEPTH)]) for i in range(5))
    gr_w_in = jnp.swapaxes(jnp.stack(g_wt), 1, 2)
    gr_w_out = jnp.stack(g_wo)

    grads = [gr_norm, gr_w_in, gr_qn, gr_kn, gr_sk, gr_ws, gr_bs, gr_w_out]
    weights = [norm_g, w_in, q_norm, k_norm, sinks, w_s, b_s, w_out]
    moms = [m_norm_g, m_w_in, m_q_norm, m_k_norm, m_sinks, m_w_s, m_b_s, m_w_out]
    vels = [v_norm_g, v_w_in, v_q_norm, v_k_norm, v_sinks, v_w_s, v_b_s, v_w_out]
    tiles = [None, 512, None, None, None, 1024, None, 256]
    names = ["norm_g", "w_in", "q_norm", "k_norm", "sinks", "w_s", "b_s", "w_out"]
    deltas, new_m, new_v = [], [], []
    for nm, w, g, m, v, tr in zip(names, weights, grads, moms, vels, tiles):
        d, mo, vo = _adamw("adamw_" + nm, w, g, m, v, tr)
        deltas.append(d)
        new_m.append(mo)
        new_v.append(vo)

    return (loss, dx[None], *grads, *deltas, *new_m, *new_v)
```

```python
import numpy as np
import jax
import jax.numpy as jnp
from jax import lax
from jax.experimental import pallas as pl
from jax.experimental.pallas import tpu as pltpu

F32 = jnp.float32
BF16 = jnp.bfloat16

D_MODEL = 2048
D_ATTN = 1024
D_IN = 5632
N_HEADS = 16
DEPTH = 4
BLK = 128
HALF = 64
RMS_EPS = 1e-6
C_Q, C_K, C_V, C_GA, C_ZU, C_ZV, C_GB = 0, 1024, 1280, 1536, 2560, 3584, 4608
NEG = -1e30
N_DEV = 8
W_IN_SHARD = D_IN // N_DEV
W_OUT_SHARD = D_MODEL // N_DEV
INV_SQRT2 = 0.7071067811865476
INV_SQRT_2PI = 0.3989422804014327

TM_RESIDENT = 256
TM_STREAM = 512
TN_PROJ = 512
DW_IN_ROWS = D_IN // 4
DW_IN_TOK = 1024

ADAM_LR = 0.001
ADAM_B1 = 0.9
ADAM_B2 = 0.999
ADAM_EPS = 1e-08
ADAM_WD = 0.01
ADAM_STEP = 10

NT_DIMS = (((1,), (1,)), ((), ()))
TN_DIMS = (((0,), (0,)), ((), ()))
MESH_AXES = ("x", "y", "c")


def _sigmoid(v):
    return 1.0 / (1.0 + jnp.exp(-v))


def _gelu_cdf(z):
    return 0.5 * (1.0 + lax.erf(z * INV_SQRT2))


def _gelu_grad(z, cdf):
    return cdf + z * (jnp.exp(-0.5 * z * z) * INV_SQRT_2PI)


def _lane_halves(rows):
    lane = lax.broadcasted_iota(jnp.int32, (rows, BLK), 1)
    return lane < HALF, lane >= HALF


def _half_masks():
    return {BLK: _lane_halves(BLK), 2 * BLK: _lane_halves(2 * BLK)}


def _half_sum(v, ones):
    h_a, h_b = ones[v.shape[0]]
    s_a = jnp.sum(jnp.where(h_a, v, 0.0), axis=-1, keepdims=True)
    s_b = jnp.sum(jnp.where(h_b, v, 0.0), axis=-1, keepdims=True)
    return jnp.where(h_a, s_a, s_b)


def _half_rms(v, w, ones):
    r = lax.rsqrt(_half_sum(v * v, ones) * (1.0 / HALF) + RMS_EPS)
    yhat = v * r
    return yhat, r, yhat * w


def _half_rms_bwd(dy, yhat, r, w, ones):
    dyh = dy * w
    c = _half_sum(dyh * yhat, ones) * (1.0 / HALF)
    return r * (dyh - yhat * c)


def _band_mask(n):
    t = lax.broadcasted_iota(jnp.int32, (BLK, 2 * BLK), 0)
    kk = lax.broadcasted_iota(jnp.int32, (BLK, 2 * BLK), 1)
    dist = t + BLK - kk
    first_key = jnp.where(n > 0, 0, BLK)
    ok = (dist >= 0) & (dist < BLK) & (kk >= first_key)
    return ok, dist.astype(F32)


def _alibi_table(n, slopes_ref, bias_scr):
    ok, distf = _band_mask(n)
    for hd in range(N_HEADS):
        bias_scr[BLK * hd:BLK * (hd + 1), :] = jnp.where(ok, -(slopes_ref[hd] * distf), NEG)


def _sink_col(sinks_ref, kt):
    return jnp.concatenate([jnp.full((BLK, 1), sinks_ref[8 * kt + i], F32) for i in range(8)], axis=0)


def _softmax_sink(s_scaled, bias, sink):
    s = s_scaled + bias
    m = jnp.maximum(jnp.max(s, axis=-1, keepdims=True), sink)
    p = jnp.exp(s - m)
    es = jnp.exp(sink - m)
    inv = 1.0 / (jnp.sum(p, axis=-1, keepdims=True) + es)
    return p * inv, es * inv


def _rows(i):
    return slice(BLK * i, BLK * (i + 1))


def _cols(base, j):
    return slice(base + BLK * j, base + BLK * (j + 1))


def _to_half(v, have, want):
    return v if have == want else pltpu.roll(v, HALF, 1)


def _tril_mask():
    row = lax.broadcasted_iota(jnp.int32, (BLK, BLK), 0)
    col = lax.broadcasted_iota(jnp.int32, (BLK, BLK), 1)
    return row >= col


def _tile(limit, s):
    t = min(limit, s)
    assert s % t == 0, (s, t)
    return t


def _mesh_place():
    x, y, c = lax.axis_index("x"), lax.axis_index("y"), lax.axis_index("c")
    return x, y, c, 4 * x + 2 * y + c


def _peer(x, y, c, k):
    px = 1 - x if k & 4 else x
    py = 1 - y if k & 2 else y
    pc = 1 - c if k & 1 else c
    return (px, py, pc), 4 * px + 2 * py + pc


class _Pushes:
    def __init__(self, n_arrays, src_view, dst_view):
        self.na = n_arrays
        self.src_view = src_view
        self.dst_view = dst_view

    def scratch(self):
        n = self.na * (N_DEV - 1)
        return [pltpu.SemaphoreType.DMA((n,)), pltpu.SemaphoreType.DMA((n,)), pltpu.SemaphoreType.DMA((self.na,))]

    def copies(self, src_refs, dst_refs, send_sems, recv_sems, local_sems):
        x, y, c, me = _mesh_place()
        cps = []
        for a in range(self.na):
            cps.append(pltpu.make_async_copy(self.src_view(a, src_refs[a], me), self.dst_view(a, dst_refs[a], me),
                                             local_sems.at[a]))
        for k in range(1, N_DEV):
            peer, pidx = _peer(x, y, c, k)
            for a in range(self.na):
                sem = a * (N_DEV - 1) + k - 1
                cps.append(pltpu.make_async_remote_copy(
                    src_ref=self.src_view(a, src_refs[a], pidx), dst_ref=self.dst_view(a, dst_refs[a], me),
                    send_sem=send_sems.at[sem], recv_sem=recv_sems.at[sem],
                    device_id=peer, device_id_type=pl.DeviceIdType.MESH))
        return cps


def _call(name, compute, grid, in_specs, out_specs, out_shape, scratch, args, push=None):
    sem = pltpu.CompilerParams(dimension_semantics=("arbitrary",) * len(grid))
    if push is None:
        return pl.pallas_call(compute, name=name, grid=grid, in_specs=in_specs, out_specs=out_specs,
                              out_shape=out_shape, scratch_shapes=scratch, compiler_params=sem)(*args)
    pushes, srcs, xshapes = push
    n_in, n_out, n_scr, na = len(args), len(out_shape), len(scratch), pushes.na
    hbm = pl.BlockSpec(memory_space=pltpu.HBM)

    def body(*refs):
        ins, refs = refs[:n_in], refs[n_in:]
        xin, refs = refs[:na], refs[na:]
        outs, refs = refs[:n_out], refs[n_out:]
        xout, refs = refs[:na], refs[na:]
        scr, sems = refs[:n_scr], refs[n_scr:]
        cps = pushes.copies(xin, xout, *sems)
        first = pl.program_id(0) == 0
        last = pl.program_id(0) == grid[0] - 1
        for d in range(1, len(grid)):
            first = first & (pl.program_id(d) == 0)
            last = last & (pl.program_id(d) == grid[d] - 1)

        @pl.when(first)
        def _():
            for cp in cps:
                cp.start()

        compute(*ins, *outs, *scr)

        @pl.when(last)
        def _():
            for cp in cps:
                cp.wait()

    return pl.pallas_call(
        body, name=name, grid=grid,
        in_specs=list(in_specs) + [hbm] * na, out_specs=list(out_specs) + [hbm] * na,
        out_shape=list(out_shape) + list(xshapes),
        scratch_shapes=list(scratch) + pushes.scratch(), compiler_params=sem)(*args, *srcs)


def _exchange(name, pushes, srcs, out_shapes):
    na = pushes.na
    hbm = pl.BlockSpec(memory_space=pltpu.HBM)

    def body(*refs):
        cps = pushes.copies(refs[:na], refs[na:2 * na], *refs[2 * na:])
        for cp in cps:
            cp.start()
        for cp in cps:
            cp.wait()

    return pl.pallas_call(body, name=name, in_specs=[hbm] * na, out_specs=[hbm] * na, out_shape=out_shapes,
                          scratch_shapes=pushes.scratch())(*srcs)


def _gather_rows(shard_rows, layer):
    def src_view(a, ref, idx):
        return ref.at[layer]

    def dst_view(a, ref, idx):
        r = shard_rows[a]
        return ref.at[pl.ds(pl.multiple_of(idx * r, 64), r), :]

    return _Pushes(len(shard_rows), src_view, dst_view)


def _scatter_rows(shard_rows):
    def src_view(a, ref, idx):
        r = shard_rows[a]
        return ref.at[pl.ds(pl.multiple_of(idx * r, 64), r), :]

    def dst_view(a, ref, idx):
        return ref.at[idx]

    return _Pushes(len(shard_rows), src_view, dst_view)


def _gather_slots():
    return _Pushes(1, lambda a, ref, idx: ref, lambda a, ref, idx: ref.at[idx])


W_FULL = {W_IN_SHARD: jax.ShapeDtypeStruct((D_IN, D_MODEL), BF16),
          W_OUT_SHARD: jax.ShapeDtypeStruct((D_MODEL, D_MODEL), BF16)}


def _slots_shape(rows, cols, dtype):
    return jax.ShapeDtypeStruct((N_DEV, rows, cols), dtype)


def _sum_slots(slots, tr):
    _, r, c = slots.shape

    def compute(s_ref, o_ref):
        tot = s_ref[0].astype(F32)
        for d in range(1, N_DEV):
            tot = tot + s_ref[d].astype(F32)
        o_ref[...] = tot

    return _call("sum_slots", compute, (r // tr,),
                 [pl.BlockSpec((N_DEV, tr, c), lambda i: (0, i, 0))], [pl.BlockSpec((tr, c), lambda i: (i, 0))],
                 [jax.ShapeDtypeStruct((r, c), F32)], [], [slots])[0]


def _resident(shape):
    return pl.BlockSpec(shape, lambda *_: (0,) * len(shape), pipeline_mode=pl.Buffered(1))


def _norm_proj(x, g, w_t, push=None):
    s = x.shape[0]
    tm = _tile(TM_RESIDENT, s)

    def compute(x_ref, g_ref, w_ref, h_ref, p_ref):
        xf = x_ref[...]
        r = lax.rsqrt(jnp.mean(xf * xf, axis=-1, keepdims=True) + RMS_EPS)
        h = ((xf * r) * g_ref[...]).astype(BF16)
        h_ref[...] = h
        for j in range(D_IN // TN_PROJ):
            cols = slice(j * TN_PROJ, (j + 1) * TN_PROJ)
            p_ref[:, cols] = lax.dot_general(h, w_ref[cols, :], NT_DIMS, preferred_element_type=F32)

    return _call(
        "norm_proj", compute, (s // tm,),
        [pl.BlockSpec((tm, D_MODEL), lambda i: (i, 0)), pl.BlockSpec((1, D_MODEL), lambda i: (0, 0)),
         _resident((D_IN, D_MODEL))],
        [pl.BlockSpec((tm, D_MODEL), lambda i: (i, 0)), pl.BlockSpec((tm, D_IN), lambda i: (i, 0))],
        [jax.ShapeDtypeStruct((s, D_MODEL), BF16), jax.ShapeDtypeStruct((s, D_IN), F32)],
        [], [x, g, w_t], push)


def _sgu_weights(ws_ref, wtril_scr, wtril_t_scr=None):
    tril = _tril_mask()
    for hd in range(N_HEADS):
        w = jnp.where(tril, ws_ref[hd], 0.0)
        wtril_scr[BLK * hd:BLK * (hd + 1), :] = w.astype(BF16)
        if wtril_t_scr is not None:
            wtril_t_scr[hd // 2, :, BLK * (hd % 2):BLK * (hd % 2 + 1)] = w.T.astype(BF16)


def _kv_band(kt, p_ref, kvp_ref, kw_v, ones):
    kband = jnp.concatenate([kvp_ref[:, _cols(0, kt)], p_ref[:, _cols(C_K, kt)]], axis=0)
    kyhat, kr, kn = _half_rms(kband, kw_v, ones)
    vband = jnp.concatenate([kvp_ref[:, _cols(256, kt)], p_ref[:, _cols(C_V, kt)]], axis=0)
    return kyhat, kr, (kn * 0.125).astype(BF16), vband.astype(BF16)


def _stack_heads(tiles, halves):
    parts = []
    for tt, tile in enumerate(tiles):
        for qh in range(2):
            parts.append(_to_half(jnp.where(halves[qh], tile, 0.0), qh, tt // 2).astype(BF16))
    return jnp.concatenate(parts, axis=0)


def _unstack_heads(stacked, tt, h_a):
    return jnp.where(h_a, _to_half(stacked[_rows(2 * tt)], tt // 2, 0), _to_half(stacked[_rows(2 * tt + 1)], tt // 2, 1))


def _mixer_fwd(proj, qw, kw, sinks, slopes, w_s, bmap, push=None):
    s = proj.shape[0]
    nb = s // BLK

    def compute(sinks_ref, slopes_ref, p_ref, kvp_ref, qw_ref, kw_ref, ws_ref, bmap_ref, mix_ref,
                wtril_scr, bias_scr):
        n = pl.program_id(0)
        h_a, h_b = _lane_halves(BLK)
        ones = _half_masks()
        halves = (h_a, h_b)
        qw_v = qw_ref[...]
        kw_v = kw_ref[...]

        @pl.when(n == 0)
        def _():
            _sgu_weights(ws_ref, wtril_scr)

        @pl.when(n <= 1)
        def _():
            _alibi_table(n, slopes_ref, bias_scr)

        bands = [_kv_band(kt, p_ref, kvp_ref, kw_v, ones) for kt in range(2)]
        sc = []
        for kt in range(2):
            qn = [_half_rms(p_ref[:, _cols(C_Q, 4 * kt + tt)], qw_v, ones)[2] for tt in range(4)]
            sc.append(lax.dot_general(_stack_heads(qn, halves), bands[kt][2], NT_DIMS, preferred_element_type=F32))
        zu, mixed = [], []
        for j in range(8):
            zu_pre = p_ref[:, _cols(C_ZU, j)]
            zv_pre = p_ref[:, _cols(C_ZV, j)]
            zu.append(zu_pre * _gelu_cdf(zu_pre))
            zvb = (zv_pre * _gelu_cdf(zv_pre)).astype(BF16)
            mixed.append(jnp.dot(wtril_scr[2 * BLK * j:2 * BLK * (j + 1), :], zvb, preferred_element_type=F32))
        o = []
        for kt in range(2):
            p, _ = _softmax_sink(sc[kt], bias_scr[8 * BLK * kt:8 * BLK * (kt + 1), :], _sink_col(sinks_ref, kt))
            o.append(jnp.dot(p.astype(BF16), bands[kt][3], preferred_element_type=F32))
        for j in range(8):
            gb = p_ref[:, _cols(C_GB, j)]
            mx = jnp.where(h_a, mixed[j][0:BLK], mixed[j][BLK:2 * BLK]) + bmap_ref[:, _cols(0, j)]
            mix_ref[:, _cols(D_ATTN, j)] = ((zu[j] * mx) * (gb * _sigmoid(gb))).astype(BF16)
        for kt in range(2):
            for tt in range(4):
                j = 4 * kt + tt
                ga = p_ref[:, _cols(C_GA, j)]
                mix_ref[:, _cols(0, j)] = (_unstack_heads(o[kt], tt, h_a) * (ga * _sigmoid(ga))).astype(BF16)

    smem = pl.BlockSpec(memory_space=pltpu.SMEM)
    return _call(
        "mixer_fwd", compute, (nb,),
        [smem, smem,
         pl.BlockSpec((BLK, D_IN), lambda n: (n, 0)),
         pl.BlockSpec((BLK, 512), lambda n: (jnp.maximum(n - 1, 0), 2)),
         pl.BlockSpec((1, BLK), lambda n: (0, 0)),
         pl.BlockSpec((1, BLK), lambda n: (0, 0)),
         pl.BlockSpec((N_HEADS, BLK, BLK), lambda n: (0, 0, 0)),
         pl.BlockSpec((BLK, D_ATTN), lambda n: (0, 0))],
        [pl.BlockSpec((BLK, D_MODEL), lambda n: (n, 0))],
        [jax.ShapeDtypeStruct((s, D_MODEL), BF16)],
        [pltpu.VMEM((N_HEADS * BLK, BLK), BF16), pltpu.VMEM((N_HEADS * BLK, 2 * BLK), F32)],
        [sinks, slopes, proj, proj, qw, kw, w_s, bmap], push)


def _out_proj(mix, w_o, x):
    s = x.shape[0]
    tm = _tile(TM_STREAM, s)

    def compute(m_ref, w_ref, x_ref, o_ref):
        o_ref[...] = x_ref[...] + jnp.dot(m_ref[...], w_ref[...], preferred_element_type=F32)

    return _call(
        "out_proj", compute, (s // tm,),
        [pl.BlockSpec((tm, D_MODEL), lambda i: (i, 0)), _resident((D_MODEL, D_MODEL)),
         pl.BlockSpec((tm, D_MODEL), lambda i: (i, 0))],
        [pl.BlockSpec((tm, D_MODEL), lambda i: (i, 0))],
        [jax.ShapeDtypeStruct((s, D_MODEL), F32)], [], [mix, w_o, x])[0]


def _loss_grad(y, tgt):
    s = y.shape[0]
    tm = _tile(TM_STREAM, s)

    def compute(y_ref, t_ref, dy_ref, sq_ref):
        @pl.when(pl.program_id(0) == 0)
        def _():
            sq_ref[...] = jnp.zeros_like(sq_ref)

        e = y_ref[...] - t_ref[...]
        dy_ref[...] = e * (1.0 / D_MODEL)
        sq_ref[...] += jnp.sum(e * e, axis=0, keepdims=True)

    return _call(
        "loss_grad", compute, (s // tm,),
        [pl.BlockSpec((tm, D_MODEL), lambda i: (i, 0)), pl.BlockSpec((tm, D_MODEL), lambda i: (i, 0))],
        [pl.BlockSpec((tm, D_MODEL), lambda i: (i, 0)), pl.BlockSpec((1, D_MODEL), lambda i: (0, 0))],
        [jax.ShapeDtypeStruct((s, D_MODEL), F32), jax.ShapeDtypeStruct((1, D_MODEL), F32)], [], [y, tgt])


def _dmix(dx, w_o):
    s = dx.shape[0]
    tm = _tile(TM_STREAM, s)

    def compute(d_ref, w_ref, o_ref):
        o_ref[...] = lax.dot_general(d_ref[...].astype(BF16), w_ref[...], NT_DIMS, preferred_element_type=F32)

    return _call(
        "dmix", compute, (s // tm,),
        [pl.BlockSpec((tm, D_MODEL), lambda i: (i, 0)), _resident((D_MODEL, D_MODEL))],
        [pl.BlockSpec((tm, D_MODEL), lambda i: (i, 0))],
        [jax.ShapeDtypeStruct((s, D_MODEL), F32)], [], [dx, w_o])[0]


def _dw_out(mix, dx):
    s = dx.shape[0]
    tk = _tile(TM_STREAM, s)
    nk = s // tk

    def compute(m_ref, d_ref, o_ref, acc):
        k = pl.program_id(0)

        @pl.when(k == 0)
        def _():
            acc[...] = jnp.zeros_like(acc)

        acc[...] += lax.dot_general(m_ref[...], d_ref[...].astype(BF16), TN_DIMS, preferred_element_type=F32)

        @pl.when(k == nk - 1)
        def _():
            o_ref[...] = acc[...].astype(BF16)

    return _call(
        "dw_out", compute, (nk,),
        [pl.BlockSpec((tk, D_MODEL), lambda k: (k, 0)), pl.BlockSpec((tk, D_MODEL), lambda k: (k, 0))],
        [pl.BlockSpec((D_MODEL, D_MODEL), lambda k: (0, 0))],
        [jax.ShapeDtypeStruct((D_MODEL, D_MODEL), BF16)],
        [pltpu.VMEM((D_MODEL, D_MODEL), F32)], [mix, dx])[0]


def _mixer_bwd(proj, dmix, qw, kw, sinks, slopes, w_s, bmap, push=None):
    s = proj.shape[0]
    nb = s // BLK

    def compute(sinks_ref, slopes_ref, p_ref, kvp_ref, dm_ref, qw_ref, kw_ref, ws_ref, bmap_ref,
                dp_ref, dqw_ref, dkw_ref, dsk_ref, dws_ref, dbs_ref,
                pend, accq, acck, accs, accb, wtril_scr, wtril_t_scr, bias_scr):
        n = pl.program_id(0)
        h_a, h_b = _lane_halves(BLK)
        ones = _half_masks()
        halves = (h_a, h_b)
        lane = lax.broadcasted_iota(jnp.int32, (BLK, BLK), 1)

        @pl.when(n == 0)
        def _():
            accq[...] = jnp.zeros_like(accq)
            acck[...] = jnp.zeros_like(acck)
            accs[...] = jnp.zeros_like(accs)
            accb[...] = jnp.zeros_like(accb)
            dws_ref[...] = jnp.zeros_like(dws_ref)
            _sgu_weights(ws_ref, wtril_scr, wtril_t_scr)

        @pl.when(n >= 1)
        def _():
            dp_ref[:, 0:C_K] = pend[:, 0:C_K].astype(BF16)
            dp_ref[:, C_GA:D_IN] = pend[:, C_GA:D_IN].astype(BF16)

        @pl.when(n <= 1)
        def _():
            _alibi_table(n, slopes_ref, bias_scr)

        @pl.when(n < nb)
        def _():
            qw_v = qw_ref[...]
            kw_v = kw_ref[...]
            bands = [_kv_band(kt, p_ref, kvp_ref, kw_v, ones) for kt in range(2)]
            tiles, qst, dost, sc, dpm = [], [], [], [], []
            for kt in range(2):
                qn, d_o, tl = [], [], []
                for tt in range(4):
                    j = 4 * kt + tt
                    qyhat, qr, qn_t = _half_rms(p_ref[:, _cols(C_Q, j)], qw_v, ones)
                    ga = p_ref[:, _cols(C_GA, j)]
                    sg = _sigmoid(ga)
                    dma = dm_ref[:, _cols(0, j)]
                    qn.append(qn_t)
                    d_o.append(dma * (ga * sg))
                    tl.append((qyhat, qr, dma * (sg * (1.0 + ga * (1.0 - sg)))))
                tiles.append(tl)
                qst.append(_stack_heads(qn, halves))
                dost.append(_stack_heads(d_o, halves))
                sc.append(lax.dot_general(qst[kt], bands[kt][2], NT_DIMS, preferred_element_type=F32))
                dpm.append(lax.dot_general(dost[kt], bands[kt][3], NT_DIMS, preferred_element_type=F32))
            sgu = []
            for j in range(8):
                zu_pre = p_ref[:, _cols(C_ZU, j)]
                zv_pre = p_ref[:, _cols(C_ZV, j)]
                cu = _gelu_cdf(zu_pre)
                cv = _gelu_cdf(zv_pre)
                zvb = (zv_pre * cv).astype(BF16)
                sgu.append((zu_pre * cu, zvb, _gelu_grad(zu_pre, cu), _gelu_grad(zv_pre, cv),
                            jnp.dot(wtril_scr[2 * BLK * j:2 * BLK * (j + 1), :], zvb, preferred_element_type=F32)))
            dsink = jnp.zeros((BLK, BLK), F32)
            pst, dqkst = [], []
            for kt in range(2):
                p, p_sink = _softmax_sink(sc[kt], bias_scr[8 * BLK * kt:8 * BLK * (kt + 1), :],
                                          _sink_col(sinks_ref, kt))
                dsum = jnp.sum(p * dpm[kt], axis=-1, keepdims=True)
                dsink_col = -(p_sink * dsum)
                for i in range(8):
                    dsink = dsink + jnp.where(lane == 8 * kt + i, dsink_col[_rows(i)], 0.0)
                pst.append(p.astype(BF16))
                dqkst.append((p * (dpm[kt] - dsum)).astype(BF16))
            o, dqn_all, dvb, dkn = [], [], [], []
            for kt in range(2):
                o.append(jnp.dot(pst[kt], bands[kt][3], preferred_element_type=F32))
                dqn_all.append(jnp.dot(dqkst[kt], bands[kt][2], preferred_element_type=F32))
                dvb.append(lax.dot_general(pst[kt], dost[kt], TN_DIMS, preferred_element_type=F32))
                dkn.append(0.125 * lax.dot_general(dqkst[kt], qst[kt], TN_DIMS, preferred_element_type=F32))
            dms = []
            for j in range(8):
                zu, zvb, gu, gv, m_ab = sgu[j]
                gb = p_ref[:, _cols(C_GB, j)]
                dmb = dm_ref[:, _cols(D_ATTN, j)]
                mixed = jnp.where(h_a, m_ab[0:BLK], m_ab[BLK:2 * BLK]) + bmap_ref[:, _cols(0, j)]
                sgb = _sigmoid(gb)
                dgate = dmb * (gb * sgb)
                pend[:, _cols(C_ZU, j)] = (dgate * mixed) * gu
                pend[:, _cols(C_GB, j)] = (dmb * (zu * mixed)) * (sgb * (1.0 + gb * (1.0 - sgb)))
                dmixed = dgate * zu
                accb[:, _cols(0, j)] += dmixed
                dms.append(jnp.concatenate([jnp.where(h_a, dmixed, 0.0).astype(BF16),
                                            jnp.where(h_b, dmixed, 0.0).astype(BF16)], axis=0))
            dzv = []
            for j in range(8):
                dzv.append(jnp.dot(wtril_t_scr[j], dms[j], preferred_element_type=F32))
                dw_ab = lax.dot_general(dms[j], sgu[j][1], NT_DIMS, preferred_element_type=F32)
                dws_ref[2 * j] += dw_ab[0:BLK]
                dws_ref[2 * j + 1] += dw_ab[BLK:2 * BLK]
            dq_w = jnp.zeros((BLK, BLK), F32)
            for kt in range(2):
                for tt in range(4):
                    j = 4 * kt + tt
                    qyhat, qr, dsilu = tiles[kt][tt]
                    dqn = _unstack_heads(dqn_all[kt], tt, h_a)
                    pend[:, _cols(C_GA, j)] = _unstack_heads(o[kt], tt, h_a) * dsilu
                    pend[:, _cols(C_Q, j)] = _half_rms_bwd(dqn, qyhat, qr, qw_v, ones)
                    dq_w = dq_w + dqn * qyhat
            dk_w = jnp.zeros((BLK, BLK), F32)
            for kt in range(2):
                kyhat, kr = bands[kt][0], bands[kt][1]
                dk = _half_rms_bwd(dkn[kt], kyhat, kr, kw_v, ones)
                dkw_part = dkn[kt] * kyhat
                dk_w = dk_w + (dkw_part[0:BLK] + dkw_part[BLK:2 * BLK])
                dv = dvb[kt]

                @pl.when(n >= 1)
                def _():
                    dp_ref[:, _cols(C_K, kt)] = (pend[:, _cols(C_K, kt)] + dk[0:BLK]).astype(BF16)
                    dp_ref[:, _cols(C_V, kt)] = (pend[:, _cols(C_V, kt)] + dv[0:BLK]).astype(BF16)

                pend[:, _cols(C_K, kt)] = dk[BLK:2 * BLK]
                pend[:, _cols(C_V, kt)] = dv[BLK:2 * BLK]
            accq[...] += dq_w
            acck[...] += dk_w
            accs[...] += dsink
            for j in range(8):
                pend[:, _cols(C_ZV, j)] = dzv[j] * sgu[j][3]

        @pl.when(n == nb)
        def _():
            tril = _tril_mask()
            dp_ref[:, C_K:C_GA] = pend[:, C_K:C_GA].astype(BF16)
            aq = accq[...]
            ak = acck[...]
            dqw_ref[...] = jnp.sum(aq + pltpu.roll(aq, HALF, 1), axis=0, keepdims=True)
            dkw_ref[...] = jnp.sum(ak + pltpu.roll(ak, HALF, 1), axis=0, keepdims=True)
            dsk_ref[...] = jnp.sum(accs[...], axis=0, keepdims=True)
            for hd in range(N_HEADS):
                dws_ref[hd] = jnp.where(tril, dws_ref[hd], 0.0)
            hrow = lax.broadcasted_iota(jnp.int32, (N_HEADS, D_ATTN), 0)
            hcol = lax.broadcasted_iota(jnp.int32, (N_HEADS, D_ATTN), 1)
            sel = jnp.where((hcol >= hrow * HALF) & (hcol < (hrow + 1) * HALF), 1.0, 0.0).astype(BF16)
            rem = accb[...]
            tot = jnp.zeros((N_HEADS, BLK), F32)
            for _ in range(3):
                part = rem.astype(BF16)
                tot = tot + lax.dot_general(sel, part, NT_DIMS, preferred_element_type=F32)
                rem = rem - part.astype(F32)
            dbs_ref[...] = tot

    smem = pl.BlockSpec(memory_space=pltpu.SMEM)
    last = nb - 1
    tile_f32 = pltpu.VMEM((BLK, BLK), F32)
    return _call(
        "mixer_bwd", compute, (nb + 1,),
        [smem, smem,
         pl.BlockSpec((BLK, D_IN), lambda n: (jnp.minimum(n, last), 0)),
         pl.BlockSpec((BLK, 512), lambda n: (jnp.maximum(jnp.minimum(n, last) - 1, 0), 2)),
         pl.BlockSpec((BLK, D_MODEL), lambda n: (jnp.minimum(n, last), 0)),
         pl.BlockSpec((1, BLK), lambda n: (0, 0)),
         pl.BlockSpec((1, BLK), lambda n: (0, 0)),
         pl.BlockSpec((N_HEADS, BLK, BLK), lambda n: (0, 0, 0)),
         pl.BlockSpec((BLK, D_ATTN), lambda n: (0, 0))],
        [pl.BlockSpec((BLK, D_IN), lambda n: (jnp.maximum(n - 1, 0), 0)),
         pl.BlockSpec((1, BLK), lambda n: (0, 0)),
         pl.BlockSpec((1, BLK), lambda n: (0, 0)),
         pl.BlockSpec((1, BLK), lambda n: (0, 0)),
         pl.BlockSpec((N_HEADS, BLK, BLK), lambda n: (0, 0, 0)),
         pl.BlockSpec((N_HEADS, BLK), lambda n: (0, 0))],
        [jax.ShapeDtypeStruct((s, D_IN), BF16),
         jax.ShapeDtypeStruct((1, BLK), F32),
         jax.ShapeDtypeStruct((1, BLK), F32),
         jax.ShapeDtypeStruct((1, BLK), F32),
         jax.ShapeDtypeStruct((N_HEADS, BLK, BLK), F32),
         jax.ShapeDtypeStruct((N_HEADS, BLK), F32)],
        [pltpu.VMEM((BLK, D_IN), F32), tile_f32, tile_f32, tile_f32, pltpu.VMEM((BLK, D_ATTN), F32),
         pltpu.VMEM((N_HEADS * BLK, BLK), BF16), pltpu.VMEM((N_HEADS // 2, BLK, 2 * BLK), BF16),
         pltpu.VMEM((N_HEADS * BLK, 2 * BLK), F32)],
        [sinks, slopes, proj, proj, dmix, qw, kw, w_s, bmap], push)


def _dh_norm_bwd(dproj, w_t, x, dx_out, g, push=None):
    s = x.shape[0]
    tm = _tile(TM_RESIDENT, s)

    def compute(dp_ref, w_ref, x_ref, dxo_ref, g_ref, dx_ref, dg_ref):
        @pl.when(pl.program_id(0) == 0)
        def _():
            dg_ref[...] = jnp.zeros_like(dg_ref)

        dh = jnp.dot(dp_ref[...], w_ref[...], preferred_element_type=F32)
        xf = x_ref[...]
        r = lax.rsqrt(jnp.mean(xf * xf, axis=-1, keepdims=True) + RMS_EPS)
        yhat = xf * r
        dyh = dh * g_ref[...]
        c = jnp.mean(dyh * yhat, axis=-1, keepdims=True)
        dx_ref[...] = dxo_ref[...] + r * (dyh - yhat * c)
        dg_ref[...] += jnp.sum(dh * yhat, axis=0, keepdims=True)

    return _call(
        "dh_norm_bwd", compute, (s // tm,),
        [pl.BlockSpec((tm, D_IN), lambda i: (i, 0)), _resident((D_IN, D_MODEL)),
         pl.BlockSpec((tm, D_MODEL), lambda i: (i, 0)), pl.BlockSpec((tm, D_MODEL), lambda i: (i, 0)),
         pl.BlockSpec((1, D_MODEL), lambda i: (0, 0))],
        [pl.BlockSpec((tm, D_MODEL), lambda i: (i, 0)), pl.BlockSpec((1, D_MODEL), lambda i: (0, 0))],
        [jax.ShapeDtypeStruct((s, D_MODEL), F32), jax.ShapeDtypeStruct((1, D_MODEL), F32)],
        [], [dproj, w_t, x, dx_out, g], push)


def _dw_in(dproj, h, push=None):
    s = h.shape[0]
    tk = _tile(DW_IN_TOK, s)
    nk = s // tk

    def compute(dp_ref, h_ref, o_ref, acc):
        k = pl.program_id(1)

        @pl.when(k == 0)
        def _():
            acc[...] = jnp.zeros_like(acc)

        acc[...] += lax.dot_general(dp_ref[...], h_ref[...], TN_DIMS, preferred_element_type=F32)

        @pl.when(k == nk - 1)
        def _():
            o_ref[...] = acc[...].astype(BF16)

    return _call(
        "dw_in", compute, (D_IN // DW_IN_ROWS, nk),
        [pl.BlockSpec((tk, DW_IN_ROWS), lambda j, k: (k, j)), pl.BlockSpec((tk, D_MODEL), lambda j, k: (k, 0))],
        [pl.BlockSpec((DW_IN_ROWS, D_MODEL), lambda j, k: (j, 0))],
        [jax.ShapeDtypeStruct((D_IN, D_MODEL), BF16)],
        [pltpu.VMEM((DW_IN_ROWS, D_MODEL), F32)], [dproj, h], push)


def _adamw(name, w, g, m, v, tr=None):
    shape = w.shape
    c = shape[-1]
    flat = [a.reshape(-1, c) for a in (w, g, m, v)]
    r = flat[0].shape[0]
    tr = r if tr is None else tr

    def compute(w_ref, g_ref, m_ref, v_ref, d_ref, mo_ref, vo_ref):
        gv = g_ref[...]
        m_new = ADAM_B1 * m_ref[...] + (1.0 - ADAM_B1) * gv
        v_new = ADAM_B2 * v_ref[...] + (1.0 - ADAM_B2) * jnp.square(gv)
        m_hat = m_new / (1.0 - ADAM_B1 ** ADAM_STEP)
        v_hat = v_new / (1.0 - ADAM_B2 ** ADAM_STEP)
        d_ref[...] = -ADAM_LR * (m_hat / (jnp.sqrt(v_hat) + ADAM_EPS) + ADAM_WD * w_ref[...])
        mo_ref[...] = m_new
        vo_ref[...] = v_new

    spec = pl.BlockSpec((tr, c), lambda i: (i, 0))
    outs = _call(name, compute, (r // tr,), [spec] * 4, [spec] * 3, [jax.ShapeDtypeStruct((r, c), F32)] * 3,
                 [], flat)
    return [o.reshape(shape) for o in outs]


def _pack_rows(parts):
    rows = []
    for a in parts:
        flat = a.reshape(-1)
        n = -(-flat.shape[0] // (8 * BLK)) * 8
        rows.append(jnp.pad(flat, (0, n * BLK - flat.shape[0])).reshape(n, BLK))
    return jnp.concatenate(rows, axis=0)


def _unpack_rows(packed, like):
    out = []
    row = 0
    for a in like:
        n = -(-a.size // (8 * BLK)) * 8
        out.append(packed[row:row + n].reshape(-1)[:a.size].reshape(a.shape))
        row += n
    return out


def kernel(x, norm_g, w_in, q_norm, k_norm, sinks, w_s, b_s, w_out, loss_target, m_norm_g, m_w_in, m_q_norm, m_k_norm, m_sinks, m_w_s, m_b_s, m_w_out, v_norm_g, v_w_in, v_q_norm, v_k_norm, v_sinks, v_w_s, v_b_s, v_w_out):
    xs = x[0]
    tgt = loss_target[0]
    slopes = jnp.asarray(2.0 ** (-8.0 * np.arange(1, N_HEADS + 1) / N_HEADS), dtype=F32)
    wt_sh = jnp.swapaxes(w_in, 1, 2).astype(BF16)
    wo_sh = w_out.astype(BF16)

    layer_par = []
    for l in range(DEPTH):
        layer_par.append((jnp.tile(q_norm[l], 2)[None, :], jnp.tile(k_norm[l], 2)[None, :],
                          jnp.repeat(b_s[l].T, HALF, axis=1)))

    wt_full, wo_full = _exchange("gather_w0", _gather_rows((W_IN_SHARD, W_OUT_SHARD), 0), [wt_sh, wo_sh],
                                 [W_FULL[W_IN_SHARD], W_FULL[W_OUT_SHARD]])
    saved = []
    cur = xs
    for l in range(DEPTH):
        qw, kw, bmap = layer_par[l]
        more = l + 1 < DEPTH
        res = _norm_proj(cur, norm_g[l][None, :], wt_full,
                         (_gather_rows((W_IN_SHARD,), l + 1), [wt_sh], [W_FULL[W_IN_SHARD]]) if more else None)
        h, proj = res[0], res[1]
        res2 = _mixer_fwd(proj, qw, kw, sinks[l], slopes, w_s[l], bmap,
                          (_gather_rows((W_OUT_SHARD,), l + 1), [wo_sh], [W_FULL[W_OUT_SHARD]]) if more else None)
        mix = res2[0]
        nxt = _out_proj(mix, wo_full, cur)
        saved.append((cur, h, proj, mix, wt_full, wo_full))
        cur = nxt
        if more:
            wt_full, wo_full = res[2], res2[1]

    dx, sq = _loss_grad(cur, tgt)
    loss = lax.psum(0.5 * jnp.sum(sq) / D_MODEL, MESH_AXES)

    g_wt, g_wo, g_small, g_norm = ([None] * DEPTH for _ in range(4))
    small_like = None
    for l in reversed(range(DEPTH)):
        x_l, h, proj, mix, wt_l, wo_l = saved[l]
        qw, kw, bmap = layer_par[l]
        dmix = _dmix(dx, wo_l)
        dwo_part = _dw_out(mix, dx)
        dproj, dqw, dkw, dsk, dws, dbs, wo_slots = _mixer_bwd(
            proj, dmix, qw, kw, sinks[l], slopes, w_s[l], bmap,
            (_scatter_rows((W_OUT_SHARD,)), [dwo_part], [_slots_shape(W_OUT_SHARD, D_MODEL, BF16)]))
        small_like = [dqw[0, :HALF], dkw[0, :HALF], dsk[0, :N_HEADS], dws, dbs]
        packed = _pack_rows(small_like)
        dwt_part, small_slots = _dw_in(dproj, h, (_gather_slots(), [packed], [_slots_shape(*packed.shape, F32)]))
        dx, dng, wt_slots = _dh_norm_bwd(
            dproj, wt_l, x_l, dx, norm_g[l][None, :],
            (_scatter_rows((W_IN_SHARD,)), [dwt_part], [_slots_shape(W_IN_SHARD, D_MODEL, BF16)]))
        g_wt[l] = _sum_slots(wt_slots, W_IN_SHARD // 2)
        g_wo[l] = _sum_slots(wo_slots, W_OUT_SHARD)
        g_small[l] = _unpack_rows(_sum_slots(small_slots, packed.shape[0]), small_like)
        g_norm[l] = dng[0]

    dng_all = _pack_rows([jnp.stack(g_norm)])
    dng_slots = _exchange("gather_dnorm", _gather_slots(), [dng_all], [_slots_shape(*dng_all.shape, F32)])[0]
    gr_norm = _unpack_rows(_sum_slots(dng_slots, dng_all.shape[0]), [norm_g])[0]
    gr_qn, gr_kn, gr_sk, gr_ws, gr_bs = (jnp.stack([g_small[l][i] for l in range(DEPTH)]) for i in range(5))
    gr_w_in = jnp.swapaxes(jnp.stack(g_wt), 1, 2)
    gr_w_out = jnp.stack(g_wo)

    grads = [gr_norm, gr_w_in, gr_qn, gr_kn, gr_sk, gr_ws, gr_bs, gr_w_out]
    weights = [norm_g, w_in, q_norm, k_norm, sinks, w_s, b_s, w_out]
    moms = [m_norm_g, m_w_in, m_q_norm, m_k_norm, m_sinks, m_w_s, m_b_s, m_w_out]
    vels = [v_norm_g, v_w_in, v_q_norm, v_k_norm, v_sinks, v_w_s, v_b_s, v_w_out]
    tiles = [None, 512, None, None, None, 1024, None, 256]
    names = ["norm_g", "w_in", "q_norm", "k_norm", "sinks", "w_s", "b_s", "w_out"]
    deltas, new_m, new_v = [], [], []
    for nm, w, g, m, v, tr in zip(names, weights, grads, moms, vels, tiles):
        d, mo, vo = _adamw("adamw_" + nm, w, g, m, v, tr)
        deltas.append(d)
        new_m.append(mo)
        new_v.append(vo)

    return (loss, dx[None], *grads, *deltas, *new_m, *new_v)
```

```python
import numpy as np
import jax
import jax.numpy as jnp
from jax import lax
from jax.experimental import pallas as pl
from jax.experimental.pallas import tpu as pltpu

F32 = jnp.float32
BF16 = jnp.bfloat16

D_MODEL = 2048
D_ATTN = 1024
D_IN = 5632
N_HEADS = 16
DEPTH = 4
BLK = 128
HALF = 64
RMS_EPS = 1e-6
C_Q, C_K, C_V, C_GA, C_ZU, C_ZV, C_GB = 0, 1024, 1280, 1536, 2560, 3584, 4608
NEG = -1e30
N_DEV = 8
W_IN_SHARD = D_IN // N_DEV
W_OUT_SHARD = D_MODEL // N_DEV
INV_SQRT2 = 0.7071067811865476
INV_SQRT_2PI = 0.3989422804014327

TM_RESIDENT = 256
TM_STREAM = 512
TN_PROJ = 512
DW_IN_ROWS = D_IN // 4
DW_IN_TOK = 1024

ADAM_LR = 0.001
ADAM_B1 = 0.9
ADAM_B2 = 0.999
ADAM_EPS = 1e-08
ADAM_WD = 0.01
ADAM_STEP = 10

NT_DIMS = (((1,), (1,)), ((), ()))
TN_DIMS = (((0,), (0,)), ((), ()))
MESH_AXES = ("x", "y", "c")


def _sigmoid(v):
    return 1.0 / (1.0 + jnp.exp(-v))


def _gelu_cdf(z):
    return 0.5 * (1.0 + lax.erf(z * INV_SQRT2))


def _gelu_grad(z, cdf):
    return cdf + z * (jnp.exp(-0.5 * z * z) * INV_SQRT_2PI)


def _lane_halves(rows):
    lane = lax.broadcasted_iota(jnp.int32, (rows, BLK), 1)
    return lane < HALF, lane >= HALF


def _half_masks():
    return {BLK: _lane_halves(BLK), 2 * BLK: _lane_halves(2 * BLK)}


def _half_sum(v, ones):
    h_a, h_b = ones[v.shape[0]]
    s_a = jnp.sum(jnp.where(h_a, v, 0.0), axis=-1, keepdims=True)
    s_b = jnp.sum(jnp.where(h_b, v, 0.0), axis=-1, keepdims=True)
    return jnp.where(h_a, s_a, s_b)


def _half_rms(v, w, ones):
    r = lax.rsqrt(_half_sum(v * v, ones) * (1.0 / HALF) + RMS_EPS)
    yhat = v * r
    return yhat, r, yhat * w


def _half_rms_bwd(dy, yhat, r, w, ones):
    dyh = dy * w
    c = _half_sum(dyh * yhat, ones) * (1.0 / HALF)
    return r * (dyh - yhat * c)


def _band_mask(n):
    t = lax.broadcasted_iota(jnp.int32, (BLK, 2 * BLK), 0)
    kk = lax.broadcasted_iota(jnp.int32, (BLK, 2 * BLK), 1)
    dist = t + BLK - kk
    first_key = jnp.where(n > 0, 0, BLK)
    ok = (dist >= 0) & (dist < BLK) & (kk >= first_key)
    return ok, dist.astype(F32)


def _alibi_table(n, slopes_ref, bias_scr):
    ok, distf = _band_mask(n)
    for hd in range(N_HEADS):
        bias_scr[BLK * hd:BLK * (hd + 1), :] = jnp.where(ok, -(slopes_ref[hd] * distf), NEG)


def _sink_col(sinks_ref, kt):
    return jnp.concatenate([jnp.full((BLK, 1), sinks_ref[8 * kt + i], F32) for i in range(8)], axis=0)


def _softmax_sink(s_scaled, bias, sink):
    s = s_scaled + bias
    m = jnp.maximum(jnp.max(s, axis=-1, keepdims=True), sink)
    p = jnp.exp(s - m)
    es = jnp.exp(sink - m)
    inv = 1.0 / (jnp.sum(p, axis=-1, keepdims=True) + es)
    return p * inv, es * inv


def _rows(i):
    return slice(BLK * i, BLK * (i + 1))


def _cols(base, j):
    return slice(base + BLK * j, base + BLK * (j + 1))


def _to_half(v, have, want):
    return v if have == want else pltpu.roll(v, HALF, 1)


def _tril_mask():
    row = lax.broadcasted_iota(jnp.int32, (BLK, BLK), 0)
    col = lax.broadcasted_iota(jnp.int32, (BLK, BLK), 1)
    return row >= col


def _tile(limit, s):
    t = min(limit, s)
    assert s % t == 0, (s, t)
    return t


def _mesh_place():
    x, y, c = lax.axis_index("x"), lax.axis_index("y"), lax.axis_index("c")
    return x, y, c, 4 * x + 2 * y + c


def _peer(x, y, c, k):
    px = 1 - x if k & 4 else x
    py = 1 - y if k & 2 else y
    pc = 1 - c if k & 1 else c
    return (px, py, pc), 4 * px + 2 * py + pc


class _Pushes:
    def __init__(self, n_arrays, src_view, dst_view):
        self.na = n_arrays
        self.src_view = src_view
        self.dst_view = dst_view

    def scratch(self):
        n = self.na * (N_DEV - 1)
        return [pltpu.SemaphoreType.DMA((n,)), pltpu.SemaphoreType.DMA((n,)), pltpu.SemaphoreType.DMA((self.na,))]

    def copies(self, src_refs, dst_refs, send_sems, recv_sems, local_sems):
        x, y, c, me = _mesh_place()
        cps = []
        for a in range(self.na):
            cps.append(pltpu.make_async_copy(self.src_view(a, src_refs[a], me), self.dst_view(a, dst_refs[a], me),
                                             local_sems.at[a]))
        for k in range(1, N_DEV):
            peer, pidx = _peer(x, y, c, k)
            for a in range(self.na):
                sem = a * (N_DEV - 1) + k - 1
                cps.append(pltpu.make_async_remote_copy(
                    src_ref=self.src_view(a, src_refs[a], pidx), dst_ref=self.dst_view(a, dst_refs[a], me),
                    send_sem=send_sems.at[sem], recv_sem=recv_sems.at[sem],
                    device_id=peer, device_id_type=pl.DeviceIdType.MESH))
        return cps

    def plan(self, src_refs, dst_refs, send_sems, recv_sems, local_sems):
        cps = self.copies(src_refs, dst_refs, send_sems, recv_sems, local_sems)
        return cps, [], [], [cp.wait for cp in cps]


class _TwoLevelGather:
    def __init__(self, shard_rows, layer):
        self.na = len(shard_rows)
        self.shard_rows = shard_rows
        self.layer = layer

    def scratch(self):
        n = self.na * (N_DEV - 1)
        return [pltpu.SemaphoreType.DMA((n,)), pltpu.SemaphoreType.DMA((n,)), pltpu.SemaphoreType.DMA((self.na,))]

    def plan(self, src_refs, dst_refs, send_sems, recv_sems, local_sems):
        x, y, c, _ = _mesh_place()
        sibling = (x, y, 1 - c)
        chips = [(1 - x, y), (x, 1 - y), (1 - x, 1 - y)]
        start, mid_wait, mid_start, final = [], [], [], []
        for a in range(self.na):
            r = self.shard_rows[a]
            src = src_refs[a].at[self.layer]
            dst = dst_refs[a]

            def rows(px, py, pc, r=r, dst=dst):
                return dst.at[pl.ds(pl.multiple_of((4 * px + 2 * py + pc) * r, 64), r), :]

            def remote(k, s_ref, block, to, a=a, rows=rows):
                return pltpu.make_async_remote_copy(
                    src_ref=s_ref, dst_ref=rows(*block),
                    send_sem=send_sems.at[a * (N_DEV - 1) + k], recv_sem=recv_sems.at[a * (N_DEV - 1) + k],
                    device_id=to, device_id_type=pl.DeviceIdType.MESH)

            mine = pltpu.make_async_copy(src, rows(x, y, c), local_sems.at[a])
            own = [remote(0, src, (x, y, c), sibling)]
            own += [remote(1 + j, src, (x, y, c), (*chip, c)) for j, chip in enumerate(chips)]
            passed = [remote(4 + j, rows(*chip, c), (*chip, c), sibling) for j, chip in enumerate(chips)]
            start += [mine] + own
            mid_wait += own[1:]
            mid_start += passed
            final += [own[0].wait_recv] + [cp.wait_recv for cp in passed]
            final += [cp.wait_send for cp in own + passed] + [mine.wait]
        return start, mid_wait, mid_start, final


def _call(name, compute, grid, in_specs, out_specs, out_shape, scratch, args, push=None):
    sem = pltpu.CompilerParams(dimension_semantics=("arbitrary",) * len(grid))
    if push is None:
        return pl.pallas_call(compute, name=name, grid=grid, in_specs=in_specs, out_specs=out_specs,
                              out_shape=out_shape, scratch_shapes=scratch, compiler_params=sem)(*args)
    pushes, srcs, xshapes = push
    n_in, n_out, n_scr, na = len(args), len(out_shape), len(scratch), pushes.na
    hbm = pl.BlockSpec(memory_space=pltpu.HBM)

    def body(*refs):
        ins, refs = refs[:n_in], refs[n_in:]
        xin, refs = refs[:na], refs[na:]
        outs, refs = refs[:n_out], refs[n_out:]
        xout, refs = refs[:na], refs[na:]
        scr, sems = refs[:n_scr], refs[n_scr:]
        start, mid_wait, mid_start, final = pushes.plan(xin, xout, *sems)
        first = pl.program_id(0) == 0
        middle = pl.program_id(0) == (grid[0] * 5) // 8
        last = pl.program_id(0) == grid[0] - 1
        for d in range(1, len(grid)):
            first = first & (pl.program_id(d) == 0)
            middle = middle & (pl.program_id(d) == 0)
            last = last & (pl.program_id(d) == grid[d] - 1)

        @pl.when(first)
        def _():
            for cp in start:
                cp.start()

        if mid_start:
            @pl.when(middle)
            def _():
                for cp in mid_wait:
                    cp.wait_recv()
                for cp in mid_start:
                    cp.start()

        compute(*ins, *outs, *scr)

        @pl.when(last)
        def _():
            for wait in final:
                wait()

    return pl.pallas_call(
        body, name=name, grid=grid,
        in_specs=list(in_specs) + [hbm] * na, out_specs=list(out_specs) + [hbm] * na,
        out_shape=list(out_shape) + list(xshapes),
        scratch_shapes=list(scratch) + pushes.scratch(), compiler_params=sem)(*args, *srcs)


def _exchange(name, pushes, srcs, out_shapes):
    na = pushes.na
    hbm = pl.BlockSpec(memory_space=pltpu.HBM)

    def body(*refs):
        start, mid_wait, mid_start, final = pushes.plan(refs[:na], refs[na:2 * na], *refs[2 * na:])
        for cp in start:
            cp.start()
        for cp in mid_wait:
            cp.wait_recv()
        for cp in mid_start:
            cp.start()
        for wait in final:
            wait()

    return pl.pallas_call(body, name=name, in_specs=[hbm] * na, out_specs=[hbm] * na, out_shape=out_shapes,
                          scratch_shapes=pushes.scratch())(*srcs)


def _gather_rows(shard_rows, layer):
    return _TwoLevelGather(shard_rows, layer)


def _scatter_rows(shard_rows):
    def src_view(a, ref, idx):
        r = shard_rows[a]
        return ref.at[pl.ds(pl.multiple_of(idx * r, 64), r), :]

    def dst_view(a, ref, idx):
        return ref.at[idx]

    return _Pushes(len(shard_rows), src_view, dst_view)


def _gather_slots():
    return _Pushes(1, lambda a, ref, idx: ref, lambda a, ref, idx: ref.at[idx])


W_FULL = {W_IN_SHARD: jax.ShapeDtypeStruct((D_IN, D_MODEL), BF16),
          W_OUT_SHARD: jax.ShapeDtypeStruct((D_MODEL, D_MODEL), BF16)}


def _slots_shape(rows, cols, dtype):
    return jax.ShapeDtypeStruct((N_DEV, rows, cols), dtype)


def _sum_slots(slots, tr):
    _, r, c = slots.shape

    def compute(s_ref, o_ref):
        tot = s_ref[0].astype(F32)
        for d in range(1, N_DEV):
            tot = tot + s_ref[d].astype(F32)
        o_ref[...] = tot

    return _call("sum_slots", compute, (r // tr,),
                 [pl.BlockSpec((N_DEV, tr, c), lambda i: (0, i, 0))], [pl.BlockSpec((tr, c), lambda i: (i, 0))],
                 [jax.ShapeDtypeStruct((r, c), F32)], [], [slots])[0]


def _resident(shape):
    return pl.BlockSpec(shape, lambda *_: (0,) * len(shape), pipeline_mode=pl.Buffered(1))


def _norm_proj(x, g, w_t, push=None):
    s = x.shape[0]
    tm = _tile(TM_RESIDENT, s)

    def compute(x_ref, g_ref, w_ref, h_ref, p_ref):
        xf = x_ref[...]
        r = lax.rsqrt(jnp.mean(xf * xf, axis=-1, keepdims=True) + RMS_EPS)
        h = ((xf * r) * g_ref[...]).astype(BF16)
        h_ref[...] = h
        for j in range(D_IN // TN_PROJ):
            cols = slice(j * TN_PROJ, (j + 1) * TN_PROJ)
            p_ref[:, cols] = lax.dot_general(h, w_ref[cols, :], NT_DIMS, preferred_element_type=F32)

    return _call(
        "norm_proj", compute, (s // tm,),
        [pl.BlockSpec((tm, D_MODEL), lambda i: (i, 0)), pl.BlockSpec((1, D_MODEL), lambda i: (0, 0)),
         _resident((D_IN, D_MODEL))],
        [pl.BlockSpec((tm, D_MODEL), lambda i: (i, 0)), pl.BlockSpec((tm, D_IN), lambda i: (i, 0))],
        [jax.ShapeDtypeStruct((s, D_MODEL), BF16), jax.ShapeDtypeStruct((s, D_IN), F32)],
        [], [x, g, w_t], push)


def _sgu_weights(ws_ref, wtril_scr, wtril_t_scr=None):
    tril = _tril_mask()
    for hd in range(N_HEADS):
        w = jnp.where(tril, ws_ref[hd], 0.0)
        wtril_scr[BLK * hd:BLK * (hd + 1), :] = w.astype(BF16)
        if wtril_t_scr is not None:
            wtril_t_scr[hd // 2, :, BLK * (hd % 2):BLK * (hd % 2 + 1)] = w.T.astype(BF16)


def _kv_band(kt, p_ref, kvp_ref, kw_v, ones):
    kband = jnp.concatenate([kvp_ref[:, _cols(0, kt)], p_ref[:, _cols(C_K, kt)]], axis=0)
    kyhat, kr, kn = _half_rms(kband, kw_v, ones)
    vband = jnp.concatenate([kvp_ref[:, _cols(256, kt)], p_ref[:, _cols(C_V, kt)]], axis=0)
    return kyhat, kr, (kn * 0.125).astype(BF16), vband.astype(BF16)


def _stack_heads(tiles, halves):
    parts = []
    for tt, tile in enumerate(tiles):
        for qh in range(2):
            parts.append(_to_half(jnp.where(halves[qh], tile, 0.0), qh, tt // 2).astype(BF16))
    return jnp.concatenate(parts, axis=0)


def _unstack_heads(stacked, tt, h_a):
    return jnp.where(h_a, _to_half(stacked[_rows(2 * tt)], tt // 2, 0), _to_half(stacked[_rows(2 * tt + 1)], tt // 2, 1))


def _mixer_fwd(proj, qw, kw, sinks, slopes, w_s, bmap, push=None):
    s = proj.shape[0]
    nb = s // BLK

    def compute(sinks_ref, slopes_ref, p_ref, kvp_ref, qw_ref, kw_ref, ws_ref, bmap_ref, mix_ref,
                wtril_scr, bias_scr):
        n = pl.program_id(0)
        h_a, h_b = _lane_halves(BLK)
        ones = _half_masks()
        halves = (h_a, h_b)
        qw_v = qw_ref[...]
        kw_v = kw_ref[...]

        @pl.when(n == 0)
        def _():
            _sgu_weights(ws_ref, wtril_scr)

        @pl.when(n <= 1)
        def _():
            _alibi_table(n, slopes_ref, bias_scr)

        bands = [_kv_band(kt, p_ref, kvp_ref, kw_v, ones) for kt in range(2)]
        sc = []
        for kt in range(2):
            qn = [_half_rms(p_ref[:, _cols(C_Q, 4 * kt + tt)], qw_v, ones)[2] for tt in range(4)]
            sc.append(lax.dot_general(_stack_heads(qn, halves), bands[kt][2], NT_DIMS, preferred_element_type=F32))
        zu, mixed = [], []
        for j in range(8):
            zu_pre = p_ref[:, _cols(C_ZU, j)]
            zv_pre = p_ref[:, _cols(C_ZV, j)]
            zu.append(zu_pre * _gelu_cdf(zu_pre))
            zvb = (zv_pre * _gelu_cdf(zv_pre)).astype(BF16)
            mixed.append(jnp.dot(wtril_scr[2 * BLK * j:2 * BLK * (j + 1), :], zvb, preferred_element_type=F32))
        o = []
        for kt in range(2):
            p, _ = _softmax_sink(sc[kt], bias_scr[8 * BLK * kt:8 * BLK * (kt + 1), :], _sink_col(sinks_ref, kt))
            o.append(jnp.dot(p.astype(BF16), bands[kt][3], preferred_element_type=F32))
        for j in range(8):
            gb = p_ref[:, _cols(C_GB, j)]
            mx = jnp.where(h_a, mixed[j][0:BLK], mixed[j][BLK:2 * BLK]) + bmap_ref[:, _cols(0, j)]
            mix_ref[:, _cols(D_ATTN, j)] = ((zu[j] * mx) * (gb * _sigmoid(gb))).astype(BF16)
        for kt in range(2):
            for tt in range(4):
                j = 4 * kt + tt
                ga = p_ref[:, _cols(C_GA, j)]
                mix_ref[:, _cols(0, j)] = (_unstack_heads(o[kt], tt, h_a) * (ga * _sigmoid(ga))).astype(BF16)

    smem = pl.BlockSpec(memory_space=pltpu.SMEM)
    return _call(
        "mixer_fwd", compute, (nb,),
        [smem, smem,
         pl.BlockSpec((BLK, D_IN), lambda n: (n, 0)),
         pl.BlockSpec((BLK, 512), lambda n: (jnp.maximum(n - 1, 0), 2)),
         pl.BlockSpec((1, BLK), lambda n: (0, 0)),
         pl.BlockSpec((1, BLK), lambda n: (0, 0)),
         pl.BlockSpec((N_HEADS, BLK, BLK), lambda n: (0, 0, 0)),
         pl.BlockSpec((BLK, D_ATTN), lambda n: (0, 0))],
        [pl.BlockSpec((BLK, D_MODEL), lambda n: (n, 0))],
        [jax.ShapeDtypeStruct((s, D_MODEL), BF16)],
        [pltpu.VMEM((N_HEADS * BLK, BLK), BF16), pltpu.VMEM((N_HEADS * BLK, 2 * BLK), F32)],
        [sinks, slopes, proj, proj, qw, kw, w_s, bmap], push)


def _out_proj(mix, w_o, x):
    s = x.shape[0]
    tm = _tile(TM_STREAM, s)

    def compute(m_ref, w_ref, x_ref, o_ref):
        o_ref[...] = x_ref[...] + jnp.dot(m_ref[...], w_ref[...], preferred_element_type=F32)

    return _call(
        "out_proj", compute, (s // tm,),
        [pl.BlockSpec((tm, D_MODEL), lambda i: (i, 0)), _resident((D_MODEL, D_MODEL)),
         pl.BlockSpec((tm, D_MODEL), lambda i: (i, 0))],
        [pl.BlockSpec((tm, D_MODEL), lambda i: (i, 0))],
        [jax.ShapeDtypeStruct((s, D_MODEL), F32)], [], [mix, w_o, x])[0]


def _loss_grad(y, tgt):
    s = y.shape[0]
    tm = _tile(TM_STREAM, s)

    def compute(y_ref, t_ref, dy_ref, sq_ref):
        @pl.when(pl.program_id(0) == 0)
        def _():
            sq_ref[...] = jnp.zeros_like(sq_ref)

        e = y_ref[...] - t_ref[...]
        dy_ref[...] = e * (1.0 / D_MODEL)
        sq_ref[...] += jnp.sum(e * e, axis=0, keepdims=True)

    return _call(
        "loss_grad", compute, (s // tm,),
        [pl.BlockSpec((tm, D_MODEL), lambda i: (i, 0)), pl.BlockSpec((tm, D_MODEL), lambda i: (i, 0))],
        [pl.BlockSpec((tm, D_MODEL), lambda i: (i, 0)), pl.BlockSpec((1, D_MODEL), lambda i: (0, 0))],
        [jax.ShapeDtypeStruct((s, D_MODEL), F32), jax.ShapeDtypeStruct((1, D_MODEL), F32)], [], [y, tgt])


def _dmix(dx, w_o):
    s = dx.shape[0]
    tm = _tile(TM_STREAM, s)

    def compute(d_ref, w_ref, o_ref):
        o_ref[...] = lax.dot_general(d_ref[...].astype(BF16), w_ref[...], NT_DIMS, preferred_element_type=F32)

    return _call(
        "dmix", compute, (s // tm,),
        [pl.BlockSpec((tm, D_MODEL), lambda i: (i, 0)), _resident((D_MODEL, D_MODEL))],
        [pl.BlockSpec((tm, D_MODEL), lambda i: (i, 0))],
        [jax.ShapeDtypeStruct((s, D_MODEL), F32)], [], [dx, w_o])[0]


def _dw_out(mix, dx):
    s = dx.shape[0]
    tk = _tile(TM_STREAM, s)
    nk = s // tk

    def compute(m_ref, d_ref, o_ref, acc):
        k = pl.program_id(0)

        @pl.when(k == 0)
        def _():
            acc[...] = jnp.zeros_like(acc)

        acc[...] += lax.dot_general(m_ref[...], d_ref[...].astype(BF16), TN_DIMS, preferred_element_type=F32)

        @pl.when(k == nk - 1)
        def _():
            o_ref[...] = acc[...].astype(BF16)

    return _call(
        "dw_out", compute, (nk,),
        [pl.BlockSpec((tk, D_MODEL), lambda k: (k, 0)), pl.BlockSpec((tk, D_MODEL), lambda k: (k, 0))],
        [pl.BlockSpec((D_MODEL, D_MODEL), lambda k: (0, 0))],
        [jax.ShapeDtypeStruct((D_MODEL, D_MODEL), BF16)],
        [pltpu.VMEM((D_MODEL, D_MODEL), F32)], [mix, dx])[0]


def _mixer_bwd(proj, dmix, qw, kw, sinks, slopes, w_s, bmap, push=None):
    s = proj.shape[0]
    nb = s // BLK

    def compute(sinks_ref, slopes_ref, p_ref, kvp_ref, dm_ref, qw_ref, kw_ref, ws_ref, bmap_ref,
                dp_ref, dqw_ref, dkw_ref, dsk_ref, dws_ref, dbs_ref,
                pend, accq, acck, accs, accb, wtril_scr, wtril_t_scr, bias_scr):
        n = pl.program_id(0)
        h_a, h_b = _lane_halves(BLK)
        ones = _half_masks()
        halves = (h_a, h_b)
        lane = lax.broadcasted_iota(jnp.int32, (BLK, BLK), 1)

        @pl.when(n == 0)
        def _():
            accq[...] = jnp.zeros_like(accq)
            acck[...] = jnp.zeros_like(acck)
            accs[...] = jnp.zeros_like(accs)
            accb[...] = jnp.zeros_like(accb)
            dws_ref[...] = jnp.zeros_like(dws_ref)
            _sgu_weights(ws_ref, wtril_scr, wtril_t_scr)

        @pl.when(n >= 1)
        def _():
            dp_ref[:, 0:C_K] = pend[:, 0:C_K].astype(BF16)
            dp_ref[:, C_GA:D_IN] = pend[:, C_GA:D_IN].astype(BF16)

        @pl.when(n <= 1)
        def _():
            _alibi_table(n, slopes_ref, bias_scr)

        @pl.when(n < nb)
        def _():
            qw_v = qw_ref[...]
            kw_v = kw_ref[...]
            bands = [_kv_band(kt, p_ref, kvp_ref, kw_v, ones) for kt in range(2)]
            tiles, qst, dost, sc, dpm = [], [], [], [], []
            for kt in range(2):
                qn, d_o, tl = [], [], []
                for tt in range(4):
                    j = 4 * kt + tt
                    qyhat, qr, qn_t = _half_rms(p_ref[:, _cols(C_Q, j)], qw_v, ones)
                    ga = p_ref[:, _cols(C_GA, j)]
                    sg = _sigmoid(ga)
                    dma = dm_ref[:, _cols(0, j)]
                    qn.append(qn_t)
                    d_o.append(dma * (ga * sg))
                    tl.append((qyhat, qr, dma * (sg * (1.0 + ga * (1.0 - sg)))))
                tiles.append(tl)
                qst.append(_stack_heads(qn, halves))
                dost.append(_stack_heads(d_o, halves))
                sc.append(lax.dot_general(qst[kt], bands[kt][2], NT_DIMS, preferred_element_type=F32))
                dpm.append(lax.dot_general(dost[kt], bands[kt][3], NT_DIMS, preferred_element_type=F32))
            sgu = []
            for j in range(8):
                zu_pre = p_ref[:, _cols(C_ZU, j)]
                zv_pre = p_ref[:, _cols(C_ZV, j)]
                cu = _gelu_cdf(zu_pre)
                cv = _gelu_cdf(zv_pre)
                zvb = (zv_pre * cv).astype(BF16)
                sgu.append((zu_pre * cu, zvb, _gelu_grad(zu_pre, cu), _gelu_grad(zv_pre, cv),
                            jnp.dot(wtril_scr[2 * BLK * j:2 * BLK * (j + 1), :], zvb, preferred_element_type=F32)))
            dsink = jnp.zeros((BLK, BLK), F32)
            pst, dqkst = [], []
            for kt in range(2):
                p, p_sink = _softmax_sink(sc[kt], bias_scr[8 * BLK * kt:8 * BLK * (kt + 1), :],
                                          _sink_col(sinks_ref, kt))
                dsum = jnp.sum(p * dpm[kt], axis=-1, keepdims=True)
                dsink_col = -(p_sink * dsum)
                for i in range(8):
                    dsink = dsink + jnp.where(lane == 8 * kt + i, dsink_col[_rows(i)], 0.0)
                pst.append(p.astype(BF16))
                dqkst.append((p * (dpm[kt] - dsum)).astype(BF16))
            o, dqn_all, dvb, dkn = [], [], [], []
            for kt in range(2):
                o.append(jnp.dot(pst[kt], bands[kt][3], preferred_element_type=F32))
                dqn_all.append(jnp.dot(dqkst[kt], bands[kt][2], preferred_element_type=F32))
                dvb.append(lax.dot_general(pst[kt], dost[kt], TN_DIMS, preferred_element_type=F32))
                dkn.append(0.125 * lax.dot_general(dqkst[kt], qst[kt], TN_DIMS, preferred_element_type=F32))
            dms = []
            for j in range(8):
                zu, zvb, gu, gv, m_ab = sgu[j]
                gb = p_ref[:, _cols(C_GB, j)]
                dmb = dm_ref[:, _cols(D_ATTN, j)]
                mixed = jnp.where(h_a, m_ab[0:BLK], m_ab[BLK:2 * BLK]) + bmap_ref[:, _cols(0, j)]
                sgb = _sigmoid(gb)
                dgate = dmb * (gb * sgb)
                pend[:, _cols(C_ZU, j)] = (dgate * mixed) * gu
                pend[:, _cols(C_GB, j)] = (dmb * (zu * mixed)) * (sgb * (1.0 + gb * (1.0 - sgb)))
                dmixed = dgate * zu
                accb[:, _cols(0, j)] += dmixed
                dms.append(jnp.concatenate([jnp.where(h_a, dmixed, 0.0).astype(BF16),
                                            jnp.where(h_b, dmixed, 0.0).astype(BF16)], axis=0))
            dzv = []
            for j in range(8):
                dzv.append(jnp.dot(wtril_t_scr[j], dms[j], preferred_element_type=F32))
                dw_ab = lax.dot_general(dms[j], sgu[j][1], NT_DIMS, preferred_element_type=F32)
                dws_ref[2 * j] += dw_ab[0:BLK]
                dws_ref[2 * j + 1] += dw_ab[BLK:2 * BLK]
            dq_w = jnp.zeros((BLK, BLK), F32)
            for kt in range(2):
                for tt in range(4):
                    j = 4 * kt + tt
                    qyhat, qr, dsilu = tiles[kt][tt]
                    dqn = _unstack_heads(dqn_all[kt], tt, h_a)
                    pend[:, _cols(C_GA, j)] = _unstack_heads(o[kt], tt, h_a) * dsilu
                    pend[:, _cols(C_Q, j)] = _half_rms_bwd(dqn, qyhat, qr, qw_v, ones)
                    dq_w = dq_w + dqn * qyhat
            dk_w = jnp.zeros((BLK, BLK), F32)
            for kt in range(2):
                kyhat, kr = bands[kt][0], bands[kt][1]
                dk = _half_rms_bwd(dkn[kt], kyhat, kr, kw_v, ones)
                dkw_part = dkn[kt] * kyhat
                dk_w = dk_w + (dkw_part[0:BLK] + dkw_part[BLK:2 * BLK])
                dv = dvb[kt]

                @pl.when(n >= 1)
                def _():
                    dp_ref[:, _cols(C_K, kt)] = (pend[:, _cols(C_K, kt)] + dk[0:BLK]).astype(BF16)
                    dp_ref[:, _cols(C_V, kt)] = (pend[:, _cols(C_V, kt)] + dv[0:BLK]).astype(BF16)

                pend[:, _cols(C_K, kt)] = dk[BLK:2 * BLK]
                pend[:, _cols(C_V, kt)] = dv[BLK:2 * BLK]
            accq[...] += dq_w
            acck[...] += dk_w
            accs[...] += dsink
            for j in range(8):
                pend[:, _cols(C_ZV, j)] = dzv[j] * sgu[j][3]

        @pl.when(n == nb)
        def _():
            tril = _tril_mask()
            dp_ref[:, C_K:C_GA] = pend[:, C_K:C_GA].astype(BF16)
            aq = accq[...]
            ak = acck[...]
            dqw_ref[...] = jnp.sum(aq + pltpu.roll(aq, HALF, 1), axis=0, keepdims=True)
            dkw_ref[...] = jnp.sum(ak + pltpu.roll(ak, HALF, 1), axis=0, keepdims=True)
            dsk_ref[...] = jnp.sum(accs[...], axis=0, keepdims=True)
            for hd in range(N_HEADS):
                dws_ref[hd] = jnp.where(tril, dws_ref[hd], 0.0)
            hrow = lax.broadcasted_iota(jnp.int32, (N_HEADS, D_ATTN), 0)
            hcol = lax.broadcasted_iota(jnp.int32, (N_HEADS, D_ATTN), 1)
            sel = jnp.where((hcol >= hrow * HALF) & (hcol < (hrow + 1) * HALF), 1.0, 0.0).astype(BF16)
            rem = accb[...]
            tot = jnp.zeros((N_HEADS, BLK), F32)
            for _ in range(3):
                part = rem.astype(BF16)
                tot = tot + lax.dot_general(sel, part, NT_DIMS, preferred_element_type=F32)
                rem = rem - part.astype(F32)
            dbs_ref[...] = tot

    smem = pl.BlockSpec(memory_space=pltpu.SMEM)
    last = nb - 1
    tile_f32 = pltpu.VMEM((BLK, BLK), F32)
    return _call(
        "mixer_bwd", compute, (nb + 1,),
        [smem, smem,
         pl.BlockSpec((BLK, D_IN), lambda n: (jnp.minimum(n, last), 0)),
         pl.BlockSpec((BLK, 512), lambda n: (jnp.maximum(jnp.minimum(n, last) - 1, 0), 2)),
         pl.BlockSpec((BLK, D_MODEL), lambda n: (jnp.minimum(n, last), 0)),
         pl.BlockSpec((1, BLK), lambda n: (0, 0)),
         pl.BlockSpec((1, BLK), lambda n: (0, 0)),
         pl.BlockSpec((N_HEADS, BLK, BLK), lambda n: (0, 0, 0)),
         pl.BlockSpec((BLK, D_ATTN), lambda n: (0, 0))],
        [pl.BlockSpec((BLK, D_IN), lambda n: (jnp.maximum(n - 1, 0), 0)),
         pl.BlockSpec((1, BLK), lambda n: (0, 0)),
         pl.BlockSpec((1, BLK), lambda n: (0, 0)),
         pl.BlockSpec((1, BLK), lambda n: (0, 0)),
         pl.BlockSpec((N_HEADS, BLK, BLK), lambda n: (0, 0, 0)),
         pl.BlockSpec((N_HEADS, BLK), lambda n: (0, 0))],
        [jax.ShapeDtypeStruct((s, D_IN), BF16),
         jax.ShapeDtypeStruct((1, BLK), F32),
         jax.ShapeDtypeStruct((1, BLK), F32),
         jax.ShapeDtypeStruct((1, BLK), F32),
         jax.ShapeDtypeStruct((N_HEADS, BLK, BLK), F32),
         jax.ShapeDtypeStruct((N_HEADS, BLK), F32)],
        [pltpu.VMEM((BLK, D_IN), F32), tile_f32, tile_f32, tile_f32, pltpu.VMEM((BLK, D_ATTN), F32),
         pltpu.VMEM((N_HEADS * BLK, BLK), BF16), pltpu.VMEM((N_HEADS // 2, BLK, 2 * BLK), BF16),
         pltpu.VMEM((N_HEADS * BLK, 2 * BLK), F32)],
        [sinks, slopes, proj, proj, dmix, qw, kw, w_s, bmap], push)


def _dh_norm_bwd(dproj, w_t, x, dx_out, g, push=None):
    s = x.shape[0]
    tm = _tile(TM_RESIDENT, s)

    def compute(dp_ref, w_ref, x_ref, dxo_ref, g_ref, dx_ref, dg_ref):
        @pl.when(pl.program_id(0) == 0)
        def _():
            dg_ref[...] = jnp.zeros_like(dg_ref)

        dh = jnp.dot(dp_ref[...], w_ref[...], preferred_element_type=F32)
        xf = x_ref[...]
        r = lax.rsqrt(jnp.mean(xf * xf, axis=-1, keepdims=True) + RMS_EPS)
        yhat = xf * r
        dyh = dh * g_ref[...]
        c = jnp.mean(dyh * yhat, axis=-1, keepdims=True)
        dx_ref[...] = dxo_ref[...] + r * (dyh - yhat * c)
        dg_ref[...] += jnp.sum(dh * yhat, axis=0, keepdims=True)

    return _call(
        "dh_norm_bwd", compute, (s // tm,),
        [pl.BlockSpec((tm, D_IN), lambda i: (i, 0)), _resident((D_IN, D_MODEL)),
         pl.BlockSpec((tm, D_MODEL), lambda i: (i, 0)), pl.BlockSpec((tm, D_MODEL), lambda i: (i, 0)),
         pl.BlockSpec((1, D_MODEL), lambda i: (0, 0))],
        [pl.BlockSpec((tm, D_MODEL), lambda i: (i, 0)), pl.BlockSpec((1, D_MODEL), lambda i: (0, 0))],
        [jax.ShapeDtypeStruct((s, D_MODEL), F32), jax.ShapeDtypeStruct((1, D_MODEL), F32)],
        [], [dproj, w_t, x, dx_out, g], push)


def _dw_in(dproj, h, push=None):
    s = h.shape[0]
    tk = _tile(DW_IN_TOK, s)
    nk = s // tk

    def compute(dp_ref, h_ref, o_ref, acc):
        k = pl.program_id(1)

        @pl.when(k == 0)
        def _():
            acc[...] = jnp.zeros_like(acc)

        acc[...] += lax.dot_general(dp_ref[...], h_ref[...], TN_DIMS, preferred_element_type=F32)

        @pl.when(k == nk - 1)
        def _():
            o_ref[...] = acc[...].astype(BF16)

    return _call(
        "dw_in", compute, (D_IN // DW_IN_ROWS, nk),
        [pl.BlockSpec((tk, DW_IN_ROWS), lambda j, k: (k, j)), pl.BlockSpec((tk, D_MODEL), lambda j, k: (k, 0))],
        [pl.BlockSpec((DW_IN_ROWS, D_MODEL), lambda j, k: (j, 0))],
        [jax.ShapeDtypeStruct((D_IN, D_MODEL), BF16)],
        [pltpu.VMEM((DW_IN_ROWS, D_MODEL), F32)], [dproj, h], push)


def _adamw(name, w, g, m, v, tr=None):
    shape = w.shape
    c = shape[-1]
    flat = [a.reshape(-1, c) for a in (w, g, m, v)]
    r = flat[0].shape[0]
    tr = r if tr is None else tr

    def compute(w_ref, g_ref, m_ref, v_ref, d_ref, mo_ref, vo_ref):
        gv = g_ref[...]
        m_new = ADAM_B1 * m_ref[...] + (1.0 - ADAM_B1) * gv
        v_new = ADAM_B2 * v_ref[...] + (1.0 - ADAM_B2) * jnp.square(gv)
        m_hat = m_new / (1.0 - ADAM_B1 ** ADAM_STEP)
        v_hat = v_new / (1.0 - ADAM_B2 ** ADAM_STEP)
        d_ref[...] = -ADAM_LR * (m_hat / (jnp.sqrt(v_hat) + ADAM_EPS) + ADAM_WD * w_ref[...])
        mo_ref[...] = m_new
        vo_ref[...] = v_new

    spec = pl.BlockSpec((tr, c), lambda i: (i, 0))
    outs = _call(name, compute, (r // tr,), [spec] * 4, [spec] * 3, [jax.ShapeDtypeStruct((r, c), F32)] * 3,
                 [], flat)
    return [o.reshape(shape) for o in outs]


def _pack_rows(parts):
    rows = []
    for a in parts:
        flat = a.reshape(-1)
        n = -(-flat.shape[0] // (8 * BLK)) * 8
        rows.append(jnp.pad(flat, (0, n * BLK - flat.shape[0])).reshape(n, BLK))
    return jnp.concatenate(rows, axis=0)


def _unpack_rows(packed, like):
    out = []
    row = 0
    for a in like:
        n = -(-a.size // (8 * BLK)) * 8
        out.append(packed[row:row + n].reshape(-1)[:a.size].reshape(a.shape))
        row += n
    return out


def kernel(x, norm_g, w_in, q_norm, k_norm, sinks, w_s, b_s, w_out, loss_target, m_norm_g, m_w_in, m_q_norm, m_k_norm, m_sinks, m_w_s, m_b_s, m_w_out, v_norm_g, v_w_in, v_q_norm, v_k_norm, v_sinks, v_w_s, v_b_s, v_w_out):
    xs = x[0]
    tgt = loss_target[0]
    slopes = jnp.asarray(2.0 ** (-8.0 * np.arange(1, N_HEADS + 1) / N_HEADS), dtype=F32)
    wt_sh = jnp.swapaxes(w_in, 1, 2).astype(BF16)
    wo_sh = w_out.astype(BF16)

    layer_par = []
    for l in range(DEPTH):
        layer_par.append((jnp.tile(q_norm[l], 2)[None, :], jnp.tile(k_norm[l], 2)[None, :],
                          jnp.repeat(b_s[l].T, HALF, axis=1)))

    wt_full, wo_full = _exchange("gather_w0", _gather_rows((W_IN_SHARD, W_OUT_SHARD), 0), [wt_sh, wo_sh],
                                 [W_FULL[W_IN_SHARD], W_FULL[W_OUT_SHARD]])
    saved = []
    cur = xs
    for l in range(DEPTH):
        qw, kw, bmap = layer_par[l]
        more = l + 1 < DEPTH
        res = _norm_proj(cur, norm_g[l][None, :], wt_full,
                         (_gather_rows((W_IN_SHARD,), l + 1), [wt_sh], [W_FULL[W_IN_SHARD]]) if more else None)
        h, proj = res[0], res[1]
        res2 = _mixer_fwd(proj, qw, kw, sinks[l], slopes, w_s[l], bmap,
                          (_gather_rows((W_OUT_SHARD,), l + 1), [wo_sh], [W_FULL[W_OUT_SHARD]]) if more else None)
        mix = res2[0]
        nxt = _out_proj(mix, wo_full, cur)
        saved.append((cur, h, proj, mix, wt_full, wo_full))
        cur = nxt
        if more:
            wt_full, wo_full = res[2], res2[1]

    dx, sq = _loss_grad(cur, tgt)
    loss = lax.psum(0.5 * jnp.sum(sq) / D_MODEL, MESH_AXES)

    g_wt, g_wo, g_small, g_norm = ([None] * DEPTH for _ in range(4))
    small_like = None
    for l in reversed(range(DEPTH)):
        x_l, h, proj, mix, wt_l, wo_l = saved[l]
        qw, kw, bmap = layer_par[l]
        dmix = _dmix(dx, wo_l)
        dwo_part = _dw_out(mix, dx)
        dproj, dqw, dkw, dsk, dws, dbs, wo_slots = _mixer_bwd(
            proj, dmix, qw, kw, sinks[l], slopes, w_s[l], bmap,
            (_scatter_rows((W_OUT_SHARD,)), [dwo_part], [_slots_shape(W_OUT_SHARD, D_MODEL, BF16)]))
        small_like = [dqw[0, :HALF], dkw[0, :HALF], dsk[0, :N_HEADS], dws, dbs]
        packed = _pack_rows(small_like)
        dwt_part, small_slots = _dw_in(dproj, h, (_gather_slots(), [packed], [_slots_shape(*packed.shape, F32)]))
        dx, dng, wt_slots = _dh_norm_bwd(
            dproj, wt_l, x_l, dx, norm_g[l][None, :],
            (_scatter_rows((W_IN_SHARD,)), [dwt_part], [_slots_shape(W_IN_SHARD, D_MODEL, BF16)]))
        g_wt[l] = _sum_slots(wt_slots, W_IN_SHARD // 2)
        g_wo[l] = _sum_slots(wo_slots, W_OUT_SHARD)
        g_small[l] = _unpack_rows(_sum_slots(small_slots, packed.shape[0]), small_like)
        g_norm[l] = dng[0]

    dng_all = _pack_rows([jnp.stack(g_norm)])
    dng_slots = _exchange("gather_dnorm", _gather_slots(), [dng_all], [_slots_shape(*dng_all.shape, F32)])[0]
    gr_norm = _unpack_rows(_sum_slots(dng_slots, dng_all.shape[0]), [norm_g])[0]
    gr_qn, gr_kn, gr_sk, gr_ws, gr_bs = (jnp.stack([g_small[l][i] for l in range(DEPTH)]) for i in range(5))
    gr_w_in = jnp.swapaxes(jnp.stack(g_wt), 1, 2)
    gr_w_out = jnp.stack(g_wo)

    grads = [gr_norm, gr_w_in, gr_qn, gr_kn, gr_sk, gr_ws, gr_bs, gr_w_out]
    weights = [norm_g, w_in, q_norm, k_norm, sinks, w_s, b_s, w_out]
    moms = [m_norm_g, m_w_in, m_q_norm, m_k_norm, m_sinks, m_w_s, m_b_s, m_w_out]
    vels = [v_norm_g, v_w_in, v_q_norm, v_k_norm, v_sinks, v_w_s, v_b_s, v_w_out]
    tiles = [None, 512, None, None, None, 1024, None, 256]
    names = ["norm_g", "w_in", "q_norm", "k_norm", "sinks", "w_s", "b_s", "w_out"]
    deltas, new_m, new_v = [], [], []
    for nm, w, g, m, v, tr in zip(names, weights, grads, moms, vels, tiles):
        d, mo, vo = _adamw("adamw_" + nm, w, g, m, v, tr)
        deltas.append(d)
        new_m.append(mo)
        new_v.append(vo)

    return (loss, dx[None], *grads, *deltas, *new_m, *new_v)
```

```python
import numpy as np
import jax
import jax.numpy as jnp
from jax import lax
from jax.experimental import pallas as pl
from jax.experimental.pallas import tpu as pltpu

F32 = jnp.float32
BF16 = jnp.bfloat16

D_MODEL = 2048
D_ATTN = 1024
D_IN = 5632
N_HEADS = 16
DEPTH = 4
BLK = 128
HALF = 64
RMS_EPS = 1e-6
C_Q, C_K, C_V, C_GA, C_ZU, C_ZV, C_GB = 0, 1024, 1280, 1536, 2560, 3584, 4608
NEG = -1e30
N_DEV = 8
W_IN_SHARD = D_IN // N_DEV
W_OUT_SHARD = D_MODEL // N_DEV
INV_SQRT2 = 0.7071067811865476
INV_SQRT_2PI = 0.3989422804014327

TM_RESIDENT = 256
TM_STREAM = 512
TN_PROJ = 512
DW_IN_ROWS = D_IN // 4
DW_IN_TOK = 1024

ADAM_LR = 0.001
ADAM_B1 = 0.9
ADAM_B2 = 0.999
ADAM_EPS = 1e-08
ADAM_WD = 0.01
ADAM_STEP = 10

NT_DIMS = (((1,), (1,)), ((), ()))
TN_DIMS = (((0,), (0,)), ((), ()))
MESH_AXES = ("x", "y", "c")


def _sigmoid(v):
    return 1.0 / (1.0 + jnp.exp(-v))


def _gelu_cdf(z):
    return 0.5 * (1.0 + lax.erf(z * INV_SQRT2))


def _gelu_grad(z, cdf):
    return cdf + z * (jnp.exp(-0.5 * z * z) * INV_SQRT_2PI)


def _lane_halves(rows):
    lane = lax.broadcasted_iota(jnp.int32, (rows, BLK), 1)
    return lane < HALF, lane >= HALF


def _half_masks():
    return {BLK: _lane_halves(BLK), 2 * BLK: _lane_halves(2 * BLK)}


def _half_sum(v, ones):
    h_a, h_b = ones[v.shape[0]]
    s_a = jnp.sum(jnp.where(h_a, v, 0.0), axis=-1, keepdims=True)
    s_b = jnp.sum(jnp.where(h_b, v, 0.0), axis=-1, keepdims=True)
    return jnp.where(h_a, s_a, s_b)


def _half_rms(v, w, ones):
    r = lax.rsqrt(_half_sum(v * v, ones) * (1.0 / HALF) + RMS_EPS)
    yhat = v * r
    return yhat, r, yhat * w


def _half_rms_bwd(dy, yhat, r, w, ones):
    dyh = dy * w
    c = _half_sum(dyh * yhat, ones) * (1.0 / HALF)
    return r * (dyh - yhat * c)


def _band_mask(n):
    t = lax.broadcasted_iota(jnp.int32, (BLK, 2 * BLK), 0)
    kk = lax.broadcasted_iota(jnp.int32, (BLK, 2 * BLK), 1)
    dist = t + BLK - kk
    first_key = jnp.where(n > 0, 0, BLK)
    ok = (dist >= 0) & (dist < BLK) & (kk >= first_key)
    return ok, dist.astype(F32)


def _alibi_table(n, slopes_ref, bias_scr):
    ok, distf = _band_mask(n)
    for hd in range(N_HEADS):
        bias_scr[BLK * hd:BLK * (hd + 1), :] = jnp.where(ok, -(slopes_ref[hd] * distf), NEG)


def _sink_col(sinks_ref, kt):
    return jnp.concatenate([jnp.full((BLK, 1), sinks_ref[8 * kt + i], F32) for i in range(8)], axis=0)


def _softmax_sink(s_scaled, bias, sink):
    s = s_scaled + bias
    m = jnp.maximum(jnp.max(s, axis=-1, keepdims=True), sink)
    p = jnp.exp(s - m)
    es = jnp.exp(sink - m)
    inv = 1.0 / (jnp.sum(p, axis=-1, keepdims=True) + es)
    return p * inv, es * inv


def _rows(i):
    return slice(BLK * i, BLK * (i + 1))


def _cols(base, j):
    return slice(base + BLK * j, base + BLK * (j + 1))


def _to_half(v, have, want):
    return v if have == want else pltpu.roll(v, HALF, 1)


def _tril_mask():
    row = lax.broadcasted_iota(jnp.int32, (BLK, BLK), 0)
    col = lax.broadcasted_iota(jnp.int32, (BLK, BLK), 1)
    return row >= col


def _tile(limit, s):
    t = min(limit, s)
    assert s % t == 0, (s, t)
    return t


def _mesh_place():
    x, y, c = lax.axis_index("x"), lax.axis_index("y"), lax.axis_index("c")
    return x, y, c, 4 * x + 2 * y + c


def _peer(x, y, c, k):
    px = 1 - x if k & 4 else x
    py = 1 - y if k & 2 else y
    pc = 1 - c if k & 1 else c
    return (px, py, pc), 4 * px + 2 * py + pc


class _Pushes:
    def __init__(self, n_arrays, src_view, dst_view):
        self.na = n_arrays
        self.src_view = src_view
        self.dst_view = dst_view

    def scratch(self):
        n = self.na * (N_DEV - 1)
        return [pltpu.SemaphoreType.DMA((n,)), pltpu.SemaphoreType.DMA((n,)), pltpu.SemaphoreType.DMA((self.na,))]

    def copies(self, src_refs, dst_refs, send_sems, recv_sems, local_sems):
        x, y, c, me = _mesh_place()
        cps = []
        for a in range(self.na):
            cps.append(pltpu.make_async_copy(self.src_view(a, src_refs[a], me), self.dst_view(a, dst_refs[a], me),
                                             local_sems.at[a]))
        for k in range(1, N_DEV):
            peer, pidx = _peer(x, y, c, k)
            for a in range(self.na):
                sem = a * (N_DEV - 1) + k - 1
                cps.append(pltpu.make_async_remote_copy(
                    src_ref=self.src_view(a, src_refs[a], pidx), dst_ref=self.dst_view(a, dst_refs[a], me),
                    send_sem=send_sems.at[sem], recv_sem=recv_sems.at[sem],
                    device_id=peer, device_id_type=pl.DeviceIdType.MESH))
        return cps

    def plan(self, src_refs, dst_refs, send_sems, recv_sems, local_sems):
        cps = self.copies(src_refs, dst_refs, send_sems, recv_sems, local_sems)
        return cps, [], [], [cp.wait for cp in cps]


class _TwoLevelGather:
    def __init__(self, shard_rows, layer):
        self.na = len(shard_rows)
        self.shard_rows = shard_rows
        self.layer = layer

    def scratch(self):
        n = self.na * (N_DEV - 1)
        return [pltpu.SemaphoreType.DMA((n,)), pltpu.SemaphoreType.DMA((n,)), pltpu.SemaphoreType.DMA((self.na,))]

    def plan(self, src_refs, dst_refs, send_sems, recv_sems, local_sems):
        x, y, c, _ = _mesh_place()
        sibling = (x, y, 1 - c)
        chips = [(1 - x, y), (x, 1 - y), (1 - x, 1 - y)]
        start, mid_wait, mid_start, final = [], [], [], []
        for a in range(self.na):
            r = self.shard_rows[a]
            src = src_refs[a].at[self.layer]
            dst = dst_refs[a]

            def rows(px, py, pc, r=r, dst=dst):
                return dst.at[pl.ds(pl.multiple_of((4 * px + 2 * py + pc) * r, 64), r), :]

            def remote(k, s_ref, block, to, a=a, rows=rows):
                return pltpu.make_async_remote_copy(
                    src_ref=s_ref, dst_ref=rows(*block),
                    send_sem=send_sems.at[a * (N_DEV - 1) + k], recv_sem=recv_sems.at[a * (N_DEV - 1) + k],
                    device_id=to, device_id_type=pl.DeviceIdType.MESH)

            mine = pltpu.make_async_copy(src, rows(x, y, c), local_sems.at[a])
            own = [remote(0, src, (x, y, c), sibling)]
            own += [remote(1 + j, src, (x, y, c), (*chip, c)) for j, chip in enumerate(chips)]
            passed = [remote(4 + j, rows(*chip, c), (*chip, c), sibling) for j, chip in enumerate(chips)]
            start += [mine] + own
            mid_wait += own[1:]
            mid_start += passed
            final += [own[0].wait_recv] + [cp.wait_recv for cp in passed]
            final += [cp.wait_send for cp in own + passed] + [mine.wait]
        return start, mid_wait, mid_start, final


def _call(name, compute, grid, in_specs, out_specs, out_shape, scratch, args, push=None):
    sem = pltpu.CompilerParams(dimension_semantics=("arbitrary",) * len(grid))
    if push is None:
        return pl.pallas_call(compute, name=name, grid=grid, in_specs=in_specs, out_specs=out_specs,
                              out_shape=out_shape, scratch_shapes=scratch, compiler_params=sem)(*args)
    pushes, srcs, xshapes = push
    n_in, n_out, n_scr, na = len(args), len(out_shape), len(scratch), pushes.na
    hbm = pl.BlockSpec(memory_space=pltpu.HBM)

    def body(*refs):
        ins, refs = refs[:n_in], refs[n_in:]
        xin, refs = refs[:na], refs[na:]
        outs, refs = refs[:n_out], refs[n_out:]
        xout, refs = refs[:na], refs[na:]
        scr, sems = refs[:n_scr], refs[n_scr:]
        start, mid_wait, mid_start, final = pushes.plan(xin, xout, *sems)
        first = pl.program_id(0) == 0
        middle = pl.program_id(0) == (grid[0] * 5) // 8
        last = pl.program_id(0) == grid[0] - 1
        for d in range(1, len(grid)):
            first = first & (pl.program_id(d) == 0)
            middle = middle & (pl.program_id(d) == 0)
            last = last & (pl.program_id(d) == grid[d] - 1)

        @pl.when(first)
        def _():
            for cp in start:
                cp.start()

        if mid_start:
            @pl.when(middle)
            def _():
                for cp in mid_wait:
                    cp.wait_recv()
                for cp in mid_start:
                    cp.start()

        compute(*ins, *outs, *scr)

        @pl.when(last)
        def _():
            for wait in final:
                wait()

    return pl.pallas_call(
        body, name=name, grid=grid,
        in_specs=list(in_specs) + [hbm] * na, out_specs=list(out_specs) + [hbm] * na,
        out_shape=list(out_shape) + list(xshapes),
        scratch_shapes=list(scratch) + pushes.scratch(), compiler_params=sem)(*args, *srcs)


def _exchange(name, pushes, srcs, out_shapes):
    na = pushes.na
    hbm = pl.BlockSpec(memory_space=pltpu.HBM)

    def body(*refs):
        start, mid_wait, mid_start, final = pushes.plan(refs[:na], refs[na:2 * na], *refs[2 * na:])
        for cp in start:
            cp.start()
        for cp in mid_wait:
            cp.wait_recv()
        for cp in mid_start:
            cp.start()
        for wait in final:
            wait()

    return pl.pallas_call(body, name=name, in_specs=[hbm] * na, out_specs=[hbm] * na, out_shape=out_shapes,
                          scratch_shapes=pushes.scratch())(*srcs)


def _gather_rows(shard_rows, layer):
    return _TwoLevelGather(shard_rows, layer)


def _scatter_rows(shard_rows):
    def src_view(a, ref, idx):
        r = shard_rows[a]
        return ref.at[pl.ds(pl.multiple_of(idx * r, 64), r), :]

    def dst_view(a, ref, idx):
        return ref.at[idx]

    return _Pushes(len(shard_rows), src_view, dst_view)


def _gather_slots():
    return _Pushes(1, lambda a, ref, idx: ref, lambda a, ref, idx: ref.at[idx])


def _gather_slots_and_scatter_rows(rows):
    def src_view(a, ref, idx):
        return ref if a == 0 else ref.at[pl.ds(pl.multiple_of(idx * rows, 64), rows), :]

    return _Pushes(2, src_view, lambda a, ref, idx: ref.at[idx])


W_FULL = {W_IN_SHARD: jax.ShapeDtypeStruct((D_IN, D_MODEL), BF16),
          W_OUT_SHARD: jax.ShapeDtypeStruct((D_MODEL, D_MODEL), BF16)}


def _slots_shape(rows, cols, dtype):
    return jax.ShapeDtypeStruct((N_DEV, rows, cols), dtype)


def _sum_slots(slots, tr):
    _, r, c = slots.shape

    def compute(s_ref, o_ref):
        tot = s_ref[0].astype(F32)
        for d in range(1, N_DEV):
            tot = tot + s_ref[d].astype(F32)
        o_ref[...] = tot

    return _call("sum_slots", compute, (r // tr,),
                 [pl.BlockSpec((N_DEV, tr, c), lambda i: (0, i, 0))], [pl.BlockSpec((tr, c), lambda i: (i, 0))],
                 [jax.ShapeDtypeStruct((r, c), F32)], [], [slots])[0]


def _resident(shape):
    return pl.BlockSpec(shape, lambda *_: (0,) * len(shape), pipeline_mode=pl.Buffered(1))


def _norm_proj(x, g, w_t, push=None):
    s = x.shape[0]
    tm = _tile(TM_RESIDENT, s)

    def compute(x_ref, g_ref, w_ref, h_ref, p_ref):
        xf = x_ref[...]
        r = lax.rsqrt(jnp.mean(xf * xf, axis=-1, keepdims=True) + RMS_EPS)
        h = ((xf * r) * g_ref[...]).astype(BF16)
        h_ref[...] = h
        for j in range(D_IN // TN_PROJ):
            cols = slice(j * TN_PROJ, (j + 1) * TN_PROJ)
            p_ref[:, cols] = lax.dot_general(h, w_ref[cols, :], NT_DIMS, preferred_element_type=F32)

    return _call(
        "norm_proj", compute, (s // tm,),
        [pl.BlockSpec((tm, D_MODEL), lambda i: (i, 0)), pl.BlockSpec((1, D_MODEL), lambda i: (0, 0)),
         _resident((D_IN, D_MODEL))],
        [pl.BlockSpec((tm, D_MODEL), lambda i: (i, 0)), pl.BlockSpec((tm, D_IN), lambda i: (i, 0))],
        [jax.ShapeDtypeStruct((s, D_MODEL), BF16), jax.ShapeDtypeStruct((s, D_IN), F32)],
        [], [x, g, w_t], push)


def _sgu_weights(ws_ref, wtril_scr, wtril_t_scr=None):
    tril = _tril_mask()
    for hd in range(N_HEADS):
        w = jnp.where(tril, ws_ref[hd], 0.0)
        wtril_scr[BLK * hd:BLK * (hd + 1), :] = w.astype(BF16)
        if wtril_t_scr is not None:
            wtril_t_scr[hd // 2, :, BLK * (hd % 2):BLK * (hd % 2 + 1)] = w.T.astype(BF16)


def _kv_band(kt, p_ref, kvp_ref, kw_v, ones):
    kband = jnp.concatenate([kvp_ref[:, _cols(0, kt)], p_ref[:, _cols(C_K, kt)]], axis=0)
    kyhat, kr, kn = _half_rms(kband, kw_v, ones)
    vband = jnp.concatenate([kvp_ref[:, _cols(256, kt)], p_ref[:, _cols(C_V, kt)]], axis=0)
    return kyhat, kr, (kn * 0.125).astype(BF16), vband.astype(BF16)


def _stack_heads(tiles, halves):
    parts = []
    for tt, tile in enumerate(tiles):
        for qh in range(2):
            parts.append(_to_half(jnp.where(halves[qh], tile, 0.0), qh, tt // 2).astype(BF16))
    return jnp.concatenate(parts, axis=0)


def _unstack_heads(stacked, tt, h_a):
    return jnp.where(h_a, _to_half(stacked[_rows(2 * tt)], tt // 2, 0), _to_half(stacked[_rows(2 * tt + 1)], tt // 2, 1))


def _mixer_fwd(proj, qw, kw, sinks, slopes, w_s, bmap, push=None):
    s = proj.shape[0]
    nb = s // BLK

    def compute(sinks_ref, slopes_ref, p_ref, kvp_ref, qw_ref, kw_ref, ws_ref, bmap_ref, mix_ref,
                wtril_scr, bias_scr):
        n = pl.program_id(0)
        h_a, h_b = _lane_halves(BLK)
        ones = _half_masks()
        halves = (h_a, h_b)
        qw_v = qw_ref[...]
        kw_v = kw_ref[...]

        @pl.when(n == 0)
        def _():
            _sgu_weights(ws_ref, wtril_scr)

        @pl.when(n <= 1)
        def _():
            _alibi_table(n, slopes_ref, bias_scr)

        bands = [_kv_band(kt, p_ref, kvp_ref, kw_v, ones) for kt in range(2)]
        sc = []
        for kt in range(2):
            qn = [_half_rms(p_ref[:, _cols(C_Q, 4 * kt + tt)], qw_v, ones)[2] for tt in range(4)]
            sc.append(lax.dot_general(_stack_heads(qn, halves), bands[kt][2], NT_DIMS, preferred_element_type=F32))
        zu, mixed = [], []
        for j in range(8):
            zu_pre = p_ref[:, _cols(C_ZU, j)]
            zv_pre = p_ref[:, _cols(C_ZV, j)]
            zu.append(zu_pre * _gelu_cdf(zu_pre))
            zvb = (zv_pre * _gelu_cdf(zv_pre)).astype(BF16)
            mixed.append(jnp.dot(wtril_scr[2 * BLK * j:2 * BLK * (j + 1), :], zvb, preferred_element_type=F32))
        o = []
        for kt in range(2):
            p, _ = _softmax_sink(sc[kt], bias_scr[8 * BLK * kt:8 * BLK * (kt + 1), :], _sink_col(sinks_ref, kt))
            o.append(jnp.dot(p.astype(BF16), bands[kt][3], preferred_element_type=F32))
        for j in range(8):
            gb = p_ref[:, _cols(C_GB, j)]
            mx = jnp.where(h_a, mixed[j][0:BLK], mixed[j][BLK:2 * BLK]) + bmap_ref[:, _cols(0, j)]
            mix_ref[:, _cols(D_ATTN, j)] = ((zu[j] * mx) * (gb * _sigmoid(gb))).astype(BF16)
        for kt in range(2):
            for tt in range(4):
                j = 4 * kt + tt
                ga = p_ref[:, _cols(C_GA, j)]
                mix_ref[:, _cols(0, j)] = (_unstack_heads(o[kt], tt, h_a) * (ga * _sigmoid(ga))).astype(BF16)

    smem = pl.BlockSpec(memory_space=pltpu.SMEM)
    return _call(
        "mixer_fwd", compute, (nb,),
        [smem, smem,
         pl.BlockSpec((BLK, D_IN), lambda n: (n, 0)),
         pl.BlockSpec((BLK, 512), lambda n: (jnp.maximum(n - 1, 0), 2)),
         pl.BlockSpec((1, BLK), lambda n: (0, 0)),
         pl.BlockSpec((1, BLK), lambda n: (0, 0)),
         pl.BlockSpec((N_HEADS, BLK, BLK), lambda n: (0, 0, 0)),
         pl.BlockSpec((BLK, D_ATTN), lambda n: (0, 0))],
        [pl.BlockSpec((BLK, D_MODEL), lambda n: (n, 0))],
        [jax.ShapeDtypeStruct((s, D_MODEL), BF16)],
        [pltpu.VMEM((N_HEADS * BLK, BLK), BF16), pltpu.VMEM((N_HEADS * BLK, 2 * BLK), F32)],
        [sinks, slopes, proj, proj, qw, kw, w_s, bmap], push)


def _out_proj(mix, w_o, x):
    s = x.shape[0]
    tm = _tile(TM_STREAM, s)

    def compute(m_ref, w_ref, x_ref, o_ref):
        o_ref[...] = x_ref[...] + jnp.dot(m_ref[...], w_ref[...], preferred_element_type=F32)

    return _call(
        "out_proj", compute, (s // tm,),
        [pl.BlockSpec((tm, D_MODEL), lambda i: (i, 0)), _resident((D_MODEL, D_MODEL)),
         pl.BlockSpec((tm, D_MODEL), lambda i: (i, 0))],
        [pl.BlockSpec((tm, D_MODEL), lambda i: (i, 0))],
        [jax.ShapeDtypeStruct((s, D_MODEL), F32)], [], [mix, w_o, x])[0]


def _out_proj_loss(mix, w_o, x, tgt):
    s = x.shape[0]
    tm = _tile(TM_STREAM, s)

    def compute(m_ref, w_ref, x_ref, t_ref, dy_ref, sq_ref):
        @pl.when(pl.program_id(0) == 0)
        def _():
            sq_ref[...] = jnp.zeros_like(sq_ref)

        y = x_ref[...] + jnp.dot(m_ref[...], w_ref[...], preferred_element_type=F32)
        e = y - t_ref[...]
        dy_ref[...] = e * (1.0 / D_MODEL)
        sq_ref[...] += jnp.sum(e * e, axis=0, keepdims=True)

    tok = pl.BlockSpec((tm, D_MODEL), lambda i: (i, 0))
    return _call(
        "out_proj_loss", compute, (s // tm,),
        [tok, _resident((D_MODEL, D_MODEL)), tok, tok],
        [tok, pl.BlockSpec((1, D_MODEL), lambda i: (0, 0))],
        [jax.ShapeDtypeStruct((s, D_MODEL), F32), jax.ShapeDtypeStruct((1, D_MODEL), F32)], [], [mix, w_o, x, tgt])


def _dmix(dx, w_o):
    s = dx.shape[0]
    tm = _tile(TM_STREAM, s)

    def compute(d_ref, w_ref, o_ref):
        o_ref[...] = lax.dot_general(d_ref[...].astype(BF16), w_ref[...], NT_DIMS, preferred_element_type=F32)

    return _call(
        "dmix", compute, (s // tm,),
        [pl.BlockSpec((tm, D_MODEL), lambda i: (i, 0)), _resident((D_MODEL, D_MODEL))],
        [pl.BlockSpec((tm, D_MODEL), lambda i: (i, 0))],
        [jax.ShapeDtypeStruct((s, D_MODEL), F32)], [], [dx, w_o])[0]


def _dw_out(mix, dx):
    s = dx.shape[0]
    tk = _tile(TM_STREAM, s)
    nk = s // tk

    def compute(m_ref, d_ref, o_ref, acc):
        k = pl.program_id(0)

        @pl.when(k == 0)
        def _():
            acc[...] = jnp.zeros_like(acc)

        acc[...] += lax.dot_general(m_ref[...], d_ref[...].astype(BF16), TN_DIMS, preferred_element_type=F32)

        @pl.when(k == nk - 1)
        def _():
            o_ref[...] = acc[...].astype(BF16)

    return _call(
        "dw_out", compute, (nk,),
        [pl.BlockSpec((tk, D_MODEL), lambda k: (k, 0)), pl.BlockSpec((tk, D_MODEL), lambda k: (k, 0))],
        [pl.BlockSpec((D_MODEL, D_MODEL), lambda k: (0, 0))],
        [jax.ShapeDtypeStruct((D_MODEL, D_MODEL), BF16)],
        [pltpu.VMEM((D_MODEL, D_MODEL), F32)], [mix, dx])[0]


def _mixer_bwd(proj, dmix, qw, kw, sinks, slopes, w_s, bmap, push=None):
    s = proj.shape[0]
    nb = s // BLK

    def compute(sinks_ref, slopes_ref, p_ref, kvp_ref, dm_ref, qw_ref, kw_ref, ws_ref, bmap_ref,
                dp_ref, dqw_ref, dkw_ref, dsk_ref, dws_ref, dbs_ref,
                pend, accq, acck, accs, accb, wtril_scr, wtril_t_scr, bias_scr):
        n = pl.program_id(0)
        h_a, h_b = _lane_halves(BLK)
        ones = _half_masks()
        halves = (h_a, h_b)
        lane = lax.broadcasted_iota(jnp.int32, (BLK, BLK), 1)

        @pl.when(n == 0)
        def _():
            accq[...] = jnp.zeros_like(accq)
            acck[...] = jnp.zeros_like(acck)
            accs[...] = jnp.zeros_like(accs)
            accb[...] = jnp.zeros_like(accb)
            dws_ref[...] = jnp.zeros_like(dws_ref)
            _sgu_weights(ws_ref, wtril_scr, wtril_t_scr)

        @pl.when(n >= 1)
        def _():
            dp_ref[:, 0:C_K] = pend[:, 0:C_K].astype(BF16)
            dp_ref[:, C_GA:D_IN] = pend[:, C_GA:D_IN].astype(BF16)

        @pl.when(n <= 1)
        def _():
            _alibi_table(n, slopes_ref, bias_scr)

        @pl.when(n < nb)
        def _():
            qw_v = qw_ref[...]
            kw_v = kw_ref[...]
            bands = [_kv_band(kt, p_ref, kvp_ref, kw_v, ones) for kt in range(2)]
            tiles, qst, dost, sc, dpm = [], [], [], [], []
            for kt in range(2):
                qn, d_o, tl = [], [], []
                for tt in range(4):
                    j = 4 * kt + tt
                    qyhat, qr, qn_t = _half_rms(p_ref[:, _cols(C_Q, j)], qw_v, ones)
                    ga = p_ref[:, _cols(C_GA, j)]
                    sg = _sigmoid(ga)
                    dma = dm_ref[:, _cols(0, j)]
                    qn.append(qn_t)
                    d_o.append(dma * (ga * sg))
                    tl.append((qyhat, qr, dma * (sg * (1.0 + ga * (1.0 - sg)))))
                tiles.append(tl)
                qst.append(_stack_heads(qn, halves))
                dost.append(_stack_heads(d_o, halves))
                sc.append(lax.dot_general(qst[kt], bands[kt][2], NT_DIMS, preferred_element_type=F32))
                dpm.append(lax.dot_general(dost[kt], bands[kt][3], NT_DIMS, preferred_element_type=F32))
            sgu = []
            for j in range(8):
                zu_pre = p_ref[:, _cols(C_ZU, j)]
                zv_pre = p_ref[:, _cols(C_ZV, j)]
                cu = _gelu_cdf(zu_pre)
                cv = _gelu_cdf(zv_pre)
                zvb = (zv_pre * cv).astype(BF16)
                sgu.append((zu_pre * cu, zvb, _gelu_grad(zu_pre, cu), _gelu_grad(zv_pre, cv),
                            jnp.dot(wtril_scr[2 * BLK * j:2 * BLK * (j + 1), :], zvb, preferred_element_type=F32)))
            dsink = jnp.zeros((BLK, BLK), F32)
            pst, dqkst = [], []
            for kt in range(2):
                p, p_sink = _softmax_sink(sc[kt], bias_scr[8 * BLK * kt:8 * BLK * (kt + 1), :],
                                          _sink_col(sinks_ref, kt))
                dsum = jnp.sum(p * dpm[kt], axis=-1, keepdims=True)
                dsink_col = -(p_sink * dsum)
                for i in range(8):
                    dsink = dsink + jnp.where(lane == 8 * kt + i, dsink_col[_rows(i)], 0.0)
                pst.append(p.astype(BF16))
                dqkst.append((p * (dpm[kt] - dsum)).astype(BF16))
            o, dqn_all, dvb, dkn = [], [], [], []
            for kt in range(2):
                o.append(jnp.dot(pst[kt], bands[kt][3], preferred_element_type=F32))
                dqn_all.append(jnp.dot(dqkst[kt], bands[kt][2], preferred_element_type=F32))
                dvb.append(lax.dot_general(pst[kt], dost[kt], TN_DIMS, preferred_element_type=F32))
                dkn.append(0.125 * lax.dot_general(dqkst[kt], qst[kt], TN_DIMS, preferred_element_type=F32))
            dms = []
            for j in range(8):
                zu, zvb, gu, gv, m_ab = sgu[j]
                gb = p_ref[:, _cols(C_GB, j)]
                dmb = dm_ref[:, _cols(D_ATTN, j)]
                mixed = jnp.where(h_a, m_ab[0:BLK], m_ab[BLK:2 * BLK]) + bmap_ref[:, _cols(0, j)]
                sgb = _sigmoid(gb)
                dgate = dmb * (gb * sgb)
                pend[:, _cols(C_ZU, j)] = (dgate * mixed) * gu
                pend[:, _cols(C_GB, j)] = (dmb * (zu * mixed)) * (sgb * (1.0 + gb * (1.0 - sgb)))
                dmixed = dgate * zu
                accb[:, _cols(0, j)] += dmixed
                dms.append(jnp.concatenate([jnp.where(h_a, dmixed, 0.0).astype(BF16),
                                            jnp.where(h_b, dmixed, 0.0).astype(BF16)], axis=0))
            dzv = []
            for j in range(8):
                dzv.append(jnp.dot(wtril_t_scr[j], dms[j], preferred_element_type=F32))
                dw_ab = lax.dot_general(dms[j], sgu[j][1], NT_DIMS, preferred_element_type=F32)
                dws_ref[2 * j] += dw_ab[0:BLK]
                dws_ref[2 * j + 1] += dw_ab[BLK:2 * BLK]
            dq_w = jnp.zeros((BLK, BLK), F32)
            for kt in range(2):
                for tt in range(4):
                    j = 4 * kt + tt
                    qyhat, qr, dsilu = tiles[kt][tt]
                    dqn = _unstack_heads(dqn_all[kt], tt, h_a)
                    pend[:, _cols(C_GA, j)] = _unstack_heads(o[kt], tt, h_a) * dsilu
                    pend[:, _cols(C_Q, j)] = _half_rms_bwd(dqn, qyhat, qr, qw_v, ones)
                    dq_w = dq_w + dqn * qyhat
            dk_w = jnp.zeros((BLK, BLK), F32)
            for kt in range(2):
                kyhat, kr = bands[kt][0], bands[kt][1]
                dk = _half_rms_bwd(dkn[kt], kyhat, kr, kw_v, ones)
                dkw_part = dkn[kt] * kyhat
                dk_w = dk_w + (dkw_part[0:BLK] + dkw_part[BLK:2 * BLK])
                dv = dvb[kt]

                @pl.when(n >= 1)
                def _():
                    dp_ref[:, _cols(C_K, kt)] = (pend[:, _cols(C_K, kt)] + dk[0:BLK]).astype(BF16)
                    dp_ref[:, _cols(C_V, kt)] = (pend[:, _cols(C_V, kt)] + dv[0:BLK]).astype(BF16)

                pend[:, _cols(C_K, kt)] = dk[BLK:2 * BLK]
                pend[:, _cols(C_V, kt)] = dv[BLK:2 * BLK]
            accq[...] += dq_w
            acck[...] += dk_w
            accs[...] += dsink
            for j in range(8):
                pend[:, _cols(C_ZV, j)] = dzv[j] * sgu[j][3]

        @pl.when(n == nb)
        def _():
            tril = _tril_mask()
            dp_ref[:, C_K:C_GA] = pend[:, C_K:C_GA].astype(BF16)
            aq = accq[...]
            ak = acck[...]
            dqw_ref[...] = jnp.sum(aq + pltpu.roll(aq, HALF, 1), axis=0, keepdims=True)
            dkw_ref[...] = jnp.sum(ak + pltpu.roll(ak, HALF, 1), axis=0, keepdims=True)
            dsk_ref[...] = jnp.sum(accs[...], axis=0, keepdims=True)
            for hd in range(N_HEADS):
                dws_ref[hd] = jnp.where(tril, dws_ref[hd], 0.0)
            hrow = lax.broadcasted_iota(jnp.int32, (N_HEADS, D_ATTN), 0)
            hcol = lax.broadcasted_iota(jnp.int32, (N_HEADS, D_ATTN), 1)
            sel = jnp.where((hcol >= hrow * HALF) & (hcol < (hrow + 1) * HALF), 1.0, 0.0).astype(BF16)
            rem = accb[...]
            tot = jnp.zeros((N_HEADS, BLK), F32)
            for _ in range(3):
                part = rem.astype(BF16)
                tot = tot + lax.dot_general(sel, part, NT_DIMS, preferred_element_type=F32)
                rem = rem - part.astype(F32)
            dbs_ref[...] = tot

    smem = pl.BlockSpec(memory_space=pltpu.SMEM)
    last = nb - 1
    tile_f32 = pltpu.VMEM((BLK, BLK), F32)
    return _call(
        "mixer_bwd", compute, (nb + 1,),
        [smem, smem,
         pl.BlockSpec((BLK, D_IN), lambda n: (jnp.minimum(n, last), 0)),
         pl.BlockSpec((BLK, 512), lambda n: (jnp.maximum(jnp.minimum(n, last) - 1, 0), 2)),
         pl.BlockSpec((BLK, D_MODEL), lambda n: (jnp.minimum(n, last), 0)),
         pl.BlockSpec((1, BLK), lambda n: (0, 0)),
         pl.BlockSpec((1, BLK), lambda n: (0, 0)),
         pl.BlockSpec((N_HEADS, BLK, BLK), lambda n: (0, 0, 0)),
         pl.BlockSpec((BLK, D_ATTN), lambda n: (0, 0))],
        [pl.BlockSpec((BLK, D_IN), lambda n: (jnp.maximum(n - 1, 0), 0)),
         pl.BlockSpec((1, BLK), lambda n: (0, 0)),
         pl.BlockSpec((1, BLK), lambda n: (0, 0)),
         pl.BlockSpec((1, BLK), lambda n: (0, 0)),
         pl.BlockSpec((N_HEADS, BLK, BLK), lambda n: (0, 0, 0)),
         pl.BlockSpec((N_HEADS, BLK), lambda n: (0, 0))],
        [jax.ShapeDtypeStruct((s, D_IN), BF16),
         jax.ShapeDtypeStruct((1, BLK), F32),
         jax.ShapeDtypeStruct((1, BLK), F32),
         jax.ShapeDtypeStruct((1, BLK), F32),
         jax.ShapeDtypeStruct((N_HEADS, BLK, BLK), F32),
         jax.ShapeDtypeStruct((N_HEADS, BLK), F32)],
        [pltpu.VMEM((BLK, D_IN), F32), tile_f32, tile_f32, tile_f32, pltpu.VMEM((BLK, D_ATTN), F32),
         pltpu.VMEM((N_HEADS * BLK, BLK), BF16), pltpu.VMEM((N_HEADS // 2, BLK, 2 * BLK), BF16),
         pltpu.VMEM((N_HEADS * BLK, 2 * BLK), F32)],
        [sinks, slopes, proj, proj, dmix, qw, kw, w_s, bmap], push)


def _dh_norm_bwd(dproj, w_t, x, dx_out, g, push=None):
    s = x.shape[0]
    tm = _tile(TM_RESIDENT, s)

    def compute(dp_ref, w_ref, x_ref, dxo_ref, g_ref, dx_ref, dg_ref):
        @pl.when(pl.program_id(0) == 0)
        def _():
            dg_ref[...] = jnp.zeros_like(dg_ref)

        dh = jnp.dot(dp_ref[...], w_ref[...], preferred_element_type=F32)
        xf = x_ref[...]
        r = lax.rsqrt(jnp.mean(xf * xf, axis=-1, keepdims=True) + RMS_EPS)
        yhat = xf * r
        dyh = dh * g_ref[...]
        c = jnp.mean(dyh * yhat, axis=-1, keepdims=True)
        dx_ref[...] = dxo_ref[...] + r * (dyh - yhat * c)
        dg_ref[...] += jnp.sum(dh * yhat, axis=0, keepdims=True)

    return _call(
        "dh_norm_bwd", compute, (s // tm,),
        [pl.BlockSpec((tm, D_IN), lambda i: (i, 0)), _resident((D_IN, D_MODEL)),
         pl.BlockSpec((tm, D_MODEL), lambda i: (i, 0)), pl.BlockSpec((tm, D_MODEL), lambda i: (i, 0)),
         pl.BlockSpec((1, D_MODEL), lambda i: (0, 0))],
        [pl.BlockSpec((tm, D_MODEL), lambda i: (i, 0)), pl.BlockSpec((1, D_MODEL), lambda i: (0, 0))],
        [jax.ShapeDtypeStruct((s, D_MODEL), F32), jax.ShapeDtypeStruct((1, D_MODEL), F32)],
        [], [dproj, w_t, x, dx_out, g], push)


def _dw_in(dproj, h, push=None):
    s = h.shape[0]
    tk = _tile(DW_IN_TOK, s)
    nk = s // tk

    def compute(dp_ref, h_ref, o_ref, acc):
        k = pl.program_id(1)

        @pl.when(k == 0)
        def _():
            acc[...] = jnp.zeros_like(acc)

        acc[...] += lax.dot_general(dp_ref[...], h_ref[...], TN_DIMS, preferred_element_type=F32)

        @pl.when(k == nk - 1)
        def _():
            o_ref[...] = acc[...].astype(BF16)

    return _call(
        "dw_in", compute, (D_IN // DW_IN_ROWS, nk),
        [pl.BlockSpec((tk, DW_IN_ROWS), lambda j, k: (k, j)), pl.BlockSpec((tk, D_MODEL), lambda j, k: (k, 0))],
        [pl.BlockSpec((DW_IN_ROWS, D_MODEL), lambda j, k: (j, 0))],
        [jax.ShapeDtypeStruct((D_IN, D_MODEL), BF16)],
        [pltpu.VMEM((DW_IN_ROWS, D_MODEL), F32)], [dproj, h], push)


def _adamw(name, w, g, m, v, tr=None):
    shape = w.shape
    c = shape[-1]
    flat = [a.reshape(-1, c) for a in (w, g, m, v)]
    r = flat[0].shape[0]
    tr = r if tr is None else tr

    def compute(w_ref, g_ref, m_ref, v_ref, d_ref, mo_ref, vo_ref):
        gv = g_ref[...]
        m_new = ADAM_B1 * m_ref[...] + (1.0 - ADAM_B1) * gv
        v_new = ADAM_B2 * v_ref[...] + (1.0 - ADAM_B2) * jnp.square(gv)
        m_hat = m_new / (1.0 - ADAM_B1 ** ADAM_STEP)
        v_hat = v_new / (1.0 - ADAM_B2 ** ADAM_STEP)
        d_ref[...] = -ADAM_LR * (m_hat / (jnp.sqrt(v_hat) + ADAM_EPS) + ADAM_WD * w_ref[...])
        mo_ref[...] = m_new
        vo_ref[...] = v_new

    spec = pl.BlockSpec((tr, c), lambda i: (i, 0))
    outs = _call(name, compute, (r // tr,), [spec] * 4, [spec] * 3, [jax.ShapeDtypeStruct((r, c), F32)] * 3,
                 [], flat)
    return [o.reshape(shape) for o in outs]


def _pack_rows(parts):
    rows = []
    for a in parts:
        flat = a.reshape(-1)
        n = -(-flat.shape[0] // (8 * BLK)) * 8
        rows.append(jnp.pad(flat, (0, n * BLK - flat.shape[0])).reshape(n, BLK))
    return jnp.concatenate(rows, axis=0)


def _unpack_rows(packed, like):
    out = []
    row = 0
    for a in like:
        n = -(-a.size // (8 * BLK)) * 8
        out.append(packed[row:row + n].reshape(-1)[:a.size].reshape(a.shape))
        row += n
    return out


def kernel(x, norm_g, w_in, q_norm, k_norm, sinks, w_s, b_s, w_out, loss_target, m_norm_g, m_w_in, m_q_norm, m_k_norm, m_sinks, m_w_s, m_b_s, m_w_out, v_norm_g, v_w_in, v_q_norm, v_k_norm, v_sinks, v_w_s, v_b_s, v_w_out):
    xs = x[0]
    tgt = loss_target[0]
    slopes = jnp.asarray(2.0 ** (-8.0 * np.arange(1, N_HEADS + 1) / N_HEADS), dtype=F32)
    wt_sh = jnp.swapaxes(w_in, 1, 2).astype(BF16)
    wo_sh = w_out.astype(BF16)

    layer_par = []
    for l in range(DEPTH):
        layer_par.append((jnp.tile(q_norm[l], 2)[None, :], jnp.tile(k_norm[l], 2)[None, :],
                          jnp.repeat(b_s[l].T, HALF, axis=1)))

    wt_full, wo_full = _exchange("gather_w0", _gather_rows((W_IN_SHARD, W_OUT_SHARD), 0), [wt_sh, wo_sh],
                                 [W_FULL[W_IN_SHARD], W_FULL[W_OUT_SHARD]])
    saved = []
    cur = xs
    for l in range(DEPTH):
        qw, kw, bmap = layer_par[l]
        more = l + 1 < DEPTH
        res = _norm_proj(cur, norm_g[l][None, :], wt_full,
                         (_gather_rows((W_IN_SHARD,), l + 1), [wt_sh], [W_FULL[W_IN_SHARD]]) if more else None)
        h, proj = res[0], res[1]
        res2 = _mixer_fwd(proj, qw, kw, sinks[l], slopes, w_s[l], bmap,
                          (_gather_rows((W_OUT_SHARD,), l + 1), [wo_sh], [W_FULL[W_OUT_SHARD]]) if more else None)
        mix = res2[0]
        saved.append((cur, h, proj, mix, wt_full, wo_full))
        if more:
            cur = _out_proj(mix, wo_full, cur)
            wt_full, wo_full = res[2], res2[1]
        else:
            dx, sq = _out_proj_loss(mix, wo_full, cur, tgt)
    loss = lax.psum(0.5 * jnp.sum(sq) / D_MODEL, MESH_AXES)

    g_wt, g_wo, g_small, g_norm = ([None] * DEPTH for _ in range(4))
    wt_slots_shape = _slots_shape(W_IN_SHARD, D_MODEL, BF16)
    dwt_waiting = None
    for l in reversed(range(DEPTH)):
        x_l, h, proj, mix, wt_l, wo_l = saved[l]
        qw, kw, bmap = layer_par[l]
        dmix = _dmix(dx, wo_l)
        dwo_part = _dw_out(mix, dx)
        res = _mixer_bwd(proj, dmix, qw, kw, sinks[l], slopes, w_s[l], bmap,
                         None if dwt_waiting is None else
                         (_scatter_rows((W_IN_SHARD,)), [dwt_waiting], [wt_slots_shape]))
        dproj, dqw, dkw, dsk, dws, dbs = res[:6]
        if dwt_waiting is not None:
            g_wt[l + 1] = _sum_slots(res[6], W_IN_SHARD // 2)
        small_like = [dqw[0, :HALF], dkw[0, :HALF], dsk[0, :N_HEADS], dws, dbs]
        packed = _pack_rows(small_like)
        dwt_waiting, small_slots, wo_slots = _dw_in(
            dproj, h, (_gather_slots_and_scatter_rows(W_OUT_SHARD), [packed, dwo_part],
                       [_slots_shape(*packed.shape, F32), _slots_shape(W_OUT_SHARD, D_MODEL, BF16)]))
        res = _dh_norm_bwd(dproj, wt_l, x_l, dx, norm_g[l][None, :],
                           (_scatter_rows((W_IN_SHARD,)), [dwt_waiting], [wt_slots_shape]) if l == 0 else None)
        dx, dng = res[0], res[1]
        if l == 0:
            g_wt[0] = _sum_slots(res[2], W_IN_SHARD // 2)
        g_wo[l] = _sum_slots(wo_slots, W_OUT_SHARD)
        g_small[l] = _unpack_rows(_sum_slots(small_slots, packed.shape[0]), small_like)
        g_norm[l] = dng[0]

    dng_all = _pack_rows([jnp.stack(g_norm)])
    dng_slots = _exchange("gather_dnorm", _gather_slots(), [dng_all], [_slots_shape(*dng_all.shape, F32)])[0]
    gr_norm = _unpack_rows(_sum_slots(dng_slots, dng_all.shape[0]), [norm_g])[0]
    gr_qn, gr_kn, gr_sk, gr_ws, gr_bs = (jnp.stack([g_small[l][i] for l in range(DEPTH)]) for i in range(5))
    gr_w_in = jnp.swapaxes(jnp.stack(g_wt), 1, 2)
    gr_w_out = jnp.stack(g_wo)

    grads = [gr_norm, gr_w_in, gr_qn, gr_kn, gr_sk, gr_ws, gr_bs, gr_w_out]
    weights = [norm_g, w_in, q_norm, k_norm, sinks, w_s, b_s, w_out]
    moms = [m_norm_g, m_w_in, m_q_norm, m_k_norm, m_sinks, m_w_s, m_b_s, m_w_out]
    vels = [v_norm_g, v_w_in, v_q_norm, v_k_norm, v_sinks, v_w_s, v_b_s, v_w_out]
    tiles = [None, 512, None, None, None, 1024, None, 256]
    names = ["norm_g", "w_in", "q_norm", "k_norm", "sinks", "w_s", "b_s", "w_out"]
    deltas, new_m, new_v = [], [], []
    for nm, w, g, m, v, tr in zip(names, weights, grads, moms, vels, tiles):
        d, mo, vo = _adamw("adamw_" + nm, w, g, m, v, tr)
        deltas.append(d)
        new_m.append(mo)
        new_v.append(vo)

    return (loss, dx[None], *grads, *deltas, *new_m, *new_v)
```

```python
import numpy as np
import jax
import jax.numpy as jnp
from jax import lax
from jax.experimental import pallas as pl
from jax.experimental.pallas import tpu as pltpu

F32 = jnp.float32
BF16 = jnp.bfloat16

D_MODEL = 2048
D_ATTN = 1024
D_IN = 5632
N_HEADS = 16
DEPTH = 4
BLK = 128
HALF = 64
RMS_EPS = 1e-6
C_Q, C_K, C_V, C_GA, C_ZU, C_ZV, C_GB = 0, 1024, 1280, 1536, 2560, 3584, 4608
NEG = -1e30
N_DEV = 8
W_IN_SHARD = D_IN // N_DEV
W_OUT_SHARD = D_MODEL // N_DEV
INV_SQRT2 = 0.7071067811865476
INV_SQRT_2PI = 0.3989422804014327

TM_RESIDENT = 256
TM_STREAM = 512
TN_PROJ = 512
DW_IN_ROWS = D_IN // 4
DW_IN_TOK = 1024

ADAM_LR = 0.001
ADAM_B1 = 0.9
ADAM_B2 = 0.999
ADAM_EPS = 1e-08
ADAM_WD = 0.01
ADAM_STEP = 10

NT_DIMS = (((1,), (1,)), ((), ()))
TN_DIMS = (((0,), (0,)), ((), ()))
MESH_AXES = ("x", "y", "c")


def _sigmoid(v):
    return 1.0 / (1.0 + jnp.exp(-v))


def _gelu_cdf(z):
    return 0.5 * (1.0 + lax.erf(z * INV_SQRT2))


def _gelu_grad(z, cdf):
    return cdf + z * (jnp.exp(-0.5 * z * z) * INV_SQRT_2PI)


def _lane_halves(rows):
    lane = lax.broadcasted_iota(jnp.int32, (rows, BLK), 1)
    return lane < HALF, lane >= HALF


def _half_masks():
    return {BLK: _lane_halves(BLK), 2 * BLK: _lane_halves(2 * BLK)}


def _half_sum(v, ones):
    h_a, h_b = ones[v.shape[0]]
    s_a = jnp.sum(jnp.where(h_a, v, 0.0), axis=-1, keepdims=True)
    s_b = jnp.sum(jnp.where(h_b, v, 0.0), axis=-1, keepdims=True)
    return jnp.where(h_a, s_a, s_b)


def _half_rms(v, w, ones):
    r = lax.rsqrt(_half_sum(v * v, ones) * (1.0 / HALF) + RMS_EPS)
    yhat = v * r
    return yhat, r, yhat * w


def _half_rms_bwd(dy, yhat, r, w, ones):
    dyh = dy * w
    c = _half_sum(dyh * yhat, ones) * (1.0 / HALF)
    return r * (dyh - yhat * c)


def _band_mask(n):
    t = lax.broadcasted_iota(jnp.int32, (BLK, 2 * BLK), 0)
    kk = lax.broadcasted_iota(jnp.int32, (BLK, 2 * BLK), 1)
    dist = t + BLK - kk
    first_key = jnp.where(n > 0, 0, BLK)
    ok = (dist >= 0) & (dist < BLK) & (kk >= first_key)
    return ok, dist.astype(F32)


def _alibi_table(n, slopes_ref, bias_scr):
    ok, distf = _band_mask(n)
    for hd in range(N_HEADS):
        bias_scr[BLK * hd:BLK * (hd + 1), :] = jnp.where(ok, -(slopes_ref[hd] * distf), NEG)


def _sink_col(sinks_ref, kt):
    return jnp.concatenate([jnp.full((BLK, 1), sinks_ref[8 * kt + i], F32) for i in range(8)], axis=0)


def _softmax_sink(s_scaled, bias, sink):
    s = s_scaled + bias
    m = jnp.maximum(jnp.max(s, axis=-1, keepdims=True), sink)
    p = jnp.exp(s - m)
    es = jnp.exp(sink - m)
    inv = 1.0 / (jnp.sum(p, axis=-1, keepdims=True) + es)
    return p * inv, es * inv


def _rows(i):
    return slice(BLK * i, BLK * (i + 1))


def _cols(base, j):
    return slice(base + BLK * j, base + BLK * (j + 1))


def _to_half(v, have, want):
    return v if have == want else pltpu.roll(v, HALF, 1)


def _tril_mask():
    row = lax.broadcasted_iota(jnp.int32, (BLK, BLK), 0)
    col = lax.broadcasted_iota(jnp.int32, (BLK, BLK), 1)
    return row >= col


def _tile(limit, s):
    t = min(limit, s)
    assert s % t == 0, (s, t)
    return t


def _mesh_place():
    x, y, c = lax.axis_index("x"), lax.axis_index("y"), lax.axis_index("c")
    return x, y, c, 4 * x + 2 * y + c


def _peer(x, y, c, k):
    px = 1 - x if k & 4 else x
    py = 1 - y if k & 2 else y
    pc = 1 - c if k & 1 else c
    return (px, py, pc), 4 * px + 2 * py + pc


class _Pushes:
    def __init__(self, n_arrays, src_view, dst_view):
        self.na = n_arrays
        self.src_view = src_view
        self.dst_view = dst_view

    def scratch(self):
        n = self.na * (N_DEV - 1)
        return [pltpu.SemaphoreType.DMA((n,)), pltpu.SemaphoreType.DMA((n,)), pltpu.SemaphoreType.DMA((self.na,))]

    def copies(self, src_refs, dst_refs, send_sems, recv_sems, local_sems):
        x, y, c, me = _mesh_place()
        cps = []
        for a in range(self.na):
            cps.append(pltpu.make_async_copy(self.src_view(a, src_refs[a], me), self.dst_view(a, dst_refs[a], me),
                                             local_sems.at[a]))
        for k in range(1, N_DEV):
            peer, pidx = _peer(x, y, c, k)
            for a in range(self.na):
                sem = a * (N_DEV - 1) + k - 1
                cps.append(pltpu.make_async_remote_copy(
                    src_ref=self.src_view(a, src_refs[a], pidx), dst_ref=self.dst_view(a, dst_refs[a], me),
                    send_sem=send_sems.at[sem], recv_sem=recv_sems.at[sem],
                    device_id=peer, device_id_type=pl.DeviceIdType.MESH))
        return cps

    def plan(self, src_refs, dst_refs, send_sems, recv_sems, local_sems):
        cps = self.copies(src_refs, dst_refs, send_sems, recv_sems, local_sems)
        return cps, [], [], [cp.wait for cp in cps]


class _TwoLevelGather:
    def __init__(self, shard_rows, layers):
        self.na = len(shard_rows)
        self.shard_rows = shard_rows
        self.layers = layers

    def scratch(self):
        n = self.na * (N_DEV - 1)
        return [pltpu.SemaphoreType.DMA((n,)), pltpu.SemaphoreType.DMA((n,)), pltpu.SemaphoreType.DMA((self.na,))]

    def plan(self, src_refs, dst_refs, send_sems, recv_sems, local_sems):
        x, y, c, _ = _mesh_place()
        sibling = (x, y, 1 - c)
        chips = [(1 - x, y), (x, 1 - y), (1 - x, 1 - y)]
        start, mid_wait, mid_start, final = [], [], [], []
        for a in range(self.na):
            r = self.shard_rows[a]
            src = src_refs[a].at[self.layers[a]]
            dst = dst_refs[a]

            def rows(px, py, pc, r=r, dst=dst):
                return dst.at[pl.ds(pl.multiple_of((4 * px + 2 * py + pc) * r, 64), r), :]

            def remote(k, s_ref, block, to, a=a, rows=rows):
                return pltpu.make_async_remote_copy(
                    src_ref=s_ref, dst_ref=rows(*block),
                    send_sem=send_sems.at[a * (N_DEV - 1) + k], recv_sem=recv_sems.at[a * (N_DEV - 1) + k],
                    device_id=to, device_id_type=pl.DeviceIdType.MESH)

            mine = pltpu.make_async_copy(src, rows(x, y, c), local_sems.at[a])
            own = [remote(0, src, (x, y, c), sibling)]
            own += [remote(1 + j, src, (x, y, c), (*chip, c)) for j, chip in enumerate(chips)]
            passed = [remote(4 + j, rows(*chip, c), (*chip, c), sibling) for j, chip in enumerate(chips)]
            start += [mine] + own
            mid_wait += own[1:]
            mid_start += passed
            final += [own[0].wait_recv] + [cp.wait_recv for cp in passed]
            final += [cp.wait_send for cp in own + passed] + [mine.wait]
        return start, mid_wait, mid_start, final


def _call(name, compute, grid, in_specs, out_specs, out_shape, scratch, args, push=None):
    sem = pltpu.CompilerParams(dimension_semantics=("arbitrary",) * len(grid))
    if push is None:
        return pl.pallas_call(compute, name=name, grid=grid, in_specs=in_specs, out_specs=out_specs,
                              out_shape=out_shape, scratch_shapes=scratch, compiler_params=sem)(*args)
    pushes, srcs, xshapes = push
    n_in, n_out, n_scr, na = len(args), len(out_shape), len(scratch), pushes.na
    hbm = pl.BlockSpec(memory_space=pltpu.HBM)

    def body(*refs):
        ins, refs = refs[:n_in], refs[n_in:]
        xin, refs = refs[:na], refs[na:]
        outs, refs = refs[:n_out], refs[n_out:]
        xout, refs = refs[:na], refs[na:]
        scr, sems = refs[:n_scr], refs[n_scr:]
        start, mid_wait, mid_start, final = pushes.plan(xin, xout, *sems)
        first = pl.program_id(0) == 0
        middle = pl.program_id(0) == (grid[0] * 5) // 8
        last = pl.program_id(0) == grid[0] - 1
        for d in range(1, len(grid)):
            first = first & (pl.program_id(d) == 0)
            middle = middle & (pl.program_id(d) == 0)
            last = last & (pl.program_id(d) == grid[d] - 1)

        @pl.when(first)
        def _():
            for cp in start:
                cp.start()

        if mid_start:
            @pl.when(middle)
            def _():
                for cp in mid_wait:
                    cp.wait_recv()
                for cp in mid_start:
                    cp.start()

        compute(*ins, *outs, *scr)

        @pl.when(last)
        def _():
            for wait in final:
                wait()

    return pl.pallas_call(
        body, name=name, grid=grid,
        in_specs=list(in_specs) + [hbm] * na, out_specs=list(out_specs) + [hbm] * na,
        out_shape=list(out_shape) + list(xshapes),
        scratch_shapes=list(scratch) + pushes.scratch(), compiler_params=sem)(*args, *srcs)


def _exchange(name, pushes, srcs, out_shapes):
    na = pushes.na
    hbm = pl.BlockSpec(memory_space=pltpu.HBM)

    def body(*refs):
        start, mid_wait, mid_start, final = pushes.plan(refs[:na], refs[na:2 * na], *refs[2 * na:])
        for cp in start:
            cp.start()
        for cp in mid_wait:
            cp.wait_recv()
        for cp in mid_start:
            cp.start()
        for wait in final:
            wait()

    return pl.pallas_call(body, name=name, in_specs=[hbm] * na, out_specs=[hbm] * na, out_shape=out_shapes,
                          scratch_shapes=pushes.scratch())(*srcs)


def _gather_rows(shard_rows, layers):
    return _TwoLevelGather(shard_rows, layers)


def _scatter_rows(shard_rows):
    def src_view(a, ref, idx):
        r = shard_rows[a]
        return ref.at[pl.ds(pl.multiple_of(idx * r, 64), r), :]

    def dst_view(a, ref, idx):
        return ref.at[idx]

    return _Pushes(len(shard_rows), src_view, dst_view)


def _gather_slots():
    return _Pushes(1, lambda a, ref, idx: ref, lambda a, ref, idx: ref.at[idx])


def _gather_slots_and_scatter_rows(rows):
    def src_view(a, ref, idx):
        return ref if a == 0 else ref.at[pl.ds(pl.multiple_of(idx * rows, 64), rows), :]

    return _Pushes(2, src_view, lambda a, ref, idx: ref.at[idx])


W_FULL = {W_IN_SHARD: jax.ShapeDtypeStruct((D_IN, D_MODEL), BF16),
          W_OUT_SHARD: jax.ShapeDtypeStruct((D_MODEL, D_MODEL), BF16)}


def _slots_shape(rows, cols, dtype):
    return jax.ShapeDtypeStruct((N_DEV, rows, cols), dtype)


def _sum_slots(slots, tr):
    _, r, c = slots.shape

    def compute(s_ref, o_ref):
        tot = s_ref[0].astype(F32)
        for d in range(1, N_DEV):
            tot = tot + s_ref[d].astype(F32)
        o_ref[...] = tot

    return _call("sum_slots", compute, (r // tr,),
                 [pl.BlockSpec((N_DEV, tr, c), lambda i: (0, i, 0))], [pl.BlockSpec((tr, c), lambda i: (i, 0))],
                 [jax.ShapeDtypeStruct((r, c), F32)], [], [slots])[0]


def _resident(shape):
    return pl.BlockSpec(shape, lambda *_: (0,) * len(shape), pipeline_mode=pl.Buffered(1))


def _norm_proj(x, g, w_t, push=None):
    s = x.shape[0]
    tm = _tile(TM_RESIDENT, s)

    def compute(x_ref, g_ref, w_ref, h_ref, p_ref):
        xf = x_ref[...]
        r = lax.rsqrt(jnp.mean(xf * xf, axis=-1, keepdims=True) + RMS_EPS)
        h = ((xf * r) * g_ref[...]).astype(BF16)
        h_ref[...] = h
        for j in range(D_IN // TN_PROJ):
            cols = slice(j * TN_PROJ, (j + 1) * TN_PROJ)
            p_ref[:, cols] = lax.dot_general(h, w_ref[cols, :], NT_DIMS, preferred_element_type=F32)

    return _call(
        "norm_proj", compute, (s // tm,),
        [pl.BlockSpec((tm, D_MODEL), lambda i: (i, 0)), pl.BlockSpec((1, D_MODEL), lambda i: (0, 0)),
         _resident((D_IN, D_MODEL))],
        [pl.BlockSpec((tm, D_MODEL), lambda i: (i, 0)), pl.BlockSpec((tm, D_IN), lambda i: (i, 0))],
        [jax.ShapeDtypeStruct((s, D_MODEL), BF16), jax.ShapeDtypeStruct((s, D_IN), F32)],
        [], [x, g, w_t], push)


def _sgu_weights(ws_ref, wtril_scr, wtril_t_scr=None):
    tril = _tril_mask()
    for hd in range(N_HEADS):
        w = jnp.where(tril, ws_ref[hd], 0.0)
        wtril_scr[BLK * hd:BLK * (hd + 1), :] = w.astype(BF16)
        if wtril_t_scr is not None:
            wtril_t_scr[hd // 2, :, BLK * (hd % 2):BLK * (hd % 2 + 1)] = w.T.astype(BF16)


def _kv_band(kt, p_ref, kvp_ref, kw_v, ones):
    kband = jnp.concatenate([kvp_ref[:, _cols(0, kt)], p_ref[:, _cols(C_K, kt)]], axis=0)
    kyhat, kr, kn = _half_rms(kband, kw_v, ones)
    vband = jnp.concatenate([kvp_ref[:, _cols(256, kt)], p_ref[:, _cols(C_V, kt)]], axis=0)
    return kyhat, kr, (kn * 0.125).astype(BF16), vband.astype(BF16)


def _stack_heads(tiles, halves):
    parts = []
    for tt, tile in enumerate(tiles):
        for qh in range(2):
            parts.append(_to_half(jnp.where(halves[qh], tile, 0.0), qh, tt // 2).astype(BF16))
    return jnp.concatenate(parts, axis=0)


def _unstack_heads(stacked, tt, h_a):
    return jnp.where(h_a, _to_half(stacked[_rows(2 * tt)], tt // 2, 0), _to_half(stacked[_rows(2 * tt + 1)], tt // 2, 1))


def _mixer_fwd(proj, qw, kw, sinks, slopes, w_s, bmap, push=None):
    s = proj.shape[0]
    nb = s // BLK

    def compute(sinks_ref, slopes_ref, p_ref, kvp_ref, qw_ref, kw_ref, ws_ref, bmap_ref, mix_ref,
                wtril_scr, bias_scr):
        n = pl.program_id(0)
        h_a, h_b = _lane_halves(BLK)
        ones = _half_masks()
        halves = (h_a, h_b)
        qw_v = qw_ref[...]
        kw_v = kw_ref[...]

        @pl.when(n == 0)
        def _():
            _sgu_weights(ws_ref, wtril_scr)

        @pl.when(n <= 1)
        def _():
            _alibi_table(n, slopes_ref, bias_scr)

        bands = [_kv_band(kt, p_ref, kvp_ref, kw_v, ones) for kt in range(2)]
        sc = []
        for kt in range(2):
            qn = [_half_rms(p_ref[:, _cols(C_Q, 4 * kt + tt)], qw_v, ones)[2] for tt in range(4)]
            sc.append(lax.dot_general(_stack_heads(qn, halves), bands[kt][2], NT_DIMS, preferred_element_type=F32))
        zu, mixed = [], []
        for j in range(8):
            zu_pre = p_ref[:, _cols(C_ZU, j)]
            zv_pre = p_ref[:, _cols(C_ZV, j)]
            zu.append(zu_pre * _gelu_cdf(zu_pre))
            zvb = (zv_pre * _gelu_cdf(zv_pre)).astype(BF16)
            mixed.append(jnp.dot(wtril_scr[2 * BLK * j:2 * BLK * (j + 1), :], zvb, preferred_element_type=F32))
        o = []
        for kt in range(2):
            p, _ = _softmax_sink(sc[kt], bias_scr[8 * BLK * kt:8 * BLK * (kt + 1), :], _sink_col(sinks_ref, kt))
            o.append(jnp.dot(p.astype(BF16), bands[kt][3], preferred_element_type=F32))
        for j in range(8):
            gb = p_ref[:, _cols(C_GB, j)]
            mx = jnp.where(h_a, mixed[j][0:BLK], mixed[j][BLK:2 * BLK]) + bmap_ref[:, _cols(0, j)]
            mix_ref[:, _cols(D_ATTN, j)] = ((zu[j] * mx) * (gb * _sigmoid(gb))).astype(BF16)
        for kt in range(2):
            for tt in range(4):
                j = 4 * kt + tt
                ga = p_ref[:, _cols(C_GA, j)]
                mix_ref[:, _cols(0, j)] = (_unstack_heads(o[kt], tt, h_a) * (ga * _sigmoid(ga))).astype(BF16)

    smem = pl.BlockSpec(memory_space=pltpu.SMEM)
    return _call(
        "mixer_fwd", compute, (nb,),
        [smem, smem,
         pl.BlockSpec((BLK, D_IN), lambda n: (n, 0)),
         pl.BlockSpec((BLK, 512), lambda n: (jnp.maximum(n - 1, 0), 2)),
         pl.BlockSpec((1, BLK), lambda n: (0, 0)),
         pl.BlockSpec((1, BLK), lambda n: (0, 0)),
         pl.BlockSpec((N_HEADS, BLK, BLK), lambda n: (0, 0, 0)),
         pl.BlockSpec((BLK, D_ATTN), lambda n: (0, 0))],
        [pl.BlockSpec((BLK, D_MODEL), lambda n: (n, 0))],
        [jax.ShapeDtypeStruct((s, D_MODEL), BF16)],
        [pltpu.VMEM((N_HEADS * BLK, BLK), BF16), pltpu.VMEM((N_HEADS * BLK, 2 * BLK), F32)],
        [sinks, slopes, proj, proj, qw, kw, w_s, bmap], push)


def _out_proj(mix, w_o, x):
    s = x.shape[0]
    tm = _tile(TM_STREAM, s)

    def compute(m_ref, w_ref, x_ref, o_ref):
        o_ref[...] = x_ref[...] + jnp.dot(m_ref[...], w_ref[...], preferred_element_type=F32)

    return _call(
        "out_proj", compute, (s // tm,),
        [pl.BlockSpec((tm, D_MODEL), lambda i: (i, 0)), _resident((D_MODEL, D_MODEL)),
         pl.BlockSpec((tm, D_MODEL), lambda i: (i, 0))],
        [pl.BlockSpec((tm, D_MODEL), lambda i: (i, 0))],
        [jax.ShapeDtypeStruct((s, D_MODEL), F32)], [], [mix, w_o, x])[0]


def _out_proj_loss(mix, w_o, x, tgt):
    s = x.shape[0]
    tm = _tile(TM_STREAM, s)

    def compute(m_ref, w_ref, x_ref, t_ref, dy_ref, sq_ref):
        @pl.when(pl.program_id(0) == 0)
        def _():
            sq_ref[...] = jnp.zeros_like(sq_ref)

        y = x_ref[...] + jnp.dot(m_ref[...], w_ref[...], preferred_element_type=F32)
        e = y - t_ref[...]
        dy_ref[...] = e * (1.0 / D_MODEL)
        sq_ref[...] += jnp.sum(e * e, axis=0, keepdims=True)

    tok = pl.BlockSpec((tm, D_MODEL), lambda i: (i, 0))
    return _call(
        "out_proj_loss", compute, (s // tm,),
        [tok, _resident((D_MODEL, D_MODEL)), tok, tok],
        [tok, pl.BlockSpec((1, D_MODEL), lambda i: (0, 0))],
        [jax.ShapeDtypeStruct((s, D_MODEL), F32), jax.ShapeDtypeStruct((1, D_MODEL), F32)], [], [mix, w_o, x, tgt])


def _dmix(dx, w_o):
    s = dx.shape[0]
    tm = _tile(TM_STREAM, s)

    def compute(d_ref, w_ref, o_ref):
        o_ref[...] = lax.dot_general(d_ref[...].astype(BF16), w_ref[...], NT_DIMS, preferred_element_type=F32)

    return _call(
        "dmix", compute, (s // tm,),
        [pl.BlockSpec((tm, D_MODEL), lambda i: (i, 0)), _resident((D_MODEL, D_MODEL))],
        [pl.BlockSpec((tm, D_MODEL), lambda i: (i, 0))],
        [jax.ShapeDtypeStruct((s, D_MODEL), F32)], [], [dx, w_o])[0]


def _dw_out(mix, dx):
    s = dx.shape[0]
    tk = _tile(TM_STREAM, s)
    nk = s // tk

    def compute(m_ref, d_ref, o_ref, acc):
        k = pl.program_id(0)

        @pl.when(k == 0)
        def _():
            acc[...] = jnp.zeros_like(acc)

        acc[...] += lax.dot_general(m_ref[...], d_ref[...].astype(BF16), TN_DIMS, preferred_element_type=F32)

        @pl.when(k == nk - 1)
        def _():
            o_ref[...] = acc[...].astype(BF16)

    return _call(
        "dw_out", compute, (nk,),
        [pl.BlockSpec((tk, D_MODEL), lambda k: (k, 0)), pl.BlockSpec((tk, D_MODEL), lambda k: (k, 0))],
        [pl.BlockSpec((D_MODEL, D_MODEL), lambda k: (0, 0))],
        [jax.ShapeDtypeStruct((D_MODEL, D_MODEL), BF16)],
        [pltpu.VMEM((D_MODEL, D_MODEL), F32)], [mix, dx])[0]


def _mixer_bwd(proj, dmix, qw, kw, sinks, slopes, w_s, bmap, push=None):
    s = proj.shape[0]
    nb = s // BLK

    def compute(sinks_ref, slopes_ref, p_ref, kvp_ref, dm_ref, qw_ref, kw_ref, ws_ref, bmap_ref,
                dp_ref, dqw_ref, dkw_ref, dsk_ref, dws_ref, dbs_ref,
                pend, accq, acck, accs, accb, wtril_scr, wtril_t_scr, bias_scr):
        n = pl.program_id(0)
        h_a, h_b = _lane_halves(BLK)
        ones = _half_masks()
        halves = (h_a, h_b)
        lane = lax.broadcasted_iota(jnp.int32, (BLK, BLK), 1)

        @pl.when(n == 0)
        def _():
            accq[...] = jnp.zeros_like(accq)
            acck[...] = jnp.zeros_like(acck)
            accs[...] = jnp.zeros_like(accs)
            accb[...] = jnp.zeros_like(accb)
            dws_ref[...] = jnp.zeros_like(dws_ref)
            _sgu_weights(ws_ref, wtril_scr, wtril_t_scr)

        @pl.when(n >= 1)
        def _():
            dp_ref[:, 0:C_K] = pend[:, 0:C_K].astype(BF16)
            dp_ref[:, C_GA:D_IN] = pend[:, C_GA:D_IN].astype(BF16)

        @pl.when(n <= 1)
        def _():
            _alibi_table(n, slopes_ref, bias_scr)

        @pl.when(n < nb)
        def _():
            qw_v = qw_ref[...]
            kw_v = kw_ref[...]
            bands = [_kv_band(kt, p_ref, kvp_ref, kw_v, ones) for kt in range(2)]
            tiles, qst, dost, sc, dpm = [], [], [], [], []
            for kt in range(2):
                qn, d_o, tl = [], [], []
                for tt in range(4):
                    j = 4 * kt + tt
                    qyhat, qr, qn_t = _half_rms(p_ref[:, _cols(C_Q, j)], qw_v, ones)
                    ga = p_ref[:, _cols(C_GA, j)]
                    sg = _sigmoid(ga)
                    dma = dm_ref[:, _cols(0, j)]
                    qn.append(qn_t)
                    d_o.append(dma * (ga * sg))
                    tl.append((qyhat, qr, dma * (sg * (1.0 + ga * (1.0 - sg)))))
                tiles.append(tl)
                qst.append(_stack_heads(qn, halves))
                dost.append(_stack_heads(d_o, halves))
                sc.append(lax.dot_general(qst[kt], bands[kt][2], NT_DIMS, preferred_element_type=F32))
                dpm.append(lax.dot_general(dost[kt], bands[kt][3], NT_DIMS, preferred_element_type=F32))
            sgu = []
            for j in range(8):
                zu_pre = p_ref[:, _cols(C_ZU, j)]
                zv_pre = p_ref[:, _cols(C_ZV, j)]
                cu = _gelu_cdf(zu_pre)
                cv = _gelu_cdf(zv_pre)
                zvb = (zv_pre * cv).astype(BF16)
                sgu.append((zu_pre * cu, zvb, _gelu_grad(zu_pre, cu), _gelu_grad(zv_pre, cv),
                            jnp.dot(wtril_scr[2 * BLK * j:2 * BLK * (j + 1), :], zvb, preferred_element_type=F32)))
            dsink = jnp.zeros((BLK, BLK), F32)
            pst, dqkst = [], []
            for kt in range(2):
                p, p_sink = _softmax_sink(sc[kt], bias_scr[8 * BLK * kt:8 * BLK * (kt + 1), :],
                                          _sink_col(sinks_ref, kt))
                dsum = jnp.sum(p * dpm[kt], axis=-1, keepdims=True)
                dsink_col = -(p_sink * dsum)
                for i in range(8):
                    dsink = dsink + jnp.where(lane == 8 * kt + i, dsink_col[_rows(i)], 0.0)
                pst.append(p.astype(BF16))
                dqkst.append((p * (dpm[kt] - dsum)).astype(BF16))
            o, dqn_all, dvb, dkn = [], [], [], []
            for kt in range(2):
                o.append(jnp.dot(pst[kt], bands[kt][3], preferred_element_type=F32))
                dqn_all.append(jnp.dot(dqkst[kt], bands[kt][2], preferred_element_type=F32))
                dvb.append(lax.dot_general(pst[kt], dost[kt], TN_DIMS, preferred_element_type=F32))
                dkn.append(0.125 * lax.dot_general(dqkst[kt], qst[kt], TN_DIMS, preferred_element_type=F32))
            dms = []
            for j in range(8):
                zu, zvb, gu, gv, m_ab = sgu[j]
                gb = p_ref[:, _cols(C_GB, j)]
                dmb = dm_ref[:, _cols(D_ATTN, j)]
                mixed = jnp.where(h_a, m_ab[0:BLK], m_ab[BLK:2 * BLK]) + bmap_ref[:, _cols(0, j)]
                sgb = _sigmoid(gb)
                dgate = dmb * (gb * sgb)
                pend[:, _cols(C_ZU, j)] = (dgate * mixed) * gu
                pend[:, _cols(C_GB, j)] = (dmb * (zu * mixed)) * (sgb * (1.0 + gb * (1.0 - sgb)))
                dmixed = dgate * zu
                accb[:, _cols(0, j)] += dmixed
                dms.append(jnp.concatenate([jnp.where(h_a, dmixed, 0.0).astype(BF16),
                                            jnp.where(h_b, dmixed, 0.0).astype(BF16)], axis=0))
            dzv = []
            for j in range(8):
                dzv.append(jnp.dot(wtril_t_scr[j], dms[j], preferred_element_type=F32))
                dw_ab = lax.dot_general(dms[j], sgu[j][1], NT_DIMS, preferred_element_type=F32)
                dws_ref[2 * j] += dw_ab[0:BLK]
                dws_ref[2 * j + 1] += dw_ab[BLK:2 * BLK]
            dq_w = jnp.zeros((BLK, BLK), F32)
            for kt in range(2):
                for tt in range(4):
                    j = 4 * kt + tt
                    qyhat, qr, dsilu = tiles[kt][tt]
                    dqn = _unstack_heads(dqn_all[kt], tt, h_a)
                    pend[:, _cols(C_GA, j)] = _unstack_heads(o[kt], tt, h_a) * dsilu
                    pend[:, _cols(C_Q, j)] = _half_rms_bwd(dqn, qyhat, qr, qw_v, ones)
                    dq_w = dq_w + dqn * qyhat
            dk_w = jnp.zeros((BLK, BLK), F32)
            for kt in range(2):
                kyhat, kr = bands[kt][0], bands[kt][1]
                dk = _half_rms_bwd(dkn[kt], kyhat, kr, kw_v, ones)
                dkw_part = dkn[kt] * kyhat
                dk_w = dk_w + (dkw_part[0:BLK] + dkw_part[BLK:2 * BLK])
                dv = dvb[kt]

                @pl.when(n >= 1)
                def _():
                    dp_ref[:, _cols(C_K, kt)] = (pend[:, _cols(C_K, kt)] + dk[0:BLK]).astype(BF16)
                    dp_ref[:, _cols(C_V, kt)] = (pend[:, _cols(C_V, kt)] + dv[0:BLK]).astype(BF16)

                pend[:, _cols(C_K, kt)] = dk[BLK:2 * BLK]
                pend[:, _cols(C_V, kt)] = dv[BLK:2 * BLK]
            accq[...] += dq_w
            acck[...] += dk_w
            accs[...] += dsink
            for j in range(8):
                pend[:, _cols(C_ZV, j)] = dzv[j] * sgu[j][3]

        @pl.when(n == nb)
        def _():
            tril = _tril_mask()
            dp_ref[:, C_K:C_GA] = pend[:, C_K:C_GA].astype(BF16)
            aq = accq[...]
            ak = acck[...]
            dqw_ref[...] = jnp.sum(aq + pltpu.roll(aq, HALF, 1), axis=0, keepdims=True)
            dkw_ref[...] = jnp.sum(ak + pltpu.roll(ak, HALF, 1), axis=0, keepdims=True)
            dsk_ref[...] = jnp.sum(accs[...], axis=0, keepdims=True)
            for hd in range(N_HEADS):
                dws_ref[hd] = jnp.where(tril, dws_ref[hd], 0.0)
            hrow = lax.broadcasted_iota(jnp.int32, (N_HEADS, D_ATTN), 0)
            hcol = lax.broadcasted_iota(jnp.int32, (N_HEADS, D_ATTN), 1)
            sel = jnp.where((hcol >= hrow * HALF) & (hcol < (hrow + 1) * HALF), 1.0, 0.0).astype(BF16)
            rem = accb[...]
            tot = jnp.zeros((N_HEADS, BLK), F32)
            for _ in range(3):
                part = rem.astype(BF16)
                tot = tot + lax.dot_general(sel, part, NT_DIMS, preferred_element_type=F32)
                rem = rem - part.astype(F32)
            dbs_ref[...] = tot

    smem = pl.BlockSpec(memory_space=pltpu.SMEM)
    last = nb - 1
    tile_f32 = pltpu.VMEM((BLK, BLK), F32)
    return _call(
        "mixer_bwd", compute, (nb + 1,),
        [smem, smem,
         pl.BlockSpec((BLK, D_IN), lambda n: (jnp.minimum(n, last), 0)),
         pl.BlockSpec((BLK, 512), lambda n: (jnp.maximum(jnp.minimum(n, last) - 1, 0), 2)),
         pl.BlockSpec((BLK, D_MODEL), lambda n: (jnp.minimum(n, last), 0)),
         pl.BlockSpec((1, BLK), lambda n: (0, 0)),
         pl.BlockSpec((1, BLK), lambda n: (0, 0)),
         pl.BlockSpec((N_HEADS, BLK, BLK), lambda n: (0, 0, 0)),
         pl.BlockSpec((BLK, D_ATTN), lambda n: (0, 0))],
        [pl.BlockSpec((BLK, D_IN), lambda n: (jnp.maximum(n - 1, 0), 0)),
         pl.BlockSpec((1, BLK), lambda n: (0, 0)),
         pl.BlockSpec((1, BLK), lambda n: (0, 0)),
         pl.BlockSpec((1, BLK), lambda n: (0, 0)),
         pl.BlockSpec((N_HEADS, BLK, BLK), lambda n: (0, 0, 0)),
         pl.BlockSpec((N_HEADS, BLK), lambda n: (0, 0))],
        [jax.ShapeDtypeStruct((s, D_IN), BF16),
         jax.ShapeDtypeStruct((1, BLK), F32),
         jax.ShapeDtypeStruct((1, BLK), F32),
         jax.ShapeDtypeStruct((1, BLK), F32),
         jax.ShapeDtypeStruct((N_HEADS, BLK, BLK), F32),
         jax.ShapeDtypeStruct((N_HEADS, BLK), F32)],
        [pltpu.VMEM((BLK, D_IN), F32), tile_f32, tile_f32, tile_f32, pltpu.VMEM((BLK, D_ATTN), F32),
         pltpu.VMEM((N_HEADS * BLK, BLK), BF16), pltpu.VMEM((N_HEADS // 2, BLK, 2 * BLK), BF16),
         pltpu.VMEM((N_HEADS * BLK, 2 * BLK), F32)],
        [sinks, slopes, proj, proj, dmix, qw, kw, w_s, bmap], push)


def _dh_norm_bwd(dproj, w_t, x, dx_out, g, push=None):
    s = x.shape[0]
    tm = _tile(TM_RESIDENT, s)

    def compute(dp_ref, w_ref, x_ref, dxo_ref, g_ref, dx_ref, dg_ref):
        @pl.when(pl.program_id(0) == 0)
        def _():
            dg_ref[...] = jnp.zeros_like(dg_ref)

        dh = jnp.dot(dp_ref[...], w_ref[...], preferred_element_type=F32)
        xf = x_ref[...]
        r = lax.rsqrt(jnp.mean(xf * xf, axis=-1, keepdims=True) + RMS_EPS)
        yhat = xf * r
        dyh = dh * g_ref[...]
        c = jnp.mean(dyh * yhat, axis=-1, keepdims=True)
        dx_ref[...] = dxo_ref[...] + r * (dyh - yhat * c)
        dg_ref[...] += jnp.sum(dh * yhat, axis=0, keepdims=True)

    return _call(
        "dh_norm_bwd", compute, (s // tm,),
        [pl.BlockSpec((tm, D_IN), lambda i: (i, 0)), _resident((D_IN, D_MODEL)),
         pl.BlockSpec((tm, D_MODEL), lambda i: (i, 0)), pl.BlockSpec((tm, D_MODEL), lambda i: (i, 0)),
         pl.BlockSpec((1, D_MODEL), lambda i: (0, 0))],
        [pl.BlockSpec((tm, D_MODEL), lambda i: (i, 0)), pl.BlockSpec((1, D_MODEL), lambda i: (0, 0))],
        [jax.ShapeDtypeStruct((s, D_MODEL), F32), jax.ShapeDtypeStruct((1, D_MODEL), F32)],
        [], [dproj, w_t, x, dx_out, g], push)


def _dw_in(dproj, h, push=None):
    s = h.shape[0]
    tk = _tile(DW_IN_TOK, s)
    nk = s // tk

    def compute(dp_ref, h_ref, o_ref, acc):
        k = pl.program_id(1)

        @pl.when(k == 0)
        def _():
            acc[...] = jnp.zeros_like(acc)

        acc[...] += lax.dot_general(dp_ref[...], h_ref[...], TN_DIMS, preferred_element_type=F32)

        @pl.when(k == nk - 1)
        def _():
            o_ref[...] = acc[...].astype(BF16)

    return _call(
        "dw_in", compute, (D_IN // DW_IN_ROWS, nk),
        [pl.BlockSpec((tk, DW_IN_ROWS), lambda j, k: (k, j)), pl.BlockSpec((tk, D_MODEL), lambda j, k: (k, 0))],
        [pl.BlockSpec((DW_IN_ROWS, D_MODEL), lambda j, k: (j, 0))],
        [jax.ShapeDtypeStruct((D_IN, D_MODEL), BF16)],
        [pltpu.VMEM((DW_IN_ROWS, D_MODEL), F32)], [dproj, h], push)


def _adamw(name, w, g, m, v, tr=None):
    shape = w.shape
    c = shape[-1]
    flat = [a.reshape(-1, c) for a in (w, g, m, v)]
    r = flat[0].shape[0]
    tr = r if tr is None else tr

    def compute(w_ref, g_ref, m_ref, v_ref, d_ref, mo_ref, vo_ref):
        gv = g_ref[...]
        m_new = ADAM_B1 * m_ref[...] + (1.0 - ADAM_B1) * gv
        v_new = ADAM_B2 * v_ref[...] + (1.0 - ADAM_B2) * jnp.square(gv)
        m_hat = m_new / (1.0 - ADAM_B1 ** ADAM_STEP)
        v_hat = v_new / (1.0 - ADAM_B2 ** ADAM_STEP)
        d_ref[...] = -ADAM_LR * (m_hat / (jnp.sqrt(v_hat) + ADAM_EPS) + ADAM_WD * w_ref[...])
        mo_ref[...] = m_new
        vo_ref[...] = v_new

    spec = pl.BlockSpec((tr, c), lambda i: (i, 0))
    outs = _call(name, compute, (r // tr,), [spec] * 4, [spec] * 3, [jax.ShapeDtypeStruct((r, c), F32)] * 3,
                 [], flat)
    return [o.reshape(shape) for o in outs]


def _pack_rows(parts):
    rows = []
    for a in parts:
        flat = a.reshape(-1)
        n = -(-flat.shape[0] // (8 * BLK)) * 8
        rows.append(jnp.pad(flat, (0, n * BLK - flat.shape[0])).reshape(n, BLK))
    return jnp.concatenate(rows, axis=0)


def _unpack_rows(packed, like):
    out = []
    row = 0
    for a in like:
        n = -(-a.size // (8 * BLK)) * 8
        out.append(packed[row:row + n].reshape(-1)[:a.size].reshape(a.shape))
        row += n
    return out


def kernel(x, norm_g, w_in, q_norm, k_norm, sinks, w_s, b_s, w_out, loss_target, m_norm_g, m_w_in, m_q_norm, m_k_norm, m_sinks, m_w_s, m_b_s, m_w_out, v_norm_g, v_w_in, v_q_norm, v_k_norm, v_sinks, v_w_s, v_b_s, v_w_out):
    xs = x[0]
    tgt = loss_target[0]
    slopes = jnp.asarray(2.0 ** (-8.0 * np.arange(1, N_HEADS + 1) / N_HEADS), dtype=F32)
    wt_sh = jnp.swapaxes(w_in, 1, 2).astype(BF16)
    wo_sh = w_out.astype(BF16)

    layer_par = []
    for l in range(DEPTH):
        layer_par.append((jnp.tile(q_norm[l], 2)[None, :], jnp.tile(k_norm[l], 2)[None, :],
                          jnp.repeat(b_s[l].T, HALF, axis=1)))

    wt_full = _exchange("gather_w0", _gather_rows((W_IN_SHARD,), (0,)), [wt_sh], [W_FULL[W_IN_SHARD]])[0]
    wo_full = None
    saved = []
    cur = xs
    for l in range(DEPTH):
        qw, kw, bmap = layer_par[l]
        more = l + 1 < DEPTH
        if l == 0:
            h, proj, wt_next, wo_full = _norm_proj(
                cur, norm_g[l][None, :], wt_full,
                (_gather_rows((W_IN_SHARD, W_OUT_SHARD), (1, 0)), [wt_sh, wo_sh],
                 [W_FULL[W_IN_SHARD], W_FULL[W_OUT_SHARD]]))
        else:
            res = _norm_proj(cur, norm_g[l][None, :], wt_full,
                             (_gather_rows((W_IN_SHARD,), (l + 1,)), [wt_sh], [W_FULL[W_IN_SHARD]]) if more else None)
            h, proj, wt_next = res[0], res[1], res[2] if more else None
        res2 = _mixer_fwd(proj, qw, kw, sinks[l], slopes, w_s[l], bmap,
                          (_gather_rows((W_OUT_SHARD,), (l + 1,)), [wo_sh], [W_FULL[W_OUT_SHARD]]) if more else None)
        mix = res2[0]
        saved.append((cur, h, proj, mix, wt_full, wo_full))
        if more:
            cur = _out_proj(mix, wo_full, cur)
            wt_full, wo_full = wt_next, res2[1]
        else:
            dx, sq = _out_proj_loss(mix, wo_full, cur, tgt)
    loss = lax.psum(0.5 * jnp.sum(sq) / D_MODEL, MESH_AXES)

    g_wt, g_wo, g_small, g_norm = ([None] * DEPTH for _ in range(4))
    wt_slots_shape = _slots_shape(W_IN_SHARD, D_MODEL, BF16)
    dwt_waiting = None
    for l in reversed(range(DEPTH)):
        x_l, h, proj, mix, wt_l, wo_l = saved[l]
        qw, kw, bmap = layer_par[l]
        dmix = _dmix(dx, wo_l)
        dwo_part = _dw_out(mix, dx)
        res = _mixer_bwd(proj, dmix, qw, kw, sinks[l], slopes, w_s[l], bmap,
                         None if dwt_waiting is None else
                         (_scatter_rows((W_IN_SHARD,)), [dwt_waiting], [wt_slots_shape]))
        dproj, dqw, dkw, dsk, dws, dbs = res[:6]
        if dwt_waiting is not None:
            g_wt[l + 1] = _sum_slots(res[6], W_IN_SHARD // 2)
        small_like = [dqw[0, :HALF], dkw[0, :HALF], dsk[0, :N_HEADS], dws, dbs]
        packed = _pack_rows(small_like)
        dwt_waiting, small_slots, wo_slots = _dw_in(
            dproj, h, (_gather_slots_and_scatter_rows(W_OUT_SHARD), [packed, dwo_part],
                       [_slots_shape(*packed.shape, F32), _slots_shape(W_OUT_SHARD, D_MODEL, BF16)]))
        res = _dh_norm_bwd(dproj, wt_l, x_l, dx, norm_g[l][None, :],
                           (_scatter_rows((W_IN_SHARD,)), [dwt_waiting], [wt_slots_shape]) if l == 0 else None)
        dx, dng = res[0], res[1]
        if l == 0:
            g_wt[0] = _sum_slots(res[2], W_IN_SHARD // 2)
        g_wo[l] = _sum_slots(wo_slots, W_OUT_SHARD)
        g_small[l] = _unpack_rows(_sum_slots(small_slots, packed.shape[0]), small_like)
        g_norm[l] = dng[0]

    dng_all = _pack_rows([jnp.stack(g_norm)])
    dng_slots = _exchange("gather_dnorm", _gather_slots(), [dng_all], [_slots_shape(*dng_all.shape, F32)])[0]
    gr_norm = _unpack_rows(_sum_slots(dng_slots, dng_all.shape[0]), [norm_g])[0]
    gr_qn, gr_kn, gr_sk, gr_ws, gr_bs = (jnp.stack([g_small[l][i] for l in range(DEPTH)]) for i in range(5))
    gr_w_out = jnp.stack(g_wo)

    def t(a):
        return jnp.swapaxes(a, 1, 2)

    grads = [gr_norm, jnp.stack(g_wt), gr_qn, gr_kn, gr_sk, gr_ws, gr_bs, gr_w_out]
    weights = [norm_g, t(w_in), q_norm, k_norm, sinks, w_s, b_s, w_out]
    moms = [m_norm_g, t(m_w_in), m_q_norm, m_k_norm, m_sinks, m_w_s, m_b_s, m_w_out]
    vels = [v_norm_g, t(v_w_in), v_q_norm, v_k_norm, v_sinks, v_w_s, v_b_s, v_w_out]
    tiles = [None, W_IN_SHARD // 2, None, None, None, 1024, None, 256]
    names = ["norm_g", "w_in", "q_norm", "k_norm", "sinks", "w_s", "b_s", "w_out"]
    deltas, new_m, new_v = [], [], []
    for nm, w, g, m, v, tr in zip(names, weights, grads, moms, vels, tiles):
        d, mo, vo = _adamw("adamw_" + nm, w, g, m, v, tr)
        deltas.append(d)
        new_m.append(mo)
        new_v.append(vo)
    for group in (grads, deltas, new_m, new_v):
        group[1] = t(group[1])

    return (loss, dx[None], *grads, *deltas, *new_m, *new_v)
```

```python
import numpy as np
import jax
import jax.numpy as jnp
from jax import lax
from jax.experimental import pallas as pl
from jax.experimental.pallas import tpu as pltpu

F32 = jnp.float32
BF16 = jnp.bfloat16

D_MODEL = 2048
D_ATTN = 1024
D_IN = 5632
N_HEADS = 16
DEPTH = 4
BLK = 128
HALF = 64
RMS_EPS = 1e-6
C_Q, C_K, C_V, C_GA, C_ZU, C_ZV, C_GB = 0, 1024, 1280, 1536, 2560, 3584, 4608
NEG = -1e30
N_DEV = 8
W_IN_SHARD = D_IN // N_DEV
W_OUT_SHARD = D_MODEL // N_DEV
INV_SQRT2 = 0.7071067811865476
INV_SQRT_2PI = 0.3989422804014327

TM_RESIDENT = 256
TM_STREAM = 512
TN_PROJ = 512
DW_IN_ROWS = D_IN // 4
DW_IN_TOK = 2048

ADAM_LR = 0.001
ADAM_B1 = 0.9
ADAM_B2 = 0.999
ADAM_EPS = 1e-08
ADAM_WD = 0.01
ADAM_STEP = 10

NT_DIMS = (((1,), (1,)), ((), ()))
TN_DIMS = (((0,), (0,)), ((), ()))
MESH_AXES = ("x", "y", "c")


def _sigmoid(v):
    return 1.0 / (1.0 + jnp.exp(-v))


def _gelu_cdf(z):
    return 0.5 * (1.0 + lax.erf(z * INV_SQRT2))


def _gelu_grad(z, cdf):
    return cdf + z * (jnp.exp(-0.5 * z * z) * INV_SQRT_2PI)


def _lane_halves(rows):
    lane = lax.broadcasted_iota(jnp.int32, (rows, BLK), 1)
    return lane < HALF, lane >= HALF


def _half_masks():
    return {BLK: _lane_halves(BLK), 2 * BLK: _lane_halves(2 * BLK)}


def _half_sum(v, ones):
    h_a, h_b = ones[v.shape[0]]
    s_a = jnp.sum(jnp.where(h_a, v, 0.0), axis=-1, keepdims=True)
    s_b = jnp.sum(jnp.where(h_b, v, 0.0), axis=-1, keepdims=True)
    return jnp.where(h_a, s_a, s_b)


def _half_rms(v, w, ones):
    r = lax.rsqrt(_half_sum(v * v, ones) * (1.0 / HALF) + RMS_EPS)
    yhat = v * r
    return yhat, r, yhat * w


def _half_rms_bwd(dy, yhat, r, w, ones):
    dyh = dy * w
    c = _half_sum(dyh * yhat, ones) * (1.0 / HALF)
    return r * (dyh - yhat * c)


def _band_mask(n):
    t = lax.broadcasted_iota(jnp.int32, (BLK, 2 * BLK), 0)
    kk = lax.broadcasted_iota(jnp.int32, (BLK, 2 * BLK), 1)
    dist = t + BLK - kk
    first_key = jnp.where(n > 0, 0, BLK)
    ok = (dist >= 0) & (dist < BLK) & (kk >= first_key)
    return ok, dist.astype(F32)


def _alibi_table(n, slopes_ref, bias_scr):
    ok, distf = _band_mask(n)
    for hd in range(N_HEADS):
        bias_scr[BLK * hd:BLK * (hd + 1), :] = jnp.where(ok, -(slopes_ref[hd] * distf), NEG)


def _sink_col(sinks_ref, kt):
    return jnp.concatenate([jnp.full((BLK, 1), sinks_ref[8 * kt + i], F32) for i in range(8)], axis=0)


def _softmax_sink(s_scaled, bias, sink):
    s = s_scaled + bias
    m = jnp.maximum(jnp.max(s, axis=-1, keepdims=True), sink)
    p = jnp.exp(s - m)
    es = jnp.exp(sink - m)
    inv = 1.0 / (jnp.sum(p, axis=-1, keepdims=True) + es)
    return p * inv, es * inv


def _rows(i):
    return slice(BLK * i, BLK * (i + 1))


def _cols(base, j):
    return slice(base + BLK * j, base + BLK * (j + 1))


def _to_half(v, have, want):
    return v if have == want else pltpu.roll(v, HALF, 1)


def _tril_mask():
    row = lax.broadcasted_iota(jnp.int32, (BLK, BLK), 0)
    col = lax.broadcasted_iota(jnp.int32, (BLK, BLK), 1)
    return row >= col


def _tile(limit, s):
    t = min(limit, s)
    assert s % t == 0, (s, t)
    return t


def _mesh_place():
    x, y, c = lax.axis_index("x"), lax.axis_index("y"), lax.axis_index("c")
    return x, y, c, 4 * x + 2 * y + c


def _peer(x, y, c, k):
    px = 1 - x if k & 4 else x
    py = 1 - y if k & 2 else y
    pc = 1 - c if k & 1 else c
    return (px, py, pc), 4 * px + 2 * py + pc


class _Pushes:
    def __init__(self, n_arrays, src_view, dst_view):
        self.na = n_arrays
        self.src_view = src_view
        self.dst_view = dst_view

    def scratch(self):
        n = self.na * (N_DEV - 1)
        return [pltpu.SemaphoreType.DMA((n,)), pltpu.SemaphoreType.DMA((n,)), pltpu.SemaphoreType.DMA((self.na,))]

    def copies(self, src_refs, dst_refs, send_sems, recv_sems, local_sems):
        x, y, c, me = _mesh_place()
        cps = []
        for a in range(self.na):
            cps.append(pltpu.make_async_copy(self.src_view(a, src_refs[a], me), self.dst_view(a, dst_refs[a], me),
                                             local_sems.at[a]))
        for k in range(1, N_DEV):
            peer, pidx = _peer(x, y, c, k)
            for a in range(self.na):
                sem = a * (N_DEV - 1) + k - 1
                cps.append(pltpu.make_async_remote_copy(
                    src_ref=self.src_view(a, src_refs[a], pidx), dst_ref=self.dst_view(a, dst_refs[a], me),
                    send_sem=send_sems.at[sem], recv_sem=recv_sems.at[sem],
                    device_id=peer, device_id_type=pl.DeviceIdType.MESH))
        return cps

    def plan(self, src_refs, dst_refs, send_sems, recv_sems, local_sems):
        cps = self.copies(src_refs, dst_refs, send_sems, recv_sems, local_sems)
        return cps, [], [], [cp.wait for cp in cps]


class _TwoLevelGather:
    def __init__(self, shard_rows, layers):
        self.na = len(shard_rows)
        self.shard_rows = shard_rows
        self.layers = layers

    def scratch(self):
        n = self.na * (N_DEV - 1)
        return [pltpu.SemaphoreType.DMA((n,)), pltpu.SemaphoreType.DMA((n,)), pltpu.SemaphoreType.DMA((self.na,))]

    def plan(self, src_refs, dst_refs, send_sems, recv_sems, local_sems):
        x, y, c, _ = _mesh_place()
        sibling = (x, y, 1 - c)
        chips = [(1 - x, y), (x, 1 - y), (1 - x, 1 - y)]
        start, mid_wait, mid_start, final = [], [], [], []
        for a in range(self.na):
            r = self.shard_rows[a]
            src = src_refs[a].at[self.layers[a]]
            dst = dst_refs[a]

            def rows(px, py, pc, r=r, dst=dst):
                return dst.at[pl.ds(pl.multiple_of((4 * px + 2 * py + pc) * r, 64), r), :]

            def remote(k, s_ref, block, to, a=a, rows=rows):
                return pltpu.make_async_remote_copy(
                    src_ref=s_ref, dst_ref=rows(*block),
                    send_sem=send_sems.at[a * (N_DEV - 1) + k], recv_sem=recv_sems.at[a * (N_DEV - 1) + k],
                    device_id=to, device_id_type=pl.DeviceIdType.MESH)

            mine = pltpu.make_async_copy(src, rows(x, y, c), local_sems.at[a])
            own = [remote(0, src, (x, y, c), sibling)]
            own += [remote(1 + j, src, (x, y, c), (*chip, c)) for j, chip in enumerate(chips)]
            passed = [remote(4 + j, rows(*chip, c), (*chip, c), sibling) for j, chip in enumerate(chips)]
            start += [mine] + own
            mid_wait += own[1:]
            mid_start += passed
            final += [own[0].wait_recv] + [cp.wait_recv for cp in passed]
            final += [cp.wait_send for cp in own + passed] + [mine.wait]
        return start, mid_wait, mid_start, final


def _call(name, compute, grid, in_specs, out_specs, out_shape, scratch, args, push=None):
    sem = pltpu.CompilerParams(dimension_semantics=("arbitrary",) * len(grid))
    if push is None:
        return pl.pallas_call(compute, name=name, grid=grid, in_specs=in_specs, out_specs=out_specs,
                              out_shape=out_shape, scratch_shapes=scratch, compiler_params=sem)(*args)
    pushes, srcs, xshapes = push
    n_in, n_out, n_scr, na = len(args), len(out_shape), len(scratch), pushes.na
    hbm = pl.BlockSpec(memory_space=pltpu.HBM)

    def body(*refs):
        ins, refs = refs[:n_in], refs[n_in:]
        xin, refs = refs[:na], refs[na:]
        outs, refs = refs[:n_out], refs[n_out:]
        xout, refs = refs[:na], refs[na:]
        scr, sems = refs[:n_scr], refs[n_scr:]
        start, mid_wait, mid_start, final = pushes.plan(xin, xout, *sems)
        first = pl.program_id(0) == 0
        middle = pl.program_id(0) == (grid[0] * 5) // 8
        last = pl.program_id(0) == grid[0] - 1
        for d in range(1, len(grid)):
            first = first & (pl.program_id(d) == 0)
            middle = middle & (pl.program_id(d) == 0)
            last = last & (pl.program_id(d) == grid[d] - 1)

        @pl.when(first)
        def _():
            for cp in start:
                cp.start()

        if mid_start:
            @pl.when(middle)
            def _():
                for cp in mid_wait:
                    cp.wait_recv()
                for cp in mid_start:
                    cp.start()

        compute(*ins, *outs, *scr)

        @pl.when(last)
        def _():
            for wait in final:
                wait()

    return pl.pallas_call(
        body, name=name, grid=grid,
        in_specs=list(in_specs) + [hbm] * na, out_specs=list(out_specs) + [hbm] * na,
        out_shape=list(out_shape) + list(xshapes),
        scratch_shapes=list(scratch) + pushes.scratch(), compiler_params=sem)(*args, *srcs)


def _exchange(name, pushes, srcs, out_shapes):
    na = pushes.na
    hbm = pl.BlockSpec(memory_space=pltpu.HBM)

    def body(*refs):
        start, mid_wait, mid_start, final = pushes.plan(refs[:na], refs[na:2 * na], *refs[2 * na:])
        for cp in start:
            cp.start()
        for cp in mid_wait:
            cp.wait_recv()
        for cp in mid_start:
            cp.start()
        for wait in final:
            wait()

    return pl.pallas_call(body, name=name, in_specs=[hbm] * na, out_specs=[hbm] * na, out_shape=out_shapes,
                          scratch_shapes=pushes.scratch())(*srcs)


def _gather_rows(shard_rows, layers):
    return _TwoLevelGather(shard_rows, layers)


def _scatter_rows(shard_rows):
    def src_view(a, ref, idx):
        r = shard_rows[a]
        return ref.at[pl.ds(pl.multiple_of(idx * r, 64), r), :]

    def dst_view(a, ref, idx):
        return ref.at[idx]

    return _Pushes(len(shard_rows), src_view, dst_view)


def _gather_slots():
    return _Pushes(1, lambda a, ref, idx: ref, lambda a, ref, idx: ref.at[idx])


def _gather_slots_and_scatter_rows(rows):
    def src_view(a, ref, idx):
        return ref if a == 0 else ref.at[pl.ds(pl.multiple_of(idx * rows, 64), rows), :]

    return _Pushes(2, src_view, lambda a, ref, idx: ref.at[idx])


W_FULL = {W_IN_SHARD: jax.ShapeDtypeStruct((D_IN, D_MODEL), BF16),
          W_OUT_SHARD: jax.ShapeDtypeStruct((D_MODEL, D_MODEL), BF16)}


def _slots_shape(rows, cols, dtype):
    return jax.ShapeDtypeStruct((N_DEV, rows, cols), dtype)


def _sum_slots(slots, tr):
    _, r, c = slots.shape

    def compute(s_ref, o_ref):
        tot = s_ref[0].astype(F32)
        for d in range(1, N_DEV):
            tot = tot + s_ref[d].astype(F32)
        o_ref[...] = tot

    return _call("sum_slots", compute, (r // tr,),
                 [pl.BlockSpec((N_DEV, tr, c), lambda i: (0, i, 0))], [pl.BlockSpec((tr, c), lambda i: (i, 0))],
                 [jax.ShapeDtypeStruct((r, c), F32)], [], [slots])[0]


def _resident(shape):
    return pl.BlockSpec(shape, lambda *_: (0,) * len(shape), pipeline_mode=pl.Buffered(1))


def _norm_proj(x, g, w_t, push=None):
    s = x.shape[0]
    tm = _tile(TM_RESIDENT, s)

    def compute(x_ref, g_ref, w_ref, h_ref, p_ref):
        xf = x_ref[...]
        r = lax.rsqrt(jnp.mean(xf * xf, axis=-1, keepdims=True) + RMS_EPS)
        h = ((xf * r) * g_ref[...]).astype(BF16)
        h_ref[...] = h
        for j in range(D_IN // TN_PROJ):
            cols = slice(j * TN_PROJ, (j + 1) * TN_PROJ)
            p_ref[:, cols] = lax.dot_general(h, w_ref[cols, :], NT_DIMS, preferred_element_type=F32)

    return _call(
        "norm_proj", compute, (s // tm,),
        [pl.BlockSpec((tm, D_MODEL), lambda i: (i, 0)), pl.BlockSpec((1, D_MODEL), lambda i: (0, 0)),
         _resident((D_IN, D_MODEL))],
        [pl.BlockSpec((tm, D_MODEL), lambda i: (i, 0)), pl.BlockSpec((tm, D_IN), lambda i: (i, 0))],
        [jax.ShapeDtypeStruct((s, D_MODEL), BF16), jax.ShapeDtypeStruct((s, D_IN), F32)],
        [], [x, g, w_t], push)


def _sgu_weights(ws_ref, wtril_scr, wtril_t_scr=None):
    tril = _tril_mask()
    for hd in range(N_HEADS):
        w = jnp.where(tril, ws_ref[hd], 0.0)
        wtril_scr[BLK * hd:BLK * (hd + 1), :] = w.astype(BF16)
        if wtril_t_scr is not None:
            wtril_t_scr[hd // 2, :, BLK * (hd % 2):BLK * (hd % 2 + 1)] = w.T.astype(BF16)


def _kv_band(kt, p_ref, kvp_ref, kw_v, ones):
    kband = jnp.concatenate([kvp_ref[:, _cols(0, kt)], p_ref[:, _cols(C_K, kt)]], axis=0)
    kyhat, kr, kn = _half_rms(kband, kw_v, ones)
    vband = jnp.concatenate([kvp_ref[:, _cols(256, kt)], p_ref[:, _cols(C_V, kt)]], axis=0)
    return kyhat, kr, (kn * 0.125).astype(BF16), vband.astype(BF16)


def _stack_heads(tiles, halves):
    parts = []
    for tt, tile in enumerate(tiles):
        for qh in range(2):
            parts.append(_to_half(jnp.where(halves[qh], tile, 0.0), qh, tt // 2).astype(BF16))
    return jnp.concatenate(parts, axis=0)


def _unstack_heads(stacked, tt, h_a):
    return jnp.where(h_a, _to_half(stacked[_rows(2 * tt)], tt // 2, 0), _to_half(stacked[_rows(2 * tt + 1)], tt // 2, 1))


def _mixer_fwd(proj, qw, kw, sinks, slopes, w_s, bmap, push=None):
    s = proj.shape[0]
    nb = s // BLK

    def compute(sinks_ref, slopes_ref, p_ref, kvp_ref, qw_ref, kw_ref, ws_ref, bmap_ref, mix_ref,
                wtril_scr, bias_scr):
        n = pl.program_id(0)
        h_a, h_b = _lane_halves(BLK)
        ones = _half_masks()
        halves = (h_a, h_b)
        qw_v = qw_ref[...]
        kw_v = kw_ref[...]

        @pl.when(n == 0)
        def _():
            _sgu_weights(ws_ref, wtril_scr)

        @pl.when(n <= 1)
        def _():
            _alibi_table(n, slopes_ref, bias_scr)

        bands = [_kv_band(kt, p_ref, kvp_ref, kw_v, ones) for kt in range(2)]
        sc = []
        for kt in range(2):
            qn = [_half_rms(p_ref[:, _cols(C_Q, 4 * kt + tt)], qw_v, ones)[2] for tt in range(4)]
            sc.append(lax.dot_general(_stack_heads(qn, halves), bands[kt][2], NT_DIMS, preferred_element_type=F32))
        zu, mixed = [], []
        for j in range(8):
            zu_pre = p_ref[:, _cols(C_ZU, j)]
            zv_pre = p_ref[:, _cols(C_ZV, j)]
            zu.append(zu_pre * _gelu_cdf(zu_pre))
            zvb = (zv_pre * _gelu_cdf(zv_pre)).astype(BF16)
            mixed.append(jnp.dot(wtril_scr[2 * BLK * j:2 * BLK * (j + 1), :], zvb, preferred_element_type=F32))
        o = []
        for kt in range(2):
            p, _ = _softmax_sink(sc[kt], bias_scr[8 * BLK * kt:8 * BLK * (kt + 1), :], _sink_col(sinks_ref, kt))
            o.append(jnp.dot(p.astype(BF16), bands[kt][3], preferred_element_type=F32))
        for j in range(8):
            gb = p_ref[:, _cols(C_GB, j)]
            mx = jnp.where(h_a, mixed[j][0:BLK], mixed[j][BLK:2 * BLK]) + bmap_ref[:, _cols(0, j)]
            mix_ref[:, _cols(D_ATTN, j)] = ((zu[j] * mx) * (gb * _sigmoid(gb))).astype(BF16)
        for kt in range(2):
            for tt in range(4):
                j = 4 * kt + tt
                ga = p_ref[:, _cols(C_GA, j)]
                mix_ref[:, _cols(0, j)] = (_unstack_heads(o[kt], tt, h_a) * (ga * _sigmoid(ga))).astype(BF16)

    smem = pl.BlockSpec(memory_space=pltpu.SMEM)
    return _call(
        "mixer_fwd", compute, (nb,),
        [smem, smem,
         pl.BlockSpec((BLK, D_IN), lambda n: (n, 0)),
         pl.BlockSpec((BLK, 512), lambda n: (jnp.maximum(n - 1, 0), 2)),
         pl.BlockSpec((1, BLK), lambda n: (0, 0)),
         pl.BlockSpec((1, BLK), lambda n: (0, 0)),
         pl.BlockSpec((N_HEADS, BLK, BLK), lambda n: (0, 0, 0)),
         pl.BlockSpec((BLK, D_ATTN), lambda n: (0, 0))],
        [pl.BlockSpec((BLK, D_MODEL), lambda n: (n, 0))],
        [jax.ShapeDtypeStruct((s, D_MODEL), BF16)],
        [pltpu.VMEM((N_HEADS * BLK, BLK), BF16), pltpu.VMEM((N_HEADS * BLK, 2 * BLK), F32)],
        [sinks, slopes, proj, proj, qw, kw, w_s, bmap], push)


def _out_proj(mix, w_o, x):
    s = x.shape[0]
    tm = _tile(TM_STREAM, s)

    def compute(m_ref, w_ref, x_ref, o_ref):
        o_ref[...] = x_ref[...] + jnp.dot(m_ref[...], w_ref[...], preferred_element_type=F32)

    return _call(
        "out_proj", compute, (s // tm,),
        [pl.BlockSpec((tm, D_MODEL), lambda i: (i, 0)), _resident((D_MODEL, D_MODEL)),
         pl.BlockSpec((tm, D_MODEL), lambda i: (i, 0))],
        [pl.BlockSpec((tm, D_MODEL), lambda i: (i, 0))],
        [jax.ShapeDtypeStruct((s, D_MODEL), F32)], [], [mix, w_o, x])[0]


def _out_proj_loss(mix, w_o, x, tgt):
    s = x.shape[0]
    tm = _tile(TM_STREAM, s)

    def compute(m_ref, w_ref, x_ref, t_ref, dy_ref, sq_ref):
        @pl.when(pl.program_id(0) == 0)
        def _():
            sq_ref[...] = jnp.zeros_like(sq_ref)

        y = x_ref[...] + jnp.dot(m_ref[...], w_ref[...], preferred_element_type=F32)
        e = y - t_ref[...]
        dy_ref[...] = e * (1.0 / D_MODEL)
        sq_ref[...] += jnp.sum(e * e, axis=0, keepdims=True)

    tok = pl.BlockSpec((tm, D_MODEL), lambda i: (i, 0))
    return _call(
        "out_proj_loss", compute, (s // tm,),
        [tok, _resident((D_MODEL, D_MODEL)), tok, tok],
        [tok, pl.BlockSpec((1, D_MODEL), lambda i: (0, 0))],
        [jax.ShapeDtypeStruct((s, D_MODEL), F32), jax.ShapeDtypeStruct((1, D_MODEL), F32)], [], [mix, w_o, x, tgt])


def _dmix(dx, w_o):
    s = dx.shape[0]
    tm = _tile(TM_STREAM, s)

    def compute(d_ref, w_ref, o_ref):
        o_ref[...] = lax.dot_general(d_ref[...].astype(BF16), w_ref[...], NT_DIMS, preferred_element_type=F32)

    return _call(
        "dmix", compute, (s // tm,),
        [pl.BlockSpec((tm, D_MODEL), lambda i: (i, 0)), _resident((D_MODEL, D_MODEL))],
        [pl.BlockSpec((tm, D_MODEL), lambda i: (i, 0))],
        [jax.ShapeDtypeStruct((s, D_MODEL), F32)], [], [dx, w_o])[0]


def _dw_out(mix, dx):
    s = dx.shape[0]
    tk = _tile(TM_STREAM, s)
    nk = s // tk

    def compute(m_ref, d_ref, o_ref, acc):
        k = pl.program_id(0)

        @pl.when(k == 0)
        def _():
            acc[...] = jnp.zeros_like(acc)

        acc[...] += lax.dot_general(m_ref[...], d_ref[...].astype(BF16), TN_DIMS, preferred_element_type=F32)

        @pl.when(k == nk - 1)
        def _():
            o_ref[...] = acc[...].astype(BF16)

    return _call(
        "dw_out", compute, (nk,),
        [pl.BlockSpec((tk, D_MODEL), lambda k: (k, 0)), pl.BlockSpec((tk, D_MODEL), lambda k: (k, 0))],
        [pl.BlockSpec((D_MODEL, D_MODEL), lambda k: (0, 0))],
        [jax.ShapeDtypeStruct((D_MODEL, D_MODEL), BF16)],
        [pltpu.VMEM((D_MODEL, D_MODEL), F32)], [mix, dx])[0]


def _mixer_bwd(proj, dmix, qw, kw, sinks, slopes, w_s, bmap, push=None):
    s = proj.shape[0]
    nb = s // BLK

    def compute(sinks_ref, slopes_ref, p_ref, kvp_ref, dm_ref, qw_ref, kw_ref, ws_ref, bmap_ref,
                dp_ref, dqw_ref, dkw_ref, dsk_ref, dws_ref, dbs_ref,
                pend, accq, acck, accs, accb, wtril_scr, wtril_t_scr, bias_scr):
        n = pl.program_id(0)
        h_a, h_b = _lane_halves(BLK)
        ones = _half_masks()
        halves = (h_a, h_b)
        lane = lax.broadcasted_iota(jnp.int32, (BLK, BLK), 1)

        @pl.when(n == 0)
        def _():
            accq[...] = jnp.zeros_like(accq)
            acck[...] = jnp.zeros_like(acck)
            accs[...] = jnp.zeros_like(accs)
            accb[...] = jnp.zeros_like(accb)
            dws_ref[...] = jnp.zeros_like(dws_ref)
            _sgu_weights(ws_ref, wtril_scr, wtril_t_scr)

        @pl.when(n >= 1)
        def _():
            dp_ref[:, 0:C_K] = pend[:, 0:C_K].astype(BF16)
            dp_ref[:, C_GA:D_IN] = pend[:, C_GA:D_IN].astype(BF16)

        @pl.when(n <= 1)
        def _():
            _alibi_table(n, slopes_ref, bias_scr)

        @pl.when(n < nb)
        def _():
            qw_v = qw_ref[...]
            kw_v = kw_ref[...]
            bands = [_kv_band(kt, p_ref, kvp_ref, kw_v, ones) for kt in range(2)]
            tiles, qst, dost, sc, dpm = [], [], [], [], []
            for kt in range(2):
                qn, d_o, tl = [], [], []
                for tt in range(4):
                    j = 4 * kt + tt
                    qyhat, qr, qn_t = _half_rms(p_ref[:, _cols(C_Q, j)], qw_v, ones)
                    ga = p_ref[:, _cols(C_GA, j)]
                    sg = _sigmoid(ga)
                    dma = dm_ref[:, _cols(0, j)]
                    qn.append(qn_t)
                    d_o.append(dma * (ga * sg))
                    tl.append((qyhat, qr, dma * (sg * (1.0 + ga * (1.0 - sg)))))
                tiles.append(tl)
                qst.append(_stack_heads(qn, halves))
                dost.append(_stack_heads(d_o, halves))
                sc.append(lax.dot_general(qst[kt], bands[kt][2], NT_DIMS, preferred_element_type=F32))
                dpm.append(lax.dot_general(dost[kt], bands[kt][3], NT_DIMS, preferred_element_type=F32))
            sgu = []
            for j in range(8):
                zu_pre = p_ref[:, _cols(C_ZU, j)]
                zv_pre = p_ref[:, _cols(C_ZV, j)]
                cu = _gelu_cdf(zu_pre)
                cv = _gelu_cdf(zv_pre)
                zvb = (zv_pre * cv).astype(BF16)
                sgu.append((zu_pre * cu, zvb, _gelu_grad(zu_pre, cu), _gelu_grad(zv_pre, cv),
                            jnp.dot(wtril_scr[2 * BLK * j:2 * BLK * (j + 1), :], zvb, preferred_element_type=F32)))
            dsink = jnp.zeros((BLK, BLK), F32)
            pst, dqkst = [], []
            for kt in range(2):
                p, p_sink = _softmax_sink(sc[kt], bias_scr[8 * BLK * kt:8 * BLK * (kt + 1), :],
                                          _sink_col(sinks_ref, kt))
                dsum = jnp.sum(p * dpm[kt], axis=-1, keepdims=True)
                dsink_col = -(p_sink * dsum)
                for i in range(8):
                    dsink = dsink + jnp.where(lane == 8 * kt + i, dsink_col[_rows(i)], 0.0)
                pst.append(p.astype(BF16))
                dqkst.append((p * (dpm[kt] - dsum)).astype(BF16))
            o, dqn_all, dvb, dkn = [], [], [], []
            for kt in range(2):
                o.append(jnp.dot(pst[kt], bands[kt][3], preferred_element_type=F32))
                dqn_all.append(jnp.dot(dqkst[kt], bands[kt][2], preferred_element_type=F32))
                dvb.append(lax.dot_general(pst[kt], dost[kt], TN_DIMS, preferred_element_type=F32))
                dkn.append(0.125 * lax.dot_general(dqkst[kt], qst[kt], TN_DIMS, preferred_element_type=F32))
            dms = []
            for j in range(8):
                zu, zvb, gu, gv, m_ab = sgu[j]
                gb = p_ref[:, _cols(C_GB, j)]
                dmb = dm_ref[:, _cols(D_ATTN, j)]
                mixed = jnp.where(h_a, m_ab[0:BLK], m_ab[BLK:2 * BLK]) + bmap_ref[:, _cols(0, j)]
                sgb = _sigmoid(gb)
                dgate = dmb * (gb * sgb)
                pend[:, _cols(C_ZU, j)] = (dgate * mixed) * gu
                pend[:, _cols(C_GB, j)] = (dmb * (zu * mixed)) * (sgb * (1.0 + gb * (1.0 - sgb)))
                dmixed = dgate * zu
                accb[:, _cols(0, j)] += dmixed
                dms.append(jnp.concatenate([jnp.where(h_a, dmixed, 0.0).astype(BF16),
                                            jnp.where(h_b, dmixed, 0.0).astype(BF16)], axis=0))
            dzv = []
            for j in range(8):
                dzv.append(jnp.dot(wtril_t_scr[j], dms[j], preferred_element_type=F32))
                dw_ab = lax.dot_general(dms[j], sgu[j][1], NT_DIMS, preferred_element_type=F32)
                dws_ref[2 * j] += dw_ab[0:BLK]
                dws_ref[2 * j + 1] += dw_ab[BLK:2 * BLK]
            dq_w = jnp.zeros((BLK, BLK), F32)
            for kt in range(2):
                for tt in range(4):
                    j = 4 * kt + tt
                    qyhat, qr, dsilu = tiles[kt][tt]
                    dqn = _unstack_heads(dqn_all[kt], tt, h_a)
                    pend[:, _cols(C_GA, j)] = _unstack_heads(o[kt], tt, h_a) * dsilu
                    pend[:, _cols(C_Q, j)] = _half_rms_bwd(dqn, qyhat, qr, qw_v, ones)
                    dq_w = dq_w + dqn * qyhat
            dk_w = jnp.zeros((BLK, BLK), F32)
            for kt in range(2):
                kyhat, kr = bands[kt][0], bands[kt][1]
                dk = _half_rms_bwd(dkn[kt], kyhat, kr, kw_v, ones)
                dkw_part = dkn[kt] * kyhat
                dk_w = dk_w + (dkw_part[0:BLK] + dkw_part[BLK:2 * BLK])
                dv = dvb[kt]

                @pl.when(n >= 1)
                def _():
                    dp_ref[:, _cols(C_K, kt)] = (pend[:, _cols(C_K, kt)] + dk[0:BLK]).astype(BF16)
                    dp_ref[:, _cols(C_V, kt)] = (pend[:, _cols(C_V, kt)] + dv[0:BLK]).astype(BF16)

                pend[:, _cols(C_K, kt)] = dk[BLK:2 * BLK]
                pend[:, _cols(C_V, kt)] = dv[BLK:2 * BLK]
            accq[...] += dq_w
            acck[...] += dk_w
            accs[...] += dsink
            for j in range(8):
                pend[:, _cols(C_ZV, j)] = dzv[j] * sgu[j][3]

        @pl.when(n == nb)
        def _():
            tril = _tril_mask()
            dp_ref[:, C_K:C_GA] = pend[:, C_K:C_GA].astype(BF16)
            aq = accq[...]
            ak = acck[...]
            dqw_ref[...] = jnp.sum(aq + pltpu.roll(aq, HALF, 1), axis=0, keepdims=True)
            dkw_ref[...] = jnp.sum(ak + pltpu.roll(ak, HALF, 1), axis=0, keepdims=True)
            dsk_ref[...] = jnp.sum(accs[...], axis=0, keepdims=True)
            for hd in range(N_HEADS):
                dws_ref[hd] = jnp.where(tril, dws_ref[hd], 0.0)
            hrow = lax.broadcasted_iota(jnp.int32, (N_HEADS, D_ATTN), 0)
            hcol = lax.broadcasted_iota(jnp.int32, (N_HEADS, D_ATTN), 1)
            sel = jnp.where((hcol >= hrow * HALF) & (hcol < (hrow + 1) * HALF), 1.0, 0.0).astype(BF16)
            rem = accb[...]
            tot = jnp.zeros((N_HEADS, BLK), F32)
            for _ in range(3):
                part = rem.astype(BF16)
                tot = tot + lax.dot_general(sel, part, NT_DIMS, preferred_element_type=F32)
                rem = rem - part.astype(F32)
            dbs_ref[...] = tot

    smem = pl.BlockSpec(memory_space=pltpu.SMEM)
    last = nb - 1
    tile_f32 = pltpu.VMEM((BLK, BLK), F32)
    return _call(
        "mixer_bwd", compute, (nb + 1,),
        [smem, smem,
         pl.BlockSpec((BLK, D_IN), lambda n: (jnp.minimum(n, last), 0)),
         pl.BlockSpec((BLK, 512), lambda n: (jnp.maximum(jnp.minimum(n, last) - 1, 0), 2)),
         pl.BlockSpec((BLK, D_MODEL), lambda n: (jnp.minimum(n, last), 0)),
         pl.BlockSpec((1, BLK), lambda n: (0, 0)),
         pl.BlockSpec((1, BLK), lambda n: (0, 0)),
         pl.BlockSpec((N_HEADS, BLK, BLK), lambda n: (0, 0, 0)),
         pl.BlockSpec((BLK, D_ATTN), lambda n: (0, 0))],
        [pl.BlockSpec((BLK, D_IN), lambda n: (jnp.maximum(n - 1, 0), 0)),
         pl.BlockSpec((1, BLK), lambda n: (0, 0)),
         pl.BlockSpec((1, BLK), lambda n: (0, 0)),
         pl.BlockSpec((1, BLK), lambda n: (0, 0)),
         pl.BlockSpec((N_HEADS, BLK, BLK), lambda n: (0, 0, 0)),
         pl.BlockSpec((N_HEADS, BLK), lambda n: (0, 0))],
        [jax.ShapeDtypeStruct((s, D_IN), BF16),
         jax.ShapeDtypeStruct((1, BLK), F32),
         jax.ShapeDtypeStruct((1, BLK), F32),
         jax.ShapeDtypeStruct((1, BLK), F32),
         jax.ShapeDtypeStruct((N_HEADS, BLK, BLK), F32),
         jax.ShapeDtypeStruct((N_HEADS, BLK), F32)],
        [pltpu.VMEM((BLK, D_IN), F32), tile_f32, tile_f32, tile_f32, pltpu.VMEM((BLK, D_ATTN), F32),
         pltpu.VMEM((N_HEADS * BLK, BLK), BF16), pltpu.VMEM((N_HEADS // 2, BLK, 2 * BLK), BF16),
         pltpu.VMEM((N_HEADS * BLK, 2 * BLK), F32)],
        [sinks, slopes, proj, proj, dmix, qw, kw, w_s, bmap], push)


def _dh_norm_bwd(dproj, w_t, x, dx_out, g, push=None):
    s = x.shape[0]
    tm = _tile(TM_RESIDENT, s)

    def compute(dp_ref, w_ref, x_ref, dxo_ref, g_ref, dx_ref, dg_ref):
        @pl.when(pl.program_id(0) == 0)
        def _():
            dg_ref[...] = jnp.zeros_like(dg_ref)

        dh = jnp.dot(dp_ref[...], w_ref[...], preferred_element_type=F32)
        xf = x_ref[...]
        r = lax.rsqrt(jnp.mean(xf * xf, axis=-1, keepdims=True) + RMS_EPS)
        yhat = xf * r
        dyh = dh * g_ref[...]
        c = jnp.mean(dyh * yhat, axis=-1, keepdims=True)
        dx_ref[...] = dxo_ref[...] + r * (dyh - yhat * c)
        dg_ref[...] += jnp.sum(dh * yhat, axis=0, keepdims=True)

    return _call(
        "dh_norm_bwd", compute, (s // tm,),
        [pl.BlockSpec((tm, D_IN), lambda i: (i, 0)), _resident((D_IN, D_MODEL)),
         pl.BlockSpec((tm, D_MODEL), lambda i: (i, 0)), pl.BlockSpec((tm, D_MODEL), lambda i: (i, 0)),
         pl.BlockSpec((1, D_MODEL), lambda i: (0, 0))],
        [pl.BlockSpec((tm, D_MODEL), lambda i: (i, 0)), pl.BlockSpec((1, D_MODEL), lambda i: (0, 0))],
        [jax.ShapeDtypeStruct((s, D_MODEL), F32), jax.ShapeDtypeStruct((1, D_MODEL), F32)],
        [], [dproj, w_t, x, dx_out, g], push)


def _dw_in(dproj, h, push=None):
    s = h.shape[0]
    tk = _tile(DW_IN_TOK, s)
    nk = s // tk

    def compute(dp_ref, h_ref, o_ref, acc):
        k = pl.program_id(1)

        @pl.when(k == 0)
        def _():
            acc[...] = jnp.zeros_like(acc)

        acc[...] += lax.dot_general(dp_ref[...], h_ref[...], TN_DIMS, preferred_element_type=F32)

        @pl.when(k == nk - 1)
        def _():
            o_ref[...] = acc[...].astype(BF16)

    return _call(
        "dw_in", compute, (D_IN // DW_IN_ROWS, nk),
        [pl.BlockSpec((tk, DW_IN_ROWS), lambda j, k: (k, j)), pl.BlockSpec((tk, D_MODEL), lambda j, k: (k, 0))],
        [pl.BlockSpec((DW_IN_ROWS, D_MODEL), lambda j, k: (j, 0))],
        [jax.ShapeDtypeStruct((D_IN, D_MODEL), BF16)],
        [pltpu.VMEM((DW_IN_ROWS, D_MODEL), F32)], [dproj, h], push)


def _adamw(name, w, g, m, v, tr=None):
    shape = w.shape
    c = shape[-1]
    flat = [a.reshape(-1, c) for a in (w, g, m, v)]
    r = flat[0].shape[0]
    tr = r if tr is None else tr

    def compute(w_ref, g_ref, m_ref, v_ref, d_ref, mo_ref, vo_ref):
        gv = g_ref[...]
        m_new = ADAM_B1 * m_ref[...] + (1.0 - ADAM_B1) * gv
        v_new = ADAM_B2 * v_ref[...] + (1.0 - ADAM_B2) * jnp.square(gv)
        m_hat = m_new / (1.0 - ADAM_B1 ** ADAM_STEP)
        v_hat = v_new / (1.0 - ADAM_B2 ** ADAM_STEP)
        d_ref[...] = -ADAM_LR * (m_hat / (jnp.sqrt(v_hat) + ADAM_EPS) + ADAM_WD * w_ref[...])
        mo_ref[...] = m_new
        vo_ref[...] = v_new

    spec = pl.BlockSpec((tr, c), lambda i: (i, 0))
    outs = _call(name, compute, (r // tr,), [spec] * 4, [spec] * 3, [jax.ShapeDtypeStruct((r, c), F32)] * 3,
                 [], flat)
    return [o.reshape(shape) for o in outs]


def _pack_rows(parts):
    rows = []
    for a in parts:
        flat = a.reshape(-1)
        n = -(-flat.shape[0] // (8 * BLK)) * 8
        rows.append(jnp.pad(flat, (0, n * BLK - flat.shape[0])).reshape(n, BLK))
    return jnp.concatenate(rows, axis=0)


def _unpack_rows(packed, like):
    out = []
    row = 0
    for a in like:
        n = -(-a.size // (8 * BLK)) * 8
        out.append(packed[row:row + n].reshape(-1)[:a.size].reshape(a.shape))
        row += n
    return out


def kernel(x, norm_g, w_in, q_norm, k_norm, sinks, w_s, b_s, w_out, loss_target, m_norm_g, m_w_in, m_q_norm, m_k_norm, m_sinks, m_w_s, m_b_s, m_w_out, v_norm_g, v_w_in, v_q_norm, v_k_norm, v_sinks, v_w_s, v_b_s, v_w_out):
    xs = x[0]
    tgt = loss_target[0]
    slopes = jnp.asarray(2.0 ** (-8.0 * np.arange(1, N_HEADS + 1) / N_HEADS), dtype=F32)
    wt_sh = jnp.swapaxes(w_in, 1, 2).astype(BF16)
    wo_sh = w_out.astype(BF16)

    layer_par = []
    for l in range(DEPTH):
        layer_par.append((jnp.tile(q_norm[l], 2)[None, :], jnp.tile(k_norm[l], 2)[None, :],
                          jnp.repeat(b_s[l].T, HALF, axis=1)))

    wt_full = _exchange("gather_w0", _gather_rows((W_IN_SHARD,), (0,)), [wt_sh], [W_FULL[W_IN_SHARD]])[0]
    wo_full = None
    saved = []
    cur = xs
    for l in range(DEPTH):
        qw, kw, bmap = layer_par[l]
        more = l + 1 < DEPTH
        res = _norm_proj(cur, norm_g[l][None, :], wt_full,
                         (_gather_rows((W_IN_SHARD,), (l + 1,)), [wt_sh], [W_FULL[W_IN_SHARD]]) if more else None)
        h, proj = res[0], res[1]
        if l == 0:
            mix, wo_full, wo_next = _mixer_fwd(
                proj, qw, kw, sinks[l], slopes, w_s[l], bmap,
                (_gather_rows((W_OUT_SHARD, W_OUT_SHARD), (0, 1)), [wo_sh, wo_sh], [W_FULL[W_OUT_SHARD]] * 2))
        else:
            res2 = _mixer_fwd(proj, qw, kw, sinks[l], slopes, w_s[l], bmap,
                              (_gather_rows((W_OUT_SHARD,), (l + 1,)), [wo_sh], [W_FULL[W_OUT_SHARD]]) if more else None)
            mix, wo_next = res2[0], res2[1] if more else None
        saved.append((cur, h, proj, mix, wt_full, wo_full))
        if more:
            cur = _out_proj(mix, wo_full, cur)
            wt_full, wo_full = res[2], wo_next
        else:
            dx, sq = _out_proj_loss(mix, wo_full, cur, tgt)
    loss = lax.psum(0.5 * jnp.sum(sq) / D_MODEL, MESH_AXES)

    g_wt, g_wo, g_small, g_norm = ([None] * DEPTH for _ in range(4))
    wt_slots_shape = _slots_shape(W_IN_SHARD, D_MODEL, BF16)
    dwt_waiting = None
    for l in reversed(range(DEPTH)):
        x_l, h, proj, mix, wt_l, wo_l = saved[l]
        qw, kw, bmap = layer_par[l]
        dmix = _dmix(dx, wo_l)
        dwo_part = _dw_out(mix, dx)
        res = _mixer_bwd(proj, dmix, qw, kw, sinks[l], slopes, w_s[l], bmap,
                         None if dwt_waiting is None else
                         (_scatter_rows((W_IN_SHARD,)), [dwt_waiting], [wt_slots_shape]))
        dproj, dqw, dkw, dsk, dws, dbs = res[:6]
        if dwt_waiting is not None:
            g_wt[l + 1] = _sum_slots(res[6], W_IN_SHARD // 2)
        small_like = [dqw[0, :HALF], dkw[0, :HALF], dsk[0, :N_HEADS], dws, dbs]
        packed = _pack_rows(small_like)
        dwt_waiting, small_slots, wo_slots = _dw_in(
            dproj, h, (_gather_slots_and_scatter_rows(W_OUT_SHARD), [packed, dwo_part],
                       [_slots_shape(*packed.shape, F32), _slots_shape(W_OUT_SHARD, D_MODEL, BF16)]))
        res = _dh_norm_bwd(dproj, wt_l, x_l, dx, norm_g[l][None, :],
                           (_scatter_rows((W_IN_SHARD,)), [dwt_waiting], [wt_slots_shape]) if l == 0 else None)
        dx, dng = res[0], res[1]
        if l == 0:
            g_wt[0] = _sum_slots(res[2], W_IN_SHARD // 2)
        g_wo[l] = _sum_slots(wo_slots, W_OUT_SHARD)
        g_small[l] = _unpack_rows(_sum_slots(small_slots, packed.shape[0]), small_like)
        g_norm[l] = dng[0]

    dng_all = _pack_rows([jnp.stack(g_norm)])
    dng_slots = _exchange("gather_dnorm", _gather_slots(), [dng_all], [_slots_shape(*dng_all.shape, F32)])[0]
    gr_norm = _unpack_rows(_sum_slots(dng_slots, dng_all.shape[0]), [norm_g])[0]
    gr_qn, gr_kn, gr_sk, gr_ws, gr_bs = (jnp.stack([g_small[l][i] for l in range(DEPTH)]) for i in range(5))
    gr_w_out = jnp.stack(g_wo)

    def t(a):
        return jnp.swapaxes(a, 1, 2)

    grads = [gr_norm, jnp.stack(g_wt), gr_qn, gr_kn, gr_sk, gr_ws, gr_bs, gr_w_out]
    weights = [norm_g, t(w_in), q_norm, k_norm, sinks, w_s, b_s, w_out]
    moms = [m_norm_g, t(m_w_in), m_q_norm, m_k_norm, m_sinks, m_w_s, m_b_s, m_w_out]
    vels = [v_norm_g, t(v_w_in), v_q_norm, v_k_norm, v_sinks, v_w_s, v_b_s, v_w_out]
    tiles = [None, W_IN_SHARD // 2, None, None, None, 1024, None, 256]
    names = ["norm_g", "w_in", "q_norm", "k_norm", "sinks", "w_s", "b_s", "w_out"]
    deltas, new_m, new_v = [], [], []
    for nm, w, g, m, v, tr in zip(names, weights, grads, moms, vels, tiles):
        d, mo, vo = _adamw("adamw_" + nm, w, g, m, v, tr)
        deltas.append(d)
        new_m.append(mo)
        new_v.append(vo)
    for group in (grads, deltas, new_m, new_v):
        group[1] = t(group[1])

    return (loss, dx[None], *grads, *deltas, *new_m, *new_v)
```

```python
import numpy as np
import jax
import jax.numpy as jnp
from jax import lax
from jax.experimental import pallas as pl
from jax.experimental.pallas import tpu as pltpu

F32 = jnp.float32
BF16 = jnp.bfloat16

D_MODEL = 2048
D_ATTN = 1024
D_IN = 5632
N_HEADS = 16
DEPTH = 4
BLK = 128
HALF = 64
RMS_EPS = 1e-6
C_Q, C_K, C_V, C_GA, C_ZU, C_ZV, C_GB = 0, 1024, 1280, 1536, 2560, 3584, 4608
NEG = -1e30
N_DEV = 8
W_IN_SHARD = D_IN // N_DEV
W_OUT_SHARD = D_MODEL // N_DEV
INV_SQRT2 = 0.7071067811865476
INV_SQRT_2PI = 0.3989422804014327

TM_RESIDENT = 256
TM_STREAM = 512
TN_PROJ = 512
DW_IN_ROWS = D_IN // 4
DW_IN_TOK = 2048
ADAMW_ROWS = 64

ADAM_LR = 0.001
ADAM_B1 = 0.9
ADAM_B2 = 0.999
ADAM_EPS = 1e-08
ADAM_WD = 0.01
ADAM_STEP = 10

NT_DIMS = (((1,), (1,)), ((), ()))
TN_DIMS = (((0,), (0,)), ((), ()))
MESH_AXES = ("x", "y", "c")


def _sigmoid(v):
    return 1.0 / (1.0 + jnp.exp(-v))


def _gelu_cdf(z):
    return 0.5 * (1.0 + lax.erf(z * INV_SQRT2))


def _gelu_grad(z, cdf):
    return cdf + z * (jnp.exp(-0.5 * z * z) * INV_SQRT_2PI)


def _lane_halves(rows):
    lane = lax.broadcasted_iota(jnp.int32, (rows, BLK), 1)
    return lane < HALF, lane >= HALF


def _half_masks():
    return {BLK: _lane_halves(BLK), 2 * BLK: _lane_halves(2 * BLK)}


def _half_sum(v, ones):
    h_a, h_b = ones[v.shape[0]]
    s_a = jnp.sum(jnp.where(h_a, v, 0.0), axis=-1, keepdims=True)
    s_b = jnp.sum(jnp.where(h_b, v, 0.0), axis=-1, keepdims=True)
    return jnp.where(h_a, s_a, s_b)


def _half_rms(v, w, ones):
    r = lax.rsqrt(_half_sum(v * v, ones) * (1.0 / HALF) + RMS_EPS)
    yhat = v * r
    return yhat, r, yhat * w


def _half_rms_bwd(dy, yhat, r, w, ones):
    dyh = dy * w
    c = _half_sum(dyh * yhat, ones) * (1.0 / HALF)
    return r * (dyh - yhat * c)


def _band_mask(n):
    t = lax.broadcasted_iota(jnp.int32, (BLK, 2 * BLK), 0)
    kk = lax.broadcasted_iota(jnp.int32, (BLK, 2 * BLK), 1)
    dist = t + BLK - kk
    first_key = jnp.where(n > 0, 0, BLK)
    ok = (dist >= 0) & (dist < BLK) & (kk >= first_key)
    return ok, dist.astype(F32)


def _alibi_table(n, slopes_ref, bias_scr):
    ok, distf = _band_mask(n)
    for hd in range(N_HEADS):
        bias_scr[BLK * hd:BLK * (hd + 1), :] = jnp.where(ok, -(slopes_ref[hd] * distf), NEG)


def _sink_col(sinks_ref, kt):
    return jnp.concatenate([jnp.full((BLK, 1), sinks_ref[8 * kt + i], F32) for i in range(8)], axis=0)


def _softmax_sink(s_scaled, bias, sink):
    s = s_scaled + bias
    m = jnp.maximum(jnp.max(s, axis=-1, keepdims=True), sink)
    p = jnp.exp(s - m)
    es = jnp.exp(sink - m)
    inv = 1.0 / (jnp.sum(p, axis=-1, keepdims=True) + es)
    return p * inv, es * inv


def _rows(i):
    return slice(BLK * i, BLK * (i + 1))


def _cols(base, j):
    return slice(base + BLK * j, base + BLK * (j + 1))


def _to_half(v, have, want):
    return v if have == want else pltpu.roll(v, HALF, 1)


def _tril_mask():
    row = lax.broadcasted_iota(jnp.int32, (BLK, BLK), 0)
    col = lax.broadcasted_iota(jnp.int32, (BLK, BLK), 1)
    return row >= col


def _tile(limit, s):
    t = min(limit, s)
    assert s % t == 0, (s, t)
    return t


def _mesh_place():
    x, y, c = lax.axis_index("x"), lax.axis_index("y"), lax.axis_index("c")
    return x, y, c, 4 * x + 2 * y + c


def _peer(x, y, c, k):
    px = 1 - x if k & 4 else x
    py = 1 - y if k & 2 else y
    pc = 1 - c if k & 1 else c
    return (px, py, pc), 4 * px + 2 * py + pc


class _Pushes:
    def __init__(self, n_arrays, src_view, dst_view):
        self.na = n_arrays
        self.src_view = src_view
        self.dst_view = dst_view

    def scratch(self):
        n = self.na * (N_DEV - 1)
        return [pltpu.SemaphoreType.DMA((n,)), pltpu.SemaphoreType.DMA((n,)), pltpu.SemaphoreType.DMA((self.na,))]

    def copies(self, src_refs, dst_refs, send_sems, recv_sems, local_sems):
        x, y, c, me = _mesh_place()
        cps = []
        for a in range(self.na):
            cps.append(pltpu.make_async_copy(self.src_view(a, src_refs[a], me), self.dst_view(a, dst_refs[a], me),
                                             local_sems.at[a]))
        for k in range(1, N_DEV):
            peer, pidx = _peer(x, y, c, k)
            for a in range(self.na):
                sem = a * (N_DEV - 1) + k - 1
                cps.append(pltpu.make_async_remote_copy(
                    src_ref=self.src_view(a, src_refs[a], pidx), dst_ref=self.dst_view(a, dst_refs[a], me),
                    send_sem=send_sems.at[sem], recv_sem=recv_sems.at[sem],
                    device_id=peer, device_id_type=pl.DeviceIdType.MESH))
        return cps

    def plan(self, src_refs, dst_refs, send_sems, recv_sems, local_sems):
        cps = self.copies(src_refs, dst_refs, send_sems, recv_sems, local_sems)
        return cps, [], [], [cp.wait for cp in cps]


class _TwoLevelGather:
    def __init__(self, shard_rows, layers):
        self.na = len(shard_rows)
        self.shard_rows = shard_rows
        self.layers = layers

    def scratch(self):
        n = self.na * (N_DEV - 1)
        return [pltpu.SemaphoreType.DMA((n,)), pltpu.SemaphoreType.DMA((n,)), pltpu.SemaphoreType.DMA((self.na,))]

    def plan(self, src_refs, dst_refs, send_sems, recv_sems, local_sems):
        x, y, c, _ = _mesh_place()
        sibling = (x, y, 1 - c)
        chips = [(1 - x, y), (x, 1 - y), (1 - x, 1 - y)]
        start, mid_wait, mid_start, final = [], [], [], []
        for a in range(self.na):
            r = self.shard_rows[a]
            src = src_refs[a].at[self.layers[a]]
            dst = dst_refs[a]

            def rows(px, py, pc, r=r, dst=dst):
                return dst.at[pl.ds(pl.multiple_of((4 * px + 2 * py + pc) * r, 64), r), :]

            def remote(k, s_ref, block, to, a=a, rows=rows):
                return pltpu.make_async_remote_copy(
                    src_ref=s_ref, dst_ref=rows(*block),
                    send_sem=send_sems.at[a * (N_DEV - 1) + k], recv_sem=recv_sems.at[a * (N_DEV - 1) + k],
                    device_id=to, device_id_type=pl.DeviceIdType.MESH)

            mine = pltpu.make_async_copy(src, rows(x, y, c), local_sems.at[a])
            own = [remote(0, src, (x, y, c), sibling)]
            own += [remote(1 + j, src, (x, y, c), (*chip, c)) for j, chip in enumerate(chips)]
            passed = [remote(4 + j, rows(*chip, c), (*chip, c), sibling) for j, chip in enumerate(chips)]
            start += [mine] + own
            mid_wait += own[1:]
            mid_start += passed
            final += [own[0].wait_recv] + [cp.wait_recv for cp in passed]
            final += [cp.wait_send for cp in own + passed] + [mine.wait]
        return start, mid_wait, mid_start, final


def _call(name, compute, grid, in_specs, out_specs, out_shape, scratch, args, push=None):
    sem = pltpu.CompilerParams(dimension_semantics=("arbitrary",) * len(grid))
    if push is None:
        return pl.pallas_call(compute, name=name, grid=grid, in_specs=in_specs, out_specs=out_specs,
                              out_shape=out_shape, scratch_shapes=scratch, compiler_params=sem)(*args)
    pushes, srcs, xshapes = push
    n_in, n_out, n_scr, na = len(args), len(out_shape), len(scratch), pushes.na
    hbm = pl.BlockSpec(memory_space=pltpu.HBM)

    def body(*refs):
        ins, refs = refs[:n_in], refs[n_in:]
        xin, refs = refs[:na], refs[na:]
        outs, refs = refs[:n_out], refs[n_out:]
        xout, refs = refs[:na], refs[na:]
        scr, sems = refs[:n_scr], refs[n_scr:]
        start, mid_wait, mid_start, final = pushes.plan(xin, xout, *sems)
        first = pl.program_id(0) == 0
        middle = pl.program_id(0) == (grid[0] * 5) // 8
        last = pl.program_id(0) == grid[0] - 1
        for d in range(1, len(grid)):
            first = first & (pl.program_id(d) == 0)
            middle = middle & (pl.program_id(d) == 0)
            last = last & (pl.program_id(d) == grid[d] - 1)

        @pl.when(first)
        def _():
            for cp in start:
                cp.start()

        if mid_start:
            @pl.when(middle)
            def _():
                for cp in mid_wait:
                    cp.wait_recv()
                for cp in mid_start:
                    cp.start()

        compute(*ins, *outs, *scr)

        @pl.when(last)
        def _():
            for wait in final:
                wait()

    return pl.pallas_call(
        body, name=name, grid=grid,
        in_specs=list(in_specs) + [hbm] * na, out_specs=list(out_specs) + [hbm] * na,
        out_shape=list(out_shape) + list(xshapes),
        scratch_shapes=list(scratch) + pushes.scratch(), compiler_params=sem)(*args, *srcs)


def _exchange(name, pushes, srcs, out_shapes):
    na = pushes.na
    hbm = pl.BlockSpec(memory_space=pltpu.HBM)

    def body(*refs):
        start, mid_wait, mid_start, final = pushes.plan(refs[:na], refs[na:2 * na], *refs[2 * na:])
        for cp in start:
            cp.start()
        for cp in mid_wait:
            cp.wait_recv()
        for cp in mid_start:
            cp.start()
        for wait in final:
            wait()

    return pl.pallas_call(body, name=name, in_specs=[hbm] * na, out_specs=[hbm] * na, out_shape=out_shapes,
                          scratch_shapes=pushes.scratch())(*srcs)


def _gather_rows(shard_rows, layers):
    return _TwoLevelGather(shard_rows, layers)


def _scatter_rows(shard_rows):
    def src_view(a, ref, idx):
        r = shard_rows[a]
        return ref.at[pl.ds(pl.multiple_of(idx * r, 64), r), :]

    def dst_view(a, ref, idx):
        return ref.at[idx]

    return _Pushes(len(shard_rows), src_view, dst_view)


def _gather_slots():
    return _Pushes(1, lambda a, ref, idx: ref, lambda a, ref, idx: ref.at[idx])


def _gather_slots_and_scatter_rows(rows):
    def src_view(a, ref, idx):
        return ref if a == 0 else ref.at[pl.ds(pl.multiple_of(idx * rows, 64), rows), :]

    return _Pushes(2, src_view, lambda a, ref, idx: ref.at[idx])


W_FULL = {W_IN_SHARD: jax.ShapeDtypeStruct((D_IN, D_MODEL), BF16),
          W_OUT_SHARD: jax.ShapeDtypeStruct((D_MODEL, D_MODEL), BF16)}


def _slots_shape(rows, cols, dtype):
    return jax.ShapeDtypeStruct((N_DEV, rows, cols), dtype)


def _sum_slots(slots, tr):
    _, r, c = slots.shape

    def compute(s_ref, o_ref):
        tot = s_ref[0].astype(F32)
        for d in range(1, N_DEV):
            tot = tot + s_ref[d].astype(F32)
        o_ref[...] = tot

    return _call("sum_slots", compute, (r // tr,),
                 [pl.BlockSpec((N_DEV, tr, c), lambda i: (0, i, 0))], [pl.BlockSpec((tr, c), lambda i: (i, 0))],
                 [jax.ShapeDtypeStruct((r, c), F32)], [], [slots])[0]


def _resident(shape):
    return pl.BlockSpec(shape, lambda *_: (0,) * len(shape), pipeline_mode=pl.Buffered(1))


def _norm_proj(x, g, w_t, push=None):
    s = x.shape[0]
    tm = _tile(TM_RESIDENT, s)

    def compute(x_ref, g_ref, w_ref, h_ref, p_ref):
        xf = x_ref[...]
        r = lax.rsqrt(jnp.mean(xf * xf, axis=-1, keepdims=True) + RMS_EPS)
        h = ((xf * r) * g_ref[...]).astype(BF16)
        h_ref[...] = h
        for j in range(D_IN // TN_PROJ):
            cols = slice(j * TN_PROJ, (j + 1) * TN_PROJ)
            p_ref[:, cols] = lax.dot_general(h, w_ref[cols, :], NT_DIMS, preferred_element_type=F32)

    return _call(
        "norm_proj", compute, (s // tm,),
        [pl.BlockSpec((tm, D_MODEL), lambda i: (i, 0)), pl.BlockSpec((1, D_MODEL), lambda i: (0, 0)),
         _resident((D_IN, D_MODEL))],
        [pl.BlockSpec((tm, D_MODEL), lambda i: (i, 0)), pl.BlockSpec((tm, D_IN), lambda i: (i, 0))],
        [jax.ShapeDtypeStruct((s, D_MODEL), BF16), jax.ShapeDtypeStruct((s, D_IN), F32)],
        [], [x, g, w_t], push)


def _sgu_weights(ws_ref, wtril_scr, wtril_t_scr=None):
    tril = _tril_mask()
    for hd in range(N_HEADS):
        w = jnp.where(tril, ws_ref[hd], 0.0)
        wtril_scr[BLK * hd:BLK * (hd + 1), :] = w.astype(BF16)
        if wtril_t_scr is not None:
            wtril_t_scr[hd // 2, :, BLK * (hd % 2):BLK * (hd % 2 + 1)] = w.T.astype(BF16)


def _kv_band(kt, p_ref, kvp_ref, kw_v, ones):
    kband = jnp.concatenate([kvp_ref[:, _cols(0, kt)], p_ref[:, _cols(C_K, kt)]], axis=0)
    kyhat, kr, kn = _half_rms(kband, kw_v, ones)
    vband = jnp.concatenate([kvp_ref[:, _cols(256, kt)], p_ref[:, _cols(C_V, kt)]], axis=0)
    return kyhat, kr, (kn * 0.125).astype(BF16), vband.astype(BF16)


def _stack_heads(tiles, halves):
    parts = []
    for tt, tile in enumerate(tiles):
        for qh in range(2):
            parts.append(_to_half(jnp.where(halves[qh], tile, 0.0), qh, tt // 2).astype(BF16))
    return jnp.concatenate(parts, axis=0)


def _unstack_heads(stacked, tt, h_a):
    return jnp.where(h_a, _to_half(stacked[_rows(2 * tt)], tt // 2, 0), _to_half(stacked[_rows(2 * tt + 1)], tt // 2, 1))


def _mixer_fwd(proj, qw, kw, sinks, slopes, w_s, bmap, push=None):
    s = proj.shape[0]
    nb = s // BLK

    def compute(sinks_ref, slopes_ref, p_ref, kvp_ref, qw_ref, kw_ref, ws_ref, bmap_ref, mix_ref,
                wtril_scr, bias_scr):
        n = pl.program_id(0)
        h_a, h_b = _lane_halves(BLK)
        ones = _half_masks()
        halves = (h_a, h_b)
        qw_v = qw_ref[...]
        kw_v = kw_ref[...]

        @pl.when(n == 0)
        def _():
            _sgu_weights(ws_ref, wtril_scr)

        @pl.when(n <= 1)
        def _():
            _alibi_table(n, slopes_ref, bias_scr)

        bands = [_kv_band(kt, p_ref, kvp_ref, kw_v, ones) for kt in range(2)]
        sc = []
        for kt in range(2):
            qn = [_half_rms(p_ref[:, _cols(C_Q, 4 * kt + tt)], qw_v, ones)[2] for tt in range(4)]
            sc.append(lax.dot_general(_stack_heads(qn, halves), bands[kt][2], NT_DIMS, preferred_element_type=F32))
        zu, mixed = [], []
        for j in range(8):
            zu_pre = p_ref[:, _cols(C_ZU, j)]
            zv_pre = p_ref[:, _cols(C_ZV, j)]
            zu.append(zu_pre * _gelu_cdf(zu_pre))
            zvb = (zv_pre * _gelu_cdf(zv_pre)).astype(BF16)
            mixed.append(jnp.dot(wtril_scr[2 * BLK * j:2 * BLK * (j + 1), :], zvb, preferred_element_type=F32))
        o = []
        for kt in range(2):
            p, _ = _softmax_sink(sc[kt], bias_scr[8 * BLK * kt:8 * BLK * (kt + 1), :], _sink_col(sinks_ref, kt))
            o.append(jnp.dot(p.astype(BF16), bands[kt][3], preferred_element_type=F32))
        for j in range(8):
            gb = p_ref[:, _cols(C_GB, j)]
            mx = jnp.where(h_a, mixed[j][0:BLK], mixed[j][BLK:2 * BLK]) + bmap_ref[:, _cols(0, j)]
            mix_ref[:, _cols(D_ATTN, j)] = ((zu[j] * mx) * (gb * _sigmoid(gb))).astype(BF16)
        for kt in range(2):
            for tt in range(4):
                j = 4 * kt + tt
                ga = p_ref[:, _cols(C_GA, j)]
                mix_ref[:, _cols(0, j)] = (_unstack_heads(o[kt], tt, h_a) * (ga * _sigmoid(ga))).astype(BF16)

    smem = pl.BlockSpec(memory_space=pltpu.SMEM)
    return _call(
        "mixer_fwd", compute, (nb,),
        [smem, smem,
         pl.BlockSpec((BLK, D_IN), lambda n: (n, 0)),
         pl.BlockSpec((BLK, 512), lambda n: (jnp.maximum(n - 1, 0), 2)),
         pl.BlockSpec((1, BLK), lambda n: (0, 0)),
         pl.BlockSpec((1, BLK), lambda n: (0, 0)),
         pl.BlockSpec((N_HEADS, BLK, BLK), lambda n: (0, 0, 0)),
         pl.BlockSpec((BLK, D_ATTN), lambda n: (0, 0))],
        [pl.BlockSpec((BLK, D_MODEL), lambda n: (n, 0))],
        [jax.ShapeDtypeStruct((s, D_MODEL), BF16)],
        [pltpu.VMEM((N_HEADS * BLK, BLK), BF16), pltpu.VMEM((N_HEADS * BLK, 2 * BLK), F32)],
        [sinks, slopes, proj, proj, qw, kw, w_s, bmap], push)


def _out_proj(mix, w_o, x):
    s = x.shape[0]
    tm = _tile(TM_STREAM, s)

    def compute(m_ref, w_ref, x_ref, o_ref):
        o_ref[...] = x_ref[...] + jnp.dot(m_ref[...], w_ref[...], preferred_element_type=F32)

    return _call(
        "out_proj", compute, (s // tm,),
        [pl.BlockSpec((tm, D_MODEL), lambda i: (i, 0)), _resident((D_MODEL, D_MODEL)),
         pl.BlockSpec((tm, D_MODEL), lambda i: (i, 0))],
        [pl.BlockSpec((tm, D_MODEL), lambda i: (i, 0))],
        [jax.ShapeDtypeStruct((s, D_MODEL), F32)], [], [mix, w_o, x])[0]


def _out_proj_loss(mix, w_o, x, tgt):
    s = x.shape[0]
    tm = _tile(TM_STREAM, s)

    def compute(m_ref, w_ref, x_ref, t_ref, dy_ref, dyb_ref, sq_ref):
        @pl.when(pl.program_id(0) == 0)
        def _():
            sq_ref[...] = jnp.zeros_like(sq_ref)

        y = x_ref[...] + jnp.dot(m_ref[...], w_ref[...], preferred_element_type=F32)
        e = y - t_ref[...]
        dy = e * (1.0 / D_MODEL)
        dy_ref[...] = dy
        dyb_ref[...] = dy.astype(BF16)
        sq_ref[...] += jnp.sum(e * e, axis=0, keepdims=True)

    tok = pl.BlockSpec((tm, D_MODEL), lambda i: (i, 0))
    return _call(
        "out_proj_loss", compute, (s // tm,),
        [tok, _resident((D_MODEL, D_MODEL)), tok, tok],
        [tok, tok, pl.BlockSpec((1, D_MODEL), lambda i: (0, 0))],
        [jax.ShapeDtypeStruct((s, D_MODEL), F32), jax.ShapeDtypeStruct((s, D_MODEL), BF16),
         jax.ShapeDtypeStruct((1, D_MODEL), F32)], [], [mix, w_o, x, tgt])


def _dmix(dx, w_o):
    s = dx.shape[0]
    tm = _tile(TM_STREAM, s)

    def compute(d_ref, w_ref, o_ref):
        o_ref[...] = lax.dot_general(d_ref[...], w_ref[...], NT_DIMS, preferred_element_type=F32)

    return _call(
        "dmix", compute, (s // tm,),
        [pl.BlockSpec((tm, D_MODEL), lambda i: (i, 0)), _resident((D_MODEL, D_MODEL))],
        [pl.BlockSpec((tm, D_MODEL), lambda i: (i, 0))],
        [jax.ShapeDtypeStruct((s, D_MODEL), F32)], [], [dx, w_o])[0]


def _dw_out(mix, dx):
    s = dx.shape[0]
    tk = _tile(TM_STREAM, s)
    nk = s // tk

    def compute(m_ref, d_ref, o_ref, acc):
        k = pl.program_id(0)

        @pl.when(k == 0)
        def _():
            acc[...] = jnp.zeros_like(acc)

        acc[...] += lax.dot_general(m_ref[...], d_ref[...], TN_DIMS, preferred_element_type=F32)

        @pl.when(k == nk - 1)
        def _():
            o_ref[...] = acc[...].astype(BF16)

    return _call(
        "dw_out", compute, (nk,),
        [pl.BlockSpec((tk, D_MODEL), lambda k: (k, 0)), pl.BlockSpec((tk, D_MODEL), lambda k: (k, 0))],
        [pl.BlockSpec((D_MODEL, D_MODEL), lambda k: (0, 0))],
        [jax.ShapeDtypeStruct((D_MODEL, D_MODEL), BF16)],
        [pltpu.VMEM((D_MODEL, D_MODEL), F32)], [mix, dx])[0]


def _mixer_bwd(proj, dmix, qw, kw, sinks, slopes, w_s, bmap, push=None):
    s = proj.shape[0]
    nb = s // BLK

    def compute(sinks_ref, slopes_ref, p_ref, kvp_ref, dm_ref, qw_ref, kw_ref, ws_ref, bmap_ref,
                dp_ref, dqw_ref, dkw_ref, dsk_ref, dws_ref, dbs_ref,
                pend, accq, acck, accs, accb, wtril_scr, wtril_t_scr, bias_scr):
        n = pl.program_id(0)
        h_a, h_b = _lane_halves(BLK)
        ones = _half_masks()
        halves = (h_a, h_b)
        lane = lax.broadcasted_iota(jnp.int32, (BLK, BLK), 1)

        @pl.when(n == 0)
        def _():
            accq[...] = jnp.zeros_like(accq)
            acck[...] = jnp.zeros_like(acck)
            accs[...] = jnp.zeros_like(accs)
            accb[...] = jnp.zeros_like(accb)
            dws_ref[...] = jnp.zeros_like(dws_ref)
            _sgu_weights(ws_ref, wtril_scr, wtril_t_scr)

        @pl.when(n >= 1)
        def _():
            dp_ref[:, 0:C_K] = pend[:, 0:C_K].astype(BF16)
            dp_ref[:, C_GA:D_IN] = pend[:, C_GA:D_IN].astype(BF16)

        @pl.when(n <= 1)
        def _():
            _alibi_table(n, slopes_ref, bias_scr)

        @pl.when(n < nb)
        def _():
            qw_v = qw_ref[...]
            kw_v = kw_ref[...]
            bands = [_kv_band(kt, p_ref, kvp_ref, kw_v, ones) for kt in range(2)]
            tiles, qst, dost, sc, dpm = [], [], [], [], []
            for kt in range(2):
                qn, d_o, tl = [], [], []
                for tt in range(4):
                    j = 4 * kt + tt
                    qyhat, qr, qn_t = _half_rms(p_ref[:, _cols(C_Q, j)], qw_v, ones)
                    ga = p_ref[:, _cols(C_GA, j)]
                    sg = _sigmoid(ga)
                    dma = dm_ref[:, _cols(0, j)]
                    qn.append(qn_t)
                    d_o.append(dma * (ga * sg))
                    tl.append((qyhat, qr, dma * (sg * (1.0 + ga * (1.0 - sg)))))
                tiles.append(tl)
                qst.append(_stack_heads(qn, halves))
                dost.append(_stack_heads(d_o, halves))
                sc.append(lax.dot_general(qst[kt], bands[kt][2], NT_DIMS, preferred_element_type=F32))
                dpm.append(lax.dot_general(dost[kt], bands[kt][3], NT_DIMS, preferred_element_type=F32))
            sgu = []
            for j in range(8):
                zu_pre = p_ref[:, _cols(C_ZU, j)]
                zv_pre = p_ref[:, _cols(C_ZV, j)]
                cu = _gelu_cdf(zu_pre)
                cv = _gelu_cdf(zv_pre)
                zvb = (zv_pre * cv).astype(BF16)
                sgu.append((zu_pre * cu, zvb, _gelu_grad(zu_pre, cu), _gelu_grad(zv_pre, cv),
                            jnp.dot(wtril_scr[2 * BLK * j:2 * BLK * (j + 1), :], zvb, preferred_element_type=F32)))
            dsink = jnp.zeros((BLK, BLK), F32)
            pst, dqkst = [], []
            for kt in range(2):
                p, p_sink = _softmax_sink(sc[kt], bias_scr[8 * BLK * kt:8 * BLK * (kt + 1), :],
                                          _sink_col(sinks_ref, kt))
                dsum = jnp.sum(p * dpm[kt], axis=-1, keepdims=True)
                dsink_col = -(p_sink * dsum)
                for i in range(8):
                    dsink = dsink + jnp.where(lane == 8 * kt + i, dsink_col[_rows(i)], 0.0)
                pst.append(p.astype(BF16))
                dqkst.append((p * (dpm[kt] - dsum)).astype(BF16))
            o, dqn_all, dvb, dkn = [], [], [], []
            for kt in range(2):
                o.append(jnp.dot(pst[kt], bands[kt][3], preferred_element_type=F32))
                dqn_all.append(jnp.dot(dqkst[kt], bands[kt][2], preferred_element_type=F32))
                dvb.append(lax.dot_general(pst[kt], dost[kt], TN_DIMS, preferred_element_type=F32))
                dkn.append(0.125 * lax.dot_general(dqkst[kt], qst[kt], TN_DIMS, preferred_element_type=F32))
            dms = []
            for j in range(8):
                zu, zvb, gu, gv, m_ab = sgu[j]
                gb = p_ref[:, _cols(C_GB, j)]
                dmb = dm_ref[:, _cols(D_ATTN, j)]
                mixed = jnp.where(h_a, m_ab[0:BLK], m_ab[BLK:2 * BLK]) + bmap_ref[:, _cols(0, j)]
                sgb = _sigmoid(gb)
                dgate = dmb * (gb * sgb)
                pend[:, _cols(C_ZU, j)] = (dgate * mixed) * gu
                pend[:, _cols(C_GB, j)] = (dmb * (zu * mixed)) * (sgb * (1.0 + gb * (1.0 - sgb)))
                dmixed = dgate * zu
                accb[:, _cols(0, j)] += dmixed
                dms.append(jnp.concatenate([jnp.where(h_a, dmixed, 0.0).astype(BF16),
                                            jnp.where(h_b, dmixed, 0.0).astype(BF16)], axis=0))
            dzv = []
            for j in range(8):
                dzv.append(jnp.dot(wtril_t_scr[j], dms[j], preferred_element_type=F32))
                dw_ab = lax.dot_general(dms[j], sgu[j][1], NT_DIMS, preferred_element_type=F32)
                dws_ref[2 * j] += dw_ab[0:BLK]
                dws_ref[2 * j + 1] += dw_ab[BLK:2 * BLK]
            dq_w = jnp.zeros((BLK, BLK), F32)
            for kt in range(2):
                for tt in range(4):
                    j = 4 * kt + tt
                    qyhat, qr, dsilu = tiles[kt][tt]
                    dqn = _unstack_heads(dqn_all[kt], tt, h_a)
                    pend[:, _cols(C_GA, j)] = _unstack_heads(o[kt], tt, h_a) * dsilu
                    pend[:, _cols(C_Q, j)] = _half_rms_bwd(dqn, qyhat, qr, qw_v, ones)
                    dq_w = dq_w + dqn * qyhat
            dk_w = jnp.zeros((BLK, BLK), F32)
            for kt in range(2):
                kyhat, kr = bands[kt][0], bands[kt][1]
                dk = _half_rms_bwd(dkn[kt], kyhat, kr, kw_v, ones)
                dkw_part = dkn[kt] * kyhat
                dk_w = dk_w + (dkw_part[0:BLK] + dkw_part[BLK:2 * BLK])
                dv = dvb[kt]

                @pl.when(n >= 1)
                def _():
                    dp_ref[:, _cols(C_K, kt)] = (pend[:, _cols(C_K, kt)] + dk[0:BLK]).astype(BF16)
                    dp_ref[:, _cols(C_V, kt)] = (pend[:, _cols(C_V, kt)] + dv[0:BLK]).astype(BF16)

                pend[:, _cols(C_K, kt)] = dk[BLK:2 * BLK]
                pend[:, _cols(C_V, kt)] = dv[BLK:2 * BLK]
            accq[...] += dq_w
            acck[...] += dk_w
            accs[...] += dsink
            for j in range(8):
                pend[:, _cols(C_ZV, j)] = dzv[j] * sgu[j][3]

        @pl.when(n == nb)
        def _():
            tril = _tril_mask()
            dp_ref[:, C_K:C_GA] = pend[:, C_K:C_GA].astype(BF16)
            aq = accq[...]
            ak = acck[...]
            dqw_ref[...] = jnp.sum(aq + pltpu.roll(aq, HALF, 1), axis=0, keepdims=True)
            dkw_ref[...] = jnp.sum(ak + pltpu.roll(ak, HALF, 1), axis=0, keepdims=True)
            dsk_ref[...] = jnp.sum(accs[...], axis=0, keepdims=True)
            for hd in range(N_HEADS):
                dws_ref[hd] = jnp.where(tril, dws_ref[hd], 0.0)
            hrow = lax.broadcasted_iota(jnp.int32, (N_HEADS, D_ATTN), 0)
            hcol = lax.broadcasted_iota(jnp.int32, (N_HEADS, D_ATTN), 1)
            sel = jnp.where((hcol >= hrow * HALF) & (hcol < (hrow + 1) * HALF), 1.0, 0.0).astype(BF16)
            rem = accb[...]
            tot = jnp.zeros((N_HEADS, BLK), F32)
            for _ in range(3):
                part = rem.astype(BF16)
                tot = tot + lax.dot_general(sel, part, NT_DIMS, preferred_element_type=F32)
                rem = rem - part.astype(F32)
            dbs_ref[...] = tot

    smem = pl.BlockSpec(memory_space=pltpu.SMEM)
    last = nb - 1
    tile_f32 = pltpu.VMEM((BLK, BLK), F32)
    return _call(
        "mixer_bwd", compute, (nb + 1,),
        [smem, smem,
         pl.BlockSpec((BLK, D_IN), lambda n: (jnp.minimum(n, last), 0)),
         pl.BlockSpec((BLK, 512), lambda n: (jnp.maximum(jnp.minimum(n, last) - 1, 0), 2)),
         pl.BlockSpec((BLK, D_MODEL), lambda n: (jnp.minimum(n, last), 0)),
         pl.BlockSpec((1, BLK), lambda n: (0, 0)),
         pl.BlockSpec((1, BLK), lambda n: (0, 0)),
         pl.BlockSpec((N_HEADS, BLK, BLK), lambda n: (0, 0, 0)),
         pl.BlockSpec((BLK, D_ATTN), lambda n: (0, 0))],
        [pl.BlockSpec((BLK, D_IN), lambda n: (jnp.maximum(n - 1, 0), 0)),
         pl.BlockSpec((1, BLK), lambda n: (0, 0)),
         pl.BlockSpec((1, BLK), lambda n: (0, 0)),
         pl.BlockSpec((1, BLK), lambda n: (0, 0)),
         pl.BlockSpec((N_HEADS, BLK, BLK), lambda n: (0, 0, 0)),
         pl.BlockSpec((N_HEADS, BLK), lambda n: (0, 0))],
        [jax.ShapeDtypeStruct((s, D_IN), BF16),
         jax.ShapeDtypeStruct((1, BLK), F32),
         jax.ShapeDtypeStruct((1, BLK), F32),
         jax.ShapeDtypeStruct((1, BLK), F32),
         jax.ShapeDtypeStruct((N_HEADS, BLK, BLK), F32),
         jax.ShapeDtypeStruct((N_HEADS, BLK), F32)],
        [pltpu.VMEM((BLK, D_IN), F32), tile_f32, tile_f32, tile_f32, pltpu.VMEM((BLK, D_ATTN), F32),
         pltpu.VMEM((N_HEADS * BLK, BLK), BF16), pltpu.VMEM((N_HEADS // 2, BLK, 2 * BLK), BF16),
         pltpu.VMEM((N_HEADS * BLK, 2 * BLK), F32)],
        [sinks, slopes, proj, proj, dmix, qw, kw, w_s, bmap], push)


def _dh_norm_bwd(dproj, w_t, x, dx_out, g, bf16_copy, push=None):
    s = x.shape[0]
    tm = _tile(TM_RESIDENT, s)

    def compute(dp_ref, w_ref, x_ref, dxo_ref, g_ref, dx_ref, dg_ref, *dxb_ref):
        @pl.when(pl.program_id(0) == 0)
        def _():
            dg_ref[...] = jnp.zeros_like(dg_ref)

        dh = jnp.dot(dp_ref[...], w_ref[...], preferred_element_type=F32)
        xf = x_ref[...]
        r = lax.rsqrt(jnp.mean(xf * xf, axis=-1, keepdims=True) + RMS_EPS)
        yhat = xf * r
        dyh = dh * g_ref[...]
        c = jnp.mean(dyh * yhat, axis=-1, keepdims=True)
        dx = dxo_ref[...] + r * (dyh - yhat * c)
        dx_ref[...] = dx
        if bf16_copy:
            dxb_ref[0][...] = dx.astype(BF16)
        dg_ref[...] += jnp.sum(dh * yhat, axis=0, keepdims=True)

    tok = pl.BlockSpec((tm, D_MODEL), lambda i: (i, 0))
    return _call(
        "dh_norm_bwd", compute, (s // tm,),
        [pl.BlockSpec((tm, D_IN), lambda i: (i, 0)), _resident((D_IN, D_MODEL)), tok, tok,
         pl.BlockSpec((1, D_MODEL), lambda i: (0, 0))],
        [tok, pl.BlockSpec((1, D_MODEL), lambda i: (0, 0))] + [tok] * bf16_copy,
        [jax.ShapeDtypeStruct((s, D_MODEL), F32), jax.ShapeDtypeStruct((1, D_MODEL), F32)]
        + [jax.ShapeDtypeStruct((s, D_MODEL), BF16)] * bf16_copy,
        [], [dproj, w_t, x, dx_out, g], push)


def _dw_in(dproj, h, push=None):
    s = h.shape[0]
    tk = _tile(DW_IN_TOK, s)
    nk = s // tk

    def compute(dp_ref, h_ref, o_ref, acc):
        k = pl.program_id(1)

        @pl.when(k == 0)
        def _():
            acc[...] = jnp.zeros_like(acc)

        acc[...] += lax.dot_general(dp_ref[...], h_ref[...], TN_DIMS, preferred_element_type=F32)

        @pl.when(k == nk - 1)
        def _():
            o_ref[...] = acc[...].astype(BF16)

    return _call(
        "dw_in", compute, (D_IN // DW_IN_ROWS, nk),
        [pl.BlockSpec((tk, DW_IN_ROWS), lambda j, k: (k, j)), pl.BlockSpec((tk, D_MODEL), lambda j, k: (k, 0))],
        [pl.BlockSpec((DW_IN_ROWS, D_MODEL), lambda j, k: (j, 0))],
        [jax.ShapeDtypeStruct((D_IN, D_MODEL), BF16)],
        [pltpu.VMEM((DW_IN_ROWS, D_MODEL), F32)], [dproj, h], push)


def _adamw_math(w, g, m, v):
    m_new = ADAM_B1 * m + (1.0 - ADAM_B1) * g
    v_new = ADAM_B2 * v + (1.0 - ADAM_B2) * jnp.square(g)
    m_hat = m_new / (1.0 - ADAM_B1 ** ADAM_STEP)
    v_hat = v_new / (1.0 - ADAM_B2 ** ADAM_STEP)
    return -ADAM_LR * (m_hat / (jnp.sqrt(v_hat) + ADAM_EPS) + ADAM_WD * w), m_new, v_new


def _adamw(name, w, g, m, v, tr=None):
    shape = w.shape
    c = shape[-1]
    flat = [a.reshape(-1, c) for a in (w, g, m, v)]
    r = flat[0].shape[0]
    tr = r if tr is None else tr

    def compute(w_ref, g_ref, m_ref, v_ref, d_ref, mo_ref, vo_ref):
        d_ref[...], mo_ref[...], vo_ref[...] = _adamw_math(w_ref[...], g_ref[...], m_ref[...], v_ref[...])

    spec = pl.BlockSpec((tr, c), lambda i: (i, 0))
    outs = _call(name, compute, (r // tr,), [spec] * 4, [spec] * 3, [jax.ShapeDtypeStruct((r, c), F32)] * 3,
                 [], flat)
    return [o.reshape(shape) for o in outs]


def _adamw_from_slots(name, slots, w, m, v, tr):
    n_layers, r, c = w.shape
    nt = r // tr

    def compute(*refs):
        s_refs, (w_ref, m_ref, v_ref), (g_ref, d_ref, mo_ref, vo_ref) = refs[:n_layers], refs[n_layers:n_layers + 3], \
            refs[n_layers + 3:]
        for k in range(n_layers):
            @pl.when(pl.program_id(0) == k)
            def _(k=k):
                g = s_refs[k][0].astype(F32)
                for dev in range(1, N_DEV):
                    g = g + s_refs[k][dev].astype(F32)
                g_ref[0] = g
                d_ref[0], mo_ref[0], vo_ref[0] = _adamw_math(w_ref[0], g, m_ref[0], v_ref[0])

    def slots_spec(k):
        return pl.BlockSpec((N_DEV, tr, c),
                            lambda l, i: (0, jnp.where(l == k, i, jnp.where(l < k, 0, nt - 1)), 0))

    tile = pl.BlockSpec((1, tr, c), lambda l, i: (l, i, 0))
    return _call(name, compute, (n_layers, nt), [slots_spec(k) for k in range(n_layers)] + [tile] * 3, [tile] * 4,
                 [jax.ShapeDtypeStruct((n_layers, r, c), F32)] * 4, [], list(slots) + [w, m, v])


def _pack_rows(parts):
    rows = []
    for a in parts:
        flat = a.reshape(-1)
        n = -(-flat.shape[0] // (8 * BLK)) * 8
        rows.append(jnp.pad(flat, (0, n * BLK - flat.shape[0])).reshape(n, BLK))
    return jnp.concatenate(rows, axis=0)


def _unpack_rows(packed, like):
    out = []
    row = 0
    for a in like:
        n = -(-a.size // (8 * BLK)) * 8
        out.append(packed[row:row + n].reshape(-1)[:a.size].reshape(a.shape))
        row += n
    return out


def kernel(x, norm_g, w_in, q_norm, k_norm, sinks, w_s, b_s, w_out, loss_target, m_norm_g, m_w_in, m_q_norm, m_k_norm, m_sinks, m_w_s, m_b_s, m_w_out, v_norm_g, v_w_in, v_q_norm, v_k_norm, v_sinks, v_w_s, v_b_s, v_w_out):
    xs = x[0]
    tgt = loss_target[0]
    slopes = jnp.asarray(2.0 ** (-8.0 * np.arange(1, N_HEADS + 1) / N_HEADS), dtype=F32)
    wt_sh = jnp.swapaxes(w_in, 1, 2).astype(BF16)
    wo_sh = w_out.astype(BF16)

    layer_par = []
    for l in range(DEPTH):
        layer_par.append((jnp.tile(q_norm[l], 2)[None, :], jnp.tile(k_norm[l], 2)[None, :],
                          jnp.repeat(b_s[l].T, HALF, axis=1)))

    wt_full = _exchange("gather_w0", _gather_rows((W_IN_SHARD,), (0,)), [wt_sh], [W_FULL[W_IN_SHARD]])[0]
    wo_full = None
    saved = []
    cur = xs
    for l in range(DEPTH):
        qw, kw, bmap = layer_par[l]
        more = l + 1 < DEPTH
        res = _norm_proj(cur, norm_g[l][None, :], wt_full,
                         (_gather_rows((W_IN_SHARD,), (l + 1,)), [wt_sh], [W_FULL[W_IN_SHARD]]) if more else None)
        h, proj = res[0], res[1]
        if l == 0:
            mix, wo_full, wo_next = _mixer_fwd(
                proj, qw, kw, sinks[l], slopes, w_s[l], bmap,
                (_gather_rows((W_OUT_SHARD, W_OUT_SHARD), (0, 1)), [wo_sh, wo_sh], [W_FULL[W_OUT_SHARD]] * 2))
        else:
            res2 = _mixer_fwd(proj, qw, kw, sinks[l], slopes, w_s[l], bmap,
                              (_gather_rows((W_OUT_SHARD,), (l + 1,)), [wo_sh], [W_FULL[W_OUT_SHARD]]) if more else None)
            mix, wo_next = res2[0], res2[1] if more else None
        saved.append((cur, h, proj, mix, wt_full, wo_full))
        if more:
            cur = _out_proj(mix, wo_full, cur)
            wt_full, wo_full = res[2], wo_next
        else:
            dx, dx_b, sq = _out_proj_loss(mix, wo_full, cur, tgt)
    loss = lax.psum(0.5 * jnp.sum(sq) / D_MODEL, MESH_AXES)

    wt_slots, wo_slots, g_small, g_norm = ([None] * DEPTH for _ in range(4))
    wt_slots_shape = _slots_shape(W_IN_SHARD, D_MODEL, BF16)
    dwt_waiting = None
    for l in reversed(range(DEPTH)):
        x_l, h, proj, mix, wt_l, wo_l = saved[l]
        qw, kw, bmap = layer_par[l]
        dmix = _dmix(dx_b, wo_l)
        dwo_part = _dw_out(mix, dx_b)
        res = _mixer_bwd(proj, dmix, qw, kw, sinks[l], slopes, w_s[l], bmap,
                         None if dwt_waiting is None else
                         (_scatter_rows((W_IN_SHARD,)), [dwt_waiting], [wt_slots_shape]))
        dproj, dqw, dkw, dsk, dws, dbs = res[:6]
        if dwt_waiting is not None:
            wt_slots[l + 1] = res[6]
        small_like = [dqw[0, :HALF], dkw[0, :HALF], dsk[0, :N_HEADS], dws, dbs]
        packed = _pack_rows(small_like)
        dwt_waiting, small_slots, wo_slots[l] = _dw_in(
            dproj, h, (_gather_slots_and_scatter_rows(W_OUT_SHARD), [packed, dwo_part],
                       [_slots_shape(*packed.shape, F32), _slots_shape(W_OUT_SHARD, D_MODEL, BF16)]))
        if l > 0:
            dx, dng, dx_b = _dh_norm_bwd(dproj, wt_l, x_l, dx, norm_g[l][None, :], True)
        else:
            dx, dng, wt_slots[0] = _dh_norm_bwd(dproj, wt_l, x_l, dx, norm_g[l][None, :], False,
                                                (_scatter_rows((W_IN_SHARD,)), [dwt_waiting], [wt_slots_shape]))
        g_small[l] = _unpack_rows(_sum_slots(small_slots, packed.shape[0]), small_like)
        g_norm[l] = dng[0]

    dng_all = _pack_rows([jnp.stack(g_norm)])
    dng_slots = _exchange("gather_dnorm", _gather_slots(), [dng_all], [_slots_shape(*dng_all.shape, F32)])[0]
    gr_norm = _unpack_rows(_sum_slots(dng_slots, dng_all.shape[0]), [norm_g])[0]
    gr_qn, gr_kn, gr_sk, gr_ws, gr_bs = (jnp.stack([g_small[l][i] for l in range(DEPTH)]) for i in range(5))

    def t(a):
        return jnp.swapaxes(a, 1, 2)

    from_slots = {1: _adamw_from_slots("adamw_w_in", wt_slots, t(w_in), t(m_w_in), t(v_w_in), ADAMW_ROWS),
                  7: _adamw_from_slots("adamw_w_out", wo_slots, w_out, m_w_out, v_w_out, ADAMW_ROWS)}
    from_slots[1] = [t(a) for a in from_slots[1]]

    grads = [gr_norm, None, gr_qn, gr_kn, gr_sk, gr_ws, gr_bs, None]
    weights = [norm_g, w_in, q_norm, k_norm, sinks, w_s, b_s, w_out]
    moms = [m_norm_g, m_w_in, m_q_norm, m_k_norm, m_sinks, m_w_s, m_b_s, m_w_out]
    vels = [v_norm_g, v_w_in, v_q_norm, v_k_norm, v_sinks, v_w_s, v_b_s, v_w_out]
    tiles = [None, None, None, None, None, 1024, None, None]
    names = ["norm_g", "w_in", "q_norm", "k_norm", "sinks", "w_s", "b_s", "w_out"]
    deltas, new_m, new_v = [], [], []
    for i, (nm, w, g, m, v, tr) in enumerate(zip(names, weights, grads, moms, vels, tiles)):
        if i in from_slots:
            grads[i], d, mo, vo = from_slots[i]
        else:
            d, mo, vo = _adamw("adamw_" + nm, w, g, m, v, tr)
        deltas.append(d)
        new_m.append(mo)
        new_v.append(vo)

    return (loss, dx[None], *grads, *deltas, *new_m, *new_v)
```

```python
import numpy as np
import jax
import jax.numpy as jnp
from jax import lax
from jax.experimental import pallas as pl
from jax.experimental.pallas import tpu as pltpu

F32 = jnp.float32
BF16 = jnp.bfloat16

D_MODEL = 2048
D_ATTN = 1024
D_IN = 5632
N_HEADS = 16
DEPTH = 4
BLK = 128
HALF = 64
RMS_EPS = 1e-6
C_Q, C_K, C_V, C_GA, C_ZU, C_ZV, C_GB = 0, 1024, 1280, 1536, 2560, 3584, 4608
NEG = -1e30
N_DEV = 8
W_IN_SHARD = D_IN // N_DEV
W_OUT_SHARD = D_MODEL // N_DEV
INV_SQRT2 = 0.7071067811865476
INV_SQRT_2PI = 0.3989422804014327

TM_RESIDENT = 256
TM_STREAM = 512
TM_WIDE = 1024
TN_PROJ = 512
DW_IN_ROWS = D_IN // 4
DW_IN_TOK = 2048
ADAMW_ROWS = 64

ADAM_LR = 0.001
ADAM_B1 = 0.9
ADAM_B2 = 0.999
ADAM_EPS = 1e-08
ADAM_WD = 0.01
ADAM_STEP = 10

NT_DIMS = (((1,), (1,)), ((), ()))
TN_DIMS = (((0,), (0,)), ((), ()))
MESH_AXES = ("x", "y", "c")


def _sigmoid(v):
    return 1.0 / (1.0 + jnp.exp(-v))


def _gelu_cdf(z):
    return 0.5 * (1.0 + lax.erf(z * INV_SQRT2))


def _gelu_grad(z, cdf):
    return cdf + z * (jnp.exp(-0.5 * z * z) * INV_SQRT_2PI)


def _lane_halves(rows):
    lane = lax.broadcasted_iota(jnp.int32, (rows, BLK), 1)
    return lane < HALF, lane >= HALF


def _half_masks():
    return {BLK: _lane_halves(BLK), 2 * BLK: _lane_halves(2 * BLK)}


def _half_sum(v, ones):
    h_a, h_b = ones[v.shape[0]]
    s_a = jnp.sum(jnp.where(h_a, v, 0.0), axis=-1, keepdims=True)
    s_b = jnp.sum(jnp.where(h_b, v, 0.0), axis=-1, keepdims=True)
    return jnp.where(h_a, s_a, s_b)


def _half_rms(v, w, ones):
    r = lax.rsqrt(_half_sum(v * v, ones) * (1.0 / HALF) + RMS_EPS)
    yhat = v * r
    return yhat, r, yhat * w


def _half_rms_bwd(dy, yhat, r, w, ones):
    dyh = dy * w
    c = _half_sum(dyh * yhat, ones) * (1.0 / HALF)
    return r * (dyh - yhat * c)


def _band_mask(n):
    t = lax.broadcasted_iota(jnp.int32, (BLK, 2 * BLK), 0)
    kk = lax.broadcasted_iota(jnp.int32, (BLK, 2 * BLK), 1)
    dist = t + BLK - kk
    first_key = jnp.where(n > 0, 0, BLK)
    ok = (dist >= 0) & (dist < BLK) & (kk >= first_key)
    return ok, dist.astype(F32)


def _alibi_table(n, slopes_ref, bias_scr):
    ok, distf = _band_mask(n)
    for hd in range(N_HEADS):
        bias_scr[BLK * hd:BLK * (hd + 1), :] = jnp.where(ok, -(slopes_ref[hd] * distf), NEG)


def _sink_col(sinks_ref, kt):
    return jnp.concatenate([jnp.full((BLK, 1), sinks_ref[8 * kt + i], F32) for i in range(8)], axis=0)


def _softmax_sink(s_scaled, bias, sink):
    s = s_scaled + bias
    m = jnp.maximum(jnp.max(s, axis=-1, keepdims=True), sink)
    p = jnp.exp(s - m)
    es = jnp.exp(sink - m)
    inv = 1.0 / (jnp.sum(p, axis=-1, keepdims=True) + es)
    return p * inv, es * inv


def _rows(i):
    return slice(BLK * i, BLK * (i + 1))


def _cols(base, j):
    return slice(base + BLK * j, base + BLK * (j + 1))


def _to_half(v, have, want):
    return v if have == want else pltpu.roll(v, HALF, 1)


def _tril_mask():
    row = lax.broadcasted_iota(jnp.int32, (BLK, BLK), 0)
    col = lax.broadcasted_iota(jnp.int32, (BLK, BLK), 1)
    return row >= col


def _tile(limit, s):
    t = min(limit, s)
    assert s % t == 0, (s, t)
    return t


def _mesh_place():
    x, y, c = lax.axis_index("x"), lax.axis_index("y"), lax.axis_index("c")
    return x, y, c, 4 * x + 2 * y + c


def _peer(x, y, c, k):
    px = 1 - x if k & 4 else x
    py = 1 - y if k & 2 else y
    pc = 1 - c if k & 1 else c
    return (px, py, pc), 4 * px + 2 * py + pc


class _Pushes:
    def __init__(self, n_arrays, src_view, dst_view):
        self.na = n_arrays
        self.src_view = src_view
        self.dst_view = dst_view

    def scratch(self):
        n = self.na * (N_DEV - 1)
        return [pltpu.SemaphoreType.DMA((n,)), pltpu.SemaphoreType.DMA((n,)), pltpu.SemaphoreType.DMA((self.na,))]

    def copies(self, src_refs, dst_refs, send_sems, recv_sems, local_sems):
        x, y, c, me = _mesh_place()
        cps = []
        for a in range(self.na):
            cps.append(pltpu.make_async_copy(self.src_view(a, src_refs[a], me), self.dst_view(a, dst_refs[a], me),
                                             local_sems.at[a]))
        for k in range(1, N_DEV):
            peer, pidx = _peer(x, y, c, k)
            for a in range(self.na):
                sem = a * (N_DEV - 1) + k - 1
                cps.append(pltpu.make_async_remote_copy(
                    src_ref=self.src_view(a, src_refs[a], pidx), dst_ref=self.dst_view(a, dst_refs[a], me),
                    send_sem=send_sems.at[sem], recv_sem=recv_sems.at[sem],
                    device_id=peer, device_id_type=pl.DeviceIdType.MESH))
        return cps

    def plan(self, src_refs, dst_refs, send_sems, recv_sems, local_sems):
        cps = self.copies(src_refs, dst_refs, send_sems, recv_sems, local_sems)
        return cps, [], [], [cp.wait for cp in cps]


class _TwoLevelGather:
    def __init__(self, shard_rows, layers):
        self.na = len(shard_rows)
        self.shard_rows = shard_rows
        self.layers = layers

    def scratch(self):
        n = self.na * (N_DEV - 1)
        return [pltpu.SemaphoreType.DMA((n,)), pltpu.SemaphoreType.DMA((n,)), pltpu.SemaphoreType.DMA((self.na,))]

    def plan(self, src_refs, dst_refs, send_sems, recv_sems, local_sems):
        x, y, c, _ = _mesh_place()
        sibling = (x, y, 1 - c)
        chips = [(1 - x, y), (x, 1 - y), (1 - x, 1 - y)]
        start, mid_wait, mid_start, final = [], [], [], []
        for a in range(self.na):
            r = self.shard_rows[a]
            src = src_refs[a].at[self.layers[a]]
            dst = dst_refs[a]

            def rows(px, py, pc, r=r, dst=dst):
                return dst.at[pl.ds(pl.multiple_of((4 * px + 2 * py + pc) * r, 64), r), :]

            def remote(k, s_ref, block, to, a=a, rows=rows):
                return pltpu.make_async_remote_copy(
                    src_ref=s_ref, dst_ref=rows(*block),
                    send_sem=send_sems.at[a * (N_DEV - 1) + k], recv_sem=recv_sems.at[a * (N_DEV - 1) + k],
                    device_id=to, device_id_type=pl.DeviceIdType.MESH)

            mine = pltpu.make_async_copy(src, rows(x, y, c), local_sems.at[a])
            own = [remote(0, src, (x, y, c), sibling)]
            own += [remote(1 + j, src, (x, y, c), (*chip, c)) for j, chip in enumerate(chips)]
            passed = [remote(4 + j, rows(*chip, c), (*chip, c), sibling) for j, chip in enumerate(chips)]
            start += [mine] + own
            mid_wait += own[1:]
            mid_start += passed
            final += [own[0].wait_recv] + [cp.wait_recv for cp in passed]
            final += [cp.wait_send for cp in own + passed] + [mine.wait]
        return start, mid_wait, mid_start, final


def _call(name, compute, grid, in_specs, out_specs, out_shape, scratch, args, push=None):
    sem = pltpu.CompilerParams(dimension_semantics=("arbitrary",) * len(grid))
    if push is None:
        return pl.pallas_call(compute, name=name, grid=grid, in_specs=in_specs, out_specs=out_specs,
                              out_shape=out_shape, scratch_shapes=scratch, compiler_params=sem)(*args)
    pushes, srcs, xshapes = push
    n_in, n_out, n_scr, na = len(args), len(out_shape), len(scratch), pushes.na
    hbm = pl.BlockSpec(memory_space=pltpu.HBM)

    def body(*refs):
        ins, refs = refs[:n_in], refs[n_in:]
        xin, refs = refs[:na], refs[na:]
        outs, refs = refs[:n_out], refs[n_out:]
        xout, refs = refs[:na], refs[na:]
        scr, sems = refs[:n_scr], refs[n_scr:]
        start, mid_wait, mid_start, final = pushes.plan(xin, xout, *sems)
        first = pl.program_id(0) == 0
        middle = pl.program_id(0) == (grid[0] * 5) // 8
        last = pl.program_id(0) == grid[0] - 1
        for d in range(1, len(grid)):
            first = first & (pl.program_id(d) == 0)
            middle = middle & (pl.program_id(d) == 0)
            last = last & (pl.program_id(d) == grid[d] - 1)

        @pl.when(first)
        def _():
            for cp in start:
                cp.start()

        if mid_start:
            @pl.when(middle)
            def _():
                for cp in mid_wait:
                    cp.wait_recv()
                for cp in mid_start:
                    cp.start()

        compute(*ins, *outs, *scr)

        @pl.when(last)
        def _():
            for wait in final:
                wait()

    return pl.pallas_call(
        body, name=name, grid=grid,
        in_specs=list(in_specs) + [hbm] * na, out_specs=list(out_specs) + [hbm] * na,
        out_shape=list(out_shape) + list(xshapes),
        scratch_shapes=list(scratch) + pushes.scratch(), compiler_params=sem)(*args, *srcs)


def _exchange(name, pushes, srcs, out_shapes):
    na = pushes.na
    hbm = pl.BlockSpec(memory_space=pltpu.HBM)

    def body(*refs):
        start, mid_wait, mid_start, final = pushes.plan(refs[:na], refs[na:2 * na], *refs[2 * na:])
        for cp in start:
            cp.start()
        for cp in mid_wait:
            cp.wait_recv()
        for cp in mid_start:
            cp.start()
        for wait in final:
            wait()

    return pl.pallas_call(body, name=name, in_specs=[hbm] * na, out_specs=[hbm] * na, out_shape=out_shapes,
                          scratch_shapes=pushes.scratch())(*srcs)


def _gather_rows(shard_rows, layers):
    return _TwoLevelGather(shard_rows, layers)


def _scatter_rows(shard_rows):
    def src_view(a, ref, idx):
        r = shard_rows[a]
        return ref.at[pl.ds(pl.multiple_of(idx * r, 64), r), :]

    def dst_view(a, ref, idx):
        return ref.at[idx]

    return _Pushes(len(shard_rows), src_view, dst_view)


def _gather_slots():
    return _Pushes(1, lambda a, ref, idx: ref, lambda a, ref, idx: ref.at[idx])


def _gather_slots_and_scatter_rows(rows):
    def src_view(a, ref, idx):
        return ref if a == 0 else ref.at[pl.ds(pl.multiple_of(idx * rows, 64), rows), :]

    return _Pushes(2, src_view, lambda a, ref, idx: ref.at[idx])


W_FULL = {W_IN_SHARD: jax.ShapeDtypeStruct((D_IN, D_MODEL), BF16),
          W_OUT_SHARD: jax.ShapeDtypeStruct((D_MODEL, D_MODEL), BF16)}


def _slots_shape(rows, cols, dtype):
    return jax.ShapeDtypeStruct((N_DEV, rows, cols), dtype)


def _sum_slots(slots, tr):
    _, r, c = slots.shape

    def compute(s_ref, o_ref):
        tot = s_ref[0].astype(F32)
        for d in range(1, N_DEV):
            tot = tot + s_ref[d].astype(F32)
        o_ref[...] = tot

    return _call("sum_slots", compute, (r // tr,),
                 [pl.BlockSpec((N_DEV, tr, c), lambda i: (0, i, 0))], [pl.BlockSpec((tr, c), lambda i: (i, 0))],
                 [jax.ShapeDtypeStruct((r, c), F32)], [], [slots])[0]


def _resident(shape):
    return pl.BlockSpec(shape, lambda *_: (0,) * len(shape), pipeline_mode=pl.Buffered(1))


def _norm_proj(x, g, w_t, push=None):
    s = x.shape[0]
    tm = _tile(TM_RESIDENT, s)

    def compute(x_ref, g_ref, w_ref, h_ref, p_ref):
        xf = x_ref[...]
        r = lax.rsqrt(jnp.mean(xf * xf, axis=-1, keepdims=True) + RMS_EPS)
        h = ((xf * r) * g_ref[...]).astype(BF16)
        h_ref[...] = h
        for j in range(D_IN // TN_PROJ):
            cols = slice(j * TN_PROJ, (j + 1) * TN_PROJ)
            p_ref[:, cols] = lax.dot_general(h, w_ref[cols, :], NT_DIMS, preferred_element_type=F32)

    return _call(
        "norm_proj", compute, (s // tm,),
        [pl.BlockSpec((tm, D_MODEL), lambda i: (i, 0)), pl.BlockSpec((1, D_MODEL), lambda i: (0, 0)),
         _resident((D_IN, D_MODEL))],
        [pl.BlockSpec((tm, D_MODEL), lambda i: (i, 0)), pl.BlockSpec((tm, D_IN), lambda i: (i, 0))],
        [jax.ShapeDtypeStruct((s, D_MODEL), BF16), jax.ShapeDtypeStruct((s, D_IN), F32)],
        [], [x, g, w_t], push)


def _sgu_weights(ws_ref, wtril_scr, wtril_t_scr=None):
    tril = _tril_mask()
    for hd in range(N_HEADS):
        w = jnp.where(tril, ws_ref[hd], 0.0)
        wtril_scr[BLK * hd:BLK * (hd + 1), :] = w.astype(BF16)
        if wtril_t_scr is not None:
            wtril_t_scr[hd // 2, :, BLK * (hd % 2):BLK * (hd % 2 + 1)] = w.T.astype(BF16)


def _kv_band(kt, p_ref, kvp_ref, kw_v, ones):
    kband = jnp.concatenate([kvp_ref[:, _cols(0, kt)], p_ref[:, _cols(C_K, kt)]], axis=0)
    kyhat, kr, kn = _half_rms(kband, kw_v, ones)
    vband = jnp.concatenate([kvp_ref[:, _cols(256, kt)], p_ref[:, _cols(C_V, kt)]], axis=0)
    return kyhat, kr, (kn * 0.125).astype(BF16), vband.astype(BF16)


def _stack_heads(tiles, halves):
    parts = []
    for tt, tile in enumerate(tiles):
        for qh in range(2):
            parts.append(_to_half(jnp.where(halves[qh], tile, 0.0), qh, tt // 2).astype(BF16))
    return jnp.concatenate(parts, axis=0)


def _unstack_heads(stacked, tt, h_a):
    return jnp.where(h_a, _to_half(stacked[_rows(2 * tt)], tt // 2, 0), _to_half(stacked[_rows(2 * tt + 1)], tt // 2, 1))


def _mixer_fwd(proj, qw, kw, sinks, slopes, w_s, bmap, push=None):
    s = proj.shape[0]
    nb = s // BLK

    def compute(sinks_ref, slopes_ref, p_ref, kvp_ref, qw_ref, kw_ref, ws_ref, bmap_ref, mix_ref,
                wtril_scr, bias_scr):
        _mixer_tables(pl.program_id(0), slopes_ref, ws_ref, wtril_scr, bias_scr)
        _mixer_fwd_block(sinks_ref, p_ref, kvp_ref, qw_ref, kw_ref, bmap_ref, mix_ref, wtril_scr, bias_scr)

    smem = pl.BlockSpec(memory_space=pltpu.SMEM)
    return _call(
        "mixer_fwd", compute, (nb,),
        [smem, smem,
         pl.BlockSpec((BLK, D_IN), lambda n: (n, 0)),
         pl.BlockSpec((BLK, 512), lambda n: (jnp.maximum(n - 1, 0), 2)),
         pl.BlockSpec((1, BLK), lambda n: (0, 0)),
         pl.BlockSpec((1, BLK), lambda n: (0, 0)),
         pl.BlockSpec((N_HEADS, BLK, BLK), lambda n: (0, 0, 0)),
         pl.BlockSpec((BLK, D_ATTN), lambda n: (0, 0))],
        [pl.BlockSpec((BLK, D_MODEL), lambda n: (n, 0))],
        [jax.ShapeDtypeStruct((s, D_MODEL), BF16)],
        [pltpu.VMEM((N_HEADS * BLK, BLK), BF16), pltpu.VMEM((N_HEADS * BLK, 2 * BLK), F32)],
        [sinks, slopes, proj, proj, qw, kw, w_s, bmap], push)


def _mixer_tables(n, slopes_ref, ws_ref, wtril_scr, bias_scr, wtril_t_scr=None):
    @pl.when(n == 0)
    def _():
        _sgu_weights(ws_ref, wtril_scr, wtril_t_scr)

    @pl.when(n <= 1)
    def _():
        _alibi_table(n, slopes_ref, bias_scr)


def _mixer_fwd_block(sinks_ref, p_ref, kvp_ref, qw_ref, kw_ref, bmap_ref, mix_ref, wtril_scr, bias_scr):
    h_a, h_b = _lane_halves(BLK)
    ones = _half_masks()
    halves = (h_a, h_b)
    qw_v = qw_ref[...]
    kw_v = kw_ref[...]
    bands = [_kv_band(kt, p_ref, kvp_ref, kw_v, ones) for kt in range(2)]
    sc = []
    for kt in range(2):
        qn = [_half_rms(p_ref[:, _cols(C_Q, 4 * kt + tt)], qw_v, ones)[2] for tt in range(4)]
        sc.append(lax.dot_general(_stack_heads(qn, halves), bands[kt][2], NT_DIMS, preferred_element_type=F32))
    zu, mixed = [], []
    for j in range(8):
        zu_pre = p_ref[:, _cols(C_ZU, j)]
        zv_pre = p_ref[:, _cols(C_ZV, j)]
        zu.append(zu_pre * _gelu_cdf(zu_pre))
        zvb = (zv_pre * _gelu_cdf(zv_pre)).astype(BF16)
        mixed.append(jnp.dot(wtril_scr[2 * BLK * j:2 * BLK * (j + 1), :], zvb, preferred_element_type=F32))
    o = []
    for kt in range(2):
        p, _ = _softmax_sink(sc[kt], bias_scr[8 * BLK * kt:8 * BLK * (kt + 1), :], _sink_col(sinks_ref, kt))
        o.append(jnp.dot(p.astype(BF16), bands[kt][3], preferred_element_type=F32))
    for j in range(8):
        gb = p_ref[:, _cols(C_GB, j)]
        mx = jnp.where(h_a, mixed[j][0:BLK], mixed[j][BLK:2 * BLK]) + bmap_ref[:, _cols(0, j)]
        mix_ref[:, _cols(D_ATTN, j)] = ((zu[j] * mx) * (gb * _sigmoid(gb))).astype(BF16)
    for kt in range(2):
        for tt in range(4):
            j = 4 * kt + tt
            ga = p_ref[:, _cols(C_GA, j)]
            mix_ref[:, _cols(0, j)] = (_unstack_heads(o[kt], tt, h_a) * (ga * _sigmoid(ga))).astype(BF16)


def _out_proj(mix, w_o, x):
    s = x.shape[0]
    tm = _tile(TM_WIDE, s)

    def compute(m_ref, w_ref, x_ref, o_ref):
        o_ref[...] = x_ref[...] + jnp.dot(m_ref[...], w_ref[...], preferred_element_type=F32)

    return _call(
        "out_proj", compute, (s // tm,),
        [pl.BlockSpec((tm, D_MODEL), lambda i: (i, 0)), _resident((D_MODEL, D_MODEL)),
         pl.BlockSpec((tm, D_MODEL), lambda i: (i, 0))],
        [pl.BlockSpec((tm, D_MODEL), lambda i: (i, 0))],
        [jax.ShapeDtypeStruct((s, D_MODEL), F32)], [], [mix, w_o, x])[0]


def _out_proj_loss(mix, w_o, x, tgt):
    s = x.shape[0]
    tm = _tile(TM_STREAM, s)

    def compute(m_ref, w_ref, x_ref, t_ref, dy_ref, dyb_ref, sq_ref):
        @pl.when(pl.program_id(0) == 0)
        def _():
            sq_ref[...] = jnp.zeros_like(sq_ref)

        y = x_ref[...] + jnp.dot(m_ref[...], w_ref[...], preferred_element_type=F32)
        e = y - t_ref[...]
        dy = e * (1.0 / D_MODEL)
        dy_ref[...] = dy
        dyb_ref[...] = dy.astype(BF16)
        sq_ref[...] += jnp.sum(e * e, axis=0, keepdims=True)

    tok = pl.BlockSpec((tm, D_MODEL), lambda i: (i, 0))
    return _call(
        "out_proj_loss", compute, (s // tm,),
        [tok, _resident((D_MODEL, D_MODEL)), tok, tok],
        [tok, tok, pl.BlockSpec((1, D_MODEL), lambda i: (0, 0))],
        [jax.ShapeDtypeStruct((s, D_MODEL), F32), jax.ShapeDtypeStruct((s, D_MODEL), BF16),
         jax.ShapeDtypeStruct((1, D_MODEL), F32)], [], [mix, w_o, x, tgt])


def _dmix(dx, w_o):
    s = dx.shape[0]
    tm = _tile(TM_WIDE, s)

    def compute(d_ref, w_ref, o_ref):
        o_ref[...] = lax.dot_general(d_ref[...], w_ref[...], NT_DIMS, preferred_element_type=F32)

    return _call(
        "dmix", compute, (s // tm,),
        [pl.BlockSpec((tm, D_MODEL), lambda i: (i, 0)), _resident((D_MODEL, D_MODEL))],
        [pl.BlockSpec((tm, D_MODEL), lambda i: (i, 0))],
        [jax.ShapeDtypeStruct((s, D_MODEL), F32)], [], [dx, w_o])[0]


def _dw_out(mix, dx):
    s = dx.shape[0]
    tk = _tile(TM_WIDE, s)
    nk = s // tk

    def compute(m_ref, d_ref, o_ref, acc):
        k = pl.program_id(0)

        @pl.when(k == 0)
        def _():
            acc[...] = jnp.zeros_like(acc)

        acc[...] += lax.dot_general(m_ref[...], d_ref[...], TN_DIMS, preferred_element_type=F32)

        @pl.when(k == nk - 1)
        def _():
            o_ref[...] = acc[...].astype(BF16)

    return _call(
        "dw_out", compute, (nk,),
        [pl.BlockSpec((tk, D_MODEL), lambda k: (k, 0)), pl.BlockSpec((tk, D_MODEL), lambda k: (k, 0))],
        [pl.BlockSpec((D_MODEL, D_MODEL), lambda k: (0, 0))],
        [jax.ShapeDtypeStruct((D_MODEL, D_MODEL), BF16)],
        [pltpu.VMEM((D_MODEL, D_MODEL), F32)], [mix, dx])[0]


def _mixer_bwd(proj, dmix, qw, kw, sinks, slopes, w_s, bmap, push=None):
    s = proj.shape[0]
    nb = s // BLK

    def compute(sinks_ref, slopes_ref, p_ref, kvp_ref, dm_ref, qw_ref, kw_ref, ws_ref, bmap_ref,
                dp_ref, dqw_ref, dkw_ref, dsk_ref, dws_ref, dbs_ref,
                pend, accq, acck, accs, accb, wtril_scr, wtril_t_scr, bias_scr):
        n = pl.program_id(0)
        h_a, h_b = _lane_halves(BLK)
        ones = _half_masks()
        halves = (h_a, h_b)
        lane = lax.broadcasted_iota(jnp.int32, (BLK, BLK), 1)

        @pl.when(n == 0)
        def _():
            accq[...] = jnp.zeros_like(accq)
            acck[...] = jnp.zeros_like(acck)
            accs[...] = jnp.zeros_like(accs)
            accb[...] = jnp.zeros_like(accb)
            dws_ref[...] = jnp.zeros_like(dws_ref)
            _sgu_weights(ws_ref, wtril_scr, wtril_t_scr)

        @pl.when(n >= 1)
        def _():
            dp_ref[:, 0:C_K] = pend[:, 0:C_K].astype(BF16)
            dp_ref[:, C_GA:D_IN] = pend[:, C_GA:D_IN].astype(BF16)

        @pl.when(n <= 1)
        def _():
            _alibi_table(n, slopes_ref, bias_scr)

        @pl.when(n < nb)
        def _():
            qw_v = qw_ref[...]
            kw_v = kw_ref[...]
            bands = [_kv_band(kt, p_ref, kvp_ref, kw_v, ones) for kt in range(2)]
            tiles, qst, dost, sc, dpm = [], [], [], [], []
            for kt in range(2):
                qn, d_o, tl = [], [], []
                for tt in range(4):
                    j = 4 * kt + tt
                    qyhat, qr, qn_t = _half_rms(p_ref[:, _cols(C_Q, j)], qw_v, ones)
                    ga = p_ref[:, _cols(C_GA, j)]
                    sg = _sigmoid(ga)
                    dma = dm_ref[:, _cols(0, j)]
                    qn.append(qn_t)
                    d_o.append(dma * (ga * sg))
                    tl.append((qyhat, qr, dma * (sg * (1.0 + ga * (1.0 - sg)))))
                tiles.append(tl)
                qst.append(_stack_heads(qn, halves))
                dost.append(_stack_heads(d_o, halves))
                sc.append(lax.dot_general(qst[kt], bands[kt][2], NT_DIMS, preferred_element_type=F32))
                dpm.append(lax.dot_general(dost[kt], bands[kt][3], NT_DIMS, preferred_element_type=F32))
            sgu = []
            for j in range(8):
                zu_pre = p_ref[:, _cols(C_ZU, j)]
                zv_pre = p_ref[:, _cols(C_ZV, j)]
                cu = _gelu_cdf(zu_pre)
                cv = _gelu_cdf(zv_pre)
                zvb = (zv_pre * cv).astype(BF16)
                sgu.append((zu_pre * cu, zvb, _gelu_grad(zu_pre, cu), _gelu_grad(zv_pre, cv),
                            jnp.dot(wtril_scr[2 * BLK * j:2 * BLK * (j + 1), :], zvb, preferred_element_type=F32)))
            dsink = jnp.zeros((BLK, BLK), F32)
            pst, dqkst = [], []
            for kt in range(2):
                p, p_sink = _softmax_sink(sc[kt], bias_scr[8 * BLK * kt:8 * BLK * (kt + 1), :],
                                          _sink_col(sinks_ref, kt))
                dsum = jnp.sum(p * dpm[kt], axis=-1, keepdims=True)
                dsink_col = -(p_sink * dsum)
                for i in range(8):
                    dsink = dsink + jnp.where(lane == 8 * kt + i, dsink_col[_rows(i)], 0.0)
                pst.append(p.astype(BF16))
                dqkst.append((p * (dpm[kt] - dsum)).astype(BF16))
            o, dqn_all, dvb, dkn = [], [], [], []
            for kt in range(2):
                o.append(jnp.dot(pst[kt], bands[kt][3], preferred_element_type=F32))
                dqn_all.append(jnp.dot(dqkst[kt], bands[kt][2], preferred_element_type=F32))
                dvb.append(lax.dot_general(pst[kt], dost[kt], TN_DIMS, preferred_element_type=F32))
                dkn.append(0.125 * lax.dot_general(dqkst[kt], qst[kt], TN_DIMS, preferred_element_type=F32))
            dms = []
            for j in range(8):
                zu, zvb, gu, gv, m_ab = sgu[j]
                gb = p_ref[:, _cols(C_GB, j)]
                dmb = dm_ref[:, _cols(D_ATTN, j)]
                mixed = jnp.where(h_a, m_ab[0:BLK], m_ab[BLK:2 * BLK]) + bmap_ref[:, _cols(0, j)]
                sgb = _sigmoid(gb)
                dgate = dmb * (gb * sgb)
                pend[:, _cols(C_ZU, j)] = (dgate * mixed) * gu
                pend[:, _cols(C_GB, j)] = (dmb * (zu * mixed)) * (sgb * (1.0 + gb * (1.0 - sgb)))
                dmixed = dgate * zu
                accb[:, _cols(0, j)] += dmixed
                dms.append(jnp.concatenate([jnp.where(h_a, dmixed, 0.0).astype(BF16),
                                            jnp.where(h_b, dmixed, 0.0).astype(BF16)], axis=0))
            dzv = []
            for j in range(8):
                dzv.append(jnp.dot(wtril_t_scr[j], dms[j], preferred_element_type=F32))
                dw_ab = lax.dot_general(dms[j], sgu[j][1], NT_DIMS, preferred_element_type=F32)
                dws_ref[2 * j] += dw_ab[0:BLK]
                dws_ref[2 * j + 1] += dw_ab[BLK:2 * BLK]
            dq_w = jnp.zeros((BLK, BLK), F32)
            for kt in range(2):
                for tt in range(4):
                    j = 4 * kt + tt
                    qyhat, qr, dsilu = tiles[kt][tt]
                    dqn = _unstack_heads(dqn_all[kt], tt, h_a)
                    pend[:, _cols(C_GA, j)] = _unstack_heads(o[kt], tt, h_a) * dsilu
                    pend[:, _cols(C_Q, j)] = _half_rms_bwd(dqn, qyhat, qr, qw_v, ones)
                    dq_w = dq_w + dqn * qyhat
            dk_w = jnp.zeros((BLK, BLK), F32)
            for kt in range(2):
                kyhat, kr = bands[kt][0], bands[kt][1]
                dk = _half_rms_bwd(dkn[kt], kyhat, kr, kw_v, ones)
                dkw_part = dkn[kt] * kyhat
                dk_w = dk_w + (dkw_part[0:BLK] + dkw_part[BLK:2 * BLK])
                dv = dvb[kt]

                @pl.when(n >= 1)
                def _():
                    dp_ref[:, _cols(C_K, kt)] = (pend[:, _cols(C_K, kt)] + dk[0:BLK]).astype(BF16)
                    dp_ref[:, _cols(C_V, kt)] = (pend[:, _cols(C_V, kt)] + dv[0:BLK]).astype(BF16)

                pend[:, _cols(C_K, kt)] = dk[BLK:2 * BLK]
                pend[:, _cols(C_V, kt)] = dv[BLK:2 * BLK]
            accq[...] += dq_w
            acck[...] += dk_w
            accs[...] += dsink
            for j in range(8):
                pend[:, _cols(C_ZV, j)] = dzv[j] * sgu[j][3]

        @pl.when(n == nb)
        def _():
            tril = _tril_mask()
            dp_ref[:, C_K:C_GA] = pend[:, C_K:C_GA].astype(BF16)
            aq = accq[...]
            ak = acck[...]
            dqw_ref[...] = jnp.sum(aq + pltpu.roll(aq, HALF, 1), axis=0, keepdims=True)
            dkw_ref[...] = jnp.sum(ak + pltpu.roll(ak, HALF, 1), axis=0, keepdims=True)
            dsk_ref[...] = jnp.sum(accs[...], axis=0, keepdims=True)
            for hd in range(N_HEADS):
                dws_ref[hd] = jnp.where(tril, dws_ref[hd], 0.0)
            hrow = lax.broadcasted_iota(jnp.int32, (N_HEADS, D_ATTN), 0)
            hcol = lax.broadcasted_iota(jnp.int32, (N_HEADS, D_ATTN), 1)
            sel = jnp.where((hcol >= hrow * HALF) & (hcol < (hrow + 1) * HALF), 1.0, 0.0).astype(BF16)
            rem = accb[...]
            tot = jnp.zeros((N_HEADS, BLK), F32)
            for _ in range(3):
                part = rem.astype(BF16)
                tot = tot + lax.dot_general(sel, part, NT_DIMS, preferred_element_type=F32)
                rem = rem - part.astype(F32)
            dbs_ref[...] = tot

    smem = pl.BlockSpec(memory_space=pltpu.SMEM)
    last = nb - 1
    tile_f32 = pltpu.VMEM((BLK, BLK), F32)
    return _call(
        "mixer_bwd", compute, (nb + 1,),
        [smem, smem,
         pl.BlockSpec((BLK, D_IN), lambda n: (jnp.minimum(n, last), 0)),
         pl.BlockSpec((BLK, 512), lambda n: (jnp.maximum(jnp.minimum(n, last) - 1, 0), 2)),
         pl.BlockSpec((BLK, D_MODEL), lambda n: (jnp.minimum(n, last), 0)),
         pl.BlockSpec((1, BLK), lambda n: (0, 0)),
         pl.BlockSpec((1, BLK), lambda n: (0, 0)),
         pl.BlockSpec((N_HEADS, BLK, BLK), lambda n: (0, 0, 0)),
         pl.BlockSpec((BLK, D_ATTN), lambda n: (0, 0))],
        [pl.BlockSpec((BLK, D_IN), lambda n: (jnp.maximum(n - 1, 0), 0)),
         pl.BlockSpec((1, BLK), lambda n: (0, 0)),
         pl.BlockSpec((1, BLK), lambda n: (0, 0)),
         pl.BlockSpec((1, BLK), lambda n: (0, 0)),
         pl.BlockSpec((N_HEADS, BLK, BLK), lambda n: (0, 0, 0)),
         pl.BlockSpec((N_HEADS, BLK), lambda n: (0, 0))],
        [jax.ShapeDtypeStruct((s, D_IN), BF16),
         jax.ShapeDtypeStruct((1, BLK), F32),
         jax.ShapeDtypeStruct((1, BLK), F32),
         jax.ShapeDtypeStruct((1, BLK), F32),
         jax.ShapeDtypeStruct((N_HEADS, BLK, BLK), F32),
         jax.ShapeDtypeStruct((N_HEADS, BLK), F32)],
        [pltpu.VMEM((BLK, D_IN), F32), tile_f32, tile_f32, tile_f32, pltpu.VMEM((BLK, D_ATTN), F32),
         pltpu.VMEM((N_HEADS * BLK, BLK), BF16), pltpu.VMEM((N_HEADS // 2, BLK, 2 * BLK), BF16),
         pltpu.VMEM((N_HEADS * BLK, 2 * BLK), F32)],
        [sinks, slopes, proj, proj, dmix, qw, kw, w_s, bmap], push)


def _dh_norm_bwd(dproj, w_t, x, dx_out, g, bf16_copy, push=None):
    s = x.shape[0]
    tm = _tile(TM_RESIDENT, s)

    def compute(dp_ref, w_ref, x_ref, dxo_ref, g_ref, dx_ref, dg_ref, *dxb_ref):
        @pl.when(pl.program_id(0) == 0)
        def _():
            dg_ref[...] = jnp.zeros_like(dg_ref)

        dh = jnp.dot(dp_ref[...], w_ref[...], preferred_element_type=F32)
        xf = x_ref[...]
        r = lax.rsqrt(jnp.mean(xf * xf, axis=-1, keepdims=True) + RMS_EPS)
        yhat = xf * r
        dyh = dh * g_ref[...]
        c = jnp.mean(dyh * yhat, axis=-1, keepdims=True)
        dx = dxo_ref[...] + r * (dyh - yhat * c)
        dx_ref[...] = dx
        if bf16_copy:
            dxb_ref[0][...] = dx.astype(BF16)
        dg_ref[...] += jnp.sum(dh * yhat, axis=0, keepdims=True)

    tok = pl.BlockSpec((tm, D_MODEL), lambda i: (i, 0))
    return _call(
        "dh_norm_bwd", compute, (s // tm,),
        [pl.BlockSpec((tm, D_IN), lambda i: (i, 0)), _resident((D_IN, D_MODEL)), tok, tok,
         pl.BlockSpec((1, D_MODEL), lambda i: (0, 0))],
        [tok, pl.BlockSpec((1, D_MODEL), lambda i: (0, 0))] + [tok] * bf16_copy,
        [jax.ShapeDtypeStruct((s, D_MODEL), F32), jax.ShapeDtypeStruct((1, D_MODEL), F32)]
        + [jax.ShapeDtypeStruct((s, D_MODEL), BF16)] * bf16_copy,
        [], [dproj, w_t, x, dx_out, g], push)


def _dw_in(dproj, h, push=None):
    s = h.shape[0]
    tk = _tile(DW_IN_TOK, s)
    nk = s // tk

    def compute(dp_ref, h_ref, o_ref, acc):
        k = pl.program_id(1)

        @pl.when(k == 0)
        def _():
            acc[...] = jnp.zeros_like(acc)

        acc[...] += lax.dot_general(dp_ref[...], h_ref[...], TN_DIMS, preferred_element_type=F32)

        @pl.when(k == nk - 1)
        def _():
            o_ref[...] = acc[...].astype(BF16)

    return _call(
        "dw_in", compute, (D_IN // DW_IN_ROWS, nk),
        [pl.BlockSpec((tk, DW_IN_ROWS), lambda j, k: (k, j)), pl.BlockSpec((tk, D_MODEL), lambda j, k: (k, 0))],
        [pl.BlockSpec((DW_IN_ROWS, D_MODEL), lambda j, k: (j, 0))],
        [jax.ShapeDtypeStruct((D_IN, D_MODEL), BF16)],
        [pltpu.VMEM((DW_IN_ROWS, D_MODEL), F32)], [dproj, h], push)


def _adamw_math(w, g, m, v):
    m_new = ADAM_B1 * m + (1.0 - ADAM_B1) * g
    v_new = ADAM_B2 * v + (1.0 - ADAM_B2) * jnp.square(g)
    m_hat = m_new / (1.0 - ADAM_B1 ** ADAM_STEP)
    v_hat = v_new / (1.0 - ADAM_B2 ** ADAM_STEP)
    return -ADAM_LR * (m_hat / (jnp.sqrt(v_hat) + ADAM_EPS) + ADAM_WD * w), m_new, v_new


def _adamw(name, w, g, m, v, tr=None):
    shape = w.shape
    c = shape[-1]
    flat = [a.reshape(-1, c) for a in (w, g, m, v)]
    r = flat[0].shape[0]
    tr = r if tr is None else tr

    def compute(w_ref, g_ref, m_ref, v_ref, d_ref, mo_ref, vo_ref):
        d_ref[...], mo_ref[...], vo_ref[...] = _adamw_math(w_ref[...], g_ref[...], m_ref[...], v_ref[...])

    spec = pl.BlockSpec((tr, c), lambda i: (i, 0))
    outs = _call(name, compute, (r // tr,), [spec] * 4, [spec] * 3, [jax.ShapeDtypeStruct((r, c), F32)] * 3,
                 [], flat)
    return [o.reshape(shape) for o in outs]


def _adamw_from_slots(name, slots, w, m, v, tr):
    n_layers, r, c = w.shape
    nt = r // tr

    def compute(*refs):
        s_refs, (w_ref, m_ref, v_ref), (g_ref, d_ref, mo_ref, vo_ref) = refs[:n_layers], refs[n_layers:n_layers + 3], \
            refs[n_layers + 3:]
        for k in range(n_layers):
            @pl.when(pl.program_id(0) == k)
            def _(k=k):
                g = s_refs[k][0].astype(F32)
                for dev in range(1, N_DEV):
                    g = g + s_refs[k][dev].astype(F32)
                g_ref[0] = g
                d_ref[0], mo_ref[0], vo_ref[0] = _adamw_math(w_ref[0], g, m_ref[0], v_ref[0])

    def slots_spec(k):
        return pl.BlockSpec((N_DEV, tr, c),
                            lambda l, i: (0, jnp.where(l == k, i, jnp.where(l < k, 0, nt - 1)), 0))

    tile = pl.BlockSpec((1, tr, c), lambda l, i: (l, i, 0))
    return _call(name, compute, (n_layers, nt), [slots_spec(k) for k in range(n_layers)] + [tile] * 3, [tile] * 4,
                 [jax.ShapeDtypeStruct((n_layers, r, c), F32)] * 4, [], list(slots) + [w, m, v])


def _pack_rows(parts):
    rows = []
    for a in parts:
        flat = a.reshape(-1)
        n = -(-flat.shape[0] // (8 * BLK)) * 8
        rows.append(jnp.pad(flat, (0, n * BLK - flat.shape[0])).reshape(n, BLK))
    return jnp.concatenate(rows, axis=0)


def _unpack_rows(packed, like):
    out = []
    row = 0
    for a in like:
        n = -(-a.size // (8 * BLK)) * 8
        out.append(packed[row:row + n].reshape(-1)[:a.size].reshape(a.shape))
        row += n
    return out


def kernel(x, norm_g, w_in, q_norm, k_norm, sinks, w_s, b_s, w_out, loss_target, m_norm_g, m_w_in, m_q_norm, m_k_norm, m_sinks, m_w_s, m_b_s, m_w_out, v_norm_g, v_w_in, v_q_norm, v_k_norm, v_sinks, v_w_s, v_b_s, v_w_out):
    xs = x[0]
    tgt = loss_target[0]
    slopes = jnp.asarray(2.0 ** (-8.0 * np.arange(1, N_HEADS + 1) / N_HEADS), dtype=F32)
    wt_sh = jnp.swapaxes(w_in, 1, 2).astype(BF16)
    wo_sh = w_out.astype(BF16)

    layer_par = []
    for l in range(DEPTH):
        layer_par.append((jnp.tile(q_norm[l], 2)[None, :], jnp.tile(k_norm[l], 2)[None, :],
                          jnp.repeat(b_s[l].T, HALF, axis=1)))

    wt_full = _exchange("gather_w0", _gather_rows((W_IN_SHARD,), (0,)), [wt_sh], [W_FULL[W_IN_SHARD]])[0]
    wo_full = None
    saved = []
    cur = xs
    for l in range(DEPTH):
        qw, kw, bmap = layer_par[l]
        more = l + 1 < DEPTH
        res = _norm_proj(cur, norm_g[l][None, :], wt_full,
                         (_gather_rows((W_IN_SHARD,), (l + 1,)), [wt_sh], [W_FULL[W_IN_SHARD]]) if more else None)
        h, proj = res[0], res[1]
        if l == 0:
            mix, wo_full, wo_next = _mixer_fwd(
                proj, qw, kw, sinks[l], slopes, w_s[l], bmap,
                (_gather_rows((W_OUT_SHARD, W_OUT_SHARD), (0, 1)), [wo_sh, wo_sh], [W_FULL[W_OUT_SHARD]] * 2))
        else:
            res2 = _mixer_fwd(proj, qw, kw, sinks[l], slopes, w_s[l], bmap,
                              (_gather_rows((W_OUT_SHARD,), (l + 1,)), [wo_sh], [W_FULL[W_OUT_SHARD]]) if more else None)
            mix, wo_next = res2[0], res2[1] if more else None
        saved.append((cur, h, proj, mix, wt_full, wo_full))
        if more:
            cur = _out_proj(mix, wo_full, cur)
            wt_full, wo_full = res[2], wo_next
        else:
            dx, dx_b, sq = _out_proj_loss(mix, wo_full, cur, tgt)
    loss = lax.psum(0.5 * jnp.sum(sq) / D_MODEL, MESH_AXES)

    wt_slots, wo_slots, g_small, g_norm = ([None] * DEPTH for _ in range(4))
    wt_slots_shape = _slots_shape(W_IN_SHARD, D_MODEL, BF16)
    dwt_waiting = None
    for l in reversed(range(DEPTH)):
        x_l, h, proj, mix, wt_l, wo_l = saved[l]
        qw, kw, bmap = layer_par[l]
        dmix = _dmix(dx_b, wo_l)
        dwo_part = _dw_out(mix, dx_b)
        res = _mixer_bwd(proj, dmix, qw, kw, sinks[l], slopes, w_s[l], bmap,
                         None if dwt_waiting is None else
                         (_scatter_rows((W_IN_SHARD,)), [dwt_waiting], [wt_slots_shape]))
        dproj, dqw, dkw, dsk, dws, dbs = res[:6]
        if dwt_waiting is not None:
            wt_slots[l + 1] = res[6]
        small_like = [dqw[0, :HALF], dkw[0, :HALF], dsk[0, :N_HEADS], dws, dbs]
        packed = _pack_rows(small_like)
        dwt_waiting, small_slots, wo_slots[l] = _dw_in(
            dproj, h, (_gather_slots_and_scatter_rows(W_OUT_SHARD), [packed, dwo_part],
                       [_slots_shape(*packed.shape, F32), _slots_shape(W_OUT_SHARD, D_MODEL, BF16)]))
        if l > 0:
            dx, dng, dx_b = _dh_norm_bwd(dproj, wt_l, x_l, dx, norm_g[l][None, :], True)
        else:
            dx, dng, wt_slots[0] = _dh_norm_bwd(dproj, wt_l, x_l, dx, norm_g[l][None, :], False,
                                                (_scatter_rows((W_IN_SHARD,)), [dwt_waiting], [wt_slots_shape]))
        g_small[l] = _unpack_rows(_sum_slots(small_slots, packed.shape[0]), small_like)
        g_norm[l] = dng[0]

    dng_all = _pack_rows([jnp.stack(g_norm)])
    dng_slots = _exchange("gather_dnorm", _gather_slots(), [dng_all], [_slots_shape(*dng_all.shape, F32)])[0]
    gr_norm = _unpack_rows(_sum_slots(dng_slots, dng_all.shape[0]), [norm_g])[0]
    gr_qn, gr_kn, gr_sk, gr_ws, gr_bs = (jnp.stack([g_small[l][i] for l in range(DEPTH)]) for i in range(5))

    def t(a):
        return jnp.swapaxes(a, 1, 2)

    from_slots = {1: _adamw_from_slots("adamw_w_in", wt_slots, t(w_in), t(m_w_in), t(v_w_in), ADAMW_ROWS),
                  7: _adamw_from_slots("adamw_w_out", wo_slots, w_out, m_w_out, v_w_out, ADAMW_ROWS)}
    from_slots[1] = [t(a) for a in from_slots[1]]

    grads = [gr_norm, None, gr_qn, gr_kn, gr_sk, gr_ws, gr_bs, None]
    weights = [norm_g, w_in, q_norm, k_norm, sinks, w_s, b_s, w_out]
    moms = [m_norm_g, m_w_in, m_q_norm, m_k_norm, m_sinks, m_w_s, m_b_s, m_w_out]
    vels = [v_norm_g, v_w_in, v_q_norm, v_k_norm, v_sinks, v_w_s, v_b_s, v_w_out]
    tiles = [None, None, None, None, None, 1024, None, None]
    names = ["norm_g", "w_in", "q_norm", "k_norm", "sinks", "w_s", "b_s", "w_out"]
    deltas, new_m, new_v = [], [], []
    for i, (nm, w, g, m, v, tr) in enumerate(zip(names, weights, grads, moms, vels, tiles)):
        if i in from_slots:
            grads[i], d, mo, vo = from_slots[i]
        else:
            d, mo, vo = _adamw("adamw_" + nm, w, g, m, v, tr)
        deltas.append(d)
        new_m.append(mo)
        new_v.append(vo)

    return (loss, dx[None], *grads, *deltas, *new_m, *new_v)
```

```python
import numpy as np
import jax
import jax.numpy as jnp
from jax import lax
from jax.experimental import pallas as pl
from jax.experimental.pallas import tpu as pltpu

F32 = jnp.float32
BF16 = jnp.bfloat16

D_MODEL = 2048
D_ATTN = 1024
D_IN = 5632
N_HEADS = 16
DEPTH = 4
BLK = 128
HALF = 64
RMS_EPS = 1e-6
C_Q, C_K, C_V, C_GA, C_ZU, C_ZV, C_GB = 0, 1024, 1280, 1536, 2560, 3584, 4608
NEG = -1e30
N_DEV = 8
W_IN_SHARD = D_IN // N_DEV
W_OUT_SHARD = D_MODEL // N_DEV
INV_SQRT2 = 0.7071067811865476
INV_SQRT_2PI = 0.3989422804014327

TM_RESIDENT = 256
TM_STREAM = 512
TM_WIDE = 1024
TN_PROJ = 512
DW_IN_ROWS = D_IN // 4
DW_IN_TOK = 1024
ADAMW_ROWS = 64

ADAM_LR = 0.001
ADAM_B1 = 0.9
ADAM_B2 = 0.999
ADAM_EPS = 1e-08
ADAM_WD = 0.01
ADAM_STEP = 10

NT_DIMS = (((1,), (1,)), ((), ()))
TN_DIMS = (((0,), (0,)), ((), ()))
MESH_AXES = ("x", "y", "c")


def _sigmoid(v):
    return 1.0 / (1.0 + jnp.exp(-v))


def _gelu_cdf(z):
    return 0.5 * (1.0 + lax.erf(z * INV_SQRT2))


def _gelu_grad(z, cdf):
    return cdf + z * (jnp.exp(-0.5 * z * z) * INV_SQRT_2PI)


def _lane_halves(rows):
    lane = lax.broadcasted_iota(jnp.int32, (rows, BLK), 1)
    return lane < HALF, lane >= HALF


def _half_masks():
    return {BLK: _lane_halves(BLK), 2 * BLK: _lane_halves(2 * BLK)}


def _half_sum(v, ones):
    h_a, h_b = ones[v.shape[0]]
    s_a = jnp.sum(jnp.where(h_a, v, 0.0), axis=-1, keepdims=True)
    s_b = jnp.sum(jnp.where(h_b, v, 0.0), axis=-1, keepdims=True)
    return jnp.where(h_a, s_a, s_b)


def _half_rms(v, w, ones):
    r = lax.rsqrt(_half_sum(v * v, ones) * (1.0 / HALF) + RMS_EPS)
    yhat = v * r
    return yhat, r, yhat * w


def _half_rms_bwd(dy, yhat, r, w, ones):
    dyh = dy * w
    c = _half_sum(dyh * yhat, ones) * (1.0 / HALF)
    return r * (dyh - yhat * c)


def _band_mask(n):
    t = lax.broadcasted_iota(jnp.int32, (BLK, 2 * BLK), 0)
    kk = lax.broadcasted_iota(jnp.int32, (BLK, 2 * BLK), 1)
    dist = t + BLK - kk
    first_key = jnp.where(n > 0, 0, BLK)
    ok = (dist >= 0) & (dist < BLK) & (kk >= first_key)
    return ok, dist.astype(F32)


def _alibi_table(n, slopes_ref, bias_scr):
    ok, distf = _band_mask(n)
    for hd in range(N_HEADS):
        bias_scr[BLK * hd:BLK * (hd + 1), :] = jnp.where(ok, -(slopes_ref[hd] * distf), NEG)


def _sink_col(sinks_ref, kt):
    return jnp.concatenate([jnp.full((BLK, 1), sinks_ref[8 * kt + i], F32) for i in range(8)], axis=0)


def _softmax_sink(s_scaled, bias, sink):
    s = s_scaled + bias
    m = jnp.maximum(jnp.max(s, axis=-1, keepdims=True), sink)
    p = jnp.exp(s - m)
    es = jnp.exp(sink - m)
    inv = 1.0 / (jnp.sum(p, axis=-1, keepdims=True) + es)
    return p * inv, es * inv


def _rows(i):
    return slice(BLK * i, BLK * (i + 1))


def _cols(base, j):
    return slice(base + BLK * j, base + BLK * (j + 1))


def _to_half(v, have, want):
    return v if have == want else pltpu.roll(v, HALF, 1)


def _tril_mask():
    row = lax.broadcasted_iota(jnp.int32, (BLK, BLK), 0)
    col = lax.broadcasted_iota(jnp.int32, (BLK, BLK), 1)
    return row >= col


def _tile(limit, s):
    t = min(limit, s)
    assert s % t == 0, (s, t)
    return t


def _mesh_place():
    x, y, c = lax.axis_index("x"), lax.axis_index("y"), lax.axis_index("c")
    return x, y, c, 4 * x + 2 * y + c


def _peer(x, y, c, k):
    px = 1 - x if k & 4 else x
    py = 1 - y if k & 2 else y
    pc = 1 - c if k & 1 else c
    return (px, py, pc), 4 * px + 2 * py + pc


class _Pushes:
    def __init__(self, n_arrays, src_view, dst_view):
        self.na = n_arrays
        self.src_view = src_view
        self.dst_view = dst_view

    def scratch(self):
        n = self.na * (N_DEV - 1)
        return [pltpu.SemaphoreType.DMA((n,)), pltpu.SemaphoreType.DMA((n,)), pltpu.SemaphoreType.DMA((self.na,))]

    def copies(self, src_refs, dst_refs, send_sems, recv_sems, local_sems):
        x, y, c, me = _mesh_place()
        cps = []
        for a in range(self.na):
            cps.append(pltpu.make_async_copy(self.src_view(a, src_refs[a], me), self.dst_view(a, dst_refs[a], me),
                                             local_sems.at[a]))
        for k in range(1, N_DEV):
            peer, pidx = _peer(x, y, c, k)
            for a in range(self.na):
                sem = a * (N_DEV - 1) + k - 1
                cps.append(pltpu.make_async_remote_copy(
                    src_ref=self.src_view(a, src_refs[a], pidx), dst_ref=self.dst_view(a, dst_refs[a], me),
                    send_sem=send_sems.at[sem], recv_sem=recv_sems.at[sem],
                    device_id=peer, device_id_type=pl.DeviceIdType.MESH))
        return cps

    def plan(self, src_refs, dst_refs, send_sems, recv_sems, local_sems):
        cps = self.copies(src_refs, dst_refs, send_sems, recv_sems, local_sems)
        return cps, [], [], [cp.wait for cp in cps]


class _TwoLevelGather:
    def __init__(self, shard_rows, layers):
        self.na = len(shard_rows)
        self.shard_rows = shard_rows
        self.layers = layers

    def scratch(self):
        n = self.na * (N_DEV - 1)
        return [pltpu.SemaphoreType.DMA((n,)), pltpu.SemaphoreType.DMA((n,)), pltpu.SemaphoreType.DMA((self.na,))]

    def plan(self, src_refs, dst_refs, send_sems, recv_sems, local_sems):
        x, y, c, _ = _mesh_place()
        sibling = (x, y, 1 - c)
        chips = [(1 - x, y), (x, 1 - y), (1 - x, 1 - y)]
        start, mid_wait, mid_start, final = [], [], [], []
        for a in range(self.na):
            r = self.shard_rows[a]
            src = src_refs[a].at[self.layers[a]]
            dst = dst_refs[a]

            def rows(px, py, pc, r=r, dst=dst):
                return dst.at[pl.ds(pl.multiple_of((4 * px + 2 * py + pc) * r, 64), r), :]

            def remote(k, s_ref, block, to, a=a, rows=rows):
                return pltpu.make_async_remote_copy(
                    src_ref=s_ref, dst_ref=rows(*block),
                    send_sem=send_sems.at[a * (N_DEV - 1) + k], recv_sem=recv_sems.at[a * (N_DEV - 1) + k],
                    device_id=to, device_id_type=pl.DeviceIdType.MESH)

            mine = pltpu.make_async_copy(src, rows(x, y, c), local_sems.at[a])
            own = [remote(0, src, (x, y, c), sibling)]
            own += [remote(1 + j, src, (x, y, c), (*chip, c)) for j, chip in enumerate(chips)]
            passed = [remote(4 + j, rows(*chip, c), (*chip, c), sibling) for j, chip in enumerate(chips)]
            start += [mine] + own
            mid_wait += own[1:]
            mid_start += passed
            final += [own[0].wait_recv] + [cp.wait_recv for cp in passed]
            final += [cp.wait_send for cp in own + passed] + [mine.wait]
        return start, mid_wait, mid_start, final


def _call(name, compute, grid, in_specs, out_specs, out_shape, scratch, args, push=None):
    sem = pltpu.CompilerParams(dimension_semantics=("arbitrary",) * len(grid))
    if push is None:
        return pl.pallas_call(compute, name=name, grid=grid, in_specs=in_specs, out_specs=out_specs,
                              out_shape=out_shape, scratch_shapes=scratch, compiler_params=sem)(*args)
    pushes, srcs, xshapes = push
    n_in, n_out, n_scr, na = len(args), len(out_shape), len(scratch), pushes.na
    hbm = pl.BlockSpec(memory_space=pltpu.HBM)

    def body(*refs):
        ins, refs = refs[:n_in], refs[n_in:]
        xin, refs = refs[:na], refs[na:]
        outs, refs = refs[:n_out], refs[n_out:]
        xout, refs = refs[:na], refs[na:]
        scr, sems = refs[:n_scr], refs[n_scr:]
        start, mid_wait, mid_start, final = pushes.plan(xin, xout, *sems)
        first = pl.program_id(0) == 0
        middle = pl.program_id(0) == (grid[0] * 5) // 8
        last = pl.program_id(0) == grid[0] - 1
        for d in range(1, len(grid)):
            first = first & (pl.program_id(d) == 0)
            middle = middle & (pl.program_id(d) == 0)
            last = last & (pl.program_id(d) == grid[d] - 1)

        @pl.when(first)
        def _():
            for cp in start:
                cp.start()

        if mid_start:
            @pl.when(middle)
            def _():
                for cp in mid_wait:
                    cp.wait_recv()
                for cp in mid_start:
                    cp.start()

        compute(*ins, *outs, *scr)

        @pl.when(last)
        def _():
            for wait in final:
                wait()

    return pl.pallas_call(
        body, name=name, grid=grid,
        in_specs=list(in_specs) + [hbm] * na, out_specs=list(out_specs) + [hbm] * na,
        out_shape=list(out_shape) + list(xshapes),
        scratch_shapes=list(scratch) + pushes.scratch(), compiler_params=sem)(*args, *srcs)


def _exchange(name, pushes, srcs, out_shapes):
    na = pushes.na
    hbm = pl.BlockSpec(memory_space=pltpu.HBM)

    def body(*refs):
        start, mid_wait, mid_start, final = pushes.plan(refs[:na], refs[na:2 * na], *refs[2 * na:])
        for cp in start:
            cp.start()
        for cp in mid_wait:
            cp.wait_recv()
        for cp in mid_start:
            cp.start()
        for wait in final:
            wait()

    return pl.pallas_call(body, name=name, in_specs=[hbm] * na, out_specs=[hbm] * na, out_shape=out_shapes,
                          scratch_shapes=pushes.scratch())(*srcs)


def _gather_rows(shard_rows, layers):
    return _TwoLevelGather(shard_rows, layers)


def _scatter_rows(shard_rows):
    def src_view(a, ref, idx):
        r = shard_rows[a]
        return ref.at[pl.ds(pl.multiple_of(idx * r, 64), r), :]

    def dst_view(a, ref, idx):
        return ref.at[idx]

    return _Pushes(len(shard_rows), src_view, dst_view)


def _gather_slots():
    return _Pushes(1, lambda a, ref, idx: ref, lambda a, ref, idx: ref.at[idx])


def _gather_slots_and_scatter_rows(rows):
    def src_view(a, ref, idx):
        return ref if a == 0 else ref.at[pl.ds(pl.multiple_of(idx * rows, 64), rows), :]

    return _Pushes(2, src_view, lambda a, ref, idx: ref.at[idx])


W_FULL = {W_IN_SHARD: jax.ShapeDtypeStruct((D_IN, D_MODEL), BF16),
          W_OUT_SHARD: jax.ShapeDtypeStruct((D_MODEL, D_MODEL), BF16)}


def _slots_shape(rows, cols, dtype):
    return jax.ShapeDtypeStruct((N_DEV, rows, cols), dtype)


def _sum_slots(slots, tr):
    _, r, c = slots.shape

    def compute(s_ref, o_ref):
        tot = s_ref[0].astype(F32)
        for d in range(1, N_DEV):
            tot = tot + s_ref[d].astype(F32)
        o_ref[...] = tot

    return _call("sum_slots", compute, (r // tr,),
                 [pl.BlockSpec((N_DEV, tr, c), lambda i: (0, i, 0))], [pl.BlockSpec((tr, c), lambda i: (i, 0))],
                 [jax.ShapeDtypeStruct((r, c), F32)], [], [slots])[0]


def _resident(shape):
    return pl.BlockSpec(shape, lambda *_: (0,) * len(shape), pipeline_mode=pl.Buffered(1))


def _norm_proj(x, g, w_t, push=None):
    s = x.shape[0]
    tm = _tile(TM_RESIDENT, s)

    def compute(x_ref, g_ref, w_ref, h_ref, p_ref):
        xf = x_ref[...]
        r = lax.rsqrt(jnp.mean(xf * xf, axis=-1, keepdims=True) + RMS_EPS)
        h = ((xf * r) * g_ref[...]).astype(BF16)
        h_ref[...] = h
        for j in range(D_IN // TN_PROJ):
            cols = slice(j * TN_PROJ, (j + 1) * TN_PROJ)
            p_ref[:, cols] = lax.dot_general(h, w_ref[cols, :], NT_DIMS, preferred_element_type=F32)

    return _call(
        "norm_proj", compute, (s // tm,),
        [pl.BlockSpec((tm, D_MODEL), lambda i: (i, 0)), pl.BlockSpec((1, D_MODEL), lambda i: (0, 0)),
         _resident((D_IN, D_MODEL))],
        [pl.BlockSpec((tm, D_MODEL), lambda i: (i, 0)), pl.BlockSpec((tm, D_IN), lambda i: (i, 0))],
        [jax.ShapeDtypeStruct((s, D_MODEL), BF16), jax.ShapeDtypeStruct((s, D_IN), F32)],
        [], [x, g, w_t], push)


def _sgu_weights(ws_ref, wtril_scr, wtril_t_scr=None):
    tril = _tril_mask()
    for hd in range(N_HEADS):
        w = jnp.where(tril, ws_ref[hd], 0.0)
        wtril_scr[BLK * hd:BLK * (hd + 1), :] = w.astype(BF16)
        if wtril_t_scr is not None:
            wtril_t_scr[hd // 2, :, BLK * (hd % 2):BLK * (hd % 2 + 1)] = w.T.astype(BF16)


def _kv_band(kt, p_ref, kvp_ref, kw_v, ones):
    kband = jnp.concatenate([kvp_ref[:, _cols(0, kt)], p_ref[:, _cols(C_K, kt)]], axis=0)
    kyhat, kr, kn = _half_rms(kband, kw_v, ones)
    vband = jnp.concatenate([kvp_ref[:, _cols(256, kt)], p_ref[:, _cols(C_V, kt)]], axis=0)
    return kyhat, kr, (kn * 0.125).astype(BF16), vband.astype(BF16)


def _stack_heads(tiles, halves):
    parts = []
    for tt, tile in enumerate(tiles):
        for qh in range(2):
            parts.append(_to_half(jnp.where(halves[qh], tile, 0.0), qh, tt // 2).astype(BF16))
    return jnp.concatenate(parts, axis=0)


def _unstack_heads(stacked, tt, h_a):
    return jnp.where(h_a, _to_half(stacked[_rows(2 * tt)], tt // 2, 0), _to_half(stacked[_rows(2 * tt + 1)], tt // 2, 1))


def _mixer_fwd(proj, qw, kw, sinks, slopes, w_s, bmap, push=None):
    s = proj.shape[0]
    nb = s // BLK

    def compute(sinks_ref, slopes_ref, p_ref, kvp_ref, qw_ref, kw_ref, ws_ref, bmap_ref, mix_ref,
                wtril_scr, bias_scr):
        _mixer_tables(pl.program_id(0), slopes_ref, ws_ref, wtril_scr, bias_scr)
        _mixer_fwd_block(sinks_ref, p_ref, kvp_ref, qw_ref, kw_ref, bmap_ref, mix_ref, wtril_scr, bias_scr)

    smem = pl.BlockSpec(memory_space=pltpu.SMEM)
    return _call(
        "mixer_fwd", compute, (nb,),
        [smem, smem,
         pl.BlockSpec((BLK, D_IN), lambda n: (n, 0)),
         pl.BlockSpec((BLK, 512), lambda n: (jnp.maximum(n - 1, 0), 2)),
         pl.BlockSpec((1, BLK), lambda n: (0, 0)),
         pl.BlockSpec((1, BLK), lambda n: (0, 0)),
         pl.BlockSpec((N_HEADS, BLK, BLK), lambda n: (0, 0, 0)),
         pl.BlockSpec((BLK, D_ATTN), lambda n: (0, 0))],
        [pl.BlockSpec((BLK, D_MODEL), lambda n: (n, 0))],
        [jax.ShapeDtypeStruct((s, D_MODEL), BF16)],
        [pltpu.VMEM((N_HEADS * BLK, BLK), BF16), pltpu.VMEM((N_HEADS * BLK, 2 * BLK), F32)],
        [sinks, slopes, proj, proj, qw, kw, w_s, bmap], push)


def _mixer_tables(n, slopes_ref, ws_ref, wtril_scr, bias_scr, wtril_t_scr=None):
    @pl.when(n == 0)
    def _():
        _sgu_weights(ws_ref, wtril_scr, wtril_t_scr)

    @pl.when(n <= 1)
    def _():
        _alibi_table(n, slopes_ref, bias_scr)


def _mixer_fwd_block(sinks_ref, p_ref, kvp_ref, qw_ref, kw_ref, bmap_ref, mix_ref, wtril_scr, bias_scr):
    h_a, h_b = _lane_halves(BLK)
    ones = _half_masks()
    halves = (h_a, h_b)
    qw_v = qw_ref[...]
    kw_v = kw_ref[...]
    bands = [_kv_band(kt, p_ref, kvp_ref, kw_v, ones) for kt in range(2)]
    sc = []
    for kt in range(2):
        qn = [_half_rms(p_ref[:, _cols(C_Q, 4 * kt + tt)], qw_v, ones)[2] for tt in range(4)]
        sc.append(lax.dot_general(_stack_heads(qn, halves), bands[kt][2], NT_DIMS, preferred_element_type=F32))
    zu, mixed = [], []
    for j in range(8):
        zu_pre = p_ref[:, _cols(C_ZU, j)]
        zv_pre = p_ref[:, _cols(C_ZV, j)]
        zu.append(zu_pre * _gelu_cdf(zu_pre))
        zvb = (zv_pre * _gelu_cdf(zv_pre)).astype(BF16)
        mixed.append(jnp.dot(wtril_scr[2 * BLK * j:2 * BLK * (j + 1), :], zvb, preferred_element_type=F32))
    o = []
    for kt in range(2):
        p, _ = _softmax_sink(sc[kt], bias_scr[8 * BLK * kt:8 * BLK * (kt + 1), :], _sink_col(sinks_ref, kt))
        o.append(jnp.dot(p.astype(BF16), bands[kt][3], preferred_element_type=F32))
    for j in range(8):
        gb = p_ref[:, _cols(C_GB, j)]
        mx = jnp.where(h_a, mixed[j][0:BLK], mixed[j][BLK:2 * BLK]) + bmap_ref[:, _cols(0, j)]
        mix_ref[:, _cols(D_ATTN, j)] = ((zu[j] * mx) * (gb * _sigmoid(gb))).astype(BF16)
    for kt in range(2):
        for tt in range(4):
            j = 4 * kt + tt
            ga = p_ref[:, _cols(C_GA, j)]
            mix_ref[:, _cols(0, j)] = (_unstack_heads(o[kt], tt, h_a) * (ga * _sigmoid(ga))).astype(BF16)


def _out_proj(mix, w_o, x):
    s = x.shape[0]
    tm = _tile(TM_WIDE, s)

    def compute(m_ref, w_ref, x_ref, o_ref):
        o_ref[...] = x_ref[...] + jnp.dot(m_ref[...], w_ref[...], preferred_element_type=F32)

    return _call(
        "out_proj", compute, (s // tm,),
        [pl.BlockSpec((tm, D_MODEL), lambda i: (i, 0)), _resident((D_MODEL, D_MODEL)),
         pl.BlockSpec((tm, D_MODEL), lambda i: (i, 0))],
        [pl.BlockSpec((tm, D_MODEL), lambda i: (i, 0))],
        [jax.ShapeDtypeStruct((s, D_MODEL), F32)], [], [mix, w_o, x])[0]


def _out_proj_loss(mix, w_o, x, tgt):
    s = x.shape[0]
    tm = _tile(TM_STREAM, s)

    def compute(m_ref, w_ref, x_ref, t_ref, dy_ref, dyb_ref, sq_ref):
        @pl.when(pl.program_id(0) == 0)
        def _():
            sq_ref[...] = jnp.zeros_like(sq_ref)

        y = x_ref[...] + jnp.dot(m_ref[...], w_ref[...], preferred_element_type=F32)
        e = y - t_ref[...]
        dy = e * (1.0 / D_MODEL)
        dy_ref[...] = dy
        dyb_ref[...] = dy.astype(BF16)
        sq_ref[...] += jnp.sum(e * e, axis=0, keepdims=True)

    tok = pl.BlockSpec((tm, D_MODEL), lambda i: (i, 0))
    return _call(
        "out_proj_loss", compute, (s // tm,),
        [tok, _resident((D_MODEL, D_MODEL)), tok, tok],
        [tok, tok, pl.BlockSpec((1, D_MODEL), lambda i: (0, 0))],
        [jax.ShapeDtypeStruct((s, D_MODEL), F32), jax.ShapeDtypeStruct((s, D_MODEL), BF16),
         jax.ShapeDtypeStruct((1, D_MODEL), F32)], [], [mix, w_o, x, tgt])


def _dmix(dx, w_o):
    s = dx.shape[0]
    tm = _tile(TM_WIDE, s)

    def compute(d_ref, w_ref, o_ref):
        o_ref[...] = lax.dot_general(d_ref[...], w_ref[...], NT_DIMS, preferred_element_type=F32)

    return _call(
        "dmix", compute, (s // tm,),
        [pl.BlockSpec((tm, D_MODEL), lambda i: (i, 0)), _resident((D_MODEL, D_MODEL))],
        [pl.BlockSpec((tm, D_MODEL), lambda i: (i, 0))],
        [jax.ShapeDtypeStruct((s, D_MODEL), F32)], [], [dx, w_o])[0]


def _dw_out(mix, dx):
    s = dx.shape[0]
    tk = _tile(TM_WIDE, s)
    nk = s // tk

    def compute(m_ref, d_ref, o_ref, acc):
        k = pl.program_id(0)

        @pl.when(k == 0)
        def _():
            acc[...] = jnp.zeros_like(acc)

        acc[...] += lax.dot_general(m_ref[...], d_ref[...], TN_DIMS, preferred_element_type=F32)

        @pl.when(k == nk - 1)
        def _():
            o_ref[...] = acc[...].astype(BF16)

    return _call(
        "dw_out", compute, (nk,),
        [pl.BlockSpec((tk, D_MODEL), lambda k: (k, 0)), pl.BlockSpec((tk, D_MODEL), lambda k: (k, 0))],
        [pl.BlockSpec((D_MODEL, D_MODEL), lambda k: (0, 0))],
        [jax.ShapeDtypeStruct((D_MODEL, D_MODEL), BF16)],
        [pltpu.VMEM((D_MODEL, D_MODEL), F32)], [mix, dx])[0]


def _mixer_bwd(proj, dmix, qw, kw, sinks, slopes, w_s, bmap, push=None):
    s = proj.shape[0]
    nb = s // BLK

    def compute(sinks_ref, slopes_ref, p_ref, kvp_ref, dm_ref, qw_ref, kw_ref, ws_ref, bmap_ref,
                dp_ref, dkv_ref, dqw_ref, dkw_ref, dsk_ref, dws_ref, dbs_ref,
                pend, accq, acck, accs, accb, wtril_scr, wtril_t_scr, bias_scr):
        n = pl.program_id(0)
        h_a, h_b = _lane_halves(BLK)
        ones = _half_masks()
        halves = (h_a, h_b)
        lane = lax.broadcasted_iota(jnp.int32, (BLK, BLK), 1)

        @pl.when(n == 0)
        def _():
            accq[...] = jnp.zeros_like(accq)
            acck[...] = jnp.zeros_like(acck)
            accs[...] = jnp.zeros_like(accs)
            accb[...] = jnp.zeros_like(accb)
            dws_ref[...] = jnp.zeros_like(dws_ref)
            _sgu_weights(ws_ref, wtril_scr, wtril_t_scr)

        @pl.when(n <= 1)
        def _():
            _alibi_table(n, slopes_ref, bias_scr)

        @pl.when(n < nb)
        def _():
            qw_v = qw_ref[...]
            kw_v = kw_ref[...]
            bands = [_kv_band(kt, p_ref, kvp_ref, kw_v, ones) for kt in range(2)]
            tiles, qst, dost, sc, dpm = [], [], [], [], []
            for kt in range(2):
                qn, d_o, tl = [], [], []
                for tt in range(4):
                    j = 4 * kt + tt
                    qyhat, qr, qn_t = _half_rms(p_ref[:, _cols(C_Q, j)], qw_v, ones)
                    ga = p_ref[:, _cols(C_GA, j)]
                    sg = _sigmoid(ga)
                    dma = dm_ref[:, _cols(0, j)]
                    qn.append(qn_t)
                    d_o.append(dma * (ga * sg))
                    tl.append((qyhat, qr, dma * (sg * (1.0 + ga * (1.0 - sg)))))
                tiles.append(tl)
                qst.append(_stack_heads(qn, halves))
                dost.append(_stack_heads(d_o, halves))
                sc.append(lax.dot_general(qst[kt], bands[kt][2], NT_DIMS, preferred_element_type=F32))
                dpm.append(lax.dot_general(dost[kt], bands[kt][3], NT_DIMS, preferred_element_type=F32))
            dp_ref[:, C_K:C_GA] = jnp.zeros((BLK, C_GA - C_K), BF16)
            sgu = []
            for j in range(8):
                zu_pre = p_ref[:, _cols(C_ZU, j)]
                zv_pre = p_ref[:, _cols(C_ZV, j)]
                cu = _gelu_cdf(zu_pre)
                cv = _gelu_cdf(zv_pre)
                zvb = (zv_pre * cv).astype(BF16)
                sgu.append((zu_pre * cu, zvb, _gelu_grad(zu_pre, cu), _gelu_grad(zv_pre, cv),
                            jnp.dot(wtril_scr[2 * BLK * j:2 * BLK * (j + 1), :], zvb, preferred_element_type=F32)))
            dsink = jnp.zeros((BLK, BLK), F32)
            pst, dqkst = [], []
            for kt in range(2):
                p, p_sink = _softmax_sink(sc[kt], bias_scr[8 * BLK * kt:8 * BLK * (kt + 1), :],
                                          _sink_col(sinks_ref, kt))
                dsum = jnp.sum(p * dpm[kt], axis=-1, keepdims=True)
                dsink_col = -(p_sink * dsum)
                for i in range(8):
                    dsink = dsink + jnp.where(lane == 8 * kt + i, dsink_col[_rows(i)], 0.0)
                pst.append(p.astype(BF16))
                dqkst.append((p * (dpm[kt] - dsum)).astype(BF16))
            o, dqn_all, dvb, dkn = [], [], [], []
            for kt in range(2):
                o.append(jnp.dot(pst[kt], bands[kt][3], preferred_element_type=F32))
                dqn_all.append(jnp.dot(dqkst[kt], bands[kt][2], preferred_element_type=F32))
                dvb.append(lax.dot_general(pst[kt], dost[kt], TN_DIMS, preferred_element_type=F32))
                dkn.append(0.125 * lax.dot_general(dqkst[kt], qst[kt], TN_DIMS, preferred_element_type=F32))
            dms = []
            for j in range(8):
                zu, zvb, gu, gv, m_ab = sgu[j]
                gb = p_ref[:, _cols(C_GB, j)]
                dmb = dm_ref[:, _cols(D_ATTN, j)]
                mixed = jnp.where(h_a, m_ab[0:BLK], m_ab[BLK:2 * BLK]) + bmap_ref[:, _cols(0, j)]
                sgb = _sigmoid(gb)
                dgate = dmb * (gb * sgb)
                dp_ref[:, _cols(C_ZU, j)] = ((dgate * mixed) * gu).astype(BF16)
                dp_ref[:, _cols(C_GB, j)] = ((dmb * (zu * mixed)) * (sgb * (1.0 + gb * (1.0 - sgb)))).astype(BF16)
                dmixed = dgate * zu
                accb[:, _cols(0, j)] += dmixed
                dms.append(jnp.concatenate([jnp.where(h_a, dmixed, 0.0).astype(BF16),
                                            jnp.where(h_b, dmixed, 0.0).astype(BF16)], axis=0))
            dzv = []
            for j in range(8):
                dzv.append(jnp.dot(wtril_t_scr[j], dms[j], preferred_element_type=F32))
                dw_ab = lax.dot_general(dms[j], sgu[j][1], NT_DIMS, preferred_element_type=F32)
                dws_ref[2 * j] += dw_ab[0:BLK]
                dws_ref[2 * j + 1] += dw_ab[BLK:2 * BLK]
            dq_w = jnp.zeros((BLK, BLK), F32)
            for kt in range(2):
                for tt in range(4):
                    j = 4 * kt + tt
                    qyhat, qr, dsilu = tiles[kt][tt]
                    dqn = _unstack_heads(dqn_all[kt], tt, h_a)
                    dp_ref[:, _cols(C_GA, j)] = (_unstack_heads(o[kt], tt, h_a) * dsilu).astype(BF16)
                    dp_ref[:, _cols(C_Q, j)] = _half_rms_bwd(dqn, qyhat, qr, qw_v, ones).astype(BF16)
                    dq_w = dq_w + dqn * qyhat
            dk_w = jnp.zeros((BLK, BLK), F32)
            for kt in range(2):
                kyhat, kr = bands[kt][0], bands[kt][1]
                dk = _half_rms_bwd(dkn[kt], kyhat, kr, kw_v, ones)
                dkw_part = dkn[kt] * kyhat
                dk_w = dk_w + (dkw_part[0:BLK] + dkw_part[BLK:2 * BLK])
                dv = dvb[kt]

                @pl.when(n >= 1)
                def _():
                    dkv_ref[:, _cols(0, kt)] = (pend[:, _cols(0, kt)] + dk[0:BLK]).astype(BF16)
                    dkv_ref[:, _cols(256, kt)] = (pend[:, _cols(256, kt)] + dv[0:BLK]).astype(BF16)

                pend[:, _cols(0, kt)] = dk[BLK:2 * BLK]
                pend[:, _cols(256, kt)] = dv[BLK:2 * BLK]
            accq[...] += dq_w
            acck[...] += dk_w
            accs[...] += dsink
            for j in range(8):
                dp_ref[:, _cols(C_ZV, j)] = (dzv[j] * sgu[j][3]).astype(BF16)

        @pl.when(n == nb)
        def _():
            tril = _tril_mask()
            dkv_ref[...] = pend[...].astype(BF16)
            aq = accq[...]
            ak = acck[...]
            dqw_ref[...] = jnp.sum(aq + pltpu.roll(aq, HALF, 1), axis=0, keepdims=True)
            dkw_ref[...] = jnp.sum(ak + pltpu.roll(ak, HALF, 1), axis=0, keepdims=True)
            dsk_ref[...] = jnp.sum(accs[...], axis=0, keepdims=True)
            for hd in range(N_HEADS):
                dws_ref[hd] = jnp.where(tril, dws_ref[hd], 0.0)
            hrow = lax.broadcasted_iota(jnp.int32, (N_HEADS, D_ATTN), 0)
            hcol = lax.broadcasted_iota(jnp.int32, (N_HEADS, D_ATTN), 1)
            sel = jnp.where((hcol >= hrow * HALF) & (hcol < (hrow + 1) * HALF), 1.0, 0.0).astype(BF16)
            rem = accb[...]
            tot = jnp.zeros((N_HEADS, BLK), F32)
            for _ in range(3):
                part = rem.astype(BF16)
                tot = tot + lax.dot_general(sel, part, NT_DIMS, preferred_element_type=F32)
                rem = rem - part.astype(F32)
            dbs_ref[...] = tot

    smem = pl.BlockSpec(memory_space=pltpu.SMEM)
    last = nb - 1
    tile_f32 = pltpu.VMEM((BLK, BLK), F32)
    return _call(
        "mixer_bwd", compute, (nb + 1,),
        [smem, smem,
         pl.BlockSpec((BLK, D_IN), lambda n: (jnp.minimum(n, last), 0)),
         pl.BlockSpec((BLK, 512), lambda n: (jnp.maximum(jnp.minimum(n, last) - 1, 0), 2)),
         pl.BlockSpec((BLK, D_MODEL), lambda n: (jnp.minimum(n, last), 0)),
         pl.BlockSpec((1, BLK), lambda n: (0, 0)),
         pl.BlockSpec((1, BLK), lambda n: (0, 0)),
         pl.BlockSpec((N_HEADS, BLK, BLK), lambda n: (0, 0, 0)),
         pl.BlockSpec((BLK, D_ATTN), lambda n: (0, 0))],
        [pl.BlockSpec((BLK, D_IN), lambda n: (jnp.minimum(n, last), 0)),
         pl.BlockSpec((BLK, 512), lambda n: (jnp.maximum(n - 1, 0), 0)),
         pl.BlockSpec((1, BLK), lambda n: (0, 0)),
         pl.BlockSpec((1, BLK), lambda n: (0, 0)),
         pl.BlockSpec((1, BLK), lambda n: (0, 0)),
         pl.BlockSpec((N_HEADS, BLK, BLK), lambda n: (0, 0, 0)),
         pl.BlockSpec((N_HEADS, BLK), lambda n: (0, 0))],
        [jax.ShapeDtypeStruct((s, D_IN), BF16),
         jax.ShapeDtypeStruct((s, 512), BF16),
         jax.ShapeDtypeStruct((1, BLK), F32),
         jax.ShapeDtypeStruct((1, BLK), F32),
         jax.ShapeDtypeStruct((1, BLK), F32),
         jax.ShapeDtypeStruct((N_HEADS, BLK, BLK), F32),
         jax.ShapeDtypeStruct((N_HEADS, BLK), F32)],
        [pltpu.VMEM((BLK, 512), F32), tile_f32, tile_f32, tile_f32, pltpu.VMEM((BLK, D_ATTN), F32),
         pltpu.VMEM((N_HEADS * BLK, BLK), BF16), pltpu.VMEM((N_HEADS // 2, BLK, 2 * BLK), BF16),
         pltpu.VMEM((N_HEADS * BLK, 2 * BLK), F32)],
        [sinks, slopes, proj, proj, dmix, qw, kw, w_s, bmap], push)


def _dh_norm_bwd(dproj, dkv, w_t, x, dx_out, g, bf16_copy, push=None):
    s = x.shape[0]
    tm = _tile(TM_RESIDENT, s)

    def compute(dp_ref, dkv_ref, w_ref, x_ref, dxo_ref, g_ref, dx_ref, dg_ref, *dxb_ref):
        @pl.when(pl.program_id(0) == 0)
        def _():
            dg_ref[...] = jnp.zeros_like(dg_ref)

        dp = jnp.concatenate([dp_ref[:, :C_K], dkv_ref[...], dp_ref[:, C_GA:]], axis=1)
        dh = jnp.dot(dp, w_ref[...], preferred_element_type=F32)
        xf = x_ref[...]
        r = lax.rsqrt(jnp.mean(xf * xf, axis=-1, keepdims=True) + RMS_EPS)
        yhat = xf * r
        dyh = dh * g_ref[...]
        c = jnp.mean(dyh * yhat, axis=-1, keepdims=True)
        dx = dxo_ref[...] + r * (dyh - yhat * c)
        dx_ref[...] = dx
        if bf16_copy:
            dxb_ref[0][...] = dx.astype(BF16)
        dg_ref[...] += jnp.sum(dh * yhat, axis=0, keepdims=True)

    tok = pl.BlockSpec((tm, D_MODEL), lambda i: (i, 0))
    return _call(
        "dh_norm_bwd", compute, (s // tm,),
        [pl.BlockSpec((tm, D_IN), lambda i: (i, 0)), pl.BlockSpec((tm, C_GA - C_K), lambda i: (i, 0)),
         _resident((D_IN, D_MODEL)), tok, tok, pl.BlockSpec((1, D_MODEL), lambda i: (0, 0))],
        [tok, pl.BlockSpec((1, D_MODEL), lambda i: (0, 0))] + [tok] * bf16_copy,
        [jax.ShapeDtypeStruct((s, D_MODEL), F32), jax.ShapeDtypeStruct((1, D_MODEL), F32)]
        + [jax.ShapeDtypeStruct((s, D_MODEL), BF16)] * bf16_copy,
        [], [dproj, dkv, w_t, x, dx_out, g], push)


def _dw_in(dproj, dkv, h, push=None):
    s = h.shape[0]
    tk = _tile(DW_IN_TOK, s)
    nk = s // tk
    kv_a, kv_b = DW_IN_ROWS - C_K, C_GA - DW_IN_ROWS

    def compute(dp_ref, kva_ref, kvb_ref, h_ref, o_ref, acc):
        j = pl.program_id(0)
        k = pl.program_id(1)

        @pl.when(k == 0)
        def _():
            acc[...] = jnp.zeros_like(acc)

        dp = jnp.concatenate([jnp.where(j == 1, kvb_ref[...], dp_ref[:, :kv_b]), dp_ref[:, kv_b:C_K],
                              jnp.where(j == 0, kva_ref[...], dp_ref[:, C_K:])], axis=1)
        acc[...] += lax.dot_general(dp, h_ref[...], TN_DIMS, preferred_element_type=F32)

        @pl.when(k == nk - 1)
        def _():
            o_ref[...] = acc[...].astype(BF16)

    return _call(
        "dw_in", compute, (D_IN // DW_IN_ROWS, nk),
        [pl.BlockSpec((tk, DW_IN_ROWS), lambda j, k: (k, j)),
         pl.BlockSpec((tk, kv_a), lambda j, k: (k, 0)),
         pl.BlockSpec((tk, kv_b), lambda j, k: (k, kv_a // kv_b)),
         pl.BlockSpec((tk, D_MODEL), lambda j, k: (k, 0))],
        [pl.BlockSpec((DW_IN_ROWS, D_MODEL), lambda j, k: (j, 0))],
        [jax.ShapeDtypeStruct((D_IN, D_MODEL), BF16)],
        [pltpu.VMEM((DW_IN_ROWS, D_MODEL), F32)], [dproj, dkv, dkv, h], push)


def _adamw_math(w, g, m, v):
    m_new = ADAM_B1 * m + (1.0 - ADAM_B1) * g
    v_new = ADAM_B2 * v + (1.0 - ADAM_B2) * jnp.square(g)
    m_hat = m_new / (1.0 - ADAM_B1 ** ADAM_STEP)
    v_hat = v_new / (1.0 - ADAM_B2 ** ADAM_STEP)
    return -ADAM_LR * (m_hat / (jnp.sqrt(v_hat) + ADAM_EPS) + ADAM_WD * w), m_new, v_new


def _adamw(name, w, g, m, v, tr=None):
    shape = w.shape
    c = shape[-1]
    flat = [a.reshape(-1, c) for a in (w, g, m, v)]
    r = flat[0].shape[0]
    tr = r if tr is None else tr

    def compute(w_ref, g_ref, m_ref, v_ref, d_ref, mo_ref, vo_ref):
        d_ref[...], mo_ref[...], vo_ref[...] = _adamw_math(w_ref[...], g_ref[...], m_ref[...], v_ref[...])

    spec = pl.BlockSpec((tr, c), lambda i: (i, 0))
    outs = _call(name, compute, (r // tr,), [spec] * 4, [spec] * 3, [jax.ShapeDtypeStruct((r, c), F32)] * 3,
                 [], flat)
    return [o.reshape(shape) for o in outs]


def _adamw_from_slots(name, slots, w, m, v, tr):
    n_layers, r, c = w.shape
    nt = r // tr

    def compute(*refs):
        s_refs, (w_ref, m_ref, v_ref), (g_ref, d_ref, mo_ref, vo_ref) = refs[:n_layers], refs[n_layers:n_layers + 3], \
            refs[n_layers + 3:]
        for k in range(n_layers):
            @pl.when(pl.program_id(0) == k)
            def _(k=k):
                g = s_refs[k][0].astype(F32)
                for dev in range(1, N_DEV):
                    g = g + s_refs[k][dev].astype(F32)
                g_ref[0] = g
                d_ref[0], mo_ref[0], vo_ref[0] = _adamw_math(w_ref[0], g, m_ref[0], v_ref[0])

    def slots_spec(k):
        return pl.BlockSpec((N_DEV, tr, c),
                            lambda l, i: (0, jnp.where(l == k, i, jnp.where(l < k, 0, nt - 1)), 0))

    tile = pl.BlockSpec((1, tr, c), lambda l, i: (l, i, 0))
    return _call(name, compute, (n_layers, nt), [slots_spec(k) for k in range(n_layers)] + [tile] * 3, [tile] * 4,
                 [jax.ShapeDtypeStruct((n_layers, r, c), F32)] * 4, [], list(slots) + [w, m, v])


def _pack_rows(parts):
    rows = []
    for a in parts:
        flat = a.reshape(-1)
        n = -(-flat.shape[0] // (8 * BLK)) * 8
        rows.append(jnp.pad(flat, (0, n * BLK - flat.shape[0])).reshape(n, BLK))
    return jnp.concatenate(rows, axis=0)


def _unpack_rows(packed, like):
    out = []
    row = 0
    for a in like:
        n = -(-a.size // (8 * BLK)) * 8
        out.append(packed[row:row + n].reshape(-1)[:a.size].reshape(a.shape))
        row += n
    return out


def kernel(x, norm_g, w_in, q_norm, k_norm, sinks, w_s, b_s, w_out, loss_target, m_norm_g, m_w_in, m_q_norm, m_k_norm, m_sinks, m_w_s, m_b_s, m_w_out, v_norm_g, v_w_in, v_q_norm, v_k_norm, v_sinks, v_w_s, v_b_s, v_w_out):
    xs = x[0]
    tgt = loss_target[0]
    slopes = jnp.asarray(2.0 ** (-8.0 * np.arange(1, N_HEADS + 1) / N_HEADS), dtype=F32)
    wt_sh = jnp.swapaxes(w_in, 1, 2).astype(BF16)
    wo_sh = w_out.astype(BF16)

    layer_par = []
    for l in range(DEPTH):
        layer_par.append((jnp.tile(q_norm[l], 2)[None, :], jnp.tile(k_norm[l], 2)[None, :],
                          jnp.repeat(b_s[l].T, HALF, axis=1)))

    wt_full = _exchange("gather_w0", _gather_rows((W_IN_SHARD,), (0,)), [wt_sh], [W_FULL[W_IN_SHARD]])[0]
    wo_full = None
    saved = []
    cur = xs
    for l in range(DEPTH):
        qw, kw, bmap = layer_par[l]
        more = l + 1 < DEPTH
        res = _norm_proj(cur, norm_g[l][None, :], wt_full,
                         (_gather_rows((W_IN_SHARD,), (l + 1,)), [wt_sh], [W_FULL[W_IN_SHARD]]) if more else None)
        h, proj = res[0], res[1]
        if l == 0:
            mix, wo_full, wo_next = _mixer_fwd(
                proj, qw, kw, sinks[l], slopes, w_s[l], bmap,
                (_gather_rows((W_OUT_SHARD, W_OUT_SHARD), (0, 1)), [wo_sh, wo_sh], [W_FULL[W_OUT_SHARD]] * 2))
        else:
            res2 = _mixer_fwd(proj, qw, kw, sinks[l], slopes, w_s[l], bmap,
                              (_gather_rows((W_OUT_SHARD,), (l + 1,)), [wo_sh], [W_FULL[W_OUT_SHARD]]) if more else None)
            mix, wo_next = res2[0], res2[1] if more else None
        saved.append((cur, h, proj, mix, wt_full, wo_full))
        if more:
            cur = _out_proj(mix, wo_full, cur)
            wt_full, wo_full = res[2], wo_next
        else:
            dx, dx_b, sq = _out_proj_loss(mix, wo_full, cur, tgt)
    loss = lax.psum(0.5 * jnp.sum(sq) / D_MODEL, MESH_AXES)

    wt_slots, wo_slots, g_small, g_norm = ([None] * DEPTH for _ in range(4))
    wt_slots_shape = _slots_shape(W_IN_SHARD, D_MODEL, BF16)
    dwt_waiting = None
    for l in reversed(range(DEPTH)):
        x_l, h, proj, mix, wt_l, wo_l = saved[l]
        qw, kw, bmap = layer_par[l]
        dmix = _dmix(dx_b, wo_l)
        dwo_part = _dw_out(mix, dx_b)
        res = _mixer_bwd(proj, dmix, qw, kw, sinks[l], slopes, w_s[l], bmap,
                         None if dwt_waiting is None else
                         (_scatter_rows((W_IN_SHARD,)), [dwt_waiting], [wt_slots_shape]))
        dproj, dkv, dqw, dkw, dsk, dws, dbs = res[:7]
        if dwt_waiting is not None:
            wt_slots[l + 1] = res[7]
        small_like = [dqw[0, :HALF], dkw[0, :HALF], dsk[0, :N_HEADS], dws, dbs]
        packed = _pack_rows(small_like)
        dwt_waiting, small_slots, wo_slots[l] = _dw_in(
            dproj, dkv, h, (_gather_slots_and_scatter_rows(W_OUT_SHARD), [packed, dwo_part],
                            [_slots_shape(*packed.shape, F32), _slots_shape(W_OUT_SHARD, D_MODEL, BF16)]))
        if l > 0:
            dx, dng, dx_b = _dh_norm_bwd(dproj, dkv, wt_l, x_l, dx, norm_g[l][None, :], True)
        else:
            dx, dng, wt_slots[0] = _dh_norm_bwd(dproj, dkv, wt_l, x_l, dx, norm_g[l][None, :], False,
                                                (_scatter_rows((W_IN_SHARD,)), [dwt_waiting], [wt_slots_shape]))
        g_small[l] = _unpack_rows(_sum_slots(small_slots, packed.shape[0]), small_like)
        g_norm[l] = dng[0]

    dng_all = _pack_rows([jnp.stack(g_norm)])
    dng_slots = _exchange("gather_dnorm", _gather_slots(), [dng_all], [_slots_shape(*dng_all.shape, F32)])[0]
    gr_norm = _unpack_rows(_sum_slots(dng_slots, dng_all.shape[0]), [norm_g])[0]
    gr_qn, gr_kn, gr_sk, gr_ws, gr_bs = (jnp.stack([g_small[l][i] for l in range(DEPTH)]) for i in range(5))

    def t(a):
        return jnp.swapaxes(a, 1, 2)

    from_slots = {1: _adamw_from_slots("adamw_w_in", wt_slots, t(w_in), t(m_w_in), t(v_w_in), ADAMW_ROWS),
                  7: _adamw_from_slots("adamw_w_out", wo_slots, w_out, m_w_out, v_w_out, ADAMW_ROWS)}
    from_slots[1] = [t(a) for a in from_slots[1]]

    grads = [gr_norm, None, gr_qn, gr_kn, gr_sk, gr_ws, gr_bs, None]
    weights = [norm_g, w_in, q_norm, k_norm, sinks, w_s, b_s, w_out]
    moms = [m_norm_g, m_w_in, m_q_norm, m_k_norm, m_sinks, m_w_s, m_b_s, m_w_out]
    vels = [v_norm_g, v_w_in, v_q_norm, v_k_norm, v_sinks, v_w_s, v_b_s, v_w_out]
    tiles = [None, None, None, None, None, 1024, None, None]
    names = ["norm_g", "w_in", "q_norm", "k_norm", "sinks", "w_s", "b_s", "w_out"]
    deltas, new_m, new_v = [], [], []
    for i, (nm, w, g, m, v, tr) in enumerate(zip(names, weights, grads, moms, vels, tiles)):
        if i in from_slots:
            grads[i], d, mo, vo = from_slots[i]
        else:
            d, mo, vo = _adamw("adamw_" + nm, w, g, m, v, tr)
        deltas.append(d)
        new_m.append(mo)
        new_v.append(vo)

    return (loss, dx[None], *grads, *deltas, *new_m, *new_v)
```

```python
import numpy as np
import jax
import jax.numpy as jnp
from jax import lax
from jax.experimental import pallas as pl
from jax.experimental.pallas import tpu as pltpu

F32 = jnp.float32
BF16 = jnp.bfloat16

D_MODEL = 2048
D_ATTN = 1024
D_IN = 5632
N_HEADS = 16
DEPTH = 4
BLK = 128
HALF = 64
RMS_EPS = 1e-6
C_Q, C_K, C_V, C_GA, C_ZU, C_ZV, C_GB = 0, 1024, 1280, 1536, 2560, 3584, 4608
NEG = -1e30
N_DEV = 8
W_IN_SHARD = D_IN // N_DEV
W_OUT_SHARD = D_MODEL // N_DEV
INV_SQRT2 = 0.7071067811865476
INV_SQRT_2PI = 0.3989422804014327

TM_RESIDENT = 256
TM_STREAM = 512
TM_WIDE = 1024
TN_PROJ = 512
DW_IN_ROWS = D_IN // 4
DW_IN_TOK = 2048
ADAMW_ROWS = 64
MID_STEP_16THS = 13

ADAM_LR = 0.001
ADAM_B1 = 0.9
ADAM_B2 = 0.999
ADAM_EPS = 1e-08
ADAM_WD = 0.01
ADAM_STEP = 10

NT_DIMS = (((1,), (1,)), ((), ()))
TN_DIMS = (((0,), (0,)), ((), ()))
MESH_AXES = ("x", "y", "c")


def _sigmoid(v):
    return 1.0 / (1.0 + jnp.exp(-v))


def _gelu_cdf(z):
    return 0.5 * (1.0 + lax.erf(z * INV_SQRT2))


def _gelu_grad(z, cdf):
    return cdf + z * (jnp.exp(-0.5 * z * z) * INV_SQRT_2PI)


def _lane_halves(rows):
    lane = lax.broadcasted_iota(jnp.int32, (rows, BLK), 1)
    return lane < HALF, lane >= HALF


def _half_masks():
    return {BLK: _lane_halves(BLK), 2 * BLK: _lane_halves(2 * BLK)}


def _half_sum(v, ones):
    h_a, h_b = ones[v.shape[0]]
    s_a = jnp.sum(jnp.where(h_a, v, 0.0), axis=-1, keepdims=True)
    s_b = jnp.sum(jnp.where(h_b, v, 0.0), axis=-1, keepdims=True)
    return jnp.where(h_a, s_a, s_b)


def _half_rms(v, w, ones):
    r = lax.rsqrt(_half_sum(v * v, ones) * (1.0 / HALF) + RMS_EPS)
    yhat = v * r
    return yhat, r, yhat * w


def _half_rms_bwd(dy, yhat, r, w, ones):
    dyh = dy * w
    c = _half_sum(dyh * yhat, ones) * (1.0 / HALF)
    return r * (dyh - yhat * c)


def _band_mask(n):
    t = lax.broadcasted_iota(jnp.int32, (BLK, 2 * BLK), 0)
    kk = lax.broadcasted_iota(jnp.int32, (BLK, 2 * BLK), 1)
    dist = t + BLK - kk
    first_key = jnp.where(n > 0, 0, BLK)
    ok = (dist >= 0) & (dist < BLK) & (kk >= first_key)
    return ok, dist.astype(F32)


def _alibi_table(n, slopes_ref, bias_scr):
    ok, distf = _band_mask(n)
    for hd in range(N_HEADS):
        bias_scr[BLK * hd:BLK * (hd + 1), :] = jnp.where(ok, -(slopes_ref[hd] * distf), NEG)


def _sink_col(sinks_ref, kt):
    return jnp.concatenate([jnp.full((BLK, 1), sinks_ref[8 * kt + i], F32) for i in range(8)], axis=0)


def _softmax_sink(s_scaled, bias, sink):
    s = s_scaled + bias
    m = jnp.maximum(jnp.max(s, axis=-1, keepdims=True), sink)
    p = jnp.exp(s - m)
    es = jnp.exp(sink - m)
    inv = 1.0 / (jnp.sum(p, axis=-1, keepdims=True) + es)
    return p * inv, es * inv


def _rows(i):
    return slice(BLK * i, BLK * (i + 1))


def _cols(base, j):
    return slice(base + BLK * j, base + BLK * (j + 1))


def _to_half(v, have, want):
    return v if have == want else pltpu.roll(v, HALF, 1)


def _tril_mask():
    row = lax.broadcasted_iota(jnp.int32, (BLK, BLK), 0)
    col = lax.broadcasted_iota(jnp.int32, (BLK, BLK), 1)
    return row >= col


def _tile(limit, s):
    t = min(limit, s)
    assert s % t == 0, (s, t)
    return t


def _mesh_place():
    x, y, c = lax.axis_index("x"), lax.axis_index("y"), lax.axis_index("c")
    return x, y, c, 4 * x + 2 * y + c


def _peer(x, y, c, k):
    px = 1 - x if k & 4 else x
    py = 1 - y if k & 2 else y
    pc = 1 - c if k & 1 else c
    return (px, py, pc), 4 * px + 2 * py + pc


class _Pushes:
    def __init__(self, n_arrays, src_view, dst_view):
        self.na = n_arrays
        self.src_view = src_view
        self.dst_view = dst_view

    def scratch(self):
        n = self.na * (N_DEV - 1)
        return [pltpu.SemaphoreType.DMA((n,)), pltpu.SemaphoreType.DMA((n,)), pltpu.SemaphoreType.DMA((self.na,))]

    def copies(self, src_refs, dst_refs, send_sems, recv_sems, local_sems):
        x, y, c, me = _mesh_place()
        cps = []
        for a in range(self.na):
            cps.append(pltpu.make_async_copy(self.src_view(a, src_refs[a], me), self.dst_view(a, dst_refs[a], me),
                                             local_sems.at[a]))
        for k in range(1, N_DEV):
            peer, pidx = _peer(x, y, c, k)
            for a in range(self.na):
                sem = a * (N_DEV - 1) + k - 1
                cps.append(pltpu.make_async_remote_copy(
                    src_ref=self.src_view(a, src_refs[a], pidx), dst_ref=self.dst_view(a, dst_refs[a], me),
                    send_sem=send_sems.at[sem], recv_sem=recv_sems.at[sem],
                    device_id=peer, device_id_type=pl.DeviceIdType.MESH))
        return cps

    def plan(self, src_refs, dst_refs, send_sems, recv_sems, local_sems):
        cps = self.copies(src_refs, dst_refs, send_sems, recv_sems, local_sems)
        return cps, [], [], [cp.wait for cp in cps]


class _TwoLevelGather:
    def __init__(self, shard_rows, layers):
        self.na = len(shard_rows)
        self.shard_rows = shard_rows
        self.layers = layers

    def scratch(self):
        n = self.na * (N_DEV - 1)
        return [pltpu.SemaphoreType.DMA((n,)), pltpu.SemaphoreType.DMA((n,)), pltpu.SemaphoreType.DMA((self.na,))]

    def plan(self, src_refs, dst_refs, send_sems, recv_sems, local_sems):
        x, y, c, _ = _mesh_place()
        sibling = (x, y, 1 - c)
        chips = [(1 - x, y), (x, 1 - y), (1 - x, 1 - y)]
        start, mid_wait, mid_start, final = [], [], [], []
        for a in range(self.na):
            r = self.shard_rows[a]
            src = src_refs[a].at[self.layers[a]]
            dst = dst_refs[a]

            def rows(px, py, pc, r=r, dst=dst):
                return dst.at[pl.ds(pl.multiple_of((4 * px + 2 * py + pc) * r, 64), r), :]

            def remote(k, s_ref, block, to, a=a, rows=rows):
                return pltpu.make_async_remote_copy(
                    src_ref=s_ref, dst_ref=rows(*block),
                    send_sem=send_sems.at[a * (N_DEV - 1) + k], recv_sem=recv_sems.at[a * (N_DEV - 1) + k],
                    device_id=to, device_id_type=pl.DeviceIdType.MESH)

            mine = pltpu.make_async_copy(src, rows(x, y, c), local_sems.at[a])
            own = [remote(0, src, (x, y, c), sibling)]
            own += [remote(1 + j, src, (x, y, c), (*chip, c)) for j, chip in enumerate(chips)]
            passed = [remote(4 + j, rows(*chip, c), (*chip, c), sibling) for j, chip in enumerate(chips)]
            start += [mine] + own
            mid_wait += own[1:]
            mid_start += passed
            final += [own[0].wait_recv] + [cp.wait_recv for cp in passed]
            final += [cp.wait_send for cp in own + passed] + [mine.wait]
        return start, mid_wait, mid_start, final


def _call(name, compute, grid, in_specs, out_specs, out_shape, scratch, args, push=None):
    sem = pltpu.CompilerParams(dimension_semantics=("arbitrary",) * len(grid))
    if push is None:
        return pl.pallas_call(compute, name=name, grid=grid, in_specs=in_specs, out_specs=out_specs,
                              out_shape=out_shape, scratch_shapes=scratch, compiler_params=sem)(*args)
    pushes, srcs, xshapes = push
    n_in, n_out, n_scr, na = len(args), len(out_shape), len(scratch), pushes.na
    hbm = pl.BlockSpec(memory_space=pltpu.HBM)

    def body(*refs):
        ins, refs = refs[:n_in], refs[n_in:]
        xin, refs = refs[:na], refs[na:]
        outs, refs = refs[:n_out], refs[n_out:]
        xout, refs = refs[:na], refs[na:]
        scr, sems = refs[:n_scr], refs[n_scr:]
        start, mid_wait, mid_start, final = pushes.plan(xin, xout, *sems)
        first = pl.program_id(0) == 0
        middle = pl.program_id(0) == (grid[0] * MID_STEP_16THS) // 16
        last = pl.program_id(0) == grid[0] - 1
        for d in range(1, len(grid)):
            first = first & (pl.program_id(d) == 0)
            middle = middle & (pl.program_id(d) == 0)
            last = last & (pl.program_id(d) == grid[d] - 1)

        @pl.when(first)
        def _():
            for cp in start:
                cp.start()

        if mid_start:
            @pl.when(middle)
            def _():
                for cp in mid_wait:
                    cp.wait_recv()
                for cp in mid_start:
                    cp.start()

        compute(*ins, *outs, *scr)

        @pl.when(last)
        def _():
            for wait in final:
                wait()

    return pl.pallas_call(
        body, name=name, grid=grid,
        in_specs=list(in_specs) + [hbm] * na, out_specs=list(out_specs) + [hbm] * na,
        out_shape=list(out_shape) + list(xshapes),
        scratch_shapes=list(scratch) + pushes.scratch(), compiler_params=sem)(*args, *srcs)


def _exchange(name, pushes, srcs, out_shapes):
    na = pushes.na
    hbm = pl.BlockSpec(memory_space=pltpu.HBM)

    def body(*refs):
        start, mid_wait, mid_start, final = pushes.plan(refs[:na], refs[na:2 * na], *refs[2 * na:])
        for cp in start:
            cp.start()
        for cp in mid_wait:
            cp.wait_recv()
        for cp in mid_start:
            cp.start()
        for wait in final:
            wait()

    return pl.pallas_call(body, name=name, in_specs=[hbm] * na, out_specs=[hbm] * na, out_shape=out_shapes,
                          scratch_shapes=pushes.scratch())(*srcs)


def _gather_rows(shard_rows, layers):
    return _TwoLevelGather(shard_rows, layers)


def _scatter_rows(shard_rows):
    def src_view(a, ref, idx):
        r = shard_rows[a]
        return ref.at[pl.ds(pl.multiple_of(idx * r, 64), r), :]

    def dst_view(a, ref, idx):
        return ref.at[idx]

    return _Pushes(len(shard_rows), src_view, dst_view)


def _gather_slots():
    return _Pushes(1, lambda a, ref, idx: ref, lambda a, ref, idx: ref.at[idx])


def _gather_slots_and_scatter_rows(rows):
    def src_view(a, ref, idx):
        return ref if a == 0 else ref.at[pl.ds(pl.multiple_of(idx * rows, 64), rows), :]

    return _Pushes(2, src_view, lambda a, ref, idx: ref.at[idx])


W_FULL = {W_IN_SHARD: jax.ShapeDtypeStruct((D_IN, D_MODEL), BF16),
          W_OUT_SHARD: jax.ShapeDtypeStruct((D_MODEL, D_MODEL), BF16)}


def _slots_shape(rows, cols, dtype):
    return jax.ShapeDtypeStruct((N_DEV, rows, cols), dtype)


def _sum_slots(slots, tr):
    _, r, c = slots.shape

    def compute(s_ref, o_ref):
        tot = s_ref[0].astype(F32)
        for d in range(1, N_DEV):
            tot = tot + s_ref[d].astype(F32)
        o_ref[...] = tot

    return _call("sum_slots", compute, (r // tr,),
                 [pl.BlockSpec((N_DEV, tr, c), lambda i: (0, i, 0))], [pl.BlockSpec((tr, c), lambda i: (i, 0))],
                 [jax.ShapeDtypeStruct((r, c), F32)], [], [slots])[0]


def _resident(shape):
    return pl.BlockSpec(shape, lambda *_: (0,) * len(shape), pipeline_mode=pl.Buffered(1))


def _norm_proj(x, g, w_t, push=None):
    s = x.shape[0]
    tm = _tile(TM_RESIDENT, s)

    def compute(x_ref, g_ref, w_ref, h_ref, p_ref):
        xf = x_ref[...]
        r = lax.rsqrt(jnp.mean(xf * xf, axis=-1, keepdims=True) + RMS_EPS)
        h = ((xf * r) * g_ref[...]).astype(BF16)
        h_ref[...] = h
        for j in range(D_IN // TN_PROJ):
            cols = slice(j * TN_PROJ, (j + 1) * TN_PROJ)
            p_ref[:, cols] = lax.dot_general(h, w_ref[cols, :], NT_DIMS, preferred_element_type=F32)

    return _call(
        "norm_proj", compute, (s // tm,),
        [pl.BlockSpec((tm, D_MODEL), lambda i: (i, 0)), pl.BlockSpec((1, D_MODEL), lambda i: (0, 0)),
         _resident((D_IN, D_MODEL))],
        [pl.BlockSpec((tm, D_MODEL), lambda i: (i, 0)), pl.BlockSpec((tm, D_IN), lambda i: (i, 0))],
        [jax.ShapeDtypeStruct((s, D_MODEL), BF16), jax.ShapeDtypeStruct((s, D_IN), F32)],
        [], [x, g, w_t], push)


def _sgu_weights(ws_ref, wtril_scr, wtril_t_scr=None):
    tril = _tril_mask()
    for hd in range(N_HEADS):
        w = jnp.where(tril, ws_ref[hd], 0.0)
        wtril_scr[BLK * hd:BLK * (hd + 1), :] = w.astype(BF16)
        if wtril_t_scr is not None:
            wtril_t_scr[hd // 2, :, BLK * (hd % 2):BLK * (hd % 2 + 1)] = w.T.astype(BF16)


def _kv_band(kt, p_ref, kvp_ref, kw_v, ones):
    kband = jnp.concatenate([kvp_ref[:, _cols(0, kt)], p_ref[:, _cols(C_K, kt)]], axis=0)
    kyhat, kr, kn = _half_rms(kband, kw_v, ones)
    vband = jnp.concatenate([kvp_ref[:, _cols(256, kt)], p_ref[:, _cols(C_V, kt)]], axis=0)
    return kyhat, kr, (kn * 0.125).astype(BF16), vband.astype(BF16)


def _stack_heads(tiles, halves):
    parts = []
    for tt, tile in enumerate(tiles):
        for qh in range(2):
            parts.append(_to_half(jnp.where(halves[qh], tile, 0.0), qh, tt // 2).astype(BF16))
    return jnp.concatenate(parts, axis=0)


def _unstack_heads(stacked, tt, h_a):
    return jnp.where(h_a, _to_half(stacked[_rows(2 * tt)], tt // 2, 0), _to_half(stacked[_rows(2 * tt + 1)], tt // 2, 1))


def _mixer_fwd(proj, qw, kw, sinks, slopes, w_s, bmap, push=None):
    s = proj.shape[0]
    nb = s // BLK

    def compute(sinks_ref, slopes_ref, p_ref, kvp_ref, qw_ref, kw_ref, ws_ref, bmap_ref, mix_ref,
                wtril_scr, bias_scr):
        _mixer_tables(pl.program_id(0), slopes_ref, ws_ref, wtril_scr, bias_scr)
        _mixer_fwd_block(sinks_ref, p_ref, kvp_ref, qw_ref, kw_ref, bmap_ref, mix_ref, wtril_scr, bias_scr)

    smem = pl.BlockSpec(memory_space=pltpu.SMEM)
    return _call(
        "mixer_fwd", compute, (nb,),
        [smem, smem,
         pl.BlockSpec((BLK, D_IN), lambda n: (n, 0)),
         pl.BlockSpec((BLK, 512), lambda n: (jnp.maximum(n - 1, 0), 2)),
         pl.BlockSpec((1, BLK), lambda n: (0, 0)),
         pl.BlockSpec((1, BLK), lambda n: (0, 0)),
         pl.BlockSpec((N_HEADS, BLK, BLK), lambda n: (0, 0, 0)),
         pl.BlockSpec((BLK, D_ATTN), lambda n: (0, 0))],
        [pl.BlockSpec((BLK, D_MODEL), lambda n: (n, 0))],
        [jax.ShapeDtypeStruct((s, D_MODEL), BF16)],
        [pltpu.VMEM((N_HEADS * BLK, BLK), BF16), pltpu.VMEM((N_HEADS * BLK, 2 * BLK), F32)],
        [sinks, slopes, proj, proj, qw, kw, w_s, bmap], push)


def _mixer_tables(n, slopes_ref, ws_ref, wtril_scr, bias_scr, wtril_t_scr=None):
    @pl.when(n == 0)
    def _():
        _sgu_weights(ws_ref, wtril_scr, wtril_t_scr)

    @pl.when(n <= 1)
    def _():
        _alibi_table(n, slopes_ref, bias_scr)


def _mixer_fwd_block(sinks_ref, p_ref, kvp_ref, qw_ref, kw_ref, bmap_ref, mix_ref, wtril_scr, bias_scr):
    h_a, h_b = _lane_halves(BLK)
    ones = _half_masks()
    halves = (h_a, h_b)
    qw_v = qw_ref[...]
    kw_v = kw_ref[...]
    bands = [_kv_band(kt, p_ref, kvp_ref, kw_v, ones) for kt in range(2)]
    sc = []
    for kt in range(2):
        qn = [_half_rms(p_ref[:, _cols(C_Q, 4 * kt + tt)], qw_v, ones)[2] for tt in range(4)]
        sc.append(lax.dot_general(_stack_heads(qn, halves), bands[kt][2], NT_DIMS, preferred_element_type=F32))
    zu, mixed = [], []
    for j in range(8):
        zu_pre = p_ref[:, _cols(C_ZU, j)]
        zv_pre = p_ref[:, _cols(C_ZV, j)]
        zu.append(zu_pre * _gelu_cdf(zu_pre))
        zvb = (zv_pre * _gelu_cdf(zv_pre)).astype(BF16)
        mixed.append(jnp.dot(wtril_scr[2 * BLK * j:2 * BLK * (j + 1), :], zvb, preferred_element_type=F32))
    o = []
    for kt in range(2):
        p, _ = _softmax_sink(sc[kt], bias_scr[8 * BLK * kt:8 * BLK * (kt + 1), :], _sink_col(sinks_ref, kt))
        o.append(jnp.dot(p.astype(BF16), bands[kt][3], preferred_element_type=F32))
    for j in range(8):
        gb = p_ref[:, _cols(C_GB, j)]
        mx = jnp.where(h_a, mixed[j][0:BLK], mixed[j][BLK:2 * BLK]) + bmap_ref[:, _cols(0, j)]
        mix_ref[:, _cols(D_ATTN, j)] = ((zu[j] * mx) * (gb * _sigmoid(gb))).astype(BF16)
    for kt in range(2):
        for tt in range(4):
            j = 4 * kt + tt
            ga = p_ref[:, _cols(C_GA, j)]
            mix_ref[:, _cols(0, j)] = (_unstack_heads(o[kt], tt, h_a) * (ga * _sigmoid(ga))).astype(BF16)


def _out_proj(mix, w_o, x):
    s = x.shape[0]
    tm = _tile(TM_WIDE, s)

    def compute(m_ref, w_ref, x_ref, o_ref):
        o_ref[...] = x_ref[...] + jnp.dot(m_ref[...], w_ref[...], preferred_element_type=F32)

    return _call(
        "out_proj", compute, (s // tm,),
        [pl.BlockSpec((tm, D_MODEL), lambda i: (i, 0)), _resident((D_MODEL, D_MODEL)),
         pl.BlockSpec((tm, D_MODEL), lambda i: (i, 0))],
        [pl.BlockSpec((tm, D_MODEL), lambda i: (i, 0))],
        [jax.ShapeDtypeStruct((s, D_MODEL), F32)], [], [mix, w_o, x])[0]


def _out_proj_loss(mix, w_o, x, tgt):
    s = x.shape[0]
    tm = _tile(TM_STREAM, s)

    def compute(m_ref, w_ref, x_ref, t_ref, dy_ref, dyb_ref, sq_ref):
        @pl.when(pl.program_id(0) == 0)
        def _():
            sq_ref[...] = jnp.zeros_like(sq_ref)

        y = x_ref[...] + jnp.dot(m_ref[...], w_ref[...], preferred_element_type=F32)
        e = y - t_ref[...]
        dy = e * (1.0 / D_MODEL)
        dy_ref[...] = dy
        dyb_ref[...] = dy.astype(BF16)
        sq_ref[...] += jnp.sum(e * e, axis=0, keepdims=True)

    tok = pl.BlockSpec((tm, D_MODEL), lambda i: (i, 0))
    return _call(
        "out_proj_loss", compute, (s // tm,),
        [tok, _resident((D_MODEL, D_MODEL)), tok, tok],
        [tok, tok, pl.BlockSpec((1, D_MODEL), lambda i: (0, 0))],
        [jax.ShapeDtypeStruct((s, D_MODEL), F32), jax.ShapeDtypeStruct((s, D_MODEL), BF16),
         jax.ShapeDtypeStruct((1, D_MODEL), F32)], [], [mix, w_o, x, tgt])


def _dmix(dx, w_o):
    s = dx.shape[0]
    tm = _tile(TM_WIDE, s)

    def compute(d_ref, w_ref, o_ref):
        o_ref[...] = lax.dot_general(d_ref[...], w_ref[...], NT_DIMS, preferred_element_type=F32)

    return _call(
        "dmix", compute, (s // tm,),
        [pl.BlockSpec((tm, D_MODEL), lambda i: (i, 0)), _resident((D_MODEL, D_MODEL))],
        [pl.BlockSpec((tm, D_MODEL), lambda i: (i, 0))],
        [jax.ShapeDtypeStruct((s, D_MODEL), F32)], [], [dx, w_o])[0]


def _dw_out(mix, dx):
    s = dx.shape[0]
    tk = _tile(TM_WIDE, s)
    nk = s // tk

    def compute(m_ref, d_ref, o_ref, acc):
        k = pl.program_id(0)

        @pl.when(k == 0)
        def _():
            acc[...] = jnp.zeros_like(acc)

        acc[...] += lax.dot_general(m_ref[...], d_ref[...], TN_DIMS, preferred_element_type=F32)

        @pl.when(k == nk - 1)
        def _():
            o_ref[...] = acc[...].astype(BF16)

    return _call(
        "dw_out", compute, (nk,),
        [pl.BlockSpec((tk, D_MODEL), lambda k: (k, 0)), pl.BlockSpec((tk, D_MODEL), lambda k: (k, 0))],
        [pl.BlockSpec((D_MODEL, D_MODEL), lambda k: (0, 0))],
        [jax.ShapeDtypeStruct((D_MODEL, D_MODEL), BF16)],
        [pltpu.VMEM((D_MODEL, D_MODEL), F32)], [mix, dx])[0]


def _mixer_bwd(proj, dmix, qw, kw, sinks, slopes, w_s, bmap, push=None):
    s = proj.shape[0]
    nb = s // BLK

    def compute(sinks_ref, slopes_ref, p_ref, kvp_ref, dm_ref, qw_ref, kw_ref, ws_ref, bmap_ref,
                dp_ref, dqw_ref, dkw_ref, dsk_ref, dws_ref, dbs_ref,
                pend, accq, acck, accs, accb, wtril_scr, wtril_t_scr, bias_scr):
        n = pl.program_id(0)
        h_a, h_b = _lane_halves(BLK)
        ones = _half_masks()
        halves = (h_a, h_b)
        lane = lax.broadcasted_iota(jnp.int32, (BLK, BLK), 1)

        @pl.when(n == 0)
        def _():
            accq[...] = jnp.zeros_like(accq)
            acck[...] = jnp.zeros_like(acck)
            accs[...] = jnp.zeros_like(accs)
            accb[...] = jnp.zeros_like(accb)
            dws_ref[...] = jnp.zeros_like(dws_ref)
            _sgu_weights(ws_ref, wtril_scr, wtril_t_scr)

        @pl.when(n >= 1)
        def _():
            dp_ref[:, 0:C_K] = pend[:, 0:C_K].astype(BF16)
            dp_ref[:, C_GA:D_IN] = pend[:, C_GA:D_IN].astype(BF16)

        @pl.when(n <= 1)
        def _():
            _alibi_table(n, slopes_ref, bias_scr)

        @pl.when(n < nb)
        def _():
            qw_v = qw_ref[...]
            kw_v = kw_ref[...]
            bands = [_kv_band(kt, p_ref, kvp_ref, kw_v, ones) for kt in range(2)]
            tiles, qst, dost, sc, dpm = [], [], [], [], []
            for kt in range(2):
                qn, d_o, tl = [], [], []
                for tt in range(4):
                    j = 4 * kt + tt
                    qyhat, qr, qn_t = _half_rms(p_ref[:, _cols(C_Q, j)], qw_v, ones)
                    ga = p_ref[:, _cols(C_GA, j)]
                    sg = _sigmoid(ga)
                    dma = dm_ref[:, _cols(0, j)]
                    qn.append(qn_t)
                    d_o.append(dma * (ga * sg))
                    tl.append((qyhat, qr, dma * (sg * (1.0 + ga * (1.0 - sg)))))
                tiles.append(tl)
                qst.append(_stack_heads(qn, halves))
                dost.append(_stack_heads(d_o, halves))
                sc.append(lax.dot_general(qst[kt], bands[kt][2], NT_DIMS, preferred_element_type=F32))
                dpm.append(lax.dot_general(dost[kt], bands[kt][3], NT_DIMS, preferred_element_type=F32))
            sgu = []
            for j in range(8):
                zu_pre = p_ref[:, _cols(C_ZU, j)]
                zv_pre = p_ref[:, _cols(C_ZV, j)]
                cu = _gelu_cdf(zu_pre)
                cv = _gelu_cdf(zv_pre)
                zvb = (zv_pre * cv).astype(BF16)
                sgu.append((zu_pre * cu, zvb, _gelu_grad(zu_pre, cu), _gelu_grad(zv_pre, cv),
                            jnp.dot(wtril_scr[2 * BLK * j:2 * BLK * (j + 1), :], zvb, preferred_element_type=F32)))
            dsink = jnp.zeros((BLK, BLK), F32)
            pst, dqkst = [], []
            for kt in range(2):
                p, p_sink = _softmax_sink(sc[kt], bias_scr[8 * BLK * kt:8 * BLK * (kt + 1), :],
                                          _sink_col(sinks_ref, kt))
                dsum = jnp.sum(p * dpm[kt], axis=-1, keepdims=True)
                dsink_col = -(p_sink * dsum)
                for i in range(8):
                    dsink = dsink + jnp.where(lane == 8 * kt + i, dsink_col[_rows(i)], 0.0)
                pst.append(p.astype(BF16))
                dqkst.append((p * (dpm[kt] - dsum)).astype(BF16))
            o, dqn_all, dvb, dkn = [], [], [], []
            for kt in range(2):
                o.append(jnp.dot(pst[kt], bands[kt][3], preferred_element_type=F32))
                dqn_all.append(jnp.dot(dqkst[kt], bands[kt][2], preferred_element_type=F32))
                dvb.append(lax.dot_general(pst[kt], dost[kt], TN_DIMS, preferred_element_type=F32))
                dkn.append(0.125 * lax.dot_general(dqkst[kt], qst[kt], TN_DIMS, preferred_element_type=F32))
            dms = []
            for j in range(8):
                zu, zvb, gu, gv, m_ab = sgu[j]
                gb = p_ref[:, _cols(C_GB, j)]
                dmb = dm_ref[:, _cols(D_ATTN, j)]
                mixed = jnp.where(h_a, m_ab[0:BLK], m_ab[BLK:2 * BLK]) + bmap_ref[:, _cols(0, j)]
                sgb = _sigmoid(gb)
                dgate = dmb * (gb * sgb)
                pend[:, _cols(C_ZU, j)] = (dgate * mixed) * gu
                pend[:, _cols(C_GB, j)] = (dmb * (zu * mixed)) * (sgb * (1.0 + gb * (1.0 - sgb)))
                dmixed = dgate * zu
                accb[:, _cols(0, j)] += dmixed
                dms.append(jnp.concatenate([jnp.where(h_a, dmixed, 0.0).astype(BF16),
                                            jnp.where(h_b, dmixed, 0.0).astype(BF16)], axis=0))
            dzv = []
            for j in range(8):
                dzv.append(jnp.dot(wtril_t_scr[j], dms[j], preferred_element_type=F32))
                dw_ab = lax.dot_general(dms[j], sgu[j][1], NT_DIMS, preferred_element_type=F32)
                dws_ref[2 * j] += dw_ab[0:BLK]
                dws_ref[2 * j + 1] += dw_ab[BLK:2 * BLK]
            dq_w = jnp.zeros((BLK, BLK), F32)
            for kt in range(2):
                for tt in range(4):
                    j = 4 * kt + tt
                    qyhat, qr, dsilu = tiles[kt][tt]
                    dqn = _unstack_heads(dqn_all[kt], tt, h_a)
                    pend[:, _cols(C_GA, j)] = _unstack_heads(o[kt], tt, h_a) * dsilu
                    pend[:, _cols(C_Q, j)] = _half_rms_bwd(dqn, qyhat, qr, qw_v, ones)
                    dq_w = dq_w + dqn * qyhat
            dk_w = jnp.zeros((BLK, BLK), F32)
            for kt in range(2):
                kyhat, kr = bands[kt][0], bands[kt][1]
                dk = _half_rms_bwd(dkn[kt], kyhat, kr, kw_v, ones)
                dkw_part = dkn[kt] * kyhat
                dk_w = dk_w + (dkw_part[0:BLK] + dkw_part[BLK:2 * BLK])
                dv = dvb[kt]

                @pl.when(n >= 1)
                def _():
                    dp_ref[:, _cols(C_K, kt)] = (pend[:, _cols(C_K, kt)] + dk[0:BLK]).astype(BF16)
                    dp_ref[:, _cols(C_V, kt)] = (pend[:, _cols(C_V, kt)] + dv[0:BLK]).astype(BF16)

                pend[:, _cols(C_K, kt)] = dk[BLK:2 * BLK]
                pend[:, _cols(C_V, kt)] = dv[BLK:2 * BLK]
            accq[...] += dq_w
            acck[...] += dk_w
            accs[...] += dsink
            for j in range(8):
                pend[:, _cols(C_ZV, j)] = dzv[j] * sgu[j][3]

        @pl.when(n == nb)
        def _():
            tril = _tril_mask()
            dp_ref[:, C_K:C_GA] = pend[:, C_K:C_GA].astype(BF16)
            aq = accq[...]
            ak = acck[...]
            dqw_ref[...] = jnp.sum(aq + pltpu.roll(aq, HALF, 1), axis=0, keepdims=True)
            dkw_ref[...] = jnp.sum(ak + pltpu.roll(ak, HALF, 1), axis=0, keepdims=True)
            dsk_ref[...] = jnp.sum(accs[...], axis=0, keepdims=True)
            for hd in range(N_HEADS):
                dws_ref[hd] = jnp.where(tril, dws_ref[hd], 0.0)
            hrow = lax.broadcasted_iota(jnp.int32, (N_HEADS, D_ATTN), 0)
            hcol = lax.broadcasted_iota(jnp.int32, (N_HEADS, D_ATTN), 1)
            sel = jnp.where((hcol >= hrow * HALF) & (hcol < (hrow + 1) * HALF), 1.0, 0.0).astype(BF16)
            rem = accb[...]
            tot = jnp.zeros((N_HEADS, BLK), F32)
            for _ in range(3):
                part = rem.astype(BF16)
                tot = tot + lax.dot_general(sel, part, NT_DIMS, preferred_element_type=F32)
                rem = rem - part.astype(F32)
            dbs_ref[...] = tot

    smem = pl.BlockSpec(memory_space=pltpu.SMEM)
    last = nb - 1
    tile_f32 = pltpu.VMEM((BLK, BLK), F32)
    return _call(
        "mixer_bwd", compute, (nb + 1,),
        [smem, smem,
         pl.BlockSpec((BLK, D_IN), lambda n: (jnp.minimum(n, last), 0)),
         pl.BlockSpec((BLK, 512), lambda n: (jnp.maximum(jnp.minimum(n, last) - 1, 0), 2)),
         pl.BlockSpec((BLK, D_MODEL), lambda n: (jnp.minimum(n, last), 0)),
         pl.BlockSpec((1, BLK), lambda n: (0, 0)),
         pl.BlockSpec((1, BLK), lambda n: (0, 0)),
         pl.BlockSpec((N_HEADS, BLK, BLK), lambda n: (0, 0, 0)),
         pl.BlockSpec((BLK, D_ATTN), lambda n: (0, 0))],
        [pl.BlockSpec((BLK, D_IN), lambda n: (jnp.maximum(n - 1, 0), 0)),
         pl.BlockSpec((1, BLK), lambda n: (0, 0)),
         pl.BlockSpec((1, BLK), lambda n: (0, 0)),
         pl.BlockSpec((1, BLK), lambda n: (0, 0)),
         pl.BlockSpec((N_HEADS, BLK, BLK), lambda n: (0, 0, 0)),
         pl.BlockSpec((N_HEADS, BLK), lambda n: (0, 0))],
        [jax.ShapeDtypeStruct((s, D_IN), BF16),
         jax.ShapeDtypeStruct((1, BLK), F32),
         jax.ShapeDtypeStruct((1, BLK), F32),
         jax.ShapeDtypeStruct((1, BLK), F32),
         jax.ShapeDtypeStruct((N_HEADS, BLK, BLK), F32),
         jax.ShapeDtypeStruct((N_HEADS, BLK), F32)],
        [pltpu.VMEM((BLK, D_IN), F32), tile_f32, tile_f32, tile_f32, pltpu.VMEM((BLK, D_ATTN), F32),
         pltpu.VMEM((N_HEADS * BLK, BLK), BF16), pltpu.VMEM((N_HEADS // 2, BLK, 2 * BLK), BF16),
         pltpu.VMEM((N_HEADS * BLK, 2 * BLK), F32)],
        [sinks, slopes, proj, proj, dmix, qw, kw, w_s, bmap], push)


def _dh_norm_bwd(dproj, w_t, x, dx_out, g, bf16_copy, push=None):
    s = x.shape[0]
    tm = _tile(TM_RESIDENT, s)

    def compute(dp_ref, w_ref, x_ref, dxo_ref, g_ref, dx_ref, dg_ref, *dxb_ref):
        @pl.when(pl.program_id(0) == 0)
        def _():
            dg_ref[...] = jnp.zeros_like(dg_ref)

        dh = jnp.dot(dp_ref[...], w_ref[...], preferred_element_type=F32)
        xf = x_ref[...]
        r = lax.rsqrt(jnp.mean(xf * xf, axis=-1, keepdims=True) + RMS_EPS)
        yhat = xf * r
        dyh = dh * g_ref[...]
        c = jnp.mean(dyh * yhat, axis=-1, keepdims=True)
        dx = dxo_ref[...] + r * (dyh - yhat * c)
        dx_ref[...] = dx
        if bf16_copy:
            dxb_ref[0][...] = dx.astype(BF16)
        dg_ref[...] += jnp.sum(dh * yhat, axis=0, keepdims=True)

    tok = pl.BlockSpec((tm, D_MODEL), lambda i: (i, 0))
    return _call(
        "dh_norm_bwd", compute, (s // tm,),
        [pl.BlockSpec((tm, D_IN), lambda i: (i, 0)), _resident((D_IN, D_MODEL)), tok, tok,
         pl.BlockSpec((1, D_MODEL), lambda i: (0, 0))],
        [tok, pl.BlockSpec((1, D_MODEL), lambda i: (0, 0))] + [tok] * bf16_copy,
        [jax.ShapeDtypeStruct((s, D_MODEL), F32), jax.ShapeDtypeStruct((1, D_MODEL), F32)]
        + [jax.ShapeDtypeStruct((s, D_MODEL), BF16)] * bf16_copy,
        [], [dproj, w_t, x, dx_out, g], push)


def _dw_in(dproj, h, push=None):
    s = h.shape[0]
    tk = _tile(DW_IN_TOK, s)
    nk = s // tk

    def compute(dp_ref, h_ref, o_ref, acc):
        k = pl.program_id(1)

        @pl.when(k == 0)
        def _():
            acc[...] = jnp.zeros_like(acc)

        acc[...] += lax.dot_general(dp_ref[...], h_ref[...], TN_DIMS, preferred_element_type=F32)

        @pl.when(k == nk - 1)
        def _():
            o_ref[...] = acc[...].astype(BF16)

    return _call(
        "dw_in", compute, (D_IN // DW_IN_ROWS, nk),
        [pl.BlockSpec((tk, DW_IN_ROWS), lambda j, k: (k, j)), pl.BlockSpec((tk, D_MODEL), lambda j, k: (k, 0))],
        [pl.BlockSpec((DW_IN_ROWS, D_MODEL), lambda j, k: (j, 0))],
        [jax.ShapeDtypeStruct((D_IN, D_MODEL), BF16)],
        [pltpu.VMEM((DW_IN_ROWS, D_MODEL), F32)], [dproj, h], push)


def _adamw_math(w, g, m, v):
    m_new = ADAM_B1 * m + (1.0 - ADAM_B1) * g
    v_new = ADAM_B2 * v + (1.0 - ADAM_B2) * jnp.square(g)
    m_hat = m_new / (1.0 - ADAM_B1 ** ADAM_STEP)
    v_hat = v_new / (1.0 - ADAM_B2 ** ADAM_STEP)
    return -ADAM_LR * (m_hat / (jnp.sqrt(v_hat) + ADAM_EPS) + ADAM_WD * w), m_new, v_new


def _adamw(name, w, g, m, v, tr=None):
    shape = w.shape
    c = shape[-1]
    flat = [a.reshape(-1, c) for a in (w, g, m, v)]
    r = flat[0].shape[0]
    tr = r if tr is None else tr

    def compute(w_ref, g_ref, m_ref, v_ref, d_ref, mo_ref, vo_ref):
        d_ref[...], mo_ref[...], vo_ref[...] = _adamw_math(w_ref[...], g_ref[...], m_ref[...], v_ref[...])

    spec = pl.BlockSpec((tr, c), lambda i: (i, 0))
    outs = _call(name, compute, (r // tr,), [spec] * 4, [spec] * 3, [jax.ShapeDtypeStruct((r, c), F32)] * 3,
                 [], flat)
    return [o.reshape(shape) for o in outs]


def _adamw_from_slots(name, slots, w, m, v, tr):
    n_layers, r, c = w.shape
    nt = r // tr

    def compute(*refs):
        s_refs, (w_ref, m_ref, v_ref), (g_ref, d_ref, mo_ref, vo_ref) = refs[:n_layers], refs[n_layers:n_layers + 3], \
            refs[n_layers + 3:]
        for k in range(n_layers):
            @pl.when(pl.program_id(0) == k)
            def _(k=k):
                g = s_refs[k][0].astype(F32)
                for dev in range(1, N_DEV):
                    g = g + s_refs[k][dev].astype(F32)
                g_ref[0] = g
                d_ref[0], mo_ref[0], vo_ref[0] = _adamw_math(w_ref[0], g, m_ref[0], v_ref[0])

    def slots_spec(k):
        return pl.BlockSpec((N_DEV, tr, c),
                            lambda l, i: (0, jnp.where(l == k, i, jnp.where(l < k, 0, nt - 1)), 0))

    tile = pl.BlockSpec((1, tr, c), lambda l, i: (l, i, 0))
    return _call(name, compute, (n_layers, nt), [slots_spec(k) for k in range(n_layers)] + [tile] * 3, [tile] * 4,
                 [jax.ShapeDtypeStruct((n_layers, r, c), F32)] * 4, [], list(slots) + [w, m, v])


def _pack_rows(parts):
    rows = []
    for a in parts:
        flat = a.reshape(-1)
        n = -(-flat.shape[0] // (8 * BLK)) * 8
        rows.append(jnp.pad(flat, (0, n * BLK - flat.shape[0])).reshape(n, BLK))
    return jnp.concatenate(rows, axis=0)


def _unpack_rows(packed, like):
    out = []
    row = 0
    for a in like:
        n = -(-a.size // (8 * BLK)) * 8
        out.append(packed[row:row + n].reshape(-1)[:a.size].reshape(a.shape))
        row += n
    return out


def kernel(x, norm_g, w_in, q_norm, k_norm, sinks, w_s, b_s, w_out, loss_target, m_norm_g, m_w_in, m_q_norm, m_k_norm, m_sinks, m_w_s, m_b_s, m_w_out, v_norm_g, v_w_in, v_q_norm, v_k_norm, v_sinks, v_w_s, v_b_s, v_w_out):
    xs = x[0]
    tgt = loss_target[0]
    slopes = jnp.asarray(2.0 ** (-8.0 * np.arange(1, N_HEADS + 1) / N_HEADS), dtype=F32)
    wt_sh = jnp.swapaxes(w_in, 1, 2).astype(BF16)
    wo_sh = w_out.astype(BF16)

    layer_par = []
    for l in range(DEPTH):
        layer_par.append((jnp.tile(q_norm[l], 2)[None, :], jnp.tile(k_norm[l], 2)[None, :],
                          jnp.repeat(b_s[l].T, HALF, axis=1)))

    wt_full = _exchange("gather_w0", _gather_rows((W_IN_SHARD,), (0,)), [wt_sh], [W_FULL[W_IN_SHARD]])[0]
    saved = []
    cur = xs
    for l in range(DEPTH):
        qw, kw, bmap = layer_par[l]
        more = l + 1 < DEPTH
        if more:
            h, proj, wo_full, wt_next = _norm_proj(
                cur, norm_g[l][None, :], wt_full,
                (_gather_rows((W_OUT_SHARD, W_IN_SHARD), (l, l + 1)), [wo_sh, wt_sh],
                 [W_FULL[W_OUT_SHARD], W_FULL[W_IN_SHARD]]))
        else:
            h, proj, wo_full = _norm_proj(cur, norm_g[l][None, :], wt_full,
                                          (_gather_rows((W_OUT_SHARD,), (l,)), [wo_sh], [W_FULL[W_OUT_SHARD]]))
        mix = _mixer_fwd(proj, qw, kw, sinks[l], slopes, w_s[l], bmap)[0]
        saved.append((cur, h, proj, mix, wt_full, wo_full))
        if more:
            cur = _out_proj(mix, wo_full, cur)
            wt_full = wt_next
        else:
            dx, dx_b, sq = _out_proj_loss(mix, wo_full, cur, tgt)
    loss = lax.psum(0.5 * jnp.sum(sq) / D_MODEL, MESH_AXES)

    wt_slots, wo_slots, g_small, g_norm = ([None] * DEPTH for _ in range(4))
    wt_slots_shape = _slots_shape(W_IN_SHARD, D_MODEL, BF16)
    dwt_waiting = None
    for l in reversed(range(DEPTH)):
        x_l, h, proj, mix, wt_l, wo_l = saved[l]
        qw, kw, bmap = layer_par[l]
        dmix = _dmix(dx_b, wo_l)
        dwo_part = _dw_out(mix, dx_b)
        res = _mixer_bwd(proj, dmix, qw, kw, sinks[l], slopes, w_s[l], bmap,
                         None if dwt_waiting is None else
                         (_scatter_rows((W_IN_SHARD,)), [dwt_waiting], [wt_slots_shape]))
        dproj, dqw, dkw, dsk, dws, dbs = res[:6]
        if dwt_waiting is not None:
            wt_slots[l + 1] = res[6]
        small_like = [dqw[0, :HALF], dkw[0, :HALF], dsk[0, :N_HEADS], dws, dbs]
        packed = _pack_rows(small_like)
        dwt_waiting, small_slots, wo_slots[l] = _dw_in(
            dproj, h, (_gather_slots_and_scatter_rows(W_OUT_SHARD), [packed, dwo_part],
                       [_slots_shape(*packed.shape, F32), _slots_shape(W_OUT_SHARD, D_MODEL, BF16)]))
        if l > 0:
            dx, dng, dx_b = _dh_norm_bwd(dproj, wt_l, x_l, dx, norm_g[l][None, :], True)
        else:
            dx, dng, wt_slots[0] = _dh_norm_bwd(dproj, wt_l, x_l, dx, norm_g[l][None, :], False,
                                                (_scatter_rows((W_IN_SHARD,)), [dwt_waiting], [wt_slots_shape]))
        g_small[l] = _unpack_rows(_sum_slots(small_slots, packed.shape[0]), small_like)
        g_norm[l] = dng[0]

    dng_all = _pack_rows([jnp.stack(g_norm)])
    dng_slots = _exchange("gather_dnorm", _gather_slots(), [dng_all], [_slots_shape(*dng_all.shape, F32)])[0]
    gr_norm = _unpack_rows(_sum_slots(dng_slots, dng_all.shape[0]), [norm_g])[0]
    gr_qn, gr_kn, gr_sk, gr_ws, gr_bs = (jnp.stack([g_small[l][i] for l in range(DEPTH)]) for i in range(5))

    def t(a):
        return jnp.swapaxes(a, 1, 2)

    from_slots = {1: _adamw_from_slots("adamw_w_in", wt_slots, t(w_in), t(m_w_in), t(v_w_in), ADAMW_ROWS),
                  7: _adamw_from_slots("adamw_w_out", wo_slots, w_out, m_w_out, v_w_out, ADAMW_ROWS)}
    from_slots[1] = [t(a) for a in from_slots[1]]

    grads = [gr_norm, None, gr_qn, gr_kn, gr_sk, gr_ws, gr_bs, None]
    weights = [norm_g, w_in, q_norm, k_norm, sinks, w_s, b_s, w_out]
    moms = [m_norm_g, m_w_in, m_q_norm, m_k_norm, m_sinks, m_w_s, m_b_s, m_w_out]
    vels = [v_norm_g, v_w_in, v_q_norm, v_k_norm, v_sinks, v_w_s, v_b_s, v_w_out]
    tiles = [None, None, None, None, None, 1024, None, None]
    names = ["norm_g", "w_in", "q_norm", "k_norm", "sinks", "w_s", "b_s", "w_out"]
    deltas, new_m, new_v = [], [], []
    for i, (nm, w, g, m, v, tr) in enumerate(zip(names, weights, grads, moms, vels, tiles)):
        if i in from_slots:
            grads[i], d, mo, vo = from_slots[i]
        else:
            d, mo, vo = _adamw("adamw_" + nm, w, g, m, v, tr)
        deltas.append(d)
        new_m.append(mo)
        new_v.append(vo)

    return (loss, dx[None], *grads, *deltas, *new_m, *new_v)
```

```python
import numpy as np
import jax
import jax.numpy as jnp
from jax import lax
from jax.experimental import pallas as pl
from jax.experimental.pallas import tpu as pltpu

F32 = jnp.float32
BF16 = jnp.bfloat16

D_MODEL = 2048
D_ATTN = 1024
D_IN = 5632
N_HEADS = 16
DEPTH = 4
BLK = 128
HALF = 64
RMS_EPS = 1e-6
C_Q, C_K, C_V, C_GA, C_ZU, C_ZV, C_GB = 0, 1024, 1280, 1536, 2560, 3584, 4608
NEG = -1e30
N_DEV = 8
W_IN_SHARD = D_IN // N_DEV
W_OUT_SHARD = D_MODEL // N_DEV
INV_SQRT2 = 0.7071067811865476
INV_SQRT_2PI = 0.3989422804014327

TM_RESIDENT = 256
TM_STREAM = 512
TM_WIDE = 1024
TN_PROJ = 512
DW_IN_ROWS = D_IN // 4
DW_IN_TOK = 2048
ADAMW_ROWS = 64
MID_STEP_16THS = 13

ADAM_LR = 0.001
ADAM_B1 = 0.9
ADAM_B2 = 0.999
ADAM_EPS = 1e-08
ADAM_WD = 0.01
ADAM_STEP = 10

NT_DIMS = (((1,), (1,)), ((), ()))
TN_DIMS = (((0,), (0,)), ((), ()))
MESH_AXES = ("x", "y", "c")


def _sigmoid(v):
    return 1.0 / (1.0 + jnp.exp(-v))


def _gelu_cdf(z):
    return 0.5 * (1.0 + lax.erf(z * INV_SQRT2))


def _gelu_grad(z, cdf):
    return cdf + z * (jnp.exp(-0.5 * z * z) * INV_SQRT_2PI)


def _lane_halves(rows):
    lane = lax.broadcasted_iota(jnp.int32, (rows, BLK), 1)
    return lane < HALF, lane >= HALF


def _half_masks():
    return {BLK: _lane_halves(BLK), 2 * BLK: _lane_halves(2 * BLK)}


def _half_sum(v, ones):
    h_a, h_b = ones[v.shape[0]]
    s_a = jnp.sum(jnp.where(h_a, v, 0.0), axis=-1, keepdims=True)
    s_b = jnp.sum(jnp.where(h_b, v, 0.0), axis=-1, keepdims=True)
    return jnp.where(h_a, s_a, s_b)


def _half_rms(v, w, ones):
    r = lax.rsqrt(_half_sum(v * v, ones) * (1.0 / HALF) + RMS_EPS)
    yhat = v * r
    return yhat, r, yhat * w


def _half_rms_bwd(dy, yhat, r, w, ones):
    dyh = dy * w
    c = _half_sum(dyh * yhat, ones) * (1.0 / HALF)
    return r * (dyh - yhat * c)


def _band_mask(n):
    t = lax.broadcasted_iota(jnp.int32, (BLK, 2 * BLK), 0)
    kk = lax.broadcasted_iota(jnp.int32, (BLK, 2 * BLK), 1)
    dist = t + BLK - kk
    first_key = jnp.where(n > 0, 0, BLK)
    ok = (dist >= 0) & (dist < BLK) & (kk >= first_key)
    return ok, dist.astype(F32)


def _alibi_table(n, slopes_ref, bias_scr):
    ok, distf = _band_mask(n)
    for hd in range(N_HEADS):
        bias_scr[BLK * hd:BLK * (hd + 1), :] = jnp.where(ok, -(slopes_ref[hd] * distf), NEG)


def _sink_col(sinks_ref, kt):
    return jnp.concatenate([jnp.full((BLK, 1), sinks_ref[8 * kt + i], F32) for i in range(8)], axis=0)


def _softmax_sink(s_scaled, bias, sink):
    s = s_scaled + bias
    m = jnp.maximum(jnp.max(s, axis=-1, keepdims=True), sink)
    p = jnp.exp(s - m)
    es = jnp.exp(sink - m)
    inv = 1.0 / (jnp.sum(p, axis=-1, keepdims=True) + es)
    return p * inv, es * inv


def _rows(i):
    return slice(BLK * i, BLK * (i + 1))


def _cols(base, j):
    return slice(base + BLK * j, base + BLK * (j + 1))


def _to_half(v, have, want):
    return v if have == want else pltpu.roll(v, HALF, 1)


def _tril_mask():
    row = lax.broadcasted_iota(jnp.int32, (BLK, BLK), 0)
    col = lax.broadcasted_iota(jnp.int32, (BLK, BLK), 1)
    return row >= col


def _tile(limit, s):
    t = min(limit, s)
    assert s % t == 0, (s, t)
    return t


def _mesh_place():
    x, y, c = lax.axis_index("x"), lax.axis_index("y"), lax.axis_index("c")
    return x, y, c, 4 * x + 2 * y + c


def _peer(x, y, c, k):
    px = 1 - x if k & 4 else x
    py = 1 - y if k & 2 else y
    pc = 1 - c if k & 1 else c
    return (px, py, pc), 4 * px + 2 * py + pc


class _Pushes:
    def __init__(self, n_arrays, src_view, dst_view):
        self.na = n_arrays
        self.src_view = src_view
        self.dst_view = dst_view

    def scratch(self):
        n = self.na * (N_DEV - 1)
        return [pltpu.SemaphoreType.DMA((n,)), pltpu.SemaphoreType.DMA((n,)), pltpu.SemaphoreType.DMA((self.na,))]

    def copies(self, src_refs, dst_refs, send_sems, recv_sems, local_sems):
        x, y, c, me = _mesh_place()
        cps = []
        for a in range(self.na):
            cps.append(pltpu.make_async_copy(self.src_view(a, src_refs[a], me), self.dst_view(a, dst_refs[a], me),
                                             local_sems.at[a]))
        for k in range(1, N_DEV):
            peer, pidx = _peer(x, y, c, k)
            for a in range(self.na):
                sem = a * (N_DEV - 1) + k - 1
                cps.append(pltpu.make_async_remote_copy(
                    src_ref=self.src_view(a, src_refs[a], pidx), dst_ref=self.dst_view(a, dst_refs[a], me),
                    send_sem=send_sems.at[sem], recv_sem=recv_sems.at[sem],
                    device_id=peer, device_id_type=pl.DeviceIdType.MESH))
        return cps

    def plan(self, src_refs, dst_refs, send_sems, recv_sems, local_sems):
        cps = self.copies(src_refs, dst_refs, send_sems, recv_sems, local_sems)
        return cps, [], [], [cp.wait for cp in cps]


class _TwoLevelGather:
    def __init__(self, shard_rows, layers):
        self.na = len(shard_rows)
        self.shard_rows = shard_rows
        self.layers = layers

    def scratch(self):
        n = self.na * (N_DEV - 1)
        return [pltpu.SemaphoreType.DMA((n,)), pltpu.SemaphoreType.DMA((n,)), pltpu.SemaphoreType.DMA((self.na,))]

    def plan(self, src_refs, dst_refs, send_sems, recv_sems, local_sems):
        x, y, c, _ = _mesh_place()
        sibling = (x, y, 1 - c)
        chips = [(1 - x, y), (x, 1 - y), (1 - x, 1 - y)]
        start, mid_wait, mid_start, final = [], [], [], []
        for a in range(self.na):
            r = self.shard_rows[a]
            src = src_refs[a].at[self.layers[a]]
            dst = dst_refs[a]

            def rows(px, py, pc, r=r, dst=dst):
                return dst.at[pl.ds(pl.multiple_of((4 * px + 2 * py + pc) * r, 64), r), :]

            def remote(k, s_ref, block, to, a=a, rows=rows):
                return pltpu.make_async_remote_copy(
                    src_ref=s_ref, dst_ref=rows(*block),
                    send_sem=send_sems.at[a * (N_DEV - 1) + k], recv_sem=recv_sems.at[a * (N_DEV - 1) + k],
                    device_id=to, device_id_type=pl.DeviceIdType.MESH)

            mine = pltpu.make_async_copy(src, rows(x, y, c), local_sems.at[a])
            own = [remote(0, src, (x, y, c), sibling)]
            own += [remote(1 + j, src, (x, y, c), (*chip, c)) for j, chip in enumerate(chips)]
            passed = [remote(4 + j, rows(*chip, c), (*chip, c), sibling) for j, chip in enumerate(chips)]
            start += [mine] + own
            mid_wait += own[1:]
            mid_start += passed
            final += [own[0].wait_recv] + [cp.wait_recv for cp in passed]
            final += [cp.wait_send for cp in own + passed] + [mine.wait]
        return start, mid_wait, mid_start, final


def _call(name, compute, grid, in_specs, out_specs, out_shape, scratch, args, push=None):
    sem = pltpu.CompilerParams(dimension_semantics=("arbitrary",) * len(grid))
    if push is None:
        return pl.pallas_call(compute, name=name, grid=grid, in_specs=in_specs, out_specs=out_specs,
                              out_shape=out_shape, scratch_shapes=scratch, compiler_params=sem)(*args)
    pushes, srcs, xshapes = push
    n_in, n_out, n_scr, na = len(args), len(out_shape), len(scratch), pushes.na
    hbm = pl.BlockSpec(memory_space=pltpu.HBM)

    def body(*refs):
        ins, refs = refs[:n_in], refs[n_in:]
        xin, refs = refs[:na], refs[na:]
        outs, refs = refs[:n_out], refs[n_out:]
        xout, refs = refs[:na], refs[na:]
        scr, sems = refs[:n_scr], refs[n_scr:]
        start, mid_wait, mid_start, final = pushes.plan(xin, xout, *sems)
        first = pl.program_id(0) == 0
        middle = pl.program_id(0) == (grid[0] * MID_STEP_16THS) // 16
        last = pl.program_id(0) == grid[0] - 1
        for d in range(1, len(grid)):
            first = first & (pl.program_id(d) == 0)
            middle = middle & (pl.program_id(d) == 0)
            last = last & (pl.program_id(d) == grid[d] - 1)

        @pl.when(first)
        def _():
            for cp in start:
                cp.start()

        if mid_start:
            @pl.when(middle)
            def _():
                for cp in mid_wait:
                    cp.wait_recv()
                for cp in mid_start:
                    cp.start()

        compute(*ins, *outs, *scr)

        @pl.when(last)
        def _():
            for wait in final:
                wait()

    return pl.pallas_call(
        body, name=name, grid=grid,
        in_specs=list(in_specs) + [hbm] * na, out_specs=list(out_specs) + [hbm] * na,
        out_shape=list(out_shape) + list(xshapes),
        scratch_shapes=list(scratch) + pushes.scratch(), compiler_params=sem)(*args, *srcs)


def _exchange(name, pushes, srcs, out_shapes):
    na = pushes.na
    hbm = pl.BlockSpec(memory_space=pltpu.HBM)

    def body(*refs):
        start, mid_wait, mid_start, final = pushes.plan(refs[:na], refs[na:2 * na], *refs[2 * na:])
        for cp in start:
            cp.start()
        for cp in mid_wait:
            cp.wait_recv()
        for cp in mid_start:
            cp.start()
        for wait in final:
            wait()

    return pl.pallas_call(body, name=name, in_specs=[hbm] * na, out_specs=[hbm] * na, out_shape=out_shapes,
                          scratch_shapes=pushes.scratch())(*srcs)


def _gather_rows(shard_rows, layers):
    return _TwoLevelGather(shard_rows, layers)


def _scatter_rows(shard_rows):
    def src_view(a, ref, idx):
        r = shard_rows[a]
        return ref.at[pl.ds(pl.multiple_of(idx * r, 64), r), :]

    def dst_view(a, ref, idx):
        return ref.at[idx]

    return _Pushes(len(shard_rows), src_view, dst_view)


def _gather_slots():
    return _Pushes(1, lambda a, ref, idx: ref, lambda a, ref, idx: ref.at[idx])


def _gather_slots_and_scatter_rows(rows):
    def src_view(a, ref, idx):
        return ref if a == 0 else ref.at[pl.ds(pl.multiple_of(idx * rows, 64), rows), :]

    return _Pushes(2, src_view, lambda a, ref, idx: ref.at[idx])


W_FULL = {W_IN_SHARD: jax.ShapeDtypeStruct((D_IN, D_MODEL), BF16),
          W_OUT_SHARD: jax.ShapeDtypeStruct((D_MODEL, D_MODEL), BF16)}


def _slots_shape(rows, cols, dtype):
    return jax.ShapeDtypeStruct((N_DEV, rows, cols), dtype)


def _sum_slots(slots, tr):
    _, r, c = slots.shape

    def compute(s_ref, o_ref):
        tot = s_ref[0].astype(F32)
        for d in range(1, N_DEV):
            tot = tot + s_ref[d].astype(F32)
        o_ref[...] = tot

    return _call("sum_slots", compute, (r // tr,),
                 [pl.BlockSpec((N_DEV, tr, c), lambda i: (0, i, 0))], [pl.BlockSpec((tr, c), lambda i: (i, 0))],
                 [jax.ShapeDtypeStruct((r, c), F32)], [], [slots])[0]


def _resident(shape):
    return pl.BlockSpec(shape, lambda *_: (0,) * len(shape), pipeline_mode=pl.Buffered(1))


def _norm_proj(x, g, w_t, push=None):
    s = x.shape[0]
    tm = _tile(TM_RESIDENT, s)

    def compute(x_ref, g_ref, w_ref, h_ref, p_ref):
        xf = x_ref[...]
        r = lax.rsqrt(jnp.mean(xf * xf, axis=-1, keepdims=True) + RMS_EPS)
        h = ((xf * r) * g_ref[...]).astype(BF16)
        h_ref[...] = h
        for j in range(D_IN // TN_PROJ):
            cols = slice(j * TN_PROJ, (j + 1) * TN_PROJ)
            p_ref[:, cols] = lax.dot_general(h, w_ref[cols, :], NT_DIMS, preferred_element_type=F32)

    return _call(
        "norm_proj", compute, (s // tm,),
        [pl.BlockSpec((tm, D_MODEL), lambda i: (i, 0)), pl.BlockSpec((1, D_MODEL), lambda i: (0, 0)),
         _resident((D_IN, D_MODEL))],
        [pl.BlockSpec((tm, D_MODEL), lambda i: (i, 0)), pl.BlockSpec((tm, D_IN), lambda i: (i, 0))],
        [jax.ShapeDtypeStruct((s, D_MODEL), BF16), jax.ShapeDtypeStruct((s, D_IN), F32)],
        [], [x, g, w_t], push)


def _sgu_weights(ws_ref, wtril_scr, wtril_t_scr=None):
    tril = _tril_mask()
    for hd in range(N_HEADS):
        w = jnp.where(tril, ws_ref[hd], 0.0)
        wtril_scr[BLK * hd:BLK * (hd + 1), :] = w.astype(BF16)
        if wtril_t_scr is not None:
            wtril_t_scr[hd // 2, :, BLK * (hd % 2):BLK * (hd % 2 + 1)] = w.T.astype(BF16)


def _kv_band(kt, p_ref, kvp_ref, kw_v, ones):
    kband = jnp.concatenate([kvp_ref[:, _cols(0, kt)], p_ref[:, _cols(C_K, kt)]], axis=0)
    kyhat, kr, kn = _half_rms(kband, kw_v, ones)
    vband = jnp.concatenate([kvp_ref[:, _cols(256, kt)], p_ref[:, _cols(C_V, kt)]], axis=0)
    return kyhat, kr, (kn * 0.125).astype(BF16), vband.astype(BF16)


def _stack_heads(tiles, halves):
    parts = []
    for tt, tile in enumerate(tiles):
        for qh in range(2):
            parts.append(_to_half(jnp.where(halves[qh], tile, 0.0), qh, tt // 2).astype(BF16))
    return jnp.concatenate(parts, axis=0)


def _unstack_heads(stacked, tt, h_a):
    return jnp.where(h_a, _to_half(stacked[_rows(2 * tt)], tt // 2, 0), _to_half(stacked[_rows(2 * tt + 1)], tt // 2, 1))


def _mixer_fwd(proj, qw, kw, sinks, slopes, w_s, bmap, push=None):
    s = proj.shape[0]
    nb = s // BLK

    def compute(sinks_ref, slopes_ref, p_ref, kvp_ref, qw_ref, kw_ref, ws_ref, bmap_ref, mix_ref,
                wtril_scr, bias_scr):
        _mixer_tables(pl.program_id(0), slopes_ref, ws_ref, wtril_scr, bias_scr)
        _mixer_fwd_block(sinks_ref, p_ref, kvp_ref, qw_ref, kw_ref, bmap_ref, mix_ref, wtril_scr, bias_scr)

    smem = pl.BlockSpec(memory_space=pltpu.SMEM)
    return _call(
        "mixer_fwd", compute, (nb,),
        [smem, smem,
         pl.BlockSpec((BLK, D_IN), lambda n: (n, 0)),
         pl.BlockSpec((BLK, 512), lambda n: (jnp.maximum(n - 1, 0), 2)),
         pl.BlockSpec((1, BLK), lambda n: (0, 0)),
         pl.BlockSpec((1, BLK), lambda n: (0, 0)),
         pl.BlockSpec((N_HEADS, BLK, BLK), lambda n: (0, 0, 0)),
         pl.BlockSpec((BLK, D_ATTN), lambda n: (0, 0))],
        [pl.BlockSpec((BLK, D_MODEL), lambda n: (n, 0))],
        [jax.ShapeDtypeStruct((s, D_MODEL), BF16)],
        [pltpu.VMEM((N_HEADS * BLK, BLK), BF16), pltpu.VMEM((N_HEADS * BLK, 2 * BLK), F32)],
        [sinks, slopes, proj, proj, qw, kw, w_s, bmap], push)


def _mixer_tables(n, slopes_ref, ws_ref, wtril_scr, bias_scr, wtril_t_scr=None):
    @pl.when(n == 0)
    def _():
        _sgu_weights(ws_ref, wtril_scr, wtril_t_scr)

    @pl.when(n <= 1)
    def _():
        _alibi_table(n, slopes_ref, bias_scr)


def _mixer_fwd_block(sinks_ref, p_ref, kvp_ref, qw_ref, kw_ref, bmap_ref, mix_ref, wtril_scr, bias_scr):
    h_a, h_b = _lane_halves(BLK)
    ones = _half_masks()
    halves = (h_a, h_b)
    qw_v = qw_ref[...]
    kw_v = kw_ref[...]
    bands = [_kv_band(kt, p_ref, kvp_ref, kw_v, ones) for kt in range(2)]
    sc = []
    for kt in range(2):
        qn = [_half_rms(p_ref[:, _cols(C_Q, 4 * kt + tt)], qw_v, ones)[2] for tt in range(4)]
        sc.append(lax.dot_general(_stack_heads(qn, halves), bands[kt][2], NT_DIMS, preferred_element_type=F32))
    zu, mixed = [], []
    for j in range(8):
        zu_pre = p_ref[:, _cols(C_ZU, j)]
        zv_pre = p_ref[:, _cols(C_ZV, j)]
        zu.append(zu_pre * _gelu_cdf(zu_pre))
        zvb = (zv_pre * _gelu_cdf(zv_pre)).astype(BF16)
        mixed.append(jnp.dot(wtril_scr[2 * BLK * j:2 * BLK * (j + 1), :], zvb, preferred_element_type=F32))
    o = []
    for kt in range(2):
        p, _ = _softmax_sink(sc[kt], bias_scr[8 * BLK * kt:8 * BLK * (kt + 1), :], _sink_col(sinks_ref, kt))
        o.append(jnp.dot(p.astype(BF16), bands[kt][3], preferred_element_type=F32))
    for j in range(8):
        gb = p_ref[:, _cols(C_GB, j)]
        mx = jnp.where(h_a, mixed[j][0:BLK], mixed[j][BLK:2 * BLK]) + bmap_ref[:, _cols(0, j)]
        mix_ref[:, _cols(D_ATTN, j)] = ((zu[j] * mx) * (gb * _sigmoid(gb))).astype(BF16)
    for kt in range(2):
        for tt in range(4):
            j = 4 * kt + tt
            ga = p_ref[:, _cols(C_GA, j)]
            mix_ref[:, _cols(0, j)] = (_unstack_heads(o[kt], tt, h_a) * (ga * _sigmoid(ga))).astype(BF16)


def _out_proj(mix, w_o, x):
    s = x.shape[0]
    tm = _tile(TM_WIDE, s)

    def compute(m_ref, w_ref, x_ref, o_ref):
        o_ref[...] = x_ref[...] + jnp.dot(m_ref[...], w_ref[...], preferred_element_type=F32)

    return _call(
        "out_proj", compute, (s // tm,),
        [pl.BlockSpec((tm, D_MODEL), lambda i: (i, 0)), _resident((D_MODEL, D_MODEL)),
         pl.BlockSpec((tm, D_MODEL), lambda i: (i, 0))],
        [pl.BlockSpec((tm, D_MODEL), lambda i: (i, 0))],
        [jax.ShapeDtypeStruct((s, D_MODEL), F32)], [], [mix, w_o, x])[0]


def _out_proj_loss(mix, w_o, x, tgt):
    s = x.shape[0]
    tm = _tile(TM_STREAM, s)

    def compute(m_ref, w_ref, x_ref, t_ref, dy_ref, dyb_ref, sq_ref):
        @pl.when(pl.program_id(0) == 0)
        def _():
            sq_ref[...] = jnp.zeros_like(sq_ref)

        y = x_ref[...] + jnp.dot(m_ref[...], w_ref[...], preferred_element_type=F32)
        e = y - t_ref[...]
        dy = e * (1.0 / D_MODEL)
        dy_ref[...] = dy
        dyb_ref[...] = dy.astype(BF16)
        sq_ref[...] += jnp.sum(e * e, axis=0, keepdims=True)

    tok = pl.BlockSpec((tm, D_MODEL), lambda i: (i, 0))
    return _call(
        "out_proj_loss", compute, (s // tm,),
        [tok, _resident((D_MODEL, D_MODEL)), tok, tok],
        [tok, tok, pl.BlockSpec((1, D_MODEL), lambda i: (0, 0))],
        [jax.ShapeDtypeStruct((s, D_MODEL), F32), jax.ShapeDtypeStruct((s, D_MODEL), BF16),
         jax.ShapeDtypeStruct((1, D_MODEL), F32)], [], [mix, w_o, x, tgt])


def _dmix(dx, w_o):
    s = dx.shape[0]
    tm = _tile(TM_WIDE, s)

    def compute(d_ref, w_ref, o_ref):
        o_ref[...] = lax.dot_general(d_ref[...], w_ref[...], NT_DIMS, preferred_element_type=F32)

    return _call(
        "dmix", compute, (s // tm,),
        [pl.BlockSpec((tm, D_MODEL), lambda i: (i, 0)), _resident((D_MODEL, D_MODEL))],
        [pl.BlockSpec((tm, D_MODEL), lambda i: (i, 0))],
        [jax.ShapeDtypeStruct((s, D_MODEL), F32)], [], [dx, w_o])[0]


def _dw_out(mix, dx):
    s = dx.shape[0]
    tk = _tile(TM_WIDE, s)
    nk = s // tk

    def compute(m_ref, d_ref, o_ref, acc):
        k = pl.program_id(0)

        @pl.when(k == 0)
        def _():
            acc[...] = jnp.zeros_like(acc)

        acc[...] += lax.dot_general(m_ref[...], d_ref[...], TN_DIMS, preferred_element_type=F32)

        @pl.when(k == nk - 1)
        def _():
            o_ref[...] = acc[...].astype(BF16)

    return _call(
        "dw_out", compute, (nk,),
        [pl.BlockSpec((tk, D_MODEL), lambda k: (k, 0)), pl.BlockSpec((tk, D_MODEL), lambda k: (k, 0))],
        [pl.BlockSpec((D_MODEL, D_MODEL), lambda k: (0, 0))],
        [jax.ShapeDtypeStruct((D_MODEL, D_MODEL), BF16)],
        [pltpu.VMEM((D_MODEL, D_MODEL), F32)], [mix, dx])[0]


def _mixer_bwd(proj, dmix, qw, kw, sinks, slopes, w_s, bmap, push=None):
    s = proj.shape[0]
    nb = s // BLK

    def compute(sinks_ref, slopes_ref, p_ref, kvp_ref, dm_ref, qw_ref, kw_ref, ws_ref, bmap_ref,
                dp_ref, dqw_ref, dkw_ref, dsk_ref, dws_ref, dbs_ref,
                pend, accq, acck, accs, accb, wtril_scr, wtril_t_scr, bias_scr):
        n = pl.program_id(0)
        h_a, h_b = _lane_halves(BLK)
        ones = _half_masks()
        halves = (h_a, h_b)
        lane = lax.broadcasted_iota(jnp.int32, (BLK, BLK), 1)

        @pl.when(n == 0)
        def _():
            accq[...] = jnp.zeros_like(accq)
            acck[...] = jnp.zeros_like(acck)
            accs[...] = jnp.zeros_like(accs)
            accb[...] = jnp.zeros_like(accb)
            dws_ref[...] = jnp.zeros_like(dws_ref)
            pend[...] = jnp.zeros_like(pend)
            _sgu_weights(ws_ref, wtril_scr, wtril_t_scr)

        @pl.when(n <= 1)
        def _():
            _alibi_table(n, slopes_ref, bias_scr)

        def emit_pending():
            dp_ref[:, 0:C_K] = pend[:, 0:C_K].astype(BF16)
            dp_ref[:, C_GA:D_IN] = pend[:, C_GA:D_IN].astype(BF16)

        @pl.when(n < nb)
        def _():
            emit_pending()
            qw_v = qw_ref[...]
            kw_v = kw_ref[...]
            bands = [_kv_band(kt, p_ref, kvp_ref, kw_v, ones) for kt in range(2)]
            tiles, qst, dost, sc, dpm = [], [], [], [], []
            for kt in range(2):
                qn, d_o, tl = [], [], []
                for tt in range(4):
                    j = 4 * kt + tt
                    qyhat, qr, qn_t = _half_rms(p_ref[:, _cols(C_Q, j)], qw_v, ones)
                    ga = p_ref[:, _cols(C_GA, j)]
                    sg = _sigmoid(ga)
                    dma = dm_ref[:, _cols(0, j)]
                    qn.append(qn_t)
                    d_o.append(dma * (ga * sg))
                    tl.append((qyhat, qr, dma * (sg * (1.0 + ga * (1.0 - sg)))))
                tiles.append(tl)
                qst.append(_stack_heads(qn, halves))
                dost.append(_stack_heads(d_o, halves))
                sc.append(lax.dot_general(qst[kt], bands[kt][2], NT_DIMS, preferred_element_type=F32))
                dpm.append(lax.dot_general(dost[kt], bands[kt][3], NT_DIMS, preferred_element_type=F32))
            sgu = []
            for j in range(8):
                zu_pre = p_ref[:, _cols(C_ZU, j)]
                zv_pre = p_ref[:, _cols(C_ZV, j)]
                cu = _gelu_cdf(zu_pre)
                cv = _gelu_cdf(zv_pre)
                zvb = (zv_pre * cv).astype(BF16)
                sgu.append((zu_pre * cu, zvb, _gelu_grad(zu_pre, cu), _gelu_grad(zv_pre, cv),
                            jnp.dot(wtril_scr[2 * BLK * j:2 * BLK * (j + 1), :], zvb, preferred_element_type=F32)))
            dsink = jnp.zeros((BLK, BLK), F32)
            pst, dqkst = [], []
            for kt in range(2):
                p, p_sink = _softmax_sink(sc[kt], bias_scr[8 * BLK * kt:8 * BLK * (kt + 1), :],
                                          _sink_col(sinks_ref, kt))
                dsum = jnp.sum(p * dpm[kt], axis=-1, keepdims=True)
                dsink_col = -(p_sink * dsum)
                for i in range(8):
                    dsink = dsink + jnp.where(lane == 8 * kt + i, dsink_col[_rows(i)], 0.0)
                pst.append(p.astype(BF16))
                dqkst.append((p * (dpm[kt] - dsum)).astype(BF16))
            o, dqn_all, dvb, dkn = [], [], [], []
            for kt in range(2):
                o.append(jnp.dot(pst[kt], bands[kt][3], preferred_element_type=F32))
                dqn_all.append(jnp.dot(dqkst[kt], bands[kt][2], preferred_element_type=F32))
                dvb.append(lax.dot_general(pst[kt], dost[kt], TN_DIMS, preferred_element_type=F32))
                dkn.append(0.125 * lax.dot_general(dqkst[kt], qst[kt], TN_DIMS, preferred_element_type=F32))
            dms = []
            for j in range(8):
                zu, zvb, gu, gv, m_ab = sgu[j]
                gb = p_ref[:, _cols(C_GB, j)]
                dmb = dm_ref[:, _cols(D_ATTN, j)]
                mixed = jnp.where(h_a, m_ab[0:BLK], m_ab[BLK:2 * BLK]) + bmap_ref[:, _cols(0, j)]
                sgb = _sigmoid(gb)
                dgate = dmb * (gb * sgb)
                pend[:, _cols(C_ZU, j)] = (dgate * mixed) * gu
                pend[:, _cols(C_GB, j)] = (dmb * (zu * mixed)) * (sgb * (1.0 + gb * (1.0 - sgb)))
                dmixed = dgate * zu
                accb[:, _cols(0, j)] += dmixed
                dms.append(jnp.concatenate([jnp.where(h_a, dmixed, 0.0).astype(BF16),
                                            jnp.where(h_b, dmixed, 0.0).astype(BF16)], axis=0))
            dzv = []
            for j in range(8):
                dzv.append(jnp.dot(wtril_t_scr[j], dms[j], preferred_element_type=F32))
                dw_ab = lax.dot_general(dms[j], sgu[j][1], NT_DIMS, preferred_element_type=F32)
                dws_ref[2 * j] += dw_ab[0:BLK]
                dws_ref[2 * j + 1] += dw_ab[BLK:2 * BLK]
            dq_w = jnp.zeros((BLK, BLK), F32)
            for kt in range(2):
                for tt in range(4):
                    j = 4 * kt + tt
                    qyhat, qr, dsilu = tiles[kt][tt]
                    dqn = _unstack_heads(dqn_all[kt], tt, h_a)
                    pend[:, _cols(C_GA, j)] = _unstack_heads(o[kt], tt, h_a) * dsilu
                    pend[:, _cols(C_Q, j)] = _half_rms_bwd(dqn, qyhat, qr, qw_v, ones)
                    dq_w = dq_w + dqn * qyhat
            dk_w = jnp.zeros((BLK, BLK), F32)
            for kt in range(2):
                kyhat, kr = bands[kt][0], bands[kt][1]
                dk = _half_rms_bwd(dkn[kt], kyhat, kr, kw_v, ones)
                dkw_part = dkn[kt] * kyhat
                dk_w = dk_w + (dkw_part[0:BLK] + dkw_part[BLK:2 * BLK])
                dv = dvb[kt]
                dp_ref[:, _cols(C_K, kt)] = (pend[:, _cols(C_K, kt)] + dk[0:BLK]).astype(BF16)
                dp_ref[:, _cols(C_V, kt)] = (pend[:, _cols(C_V, kt)] + dv[0:BLK]).astype(BF16)
                pend[:, _cols(C_K, kt)] = dk[BLK:2 * BLK]
                pend[:, _cols(C_V, kt)] = dv[BLK:2 * BLK]
            accq[...] += dq_w
            acck[...] += dk_w
            accs[...] += dsink
            for j in range(8):
                pend[:, _cols(C_ZV, j)] = dzv[j] * sgu[j][3]

        @pl.when(n == nb)
        def _():
            tril = _tril_mask()
            emit_pending()
            dp_ref[:, C_K:C_GA] = pend[:, C_K:C_GA].astype(BF16)
            aq = accq[...]
            ak = acck[...]
            dqw_ref[...] = jnp.sum(aq + pltpu.roll(aq, HALF, 1), axis=0, keepdims=True)
            dkw_ref[...] = jnp.sum(ak + pltpu.roll(ak, HALF, 1), axis=0, keepdims=True)
            dsk_ref[...] = jnp.sum(accs[...], axis=0, keepdims=True)
            for hd in range(N_HEADS):
                dws_ref[hd] = jnp.where(tril, dws_ref[hd], 0.0)
            hrow = lax.broadcasted_iota(jnp.int32, (N_HEADS, D_ATTN), 0)
            hcol = lax.broadcasted_iota(jnp.int32, (N_HEADS, D_ATTN), 1)
            sel = jnp.where((hcol >= hrow * HALF) & (hcol < (hrow + 1) * HALF), 1.0, 0.0).astype(BF16)
            rem = accb[...]
            tot = jnp.zeros((N_HEADS, BLK), F32)
            for _ in range(3):
                part = rem.astype(BF16)
                tot = tot + lax.dot_general(sel, part, NT_DIMS, preferred_element_type=F32)
                rem = rem - part.astype(F32)
            dbs_ref[...] = tot

    smem = pl.BlockSpec(memory_space=pltpu.SMEM)
    last = nb - 1
    tile_f32 = pltpu.VMEM((BLK, BLK), F32)
    return _call(
        "mixer_bwd", compute, (nb + 1,),
        [smem, smem,
         pl.BlockSpec((BLK, D_IN), lambda n: (jnp.minimum(n, last), 0)),
         pl.BlockSpec((BLK, 512), lambda n: (jnp.maximum(jnp.minimum(n, last) - 1, 0), 2)),
         pl.BlockSpec((BLK, D_MODEL), lambda n: (jnp.minimum(n, last), 0)),
         pl.BlockSpec((1, BLK), lambda n: (0, 0)),
         pl.BlockSpec((1, BLK), lambda n: (0, 0)),
         pl.BlockSpec((N_HEADS, BLK, BLK), lambda n: (0, 0, 0)),
         pl.BlockSpec((BLK, D_ATTN), lambda n: (0, 0))],
        [pl.BlockSpec((BLK, D_IN), lambda n: (jnp.maximum(n - 1, 0), 0)),
         pl.BlockSpec((1, BLK), lambda n: (0, 0)),
         pl.BlockSpec((1, BLK), lambda n: (0, 0)),
         pl.BlockSpec((1, BLK), lambda n: (0, 0)),
         pl.BlockSpec((N_HEADS, BLK, BLK), lambda n: (0, 0, 0)),
         pl.BlockSpec((N_HEADS, BLK), lambda n: (0, 0))],
        [jax.ShapeDtypeStruct((s, D_IN), BF16),
         jax.ShapeDtypeStruct((1, BLK), F32),
         jax.ShapeDtypeStruct((1, BLK), F32),
         jax.ShapeDtypeStruct((1, BLK), F32),
         jax.ShapeDtypeStruct((N_HEADS, BLK, BLK), F32),
         jax.ShapeDtypeStruct((N_HEADS, BLK), F32)],
        [pltpu.VMEM((BLK, D_IN), F32), tile_f32, tile_f32, tile_f32, pltpu.VMEM((BLK, D_ATTN), F32),
         pltpu.VMEM((N_HEADS * BLK, BLK), BF16), pltpu.VMEM((N_HEADS // 2, BLK, 2 * BLK), BF16),
         pltpu.VMEM((N_HEADS * BLK, 2 * BLK), F32)],
        [sinks, slopes, proj, proj, dmix, qw, kw, w_s, bmap], push)


def _dh_norm_bwd(dproj, w_t, x, dx_out, g, bf16_copy, push=None):
    s = x.shape[0]
    tm = _tile(TM_RESIDENT, s)

    def compute(dp_ref, w_ref, x_ref, dxo_ref, g_ref, dx_ref, dg_ref, *dxb_ref):
        @pl.when(pl.program_id(0) == 0)
        def _():
            dg_ref[...] = jnp.zeros_like(dg_ref)

        dh = jnp.dot(dp_ref[...], w_ref[...], preferred_element_type=F32)
        xf = x_ref[...]
        r = lax.rsqrt(jnp.mean(xf * xf, axis=-1, keepdims=True) + RMS_EPS)
        yhat = xf * r
        dyh = dh * g_ref[...]
        c = jnp.mean(dyh * yhat, axis=-1, keepdims=True)
        dx = dxo_ref[...] + r * (dyh - yhat * c)
        dx_ref[...] = dx
        if bf16_copy:
            dxb_ref[0][...] = dx.astype(BF16)
        dg_ref[...] += jnp.sum(dh * yhat, axis=0, keepdims=True)

    tok = pl.BlockSpec((tm, D_MODEL), lambda i: (i, 0))
    return _call(
        "dh_norm_bwd", compute, (s // tm,),
        [pl.BlockSpec((tm, D_IN), lambda i: (i, 0)), _resident((D_IN, D_MODEL)), tok, tok,
         pl.BlockSpec((1, D_MODEL), lambda i: (0, 0))],
        [tok, pl.BlockSpec((1, D_MODEL), lambda i: (0, 0))] + [tok] * bf16_copy,
        [jax.ShapeDtypeStruct((s, D_MODEL), F32), jax.ShapeDtypeStruct((1, D_MODEL), F32)]
        + [jax.ShapeDtypeStruct((s, D_MODEL), BF16)] * bf16_copy,
        [], [dproj, w_t, x, dx_out, g], push)


def _dw_in(dproj, h, push=None):
    s = h.shape[0]
    tk = _tile(DW_IN_TOK, s)
    nk = s // tk

    def compute(dp_ref, h_ref, o_ref, acc):
        k = pl.program_id(1)

        @pl.when(k == 0)
        def _():
            acc[...] = jnp.zeros_like(acc)

        acc[...] += lax.dot_general(dp_ref[...], h_ref[...], TN_DIMS, preferred_element_type=F32)

        @pl.when(k == nk - 1)
        def _():
            o_ref[...] = acc[...].astype(BF16)

    return _call(
        "dw_in", compute, (D_IN // DW_IN_ROWS, nk),
        [pl.BlockSpec((tk, DW_IN_ROWS), lambda j, k: (k, j)), pl.BlockSpec((tk, D_MODEL), lambda j, k: (k, 0))],
        [pl.BlockSpec((DW_IN_ROWS, D_MODEL), lambda j, k: (j, 0))],
        [jax.ShapeDtypeStruct((D_IN, D_MODEL), BF16)],
        [pltpu.VMEM((DW_IN_ROWS, D_MODEL), F32)], [dproj, h], push)


def _adamw_math(w, g, m, v):
    m_new = ADAM_B1 * m + (1.0 - ADAM_B1) * g
    v_new = ADAM_B2 * v + (1.0 - ADAM_B2) * jnp.square(g)
    m_hat = m_new / (1.0 - ADAM_B1 ** ADAM_STEP)
    v_hat = v_new / (1.0 - ADAM_B2 ** ADAM_STEP)
    return -ADAM_LR * (m_hat / (jnp.sqrt(v_hat) + ADAM_EPS) + ADAM_WD * w), m_new, v_new


def _adamw(name, w, g, m, v, tr=None):
    shape = w.shape
    c = shape[-1]
    flat = [a.reshape(-1, c) for a in (w, g, m, v)]
    r = flat[0].shape[0]
    tr = r if tr is None else tr

    def compute(w_ref, g_ref, m_ref, v_ref, d_ref, mo_ref, vo_ref):
        d_ref[...], mo_ref[...], vo_ref[...] = _adamw_math(w_ref[...], g_ref[...], m_ref[...], v_ref[...])

    spec = pl.BlockSpec((tr, c), lambda i: (i, 0))
    outs = _call(name, compute, (r // tr,), [spec] * 4, [spec] * 3, [jax.ShapeDtypeStruct((r, c), F32)] * 3,
                 [], flat)
    return [o.reshape(shape) for o in outs]


def _adamw_from_slots(name, slots, w, m, v, tr):
    n_layers, r, c = w.shape
    nt = r // tr

    def compute(*refs):
        s_refs, (w_ref, m_ref, v_ref), (g_ref, d_ref, mo_ref, vo_ref) = refs[:n_layers], refs[n_layers:n_layers + 3], \
            refs[n_layers + 3:]
        for k in range(n_layers):
            @pl.when(pl.program_id(0) == k)
            def _(k=k):
                g = s_refs[k][0].astype(F32)
                for dev in range(1, N_DEV):
                    g = g + s_refs[k][dev].astype(F32)
                g_ref[0] = g
                d_ref[0], mo_ref[0], vo_ref[0] = _adamw_math(w_ref[0], g, m_ref[0], v_ref[0])

    def slots_spec(k):
        return pl.BlockSpec((N_DEV, tr, c),
                            lambda l, i: (0, jnp.where(l == k, i, jnp.where(l < k, 0, nt - 1)), 0))

    tile = pl.BlockSpec((1, tr, c), lambda l, i: (l, i, 0))
    return _call(name, compute, (n_layers, nt), [slots_spec(k) for k in range(n_layers)] + [tile] * 3, [tile] * 4,
                 [jax.ShapeDtypeStruct((n_layers, r, c), F32)] * 4, [], list(slots) + [w, m, v])


def _pack_rows(parts):
    rows = []
    for a in parts:
        flat = a.reshape(-1)
        n = -(-flat.shape[0] // (8 * BLK)) * 8
        rows.append(jnp.pad(flat, (0, n * BLK - flat.shape[0])).reshape(n, BLK))
    return jnp.concatenate(rows, axis=0)


def _unpack_rows(packed, like):
    out = []
    row = 0
    for a in like:
        n = -(-a.size // (8 * BLK)) * 8
        out.append(packed[row:row + n].reshape(-1)[:a.size].reshape(a.shape))
        row += n
    return out


def kernel(x, norm_g, w_in, q_norm, k_norm, sinks, w_s, b_s, w_out, loss_target, m_norm_g, m_w_in, m_q_norm, m_k_norm, m_sinks, m_w_s, m_b_s, m_w_out, v_norm_g, v_w_in, v_q_norm, v_k_norm, v_sinks, v_w_s, v_b_s, v_w_out):
    xs = x[0]
    tgt = loss_target[0]
    slopes = jnp.asarray(2.0 ** (-8.0 * np.arange(1, N_HEADS + 1) / N_HEADS), dtype=F32)
    wt_sh = jnp.swapaxes(w_in, 1, 2).astype(BF16)
    wo_sh = w_out.astype(BF16)

    layer_par = []
    for l in range(DEPTH):
        layer_par.append((jnp.tile(q_norm[l], 2)[None, :], jnp.tile(k_norm[l], 2)[None, :],
                          jnp.repeat(b_s[l].T, HALF, axis=1)))

    wt_full = _exchange("gather_w0", _gather_rows((W_IN_SHARD,), (0,)), [wt_sh], [W_FULL[W_IN_SHARD]])[0]
    saved = []
    cur = xs
    for l in range(DEPTH):
        qw, kw, bmap = layer_par[l]
        more = l + 1 < DEPTH
        if more:
            h, proj, wo_full, wt_next = _norm_proj(
                cur, norm_g[l][None, :], wt_full,
                (_gather_rows((W_OUT_SHARD, W_IN_SHARD), (l, l + 1)), [wo_sh, wt_sh],
                 [W_FULL[W_OUT_SHARD], W_FULL[W_IN_SHARD]]))
        else:
            h, proj, wo_full = _norm_proj(cur, norm_g[l][None, :], wt_full,
                                          (_gather_rows((W_OUT_SHARD,), (l,)), [wo_sh], [W_FULL[W_OUT_SHARD]]))
        mix = _mixer_fwd(proj, qw, kw, sinks[l], slopes, w_s[l], bmap)[0]
        saved.append((cur, h, proj, mix, wt_full, wo_full))
        if more:
            cur = _out_proj(mix, wo_full, cur)
            wt_full = wt_next
        else:
            dx, dx_b, sq = _out_proj_loss(mix, wo_full, cur, tgt)
    loss = lax.psum(0.5 * jnp.sum(sq) / D_MODEL, MESH_AXES)

    wt_slots, wo_slots, g_small, g_norm = ([None] * DEPTH for _ in range(4))
    wt_slots_shape = _slots_shape(W_IN_SHARD, D_MODEL, BF16)
    dwt_waiting = None
    for l in reversed(range(DEPTH)):
        x_l, h, proj, mix, wt_l, wo_l = saved[l]
        qw, kw, bmap = layer_par[l]
        dmix = _dmix(dx_b, wo_l)
        dwo_part = _dw_out(mix, dx_b)
        res = _mixer_bwd(proj, dmix, qw, kw, sinks[l], slopes, w_s[l], bmap,
                         None if dwt_waiting is None else
                         (_scatter_rows((W_IN_SHARD,)), [dwt_waiting], [wt_slots_shape]))
        dproj, dqw, dkw, dsk, dws, dbs = res[:6]
        if dwt_waiting is not None:
            wt_slots[l + 1] = res[6]
        small_like = [dqw[0, :HALF], dkw[0, :HALF], dsk[0, :N_HEADS], dws, dbs]
        packed = _pack_rows(small_like)
        dwt_waiting, small_slots, wo_slots[l] = _dw_in(
            dproj, h, (_gather_slots_and_scatter_rows(W_OUT_SHARD), [packed, dwo_part],
                       [_slots_shape(*packed.shape, F32), _slots_shape(W_OUT_SHARD, D_MODEL, BF16)]))
        if l > 0:
            dx, dng, dx_b = _dh_norm_bwd(dproj, wt_l, x_l, dx, norm_g[l][None, :], True)
        else:
            dx, dng, wt_slots[0] = _dh_norm_bwd(dproj, wt_l, x_l, dx, norm_g[l][None, :], False,
                                                (_scatter_rows((W_IN_SHARD,)), [dwt_waiting], [wt_slots_shape]))
        g_small[l] = _unpack_rows(_sum_slots(small_slots, packed.shape[0]), small_like)
        g_norm[l] = dng[0]

    dng_all = _pack_rows([jnp.stack(g_norm)])
    dng_slots = _exchange("gather_dnorm", _gather_slots(), [dng_all], [_slots_shape(*dng_all.shape, F32)])[0]
    gr_norm = _unpack_rows(_sum_slots(dng_slots, dng_all.shape[0]), [norm_g])[0]
    gr_qn, gr_kn, gr_sk, gr_ws, gr_bs = (jnp.stack([g_small[l][i] for l in range(DEPTH)]) for i in range(5))

    def t(a):
        return jnp.swapaxes(a, 1, 2)

    from_slots = {1: _adamw_from_slots("adamw_w_in", wt_slots, t(w_in), t(m_w_in), t(v_w_in), ADAMW_ROWS),
                  7: _adamw_from_slots("adamw_w_out", wo_slots, w_out, m_w_out, v_w_out, ADAMW_ROWS)}
    from_slots[1] = [t(a) for a in from_slots[1]]

    grads = [gr_norm, None, gr_qn, gr_kn, gr_sk, gr_ws, gr_bs, None]
    weights = [norm_g, w_in, q_norm, k_norm, sinks, w_s, b_s, w_out]
    moms = [m_norm_g, m_w_in, m_q_norm, m_k_norm, m_sinks, m_w_s, m_b_s, m_w_out]
    vels = [v_norm_g, v_w_in, v_q_norm, v_k_norm, v_sinks, v_w_s, v_b_s, v_w_out]
    tiles = [None, None, None, None, None, 1024, None, None]
    names = ["norm_g", "w_in", "q_norm", "k_norm", "sinks", "w_s", "b_s", "w_out"]
    deltas, new_m, new_v = [], [], []
    for i, (nm, w, g, m, v, tr) in enumerate(zip(names, weights, grads, moms, vels, tiles)):
        if i in from_slots:
            grads[i], d, mo, vo = from_slots[i]
        else:
            d, mo, vo = _adamw("adamw_" + nm, w, g, m, v, tr)
        deltas.append(d)
        new_m.append(mo)
        new_v.append(vo)

    return (loss, dx[None], *grads, *deltas, *new_m, *new_v)
```

```python
import numpy as np
import jax
import jax.numpy as jnp
from jax import lax
from jax.experimental import pallas as pl
from jax.experimental.pallas import tpu as pltpu

F32 = jnp.float32
BF16 = jnp.bfloat16

D_MODEL = 2048
D_ATTN = 1024
D_IN = 5632
N_HEADS = 16
DEPTH = 4
BLK = 128
HALF = 64
RMS_EPS = 1e-6
C_Q, C_K, C_V, C_GA, C_ZU, C_ZV, C_GB = 0, 1024, 1280, 1536, 2560, 3584, 4608
NEG = -1e30
N_DEV = 8
W_IN_SHARD = D_IN // N_DEV
W_OUT_SHARD = D_MODEL // N_DEV
INV_SQRT2 = 0.7071067811865476
INV_SQRT_2PI = 0.3989422804014327

TM_RESIDENT = 256
TM_STREAM = 512
TM_WIDE = 1024
TN_PROJ = 512
DW_IN_ROWS = D_IN // 4
DW_IN_TOK = 2048
ADAMW_ROWS = 64
MID_STEP_16THS = 13

ADAM_LR = 0.001
ADAM_B1 = 0.9
ADAM_B2 = 0.999
ADAM_EPS = 1e-08
ADAM_WD = 0.01
ADAM_STEP = 10

NT_DIMS = (((1,), (1,)), ((), ()))
TN_DIMS = (((0,), (0,)), ((), ()))
MESH_AXES = ("x", "y", "c")


def _sigmoid(v):
    return 1.0 / (1.0 + jnp.exp(-v))


def _gelu_cdf(z):
    return 0.5 * (1.0 + lax.erf(z * INV_SQRT2))


def _gelu_grad(z, cdf):
    return cdf + z * (jnp.exp(-0.5 * z * z) * INV_SQRT_2PI)


def _lane_halves(rows):
    lane = lax.broadcasted_iota(jnp.int32, (rows, BLK), 1)
    return lane < HALF, lane >= HALF


def _half_masks():
    return {BLK: _lane_halves(BLK), 2 * BLK: _lane_halves(2 * BLK)}


def _half_sum(v, ones):
    h_a, h_b = ones[v.shape[0]]
    s_a = jnp.sum(jnp.where(h_a, v, 0.0), axis=-1, keepdims=True)
    s_b = jnp.sum(jnp.where(h_b, v, 0.0), axis=-1, keepdims=True)
    return jnp.where(h_a, s_a, s_b)


def _half_rms(v, w, ones):
    r = lax.rsqrt(_half_sum(v * v, ones) * (1.0 / HALF) + RMS_EPS)
    yhat = v * r
    return yhat, r, yhat * w


def _half_rms_bwd(dy, yhat, r, w, ones):
    dyh = dy * w
    c = _half_sum(dyh * yhat, ones) * (1.0 / HALF)
    return r * (dyh - yhat * c)


def _band_mask(n):
    t = lax.broadcasted_iota(jnp.int32, (BLK, 2 * BLK), 0)
    kk = lax.broadcasted_iota(jnp.int32, (BLK, 2 * BLK), 1)
    dist = t + BLK - kk
    first_key = jnp.where(n > 0, 0, BLK)
    ok = (dist >= 0) & (dist < BLK) & (kk >= first_key)
    return ok, dist.astype(F32)


def _alibi_table(n, slopes_ref, bias_scr):
    ok, distf = _band_mask(n)
    for hd in range(N_HEADS):
        bias_scr[BLK * hd:BLK * (hd + 1), :] = jnp.where(ok, -(slopes_ref[hd] * distf), NEG)


def _sink_col(sinks_ref, kt):
    return jnp.concatenate([jnp.full((BLK, 1), sinks_ref[8 * kt + i], F32) for i in range(8)], axis=0)


def _softmax_sink(s_scaled, bias, sink):
    s = s_scaled + bias
    m = jnp.maximum(jnp.max(s, axis=-1, keepdims=True), sink)
    p = jnp.exp(s - m)
    es = jnp.exp(sink - m)
    inv = 1.0 / (jnp.sum(p, axis=-1, keepdims=True) + es)
    return p * inv, es * inv


def _rows(i):
    return slice(BLK * i, BLK * (i + 1))


def _cols(base, j):
    return slice(base + BLK * j, base + BLK * (j + 1))


def _to_half(v, have, want):
    return v if have == want else pltpu.roll(v, HALF, 1)


def _tril_mask():
    row = lax.broadcasted_iota(jnp.int32, (BLK, BLK), 0)
    col = lax.broadcasted_iota(jnp.int32, (BLK, BLK), 1)
    return row >= col


def _tile(limit, s):
    t = min(limit, s)
    assert s % t == 0, (s, t)
    return t


def _mesh_place():
    x, y, c = lax.axis_index("x"), lax.axis_index("y"), lax.axis_index("c")
    return x, y, c, 4 * x + 2 * y + c


def _peer(x, y, c, k):
    px = 1 - x if k & 4 else x
    py = 1 - y if k & 2 else y
    pc = 1 - c if k & 1 else c
    return (px, py, pc), 4 * px + 2 * py + pc


class _Pushes:
    def __init__(self, n_arrays, src_view, dst_view):
        self.na = n_arrays
        self.src_view = src_view
        self.dst_view = dst_view

    def scratch(self):
        n = self.na * (N_DEV - 1)
        return [pltpu.SemaphoreType.DMA((n,)), pltpu.SemaphoreType.DMA((n,)), pltpu.SemaphoreType.DMA((self.na,))]

    def copies(self, src_refs, dst_refs, send_sems, recv_sems, local_sems):
        x, y, c, me = _mesh_place()
        cps = []
        for a in range(self.na):
            cps.append(pltpu.make_async_copy(self.src_view(a, src_refs[a], me), self.dst_view(a, dst_refs[a], me),
                                             local_sems.at[a]))
        for k in range(1, N_DEV):
            peer, pidx = _peer(x, y, c, k)
            for a in range(self.na):
                sem = a * (N_DEV - 1) + k - 1
                cps.append(pltpu.make_async_remote_copy(
                    src_ref=self.src_view(a, src_refs[a], pidx), dst_ref=self.dst_view(a, dst_refs[a], me),
                    send_sem=send_sems.at[sem], recv_sem=recv_sems.at[sem],
                    device_id=peer, device_id_type=pl.DeviceIdType.MESH))
        return cps

    def plan(self, src_refs, dst_refs, send_sems, recv_sems, local_sems):
        cps = self.copies(src_refs, dst_refs, send_sems, recv_sems, local_sems)
        return cps, [], [], [cp.wait for cp in cps]


class _TwoLevelGather:
    def __init__(self, shard_rows, layers):
        self.na = len(shard_rows)
        self.shard_rows = shard_rows
        self.layers = layers

    def scratch(self):
        n = self.na * (N_DEV - 1)
        return [pltpu.SemaphoreType.DMA((n,)), pltpu.SemaphoreType.DMA((n,)), pltpu.SemaphoreType.DMA((self.na,))]

    def plan(self, src_refs, dst_refs, send_sems, recv_sems, local_sems):
        x, y, c, _ = _mesh_place()
        sibling = (x, y, 1 - c)
        chips = [(1 - x, y), (x, 1 - y), (1 - x, 1 - y)]
        start, mid_wait, mid_start, final = [], [], [], []
        for a in range(self.na):
            r = self.shard_rows[a]
            src = src_refs[a].at[self.layers[a]]
            dst = dst_refs[a]

            def rows(px, py, pc, r=r, dst=dst):
                return dst.at[pl.ds(pl.multiple_of((4 * px + 2 * py + pc) * r, 64), r), :]

            def remote(k, s_ref, block, to, a=a, rows=rows):
                return pltpu.make_async_remote_copy(
                    src_ref=s_ref, dst_ref=rows(*block),
                    send_sem=send_sems.at[a * (N_DEV - 1) + k], recv_sem=recv_sems.at[a * (N_DEV - 1) + k],
                    device_id=to, device_id_type=pl.DeviceIdType.MESH)

            mine = pltpu.make_async_copy(src, rows(x, y, c), local_sems.at[a])
            own = [remote(0, src, (x, y, c), sibling)]
            own += [remote(1 + j, src, (x, y, c), (*chip, c)) for j, chip in enumerate(chips)]
            passed = [remote(4 + j, rows(*chip, c), (*chip, c), sibling) for j, chip in enumerate(chips)]
            start += [mine] + own
            mid_wait += own[1:]
            mid_start += passed
            final += [own[0].wait_recv] + [cp.wait_recv for cp in passed]
            final += [cp.wait_send for cp in own + passed] + [mine.wait]
        return start, mid_wait, mid_start, final


def _call(name, compute, grid, in_specs, out_specs, out_shape, scratch, args, push=None):
    sem = pltpu.CompilerParams(dimension_semantics=("arbitrary",) * len(grid))
    if push is None:
        return pl.pallas_call(compute, name=name, grid=grid, in_specs=in_specs, out_specs=out_specs,
                              out_shape=out_shape, scratch_shapes=scratch, compiler_params=sem)(*args)
    pushes, srcs, xshapes = push
    n_in, n_out, n_scr, na = len(args), len(out_shape), len(scratch), pushes.na
    hbm = pl.BlockSpec(memory_space=pltpu.HBM)

    def body(*refs):
        ins, refs = refs[:n_in], refs[n_in:]
        xin, refs = refs[:na], refs[na:]
        outs, refs = refs[:n_out], refs[n_out:]
        xout, refs = refs[:na], refs[na:]
        scr, sems = refs[:n_scr], refs[n_scr:]
        start, mid_wait, mid_start, final = pushes.plan(xin, xout, *sems)
        first = pl.program_id(0) == 0
        middle = pl.program_id(0) == (grid[0] * MID_STEP_16THS) // 16
        last = pl.program_id(0) == grid[0] - 1
        for d in range(1, len(grid)):
            first = first & (pl.program_id(d) == 0)
            middle = middle & (pl.program_id(d) == 0)
            last = last & (pl.program_id(d) == grid[d] - 1)

        @pl.when(first)
        def _():
            for cp in start:
                cp.start()

        if mid_start:
            @pl.when(middle)
            def _():
                for cp in mid_wait:
                    cp.wait_recv()
                for cp in mid_start:
                    cp.start()

        compute(*ins, *outs, *scr)

        @pl.when(last)
        def _():
            for wait in final:
                wait()

    return pl.pallas_call(
        body, name=name, grid=grid,
        in_specs=list(in_specs) + [hbm] * na, out_specs=list(out_specs) + [hbm] * na,
        out_shape=list(out_shape) + list(xshapes),
        scratch_shapes=list(scratch) + pushes.scratch(), compiler_params=sem)(*args, *srcs)


def _exchange(name, pushes, srcs, out_shapes):
    na = pushes.na
    hbm = pl.BlockSpec(memory_space=pltpu.HBM)

    def body(*refs):
        start, mid_wait, mid_start, final = pushes.plan(refs[:na], refs[na:2 * na], *refs[2 * na:])
        for cp in start:
            cp.start()
        for cp in mid_wait:
            cp.wait_recv()
        for cp in mid_start:
            cp.start()
        for wait in final:
            wait()

    return pl.pallas_call(body, name=name, in_specs=[hbm] * na, out_specs=[hbm] * na, out_shape=out_shapes,
                          scratch_shapes=pushes.scratch())(*srcs)


def _gather_rows(shard_rows, layers):
    return _TwoLevelGather(shard_rows, layers)


def _scatter_rows(shard_rows):
    def src_view(a, ref, idx):
        r = shard_rows[a]
        return ref.at[pl.ds(pl.multiple_of(idx * r, 64), r), :]

    def dst_view(a, ref, idx):
        return ref.at[idx]

    return _Pushes(len(shard_rows), src_view, dst_view)


def _gather_slots():
    return _Pushes(1, lambda a, ref, idx: ref, lambda a, ref, idx: ref.at[idx])


def _gather_slots_and_scatter_rows(rows):
    def src_view(a, ref, idx):
        return ref if a == 0 else ref.at[pl.ds(pl.multiple_of(idx * rows, 64), rows), :]

    return _Pushes(2, src_view, lambda a, ref, idx: ref.at[idx])


W_FULL = {W_IN_SHARD: jax.ShapeDtypeStruct((D_IN, D_MODEL), BF16),
          W_OUT_SHARD: jax.ShapeDtypeStruct((D_MODEL, D_MODEL), BF16)}


def _slots_shape(rows, cols, dtype):
    return jax.ShapeDtypeStruct((N_DEV, rows, cols), dtype)


def _sum_slots(slots, tr):
    _, r, c = slots.shape

    def compute(s_ref, o_ref):
        tot = s_ref[0].astype(F32)
        for d in range(1, N_DEV):
            tot = tot + s_ref[d].astype(F32)
        o_ref[...] = tot

    return _call("sum_slots", compute, (r // tr,),
                 [pl.BlockSpec((N_DEV, tr, c), lambda i: (0, i, 0))], [pl.BlockSpec((tr, c), lambda i: (i, 0))],
                 [jax.ShapeDtypeStruct((r, c), F32)], [], [slots])[0]


def _resident(shape):
    return pl.BlockSpec(shape, lambda *_: (0,) * len(shape), pipeline_mode=pl.Buffered(1))


def _norm_proj(x, g, w_t, push=None):
    s = x.shape[0]
    tm = _tile(TM_RESIDENT, s)

    n_chunks = D_IN // TN_PROJ

    def compute(x_ref, g_ref, w_hbm, h_ref, p_ref, w_ref, w_sems):
        first = pl.program_id(0) == 0

        def load(j):
            rows = pl.ds(j * TN_PROJ, TN_PROJ)
            return pltpu.make_async_copy(w_hbm.at[rows, :], w_ref.at[rows, :], w_sems.at[j])

        @pl.when(first)
        def _():
            for j in range(n_chunks):
                load(j).start()

        def project(wait):
            xf = x_ref[...]
            r = lax.rsqrt(jnp.mean(xf * xf, axis=-1, keepdims=True) + RMS_EPS)
            h = ((xf * r) * g_ref[...]).astype(BF16)
            h_ref[...] = h
            for j in range(n_chunks):
                cols = slice(j * TN_PROJ, (j + 1) * TN_PROJ)
                if wait:
                    load(j).wait()
                p_ref[:, cols] = lax.dot_general(h, w_ref[cols, :], NT_DIMS, preferred_element_type=F32)

        pl.when(first)(lambda: project(True))
        pl.when(jnp.logical_not(first))(lambda: project(False))

    return _call(
        "norm_proj", compute, (s // tm,),
        [pl.BlockSpec((tm, D_MODEL), lambda i: (i, 0)), pl.BlockSpec((1, D_MODEL), lambda i: (0, 0)),
         pl.BlockSpec(memory_space=pltpu.HBM)],
        [pl.BlockSpec((tm, D_MODEL), lambda i: (i, 0)), pl.BlockSpec((tm, D_IN), lambda i: (i, 0))],
        [jax.ShapeDtypeStruct((s, D_MODEL), BF16), jax.ShapeDtypeStruct((s, D_IN), F32)],
        [pltpu.VMEM((D_IN, D_MODEL), BF16), pltpu.SemaphoreType.DMA((n_chunks,))], [x, g, w_t], push)


def _sgu_weights(ws_ref, wtril_scr, wtril_t_scr=None):
    tril = _tril_mask()
    for hd in range(N_HEADS):
        w = jnp.where(tril, ws_ref[hd], 0.0)
        wtril_scr[BLK * hd:BLK * (hd + 1), :] = w.astype(BF16)
        if wtril_t_scr is not None:
            wtril_t_scr[hd // 2, :, BLK * (hd % 2):BLK * (hd % 2 + 1)] = w.T.astype(BF16)


def _kv_band(kt, p_ref, kvp_ref, kw_v, ones):
    kband = jnp.concatenate([kvp_ref[:, _cols(0, kt)], p_ref[:, _cols(C_K, kt)]], axis=0)
    kyhat, kr, kn = _half_rms(kband, kw_v, ones)
    vband = jnp.concatenate([kvp_ref[:, _cols(256, kt)], p_ref[:, _cols(C_V, kt)]], axis=0)
    return kyhat, kr, (kn * 0.125).astype(BF16), vband.astype(BF16)


def _stack_heads(tiles, halves):
    parts = []
    for tt, tile in enumerate(tiles):
        for qh in range(2):
            parts.append(_to_half(jnp.where(halves[qh], tile, 0.0), qh, tt // 2).astype(BF16))
    return jnp.concatenate(parts, axis=0)


def _unstack_heads(stacked, tt, h_a):
    return jnp.where(h_a, _to_half(stacked[_rows(2 * tt)], tt // 2, 0), _to_half(stacked[_rows(2 * tt + 1)], tt // 2, 1))


def _mixer_fwd(proj, qw, kw, sinks, slopes, w_s, bmap, push=None):
    s = proj.shape[0]
    nb = s // BLK

    def compute(sinks_ref, slopes_ref, p_ref, kvp_ref, qw_ref, kw_ref, ws_ref, bmap_ref, mix_ref,
                wtril_scr, bias_scr):
        _mixer_tables(pl.program_id(0), slopes_ref, ws_ref, wtril_scr, bias_scr)
        _mixer_fwd_block(sinks_ref, p_ref, kvp_ref, qw_ref, kw_ref, bmap_ref, mix_ref, wtril_scr, bias_scr)

    smem = pl.BlockSpec(memory_space=pltpu.SMEM)
    return _call(
        "mixer_fwd", compute, (nb,),
        [smem, smem,
         pl.BlockSpec((BLK, D_IN), lambda n: (n, 0)),
         pl.BlockSpec((BLK, 512), lambda n: (jnp.maximum(n - 1, 0), 2)),
         pl.BlockSpec((1, BLK), lambda n: (0, 0)),
         pl.BlockSpec((1, BLK), lambda n: (0, 0)),
         pl.BlockSpec((N_HEADS, BLK, BLK), lambda n: (0, 0, 0)),
         pl.BlockSpec((BLK, D_ATTN), lambda n: (0, 0))],
        [pl.BlockSpec((BLK, D_MODEL), lambda n: (n, 0))],
        [jax.ShapeDtypeStruct((s, D_MODEL), BF16)],
        [pltpu.VMEM((N_HEADS * BLK, BLK), BF16), pltpu.VMEM((N_HEADS * BLK, 2 * BLK), F32)],
        [sinks, slopes, proj, proj, qw, kw, w_s, bmap], push)


def _mixer_tables(n, slopes_ref, ws_ref, wtril_scr, bias_scr, wtril_t_scr=None):
    @pl.when(n == 0)
    def _():
        _sgu_weights(ws_ref, wtril_scr, wtril_t_scr)

    @pl.when(n <= 1)
    def _():
        _alibi_table(n, slopes_ref, bias_scr)


def _mixer_fwd_block(sinks_ref, p_ref, kvp_ref, qw_ref, kw_ref, bmap_ref, mix_ref, wtril_scr, bias_scr):
    h_a, h_b = _lane_halves(BLK)
    ones = _half_masks()
    halves = (h_a, h_b)
    qw_v = qw_ref[...]
    kw_v = kw_ref[...]
    bands = [_kv_band(kt, p_ref, kvp_ref, kw_v, ones) for kt in range(2)]
    sc = []
    for kt in range(2):
        qn = [_half_rms(p_ref[:, _cols(C_Q, 4 * kt + tt)], qw_v, ones)[2] for tt in range(4)]
        sc.append(lax.dot_general(_stack_heads(qn, halves), bands[kt][2], NT_DIMS, preferred_element_type=F32))
    zu, mixed = [], []
    for j in range(8):
        zu_pre = p_ref[:, _cols(C_ZU, j)]
        zv_pre = p_ref[:, _cols(C_ZV, j)]
        zu.append(zu_pre * _gelu_cdf(zu_pre))
        zvb = (zv_pre * _gelu_cdf(zv_pre)).astype(BF16)
        mixed.append(jnp.dot(wtril_scr[2 * BLK * j:2 * BLK * (j + 1), :], zvb, preferred_element_type=F32))
    o = []
    for kt in range(2):
        p, _ = _softmax_sink(sc[kt], bias_scr[8 * BLK * kt:8 * BLK * (kt + 1), :], _sink_col(sinks_ref, kt))
        o.append(jnp.dot(p.astype(BF16), bands[kt][3], preferred_element_type=F32))
    for j in range(8):
        gb = p_ref[:, _cols(C_GB, j)]
        mx = jnp.where(h_a, mixed[j][0:BLK], mixed[j][BLK:2 * BLK]) + bmap_ref[:, _cols(0, j)]
        mix_ref[:, _cols(D_ATTN, j)] = ((zu[j] * mx) * (gb * _sigmoid(gb))).astype(BF16)
    for kt in range(2):
        for tt in range(4):
            j = 4 * kt + tt
            ga = p_ref[:, _cols(C_GA, j)]
            mix_ref[:, _cols(0, j)] = (_unstack_heads(o[kt], tt, h_a) * (ga * _sigmoid(ga))).astype(BF16)


def _out_proj(mix, w_o, x):
    s = x.shape[0]
    tm = _tile(TM_WIDE, s)

    def compute(m_ref, w_ref, x_ref, o_ref):
        o_ref[...] = x_ref[...] + jnp.dot(m_ref[...], w_ref[...], preferred_element_type=F32)

    return _call(
        "out_proj", compute, (s // tm,),
        [pl.BlockSpec((tm, D_MODEL), lambda i: (i, 0)), _resident((D_MODEL, D_MODEL)),
         pl.BlockSpec((tm, D_MODEL), lambda i: (i, 0))],
        [pl.BlockSpec((tm, D_MODEL), lambda i: (i, 0))],
        [jax.ShapeDtypeStruct((s, D_MODEL), F32)], [], [mix, w_o, x])[0]


def _out_proj_loss(mix, w_o, x, tgt):
    s = x.shape[0]
    tm = _tile(TM_STREAM, s)

    def compute(m_ref, w_ref, x_ref, t_ref, dy_ref, dyb_ref, sq_ref):
        @pl.when(pl.program_id(0) == 0)
        def _():
            sq_ref[...] = jnp.zeros_like(sq_ref)

        y = x_ref[...] + jnp.dot(m_ref[...], w_ref[...], preferred_element_type=F32)
        e = y - t_ref[...]
        dy = e * (1.0 / D_MODEL)
        dy_ref[...] = dy
        dyb_ref[...] = dy.astype(BF16)
        sq_ref[...] += jnp.sum(e * e, axis=0, keepdims=True)

    tok = pl.BlockSpec((tm, D_MODEL), lambda i: (i, 0))
    return _call(
        "out_proj_loss", compute, (s // tm,),
        [tok, _resident((D_MODEL, D_MODEL)), tok, tok],
        [tok, tok, pl.BlockSpec((1, D_MODEL), lambda i: (0, 0))],
        [jax.ShapeDtypeStruct((s, D_MODEL), F32), jax.ShapeDtypeStruct((s, D_MODEL), BF16),
         jax.ShapeDtypeStruct((1, D_MODEL), F32)], [], [mix, w_o, x, tgt])


def _dmix(dx, w_o):
    s = dx.shape[0]
    tm = _tile(TM_WIDE, s)

    def compute(d_ref, w_ref, o_ref):
        o_ref[...] = lax.dot_general(d_ref[...], w_ref[...], NT_DIMS, preferred_element_type=F32)

    return _call(
        "dmix", compute, (s // tm,),
        [pl.BlockSpec((tm, D_MODEL), lambda i: (i, 0)), _resident((D_MODEL, D_MODEL))],
        [pl.BlockSpec((tm, D_MODEL), lambda i: (i, 0))],
        [jax.ShapeDtypeStruct((s, D_MODEL), F32)], [], [dx, w_o])[0]


def _dw_out(mix, dx):
    s = dx.shape[0]
    tk = _tile(TM_WIDE, s)
    nk = s // tk

    def compute(m_ref, d_ref, o_ref, acc):
        k = pl.program_id(0)

        @pl.when(k == 0)
        def _():
            acc[...] = jnp.zeros_like(acc)

        acc[...] += lax.dot_general(m_ref[...], d_ref[...], TN_DIMS, preferred_element_type=F32)

        @pl.when(k == nk - 1)
        def _():
            o_ref[...] = acc[...].astype(BF16)

    return _call(
        "dw_out", compute, (nk,),
        [pl.BlockSpec((tk, D_MODEL), lambda k: (k, 0)), pl.BlockSpec((tk, D_MODEL), lambda k: (k, 0))],
        [pl.BlockSpec((D_MODEL, D_MODEL), lambda k: (0, 0))],
        [jax.ShapeDtypeStruct((D_MODEL, D_MODEL), BF16)],
        [pltpu.VMEM((D_MODEL, D_MODEL), F32)], [mix, dx])[0]


def _mixer_bwd(proj, dmix, qw, kw, sinks, slopes, w_s, bmap, push=None):
    s = proj.shape[0]
    nb = s // BLK

    def compute(sinks_ref, slopes_ref, p_ref, kvp_ref, dm_ref, qw_ref, kw_ref, ws_ref, bmap_ref,
                dp_ref, dqw_ref, dkw_ref, dsk_ref, dws_ref, dbs_ref,
                pend, accq, acck, accs, accb, wtril_scr, wtril_t_scr, bias_scr):
        n = pl.program_id(0)
        h_a, h_b = _lane_halves(BLK)
        ones = _half_masks()
        halves = (h_a, h_b)
        lane = lax.broadcasted_iota(jnp.int32, (BLK, BLK), 1)

        @pl.when(n == 0)
        def _():
            accq[...] = jnp.zeros_like(accq)
            acck[...] = jnp.zeros_like(acck)
            accs[...] = jnp.zeros_like(accs)
            accb[...] = jnp.zeros_like(accb)
            dws_ref[...] = jnp.zeros_like(dws_ref)
            pend[...] = jnp.zeros_like(pend)
            _sgu_weights(ws_ref, wtril_scr, wtril_t_scr)

        @pl.when(n <= 1)
        def _():
            _alibi_table(n, slopes_ref, bias_scr)

        def emit_pending():
            dp_ref[:, 0:C_K] = pend[:, 0:C_K].astype(BF16)
            dp_ref[:, C_GA:D_IN] = pend[:, C_GA:D_IN].astype(BF16)

        @pl.when(n < nb)
        def _():
            emit_pending()
            qw_v = qw_ref[...]
            kw_v = kw_ref[...]
            bands = [_kv_band(kt, p_ref, kvp_ref, kw_v, ones) for kt in range(2)]
            tiles, qst, dost, sc, dpm = [], [], [], [], []
            for kt in range(2):
                qn, d_o, tl = [], [], []
                for tt in range(4):
                    j = 4 * kt + tt
                    qyhat, qr, qn_t = _half_rms(p_ref[:, _cols(C_Q, j)], qw_v, ones)
                    ga = p_ref[:, _cols(C_GA, j)]
                    sg = _sigmoid(ga)
                    dma = dm_ref[:, _cols(0, j)]
                    qn.append(qn_t)
                    d_o.append(dma * (ga * sg))
                    tl.append((qyhat, qr, dma * (sg * (1.0 + ga * (1.0 - sg)))))
                tiles.append(tl)
                qst.append(_stack_heads(qn, halves))
                dost.append(_stack_heads(d_o, halves))
                sc.append(lax.dot_general(qst[kt], bands[kt][2], NT_DIMS, preferred_element_type=F32))
                dpm.append(lax.dot_general(dost[kt], bands[kt][3], NT_DIMS, preferred_element_type=F32))
            sgu = []
            for j in range(8):
                zu_pre = p_ref[:, _cols(C_ZU, j)]
                zv_pre = p_ref[:, _cols(C_ZV, j)]
                cu = _gelu_cdf(zu_pre)
                cv = _gelu_cdf(zv_pre)
                zvb = (zv_pre * cv).astype(BF16)
                sgu.append((zu_pre * cu, zvb, _gelu_grad(zu_pre, cu), _gelu_grad(zv_pre, cv),
                            jnp.dot(wtril_scr[2 * BLK * j:2 * BLK * (j + 1), :], zvb, preferred_element_type=F32)))
            dsink = jnp.zeros((BLK, BLK), F32)
            pst, dqkst = [], []
            for kt in range(2):
                p, p_sink = _softmax_sink(sc[kt], bias_scr[8 * BLK * kt:8 * BLK * (kt + 1), :],
                                          _sink_col(sinks_ref, kt))
                dsum = jnp.sum(p * dpm[kt], axis=-1, keepdims=True)
                dsink_col = -(p_sink * dsum)
                for i in range(8):
                    dsink = dsink + jnp.where(lane == 8 * kt + i, dsink_col[_rows(i)], 0.0)
                pst.append(p.astype(BF16))
                dqkst.append((p * (dpm[kt] - dsum)).astype(BF16))
            o, dqn_all, dvb, dkn = [], [], [], []
            for kt in range(2):
                o.append(jnp.dot(pst[kt], bands[kt][3], preferred_element_type=F32))
                dqn_all.append(jnp.dot(dqkst[kt], bands[kt][2], preferred_element_type=F32))
                dvb.append(lax.dot_general(pst[kt], dost[kt], TN_DIMS, preferred_element_type=F32))
                dkn.append(0.125 * lax.dot_general(dqkst[kt], qst[kt], TN_DIMS, preferred_element_type=F32))
            dms = []
            for j in range(8):
                zu, zvb, gu, gv, m_ab = sgu[j]
                gb = p_ref[:, _cols(C_GB, j)]
                dmb = dm_ref[:, _cols(D_ATTN, j)]
                mixed = jnp.where(h_a, m_ab[0:BLK], m_ab[BLK:2 * BLK]) + bmap_ref[:, _cols(0, j)]
                sgb = _sigmoid(gb)
                dgate = dmb * (gb * sgb)
                pend[:, _cols(C_ZU, j)] = (dgate * mixed) * gu
                pend[:, _cols(C_GB, j)] = (dmb * (zu * mixed)) * (sgb * (1.0 + gb * (1.0 - sgb)))
                dmixed = dgate * zu
                accb[:, _cols(0, j)] += dmixed
                dms.append(jnp.concatenate([jnp.where(h_a, dmixed, 0.0).astype(BF16),
                                            jnp.where(h_b, dmixed, 0.0).astype(BF16)], axis=0))
            dzv = []
            for j in range(8):
                dzv.append(jnp.dot(wtril_t_scr[j], dms[j], preferred_element_type=F32))
                dw_ab = lax.dot_general(dms[j], sgu[j][1], NT_DIMS, preferred_element_type=F32)
                dws_ref[2 * j] += dw_ab[0:BLK]
                dws_ref[2 * j + 1] += dw_ab[BLK:2 * BLK]
            dq_w = jnp.zeros((BLK, BLK), F32)
            for kt in range(2):
                for tt in range(4):
                    j = 4 * kt + tt
                    qyhat, qr, dsilu = tiles[kt][tt]
                    dqn = _unstack_heads(dqn_all[kt], tt, h_a)
                    pend[:, _cols(C_GA, j)] = _unstack_heads(o[kt], tt, h_a) * dsilu
                    pend[:, _cols(C_Q, j)] = _half_rms_bwd(dqn, qyhat, qr, qw_v, ones)
                    dq_w = dq_w + dqn * qyhat
            dk_w = jnp.zeros((BLK, BLK), F32)
            for kt in range(2):
                kyhat, kr = bands[kt][0], bands[kt][1]
                dk = _half_rms_bwd(dkn[kt], kyhat, kr, kw_v, ones)
                dkw_part = dkn[kt] * kyhat
                dk_w = dk_w + (dkw_part[0:BLK] + dkw_part[BLK:2 * BLK])
                dv = dvb[kt]
                dp_ref[:, _cols(C_K, kt)] = (pend[:, _cols(C_K, kt)] + dk[0:BLK]).astype(BF16)
                dp_ref[:, _cols(C_V, kt)] = (pend[:, _cols(C_V, kt)] + dv[0:BLK]).astype(BF16)
                pend[:, _cols(C_K, kt)] = dk[BLK:2 * BLK]
                pend[:, _cols(C_V, kt)] = dv[BLK:2 * BLK]
            accq[...] += dq_w
            acck[...] += dk_w
            accs[...] += dsink
            for j in range(8):
                pend[:, _cols(C_ZV, j)] = dzv[j] * sgu[j][3]

        @pl.when(n == nb)
        def _():
            tril = _tril_mask()
            emit_pending()
            dp_ref[:, C_K:C_GA] = pend[:, C_K:C_GA].astype(BF16)
            aq = accq[...]
            ak = acck[...]
            dqw_ref[...] = jnp.sum(aq + pltpu.roll(aq, HALF, 1), axis=0, keepdims=True)
            dkw_ref[...] = jnp.sum(ak + pltpu.roll(ak, HALF, 1), axis=0, keepdims=True)
            dsk_ref[...] = jnp.sum(accs[...], axis=0, keepdims=True)
            for hd in range(N_HEADS):
                dws_ref[hd] = jnp.where(tril, dws_ref[hd], 0.0)
            hrow = lax.broadcasted_iota(jnp.int32, (N_HEADS, D_ATTN), 0)
            hcol = lax.broadcasted_iota(jnp.int32, (N_HEADS, D_ATTN), 1)
            sel = jnp.where((hcol >= hrow * HALF) & (hcol < (hrow + 1) * HALF), 1.0, 0.0).astype(BF16)
            rem = accb[...]
            tot = jnp.zeros((N_HEADS, BLK), F32)
            for _ in range(3):
                part = rem.astype(BF16)
                tot = tot + lax.dot_general(sel, part, NT_DIMS, preferred_element_type=F32)
                rem = rem - part.astype(F32)
            dbs_ref[...] = tot

    smem = pl.BlockSpec(memory_space=pltpu.SMEM)
    last = nb - 1
    tile_f32 = pltpu.VMEM((BLK, BLK), F32)
    return _call(
        "mixer_bwd", compute, (nb + 1,),
        [smem, smem,
         pl.BlockSpec((BLK, D_IN), lambda n: (jnp.minimum(n, last), 0)),
         pl.BlockSpec((BLK, 512), lambda n: (jnp.maximum(jnp.minimum(n, last) - 1, 0), 2)),
         pl.BlockSpec((BLK, D_MODEL), lambda n: (jnp.minimum(n, last), 0)),
         pl.BlockSpec((1, BLK), lambda n: (0, 0)),
         pl.BlockSpec((1, BLK), lambda n: (0, 0)),
         pl.BlockSpec((N_HEADS, BLK, BLK), lambda n: (0, 0, 0)),
         pl.BlockSpec((BLK, D_ATTN), lambda n: (0, 0))],
        [pl.BlockSpec((BLK, D_IN), lambda n: (jnp.maximum(n - 1, 0), 0)),
         pl.BlockSpec((1, BLK), lambda n: (0, 0)),
         pl.BlockSpec((1, BLK), lambda n: (0, 0)),
         pl.BlockSpec((1, BLK), lambda n: (0, 0)),
         pl.BlockSpec((N_HEADS, BLK, BLK), lambda n: (0, 0, 0)),
         pl.BlockSpec((N_HEADS, BLK), lambda n: (0, 0))],
        [jax.ShapeDtypeStruct((s, D_IN), BF16),
         jax.ShapeDtypeStruct((1, BLK), F32),
         jax.ShapeDtypeStruct((1, BLK), F32),
         jax.ShapeDtypeStruct((1, BLK), F32),
         jax.ShapeDtypeStruct((N_HEADS, BLK, BLK), F32),
         jax.ShapeDtypeStruct((N_HEADS, BLK), F32)],
        [pltpu.VMEM((BLK, D_IN), F32), tile_f32, tile_f32, tile_f32, pltpu.VMEM((BLK, D_ATTN), F32),
         pltpu.VMEM((N_HEADS * BLK, BLK), BF16), pltpu.VMEM((N_HEADS // 2, BLK, 2 * BLK), BF16),
         pltpu.VMEM((N_HEADS * BLK, 2 * BLK), F32)],
        [sinks, slopes, proj, proj, dmix, qw, kw, w_s, bmap], push)


def _dh_norm_bwd(dproj, w_t, x, dx_out, g, bf16_copy, push=None):
    s = x.shape[0]
    tm = _tile(TM_RESIDENT, s)

    def compute(dp_ref, w_ref, x_ref, dxo_ref, g_ref, dx_ref, dg_ref, *dxb_ref):
        @pl.when(pl.program_id(0) == 0)
        def _():
            dg_ref[...] = jnp.zeros_like(dg_ref)

        dh = jnp.dot(dp_ref[...], w_ref[...], preferred_element_type=F32)
        xf = x_ref[...]
        r = lax.rsqrt(jnp.mean(xf * xf, axis=-1, keepdims=True) + RMS_EPS)
        yhat = xf * r
        dyh = dh * g_ref[...]
        c = jnp.mean(dyh * yhat, axis=-1, keepdims=True)
        dx = dxo_ref[...] + r * (dyh - yhat * c)
        dx_ref[...] = dx
        if bf16_copy:
            dxb_ref[0][...] = dx.astype(BF16)
        dg_ref[...] += jnp.sum(dh * yhat, axis=0, keepdims=True)

    tok = pl.BlockSpec((tm, D_MODEL), lambda i: (i, 0))
    return _call(
        "dh_norm_bwd", compute, (s // tm,),
        [pl.BlockSpec((tm, D_IN), lambda i: (i, 0)), _resident((D_IN, D_MODEL)), tok, tok,
         pl.BlockSpec((1, D_MODEL), lambda i: (0, 0))],
        [tok, pl.BlockSpec((1, D_MODEL), lambda i: (0, 0))] + [tok] * bf16_copy,
        [jax.ShapeDtypeStruct((s, D_MODEL), F32), jax.ShapeDtypeStruct((1, D_MODEL), F32)]
        + [jax.ShapeDtypeStruct((s, D_MODEL), BF16)] * bf16_copy,
        [], [dproj, w_t, x, dx_out, g], push)


def _dw_in(dproj, h, push=None):
    s = h.shape[0]
    tk = _tile(DW_IN_TOK, s)
    nk = s // tk

    def compute(dp_ref, h_ref, o_ref, acc):
        k = pl.program_id(1)

        @pl.when(k == 0)
        def _():
            acc[...] = jnp.zeros_like(acc)

        acc[...] += lax.dot_general(dp_ref[...], h_ref[...], TN_DIMS, preferred_element_type=F32)

        @pl.when(k == nk - 1)
        def _():
            o_ref[...] = acc[...].astype(BF16)

    return _call(
        "dw_in", compute, (D_IN // DW_IN_ROWS, nk),
        [pl.BlockSpec((tk, DW_IN_ROWS), lambda j, k: (k, j)), pl.BlockSpec((tk, D_MODEL), lambda j, k: (k, 0))],
        [pl.BlockSpec((DW_IN_ROWS, D_MODEL), lambda j, k: (j, 0))],
        [jax.ShapeDtypeStruct((D_IN, D_MODEL), BF16)],
        [pltpu.VMEM((DW_IN_ROWS, D_MODEL), F32)], [dproj, h], push)


def _adamw_math(w, g, m, v):
    m_new = ADAM_B1 * m + (1.0 - ADAM_B1) * g
    v_new = ADAM_B2 * v + (1.0 - ADAM_B2) * jnp.square(g)
    m_hat = m_new / (1.0 - ADAM_B1 ** ADAM_STEP)
    v_hat = v_new / (1.0 - ADAM_B2 ** ADAM_STEP)
    return -ADAM_LR * (m_hat / (jnp.sqrt(v_hat) + ADAM_EPS) + ADAM_WD * w), m_new, v_new


def _adamw(name, w, g, m, v, tr=None):
    shape = w.shape
    c = shape[-1]
    flat = [a.reshape(-1, c) for a in (w, g, m, v)]
    r = flat[0].shape[0]
    tr = r if tr is None else tr

    def compute(w_ref, g_ref, m_ref, v_ref, d_ref, mo_ref, vo_ref):
        d_ref[...], mo_ref[...], vo_ref[...] = _adamw_math(w_ref[...], g_ref[...], m_ref[...], v_ref[...])

    spec = pl.BlockSpec((tr, c), lambda i: (i, 0))
    outs = _call(name, compute, (r // tr,), [spec] * 4, [spec] * 3, [jax.ShapeDtypeStruct((r, c), F32)] * 3,
                 [], flat)
    return [o.reshape(shape) for o in outs]


def _adamw_from_slots(name, slots, w, m, v, tr):
    n_layers, r, c = w.shape
    nt = r // tr

    def compute(*refs):
        s_refs, (w_ref, m_ref, v_ref), (g_ref, d_ref, mo_ref, vo_ref) = refs[:n_layers], refs[n_layers:n_layers + 3], \
            refs[n_layers + 3:]
        for k in range(n_layers):
            @pl.when(pl.program_id(0) == k)
            def _(k=k):
                g = s_refs[k][0].astype(F32)
                for dev in range(1, N_DEV):
                    g = g + s_refs[k][dev].astype(F32)
                g_ref[0] = g
                d_ref[0], mo_ref[0], vo_ref[0] = _adamw_math(w_ref[0], g, m_ref[0], v_ref[0])

    def slots_spec(k):
        return pl.BlockSpec((N_DEV, tr, c),
                            lambda l, i: (0, jnp.where(l == k, i, jnp.where(l < k, 0, nt - 1)), 0))

    tile = pl.BlockSpec((1, tr, c), lambda l, i: (l, i, 0))
    return _call(name, compute, (n_layers, nt), [slots_spec(k) for k in range(n_layers)] + [tile] * 3, [tile] * 4,
                 [jax.ShapeDtypeStruct((n_layers, r, c), F32)] * 4, [], list(slots) + [w, m, v])


def _pack_rows(parts):
    rows = []
    for a in parts:
        flat = a.reshape(-1)
        n = -(-flat.shape[0] // (8 * BLK)) * 8
        rows.append(jnp.pad(flat, (0, n * BLK - flat.shape[0])).reshape(n, BLK))
    return jnp.concatenate(rows, axis=0)


def _unpack_rows(packed, like):
    out = []
    row = 0
    for a in like:
        n = -(-a.size // (8 * BLK)) * 8
        out.append(packed[row:row + n].reshape(-1)[:a.size].reshape(a.shape))
        row += n
    return out


def kernel(x, norm_g, w_in, q_norm, k_norm, sinks, w_s, b_s, w_out, loss_target, m_norm_g, m_w_in, m_q_norm, m_k_norm, m_sinks, m_w_s, m_b_s, m_w_out, v_norm_g, v_w_in, v_q_norm, v_k_norm, v_sinks, v_w_s, v_b_s, v_w_out):
    xs = x[0]
    tgt = loss_target[0]
    slopes = jnp.asarray(2.0 ** (-8.0 * np.arange(1, N_HEADS + 1) / N_HEADS), dtype=F32)
    wt_sh = jnp.swapaxes(w_in, 1, 2).astype(BF16)
    wo_sh = w_out.astype(BF16)

    layer_par = []
    for l in range(DEPTH):
        layer_par.append((jnp.tile(q_norm[l], 2)[None, :], jnp.tile(k_norm[l], 2)[None, :],
                          jnp.repeat(b_s[l].T, HALF, axis=1)))

    wt_full = _exchange("gather_w0", _gather_rows((W_IN_SHARD,), (0,)), [wt_sh], [W_FULL[W_IN_SHARD]])[0]
    saved = []
    cur = xs
    for l in range(DEPTH):
        qw, kw, bmap = layer_par[l]
        more = l + 1 < DEPTH
        if more:
            h, proj, wo_full, wt_next = _norm_proj(
                cur, norm_g[l][None, :], wt_full,
                (_gather_rows((W_OUT_SHARD, W_IN_SHARD), (l, l + 1)), [wo_sh, wt_sh],
                 [W_FULL[W_OUT_SHARD], W_FULL[W_IN_SHARD]]))
        else:
            h, proj, wo_full = _norm_proj(cur, norm_g[l][None, :], wt_full,
                                          (_gather_rows((W_OUT_SHARD,), (l,)), [wo_sh], [W_FULL[W_OUT_SHARD]]))
        mix = _mixer_fwd(proj, qw, kw, sinks[l], slopes, w_s[l], bmap)[0]
        saved.append((cur, h, proj, mix, wt_full, wo_full))
        if more:
            cur = _out_proj(mix, wo_full, cur)
            wt_full = wt_next
        else:
            dx, dx_b, sq = _out_proj_loss(mix, wo_full, cur, tgt)
    loss = lax.psum(0.5 * jnp.sum(sq) / D_MODEL, MESH_AXES)

    wt_slots, wo_slots, g_small, g_norm = ([None] * DEPTH for _ in range(4))
    wt_slots_shape = _slots_shape(W_IN_SHARD, D_MODEL, BF16)
    dwt_waiting = None
    for l in reversed(range(DEPTH)):
        x_l, h, proj, mix, wt_l, wo_l = saved[l]
        qw, kw, bmap = layer_par[l]
        dmix = _dmix(dx_b, wo_l)
        dwo_part = _dw_out(mix, dx_b)
        res = _mixer_bwd(proj, dmix, qw, kw, sinks[l], slopes, w_s[l], bmap,
                         None if dwt_waiting is None else
                         (_scatter_rows((W_IN_SHARD,)), [dwt_waiting], [wt_slots_shape]))
        dproj, dqw, dkw, dsk, dws, dbs = res[:6]
        if dwt_waiting is not None:
            wt_slots[l + 1] = res[6]
        small_like = [dqw[0, :HALF], dkw[0, :HALF], dsk[0, :N_HEADS], dws, dbs]
        packed = _pack_rows(small_like)
        dwt_waiting, small_slots, wo_slots[l] = _dw_in(
            dproj, h, (_gather_slots_and_scatter_rows(W_OUT_SHARD), [packed, dwo_part],
                       [_slots_shape(*packed.shape, F32), _slots_shape(W_OUT_SHARD, D_MODEL, BF16)]))
        if l > 0:
            dx, dng, dx_b = _dh_norm_bwd(dproj, wt_l, x_l, dx, norm_g[l][None, :], True)
        else:
            dx, dng, wt_slots[0] = _dh_norm_bwd(dproj, wt_l, x_l, dx, norm_g[l][None, :], False,
                                                (_scatter_rows((W_IN_SHARD,)), [dwt_waiting], [wt_slots_shape]))
        g_small[l] = _unpack_rows(_sum_slots(small_slots, packed.shape[0]), small_like)
        g_norm[l] = dng[0]

    dng_all = _pack_rows([jnp.stack(g_norm)])
    dng_slots = _exchange("gather_dnorm", _gather_slots(), [dng_all], [_slots_shape(*dng_all.shape, F32)])[0]
    gr_norm = _unpack_rows(_sum_slots(dng_slots, dng_all.shape[0]), [norm_g])[0]
    gr_qn, gr_kn, gr_sk, gr_ws, gr_bs = (jnp.stack([g_small[l][i] for l in range(DEPTH)]) for i in range(5))

    def t(a):
        return jnp.swapaxes(a, 1, 2)

    from_slots = {1: _adamw_from_slots("adamw_w_in", wt_slots, t(w_in), t(m_w_in), t(v_w_in), ADAMW_ROWS),
                  7: _adamw_from_slots("adamw_w_out", wo_slots, w_out, m_w_out, v_w_out, ADAMW_ROWS)}
    from_slots[1] = [t(a) for a in from_slots[1]]

    grads = [gr_norm, None, gr_qn, gr_kn, gr_sk, gr_ws, gr_bs, None]
    weights = [norm_g, w_in, q_norm, k_norm, sinks, w_s, b_s, w_out]
    moms = [m_norm_g, m_w_in, m_q_norm, m_k_norm, m_sinks, m_w_s, m_b_s, m_w_out]
    vels = [v_norm_g, v_w_in, v_q_norm, v_k_norm, v_sinks, v_w_s, v_b_s, v_w_out]
    tiles = [None, None, None, None, None, 1024, None, None]
    names = ["norm_g", "w_in", "q_norm", "k_norm", "sinks", "w_s", "b_s", "w_out"]
    deltas, new_m, new_v = [], [], []
    for i, (nm, w, g, m, v, tr) in enumerate(zip(names, weights, grads, moms, vels, tiles)):
        if i in from_slots:
            grads[i], d, mo, vo = from_slots[i]
        else:
            d, mo, vo = _adamw("adamw_" + nm, w, g, m, v, tr)
        deltas.append(d)
        new_m.append(mo)
        new_v.append(vo)

    return (loss, dx[None], *grads, *deltas, *new_m, *new_v)
```

```python
import numpy as np
import jax
import jax.numpy as jnp
from jax import lax
from jax.experimental import pallas as pl
from jax.experimental.pallas import tpu as pltpu

F32 = jnp.float32
BF16 = jnp.bfloat16

D_MODEL = 2048
D_ATTN = 1024
D_IN = 5632
N_HEADS = 16
DEPTH = 4
BLK = 128
HALF = 64
RMS_EPS = 1e-6
C_Q, C_K, C_V, C_GA, C_ZU, C_ZV, C_GB = 0, 1024, 1280, 1536, 2560, 3584, 4608
NEG = -1e30
N_DEV = 8
W_IN_SHARD = D_IN // N_DEV
W_OUT_SHARD = D_MODEL // N_DEV
INV_SQRT2 = 0.7071067811865476
INV_SQRT_2PI = 0.3989422804014327

TM_RESIDENT = 256
TM_STREAM = 512
TM_WIDE = 1024
TN_PROJ = 512
DW_IN_ROWS = D_IN // 4
DW_IN_TOK = 2048
ADAMW_ROWS = 64
MID_STEP_16THS = 13

ADAM_LR = 0.001
ADAM_B1 = 0.9
ADAM_B2 = 0.999
ADAM_EPS = 1e-08
ADAM_WD = 0.01
ADAM_STEP = 10

NT_DIMS = (((1,), (1,)), ((), ()))
TN_DIMS = (((0,), (0,)), ((), ()))
MESH_AXES = ("x", "y", "c")


def _sigmoid(v):
    return 0.5 * jnp.tanh(0.5 * v) + 0.5


def _gelu_cdf(z):
    return 0.5 * (1.0 + lax.erf(z * INV_SQRT2))


def _gelu_grad(z, cdf):
    return cdf + z * (jnp.exp(-0.5 * z * z) * INV_SQRT_2PI)


def _lane_halves(rows):
    lane = lax.broadcasted_iota(jnp.int32, (rows, BLK), 1)
    return lane < HALF, lane >= HALF


def _half_masks():
    return {BLK: _lane_halves(BLK), 2 * BLK: _lane_halves(2 * BLK)}


def _half_sum(v, ones):
    h_a, h_b = ones[v.shape[0]]
    s_a = jnp.sum(jnp.where(h_a, v, 0.0), axis=-1, keepdims=True)
    s_b = jnp.sum(jnp.where(h_b, v, 0.0), axis=-1, keepdims=True)
    return jnp.where(h_a, s_a, s_b)


def _half_rms(v, w, ones):
    r = lax.rsqrt(_half_sum(v * v, ones) * (1.0 / HALF) + RMS_EPS)
    yhat = v * r
    return yhat, r, yhat * w


def _half_rms_bwd(dy, yhat, r, w, ones):
    dyh = dy * w
    c = _half_sum(dyh * yhat, ones) * (1.0 / HALF)
    return r * (dyh - yhat * c)


def _band_mask(n):
    t = lax.broadcasted_iota(jnp.int32, (BLK, 2 * BLK), 0)
    kk = lax.broadcasted_iota(jnp.int32, (BLK, 2 * BLK), 1)
    dist = t + BLK - kk
    first_key = jnp.where(n > 0, 0, BLK)
    ok = (dist >= 0) & (dist < BLK) & (kk >= first_key)
    return ok, dist.astype(F32)


def _alibi_table(n, slopes_ref, bias_scr):
    ok, distf = _band_mask(n)
    for hd in range(N_HEADS):
        bias_scr[BLK * hd:BLK * (hd + 1), :] = jnp.where(ok, -(slopes_ref[hd] * distf), NEG)


def _sink_col(sinks_ref, kt):
    return jnp.concatenate([jnp.full((BLK, 1), sinks_ref[8 * kt + i], F32) for i in range(8)], axis=0)


def _softmax_sink(s_scaled, bias, sink):
    s = s_scaled + bias
    m = jnp.maximum(jnp.max(s, axis=-1, keepdims=True), sink)
    p = jnp.exp(s - m)
    es = jnp.exp(sink - m)
    inv = 1.0 / (jnp.sum(p, axis=-1, keepdims=True) + es)
    return p * inv, es * inv


def _rows(i):
    return slice(BLK * i, BLK * (i + 1))


def _cols(base, j):
    return slice(base + BLK * j, base + BLK * (j + 1))


def _to_half(v, have, want):
    return v if have == want else pltpu.roll(v, HALF, 1)


def _tril_mask():
    row = lax.broadcasted_iota(jnp.int32, (BLK, BLK), 0)
    col = lax.broadcasted_iota(jnp.int32, (BLK, BLK), 1)
    return row >= col


def _tile(limit, s):
    t = min(limit, s)
    assert s % t == 0, (s, t)
    return t


def _mesh_place():
    x, y, c = lax.axis_index("x"), lax.axis_index("y"), lax.axis_index("c")
    return x, y, c, 4 * x + 2 * y + c


def _peer(x, y, c, k):
    px = 1 - x if k & 4 else x
    py = 1 - y if k & 2 else y
    pc = 1 - c if k & 1 else c
    return (px, py, pc), 4 * px + 2 * py + pc


class _Pushes:
    def __init__(self, n_arrays, src_view, dst_view):
        self.na = n_arrays
        self.src_view = src_view
        self.dst_view = dst_view

    def scratch(self):
        n = self.na * (N_DEV - 1)
        return [pltpu.SemaphoreType.DMA((n,)), pltpu.SemaphoreType.DMA((n,)), pltpu.SemaphoreType.DMA((self.na,))]

    def copies(self, src_refs, dst_refs, send_sems, recv_sems, local_sems):
        x, y, c, me = _mesh_place()
        cps = []
        for a in range(self.na):
            cps.append(pltpu.make_async_copy(self.src_view(a, src_refs[a], me), self.dst_view(a, dst_refs[a], me),
                                             local_sems.at[a]))
        for k in range(1, N_DEV):
            peer, pidx = _peer(x, y, c, k)
            for a in range(self.na):
                sem = a * (N_DEV - 1) + k - 1
                cps.append(pltpu.make_async_remote_copy(
                    src_ref=self.src_view(a, src_refs[a], pidx), dst_ref=self.dst_view(a, dst_refs[a], me),
                    send_sem=send_sems.at[sem], recv_sem=recv_sems.at[sem],
                    device_id=peer, device_id_type=pl.DeviceIdType.MESH))
        return cps

    def plan(self, src_refs, dst_refs, send_sems, recv_sems, local_sems):
        cps = self.copies(src_refs, dst_refs, send_sems, recv_sems, local_sems)
        return cps, [], [], [cp.wait for cp in cps]


class _TwoLevelGather:
    def __init__(self, shard_rows, layers):
        self.na = len(shard_rows)
        self.shard_rows = shard_rows
        self.layers = layers

    def scratch(self):
        n = self.na * (N_DEV - 1)
        return [pltpu.SemaphoreType.DMA((n,)), pltpu.SemaphoreType.DMA((n,)), pltpu.SemaphoreType.DMA((self.na,))]

    def plan(self, src_refs, dst_refs, send_sems, recv_sems, local_sems):
        x, y, c, _ = _mesh_place()
        sibling = (x, y, 1 - c)
        chips = [(1 - x, y), (x, 1 - y), (1 - x, 1 - y)]
        start, mid_wait, mid_start, final = [], [], [], []
        for a in range(self.na):
            r = self.shard_rows[a]
            src = src_refs[a].at[self.layers[a]]
            dst = dst_refs[a]

            def rows(px, py, pc, r=r, dst=dst):
                return dst.at[pl.ds(pl.multiple_of((4 * px + 2 * py + pc) * r, 64), r), :]

            def remote(k, s_ref, block, to, a=a, rows=rows):
                return pltpu.make_async_remote_copy(
                    src_ref=s_ref, dst_ref=rows(*block),
                    send_sem=send_sems.at[a * (N_DEV - 1) + k], recv_sem=recv_sems.at[a * (N_DEV - 1) + k],
                    device_id=to, device_id_type=pl.DeviceIdType.MESH)

            mine = pltpu.make_async_copy(src, rows(x, y, c), local_sems.at[a])
            own = [remote(0, src, (x, y, c), sibling)]
            own += [remote(1 + j, src, (x, y, c), (*chip, c)) for j, chip in enumerate(chips)]
            passed = [remote(4 + j, rows(*chip, c), (*chip, c), sibling) for j, chip in enumerate(chips)]
            start += [mine] + own
            mid_wait += own[1:]
            mid_start += passed
            final += [own[0].wait_recv] + [cp.wait_recv for cp in passed]
            final += [cp.wait_send for cp in own + passed] + [mine.wait]
        return start, mid_wait, mid_start, final


def _call(name, compute, grid, in_specs, out_specs, out_shape, scratch, args, push=None):
    sem = pltpu.CompilerParams(dimension_semantics=("arbitrary",) * len(grid))
    if push is None:
        return pl.pallas_call(compute, name=name, grid=grid, in_specs=in_specs, out_specs=out_specs,
                              out_shape=out_shape, scratch_shapes=scratch, compiler_params=sem)(*args)
    pushes, srcs, xshapes = push
    n_in, n_out, n_scr, na = len(args), len(out_shape), len(scratch), pushes.na
    hbm = pl.BlockSpec(memory_space=pltpu.HBM)

    def body(*refs):
        ins, refs = refs[:n_in], refs[n_in:]
        xin, refs = refs[:na], refs[na:]
        outs, refs = refs[:n_out], refs[n_out:]
        xout, refs = refs[:na], refs[na:]
        scr, sems = refs[:n_scr], refs[n_scr:]
        start, mid_wait, mid_start, final = pushes.plan(xin, xout, *sems)
        first = pl.program_id(0) == 0
        middle = pl.program_id(0) == (grid[0] * MID_STEP_16THS) // 16
        last = pl.program_id(0) == grid[0] - 1
        for d in range(1, len(grid)):
            first = first & (pl.program_id(d) == 0)
            middle = middle & (pl.program_id(d) == 0)
            last = last & (pl.program_id(d) == grid[d] - 1)

        @pl.when(first)
        def _():
            for cp in start:
                cp.start()

        if mid_start:
            @pl.when(middle)
            def _():
                for cp in mid_wait:
                    cp.wait_recv()
                for cp in mid_start:
                    cp.start()

        compute(*ins, *outs, *scr)

        @pl.when(last)
        def _():
            for wait in final:
                wait()

    return pl.pallas_call(
        body, name=name, grid=grid,
        in_specs=list(in_specs) + [hbm] * na, out_specs=list(out_specs) + [hbm] * na,
        out_shape=list(out_shape) + list(xshapes),
        scratch_shapes=list(scratch) + pushes.scratch(), compiler_params=sem)(*args, *srcs)


def _exchange(name, pushes, srcs, out_shapes):
    na = pushes.na
    hbm = pl.BlockSpec(memory_space=pltpu.HBM)

    def body(*refs):
        start, mid_wait, mid_start, final = pushes.plan(refs[:na], refs[na:2 * na], *refs[2 * na:])
        for cp in start:
            cp.start()
        for cp in mid_wait:
            cp.wait_recv()
        for cp in mid_start:
            cp.start()
        for wait in final:
            wait()

    return pl.pallas_call(body, name=name, in_specs=[hbm] * na, out_specs=[hbm] * na, out_shape=out_shapes,
                          scratch_shapes=pushes.scratch())(*srcs)


def _gather_rows(shard_rows, layers):
    return _TwoLevelGather(shard_rows, layers)


def _scatter_rows(shard_rows):
    def src_view(a, ref, idx):
        r = shard_rows[a]
        return ref.at[pl.ds(pl.multiple_of(idx * r, 64), r), :]

    def dst_view(a, ref, idx):
        return ref.at[idx]

    return _Pushes(len(shard_rows), src_view, dst_view)


def _gather_slots():
    return _Pushes(1, lambda a, ref, idx: ref, lambda a, ref, idx: ref.at[idx])


def _gather_slots_and_scatter_rows(rows):
    def src_view(a, ref, idx):
        return ref if a == 0 else ref.at[pl.ds(pl.multiple_of(idx * rows, 64), rows), :]

    return _Pushes(2, src_view, lambda a, ref, idx: ref.at[idx])


W_FULL = {W_IN_SHARD: jax.ShapeDtypeStruct((D_IN, D_MODEL), BF16),
          W_OUT_SHARD: jax.ShapeDtypeStruct((D_MODEL, D_MODEL), BF16)}


def _slots_shape(rows, cols, dtype):
    return jax.ShapeDtypeStruct((N_DEV, rows, cols), dtype)


def _sum_slots(slots, tr):
    _, r, c = slots.shape

    def compute(s_ref, o_ref):
        tot = s_ref[0].astype(F32)
        for d in range(1, N_DEV):
            tot = tot + s_ref[d].astype(F32)
        o_ref[...] = tot

    return _call("sum_slots", compute, (r // tr,),
                 [pl.BlockSpec((N_DEV, tr, c), lambda i: (0, i, 0))], [pl.BlockSpec((tr, c), lambda i: (i, 0))],
                 [jax.ShapeDtypeStruct((r, c), F32)], [], [slots])[0]


def _resident(shape):
    return pl.BlockSpec(shape, lambda *_: (0,) * len(shape), pipeline_mode=pl.Buffered(1))


def _norm_proj(x, g, w_t, push=None):
    s = x.shape[0]
    tm = _tile(TM_RESIDENT, s)

    def compute(x_ref, g_ref, w_ref, h_ref, p_ref):
        xf = x_ref[...]
        r = lax.rsqrt(jnp.mean(xf * xf, axis=-1, keepdims=True) + RMS_EPS)
        h = ((xf * r) * g_ref[...]).astype(BF16)
        h_ref[...] = h
        for j in range(D_IN // TN_PROJ):
            cols = slice(j * TN_PROJ, (j + 1) * TN_PROJ)
            p_ref[:, cols] = lax.dot_general(h, w_ref[cols, :], NT_DIMS, preferred_element_type=F32)

    return _call(
        "norm_proj", compute, (s // tm,),
        [pl.BlockSpec((tm, D_MODEL), lambda i: (i, 0)), pl.BlockSpec((1, D_MODEL), lambda i: (0, 0)),
         _resident((D_IN, D_MODEL))],
        [pl.BlockSpec((tm, D_MODEL), lambda i: (i, 0)), pl.BlockSpec((tm, D_IN), lambda i: (i, 0))],
        [jax.ShapeDtypeStruct((s, D_MODEL), BF16), jax.ShapeDtypeStruct((s, D_IN), F32)],
        [], [x, g, w_t], push)


def _sgu_weights(ws_ref, wtril_scr, wtril_t_scr=None):
    tril = _tril_mask()
    for hd in range(N_HEADS):
        w = jnp.where(tril, ws_ref[hd], 0.0)
        wtril_scr[BLK * hd:BLK * (hd + 1), :] = w.astype(BF16)
        if wtril_t_scr is not None:
            wtril_t_scr[hd // 2, :, BLK * (hd % 2):BLK * (hd % 2 + 1)] = w.T.astype(BF16)


def _kv_band(kt, p_ref, kvp_ref, kw_v, ones):
    kband = jnp.concatenate([kvp_ref[:, _cols(0, kt)], p_ref[:, _cols(C_K, kt)]], axis=0)
    kyhat, kr, kn = _half_rms(kband, kw_v, ones)
    vband = jnp.concatenate([kvp_ref[:, _cols(256, kt)], p_ref[:, _cols(C_V, kt)]], axis=0)
    return kyhat, kr, (kn * 0.125).astype(BF16), vband.astype(BF16)


def _stack_heads(tiles, halves):
    parts = []
    for tt, tile in enumerate(tiles):
        for qh in range(2):
            parts.append(_to_half(jnp.where(halves[qh], tile, 0.0), qh, tt // 2).astype(BF16))
    return jnp.concatenate(parts, axis=0)


def _unstack_heads(stacked, tt, h_a):
    return jnp.where(h_a, _to_half(stacked[_rows(2 * tt)], tt // 2, 0), _to_half(stacked[_rows(2 * tt + 1)], tt // 2, 1))


def _mixer_fwd(proj, qw, kw, sinks, slopes, w_s, bmap, push=None):
    s = proj.shape[0]
    nb = s // BLK

    def compute(sinks_ref, slopes_ref, p_ref, kvp_ref, qw_ref, kw_ref, ws_ref, bmap_ref, mix_ref,
                wtril_scr, bias_scr):
        _mixer_tables(pl.program_id(0), slopes_ref, ws_ref, wtril_scr, bias_scr)
        _mixer_fwd_block(sinks_ref, p_ref, kvp_ref, qw_ref, kw_ref, bmap_ref, mix_ref, wtril_scr, bias_scr)

    smem = pl.BlockSpec(memory_space=pltpu.SMEM)
    return _call(
        "mixer_fwd", compute, (nb,),
        [smem, smem,
         pl.BlockSpec((BLK, D_IN), lambda n: (n, 0)),
         pl.BlockSpec((BLK, 512), lambda n: (jnp.maximum(n - 1, 0), 2)),
         pl.BlockSpec((1, BLK), lambda n: (0, 0)),
         pl.BlockSpec((1, BLK), lambda n: (0, 0)),
         pl.BlockSpec((N_HEADS, BLK, BLK), lambda n: (0, 0, 0)),
         pl.BlockSpec((BLK, D_ATTN), lambda n: (0, 0))],
        [pl.BlockSpec((BLK, D_MODEL), lambda n: (n, 0))],
        [jax.ShapeDtypeStruct((s, D_MODEL), BF16)],
        [pltpu.VMEM((N_HEADS * BLK, BLK), BF16), pltpu.VMEM((N_HEADS * BLK, 2 * BLK), F32)],
        [sinks, slopes, proj, proj, qw, kw, w_s, bmap], push)


def _mixer_tables(n, slopes_ref, ws_ref, wtril_scr, bias_scr, wtril_t_scr=None):
    @pl.when(n == 0)
    def _():
        _sgu_weights(ws_ref, wtril_scr, wtril_t_scr)

    @pl.when(n <= 1)
    def _():
        _alibi_table(n, slopes_ref, bias_scr)


def _mixer_fwd_block(sinks_ref, p_ref, kvp_ref, qw_ref, kw_ref, bmap_ref, mix_ref, wtril_scr, bias_scr):
    h_a, h_b = _lane_halves(BLK)
    ones = _half_masks()
    halves = (h_a, h_b)
    qw_v = qw_ref[...]
    kw_v = kw_ref[...]
    bands = [_kv_band(kt, p_ref, kvp_ref, kw_v, ones) for kt in range(2)]
    sc = []
    for kt in range(2):
        qn = [_half_rms(p_ref[:, _cols(C_Q, 4 * kt + tt)], qw_v, ones)[2] for tt in range(4)]
        sc.append(lax.dot_general(_stack_heads(qn, halves), bands[kt][2], NT_DIMS, preferred_element_type=F32))
    zu, mixed = [], []
    for j in range(8):
        zu_pre = p_ref[:, _cols(C_ZU, j)]
        zv_pre = p_ref[:, _cols(C_ZV, j)]
        zu.append(zu_pre * _gelu_cdf(zu_pre))
        zvb = (zv_pre * _gelu_cdf(zv_pre)).astype(BF16)
        mixed.append(jnp.dot(wtril_scr[2 * BLK * j:2 * BLK * (j + 1), :], zvb, preferred_element_type=F32))
    o = []
    for kt in range(2):
        p, _ = _softmax_sink(sc[kt], bias_scr[8 * BLK * kt:8 * BLK * (kt + 1), :], _sink_col(sinks_ref, kt))
        o.append(jnp.dot(p.astype(BF16), bands[kt][3], preferred_element_type=F32))
    for j in range(8):
        gb = p_ref[:, _cols(C_GB, j)]
        mx = jnp.where(h_a, mixed[j][0:BLK], mixed[j][BLK:2 * BLK]) + bmap_ref[:, _cols(0, j)]
        mix_ref[:, _cols(D_ATTN, j)] = ((zu[j] * mx) * (gb * _sigmoid(gb))).astype(BF16)
    for kt in range(2):
        for tt in range(4):
            j = 4 * kt + tt
            ga = p_ref[:, _cols(C_GA, j)]
            mix_ref[:, _cols(0, j)] = (_unstack_heads(o[kt], tt, h_a) * (ga * _sigmoid(ga))).astype(BF16)


def _out_proj(mix, w_o, x):
    s = x.shape[0]
    tm = _tile(TM_WIDE, s)

    def compute(m_ref, w_ref, x_ref, o_ref):
        o_ref[...] = x_ref[...] + jnp.dot(m_ref[...], w_ref[...], preferred_element_type=F32)

    return _call(
        "out_proj", compute, (s // tm,),
        [pl.BlockSpec((tm, D_MODEL), lambda i: (i, 0)), _resident((D_MODEL, D_MODEL)),
         pl.BlockSpec((tm, D_MODEL), lambda i: (i, 0))],
        [pl.BlockSpec((tm, D_MODEL), lambda i: (i, 0))],
        [jax.ShapeDtypeStruct((s, D_MODEL), F32)], [], [mix, w_o, x])[0]


def _out_proj_loss(mix, w_o, x, tgt):
    s = x.shape[0]
    tm = _tile(TM_STREAM, s)

    def compute(m_ref, w_ref, x_ref, t_ref, dy_ref, dyb_ref, sq_ref):
        @pl.when(pl.program_id(0) == 0)
        def _():
            sq_ref[...] = jnp.zeros_like(sq_ref)

        y = x_ref[...] + jnp.dot(m_ref[...], w_ref[...], preferred_element_type=F32)
        e = y - t_ref[...]
        dy = e * (1.0 / D_MODEL)
        dy_ref[...] = dy
        dyb_ref[...] = dy.astype(BF16)
        sq_ref[...] += jnp.sum(e * e, axis=0, keepdims=True)

    tok = pl.BlockSpec((tm, D_MODEL), lambda i: (i, 0))
    return _call(
        "out_proj_loss", compute, (s // tm,),
        [tok, _resident((D_MODEL, D_MODEL)), tok, tok],
        [tok, tok, pl.BlockSpec((1, D_MODEL), lambda i: (0, 0))],
        [jax.ShapeDtypeStruct((s, D_MODEL), F32), jax.ShapeDtypeStruct((s, D_MODEL), BF16),
         jax.ShapeDtypeStruct((1, D_MODEL), F32)], [], [mix, w_o, x, tgt])


def _dmix(dx, w_o):
    s = dx.shape[0]
    tm = _tile(TM_WIDE, s)

    def compute(d_ref, w_ref, o_ref):
        o_ref[...] = lax.dot_general(d_ref[...], w_ref[...], NT_DIMS, preferred_element_type=F32)

    return _call(
        "dmix", compute, (s // tm,),
        [pl.BlockSpec((tm, D_MODEL), lambda i: (i, 0)), _resident((D_MODEL, D_MODEL))],
        [pl.BlockSpec((tm, D_MODEL), lambda i: (i, 0))],
        [jax.ShapeDtypeStruct((s, D_MODEL), F32)], [], [dx, w_o])[0]


def _dw_out(mix, dx):
    s = dx.shape[0]
    tk = _tile(TM_WIDE, s)
    nk = s // tk

    def compute(m_ref, d_ref, o_ref, acc):
        k = pl.program_id(0)

        @pl.when(k == 0)
        def _():
            acc[...] = jnp.zeros_like(acc)

        acc[...] += lax.dot_general(m_ref[...], d_ref[...], TN_DIMS, preferred_element_type=F32)

        @pl.when(k == nk - 1)
        def _():
            o_ref[...] = acc[...].astype(BF16)

    return _call(
        "dw_out", compute, (nk,),
        [pl.BlockSpec((tk, D_MODEL), lambda k: (k, 0)), pl.BlockSpec((tk, D_MODEL), lambda k: (k, 0))],
        [pl.BlockSpec((D_MODEL, D_MODEL), lambda k: (0, 0))],
        [jax.ShapeDtypeStruct((D_MODEL, D_MODEL), BF16)],
        [pltpu.VMEM((D_MODEL, D_MODEL), F32)], [mix, dx])[0]


def _mixer_bwd(proj, dmix, qw, kw, sinks, slopes, w_s, bmap, push=None):
    s = proj.shape[0]
    nb = s // BLK

    def compute(sinks_ref, slopes_ref, p_ref, kvp_ref, dm_ref, qw_ref, kw_ref, ws_ref, bmap_ref,
                dp_ref, dqw_ref, dkw_ref, dsk_ref, dws_ref, dbs_ref,
                pend, accq, acck, accs, accb, wtril_scr, wtril_t_scr, bias_scr):
        n = pl.program_id(0)
        h_a, h_b = _lane_halves(BLK)
        ones = _half_masks()
        halves = (h_a, h_b)
        lane = lax.broadcasted_iota(jnp.int32, (BLK, BLK), 1)

        @pl.when(n == 0)
        def _():
            accq[...] = jnp.zeros_like(accq)
            acck[...] = jnp.zeros_like(acck)
            accs[...] = jnp.zeros_like(accs)
            accb[...] = jnp.zeros_like(accb)
            dws_ref[...] = jnp.zeros_like(dws_ref)
            pend[...] = jnp.zeros_like(pend)
            _sgu_weights(ws_ref, wtril_scr, wtril_t_scr)

        @pl.when(n <= 1)
        def _():
            _alibi_table(n, slopes_ref, bias_scr)

        def emit_pending():
            dp_ref[:, 0:C_K] = pend[:, 0:C_K].astype(BF16)
            dp_ref[:, C_GA:D_IN] = pend[:, C_GA:D_IN].astype(BF16)

        @pl.when(n < nb)
        def _():
            emit_pending()
            qw_v = qw_ref[...]
            kw_v = kw_ref[...]
            bands = [_kv_band(kt, p_ref, kvp_ref, kw_v, ones) for kt in range(2)]
            tiles, qst, dost, sc, dpm = [], [], [], [], []
            for kt in range(2):
                qn, d_o, tl = [], [], []
                for tt in range(4):
                    j = 4 * kt + tt
                    qyhat, qr, qn_t = _half_rms(p_ref[:, _cols(C_Q, j)], qw_v, ones)
                    ga = p_ref[:, _cols(C_GA, j)]
                    sg = _sigmoid(ga)
                    dma = dm_ref[:, _cols(0, j)]
                    qn.append(qn_t)
                    d_o.append(dma * (ga * sg))
                    tl.append((qyhat, qr, dma * (sg * (1.0 + ga * (1.0 - sg)))))
                tiles.append(tl)
                qst.append(_stack_heads(qn, halves))
                dost.append(_stack_heads(d_o, halves))
                sc.append(lax.dot_general(qst[kt], bands[kt][2], NT_DIMS, preferred_element_type=F32))
                dpm.append(lax.dot_general(dost[kt], bands[kt][3], NT_DIMS, preferred_element_type=F32))
            sgu = []
            for j in range(8):
                zu_pre = p_ref[:, _cols(C_ZU, j)]
                zv_pre = p_ref[:, _cols(C_ZV, j)]
                cu = _gelu_cdf(zu_pre)
                cv = _gelu_cdf(zv_pre)
                zvb = (zv_pre * cv).astype(BF16)
                sgu.append((zu_pre * cu, zvb, _gelu_grad(zu_pre, cu), _gelu_grad(zv_pre, cv),
                            jnp.dot(wtril_scr[2 * BLK * j:2 * BLK * (j + 1), :], zvb, preferred_element_type=F32)))
            dsink = jnp.zeros((BLK, BLK), F32)
            pst, dqkst = [], []
            for kt in range(2):
                p, p_sink = _softmax_sink(sc[kt], bias_scr[8 * BLK * kt:8 * BLK * (kt + 1), :],
                                          _sink_col(sinks_ref, kt))
                dsum = jnp.sum(p * dpm[kt], axis=-1, keepdims=True)
                dsink_col = -(p_sink * dsum)
                for i in range(8):
                    dsink = dsink + jnp.where(lane == 8 * kt + i, dsink_col[_rows(i)], 0.0)
                pst.append(p.astype(BF16))
                dqkst.append((p * (dpm[kt] - dsum)).astype(BF16))
            o, dqn_all, dvb, dkn = [], [], [], []
            for kt in range(2):
                o.append(jnp.dot(pst[kt], bands[kt][3], preferred_element_type=F32))
                dqn_all.append(jnp.dot(dqkst[kt], bands[kt][2], preferred_element_type=F32))
                dvb.append(lax.dot_general(pst[kt], dost[kt], TN_DIMS, preferred_element_type=F32))
                dkn.append(0.125 * lax.dot_general(dqkst[kt], qst[kt], TN_DIMS, preferred_element_type=F32))
            dms = []
            for j in range(8):
                zu, zvb, gu, gv, m_ab = sgu[j]
                gb = p_ref[:, _cols(C_GB, j)]
                dmb = dm_ref[:, _cols(D_ATTN, j)]
                mixed = jnp.where(h_a, m_ab[0:BLK], m_ab[BLK:2 * BLK]) + bmap_ref[:, _cols(0, j)]
                sgb = _sigmoid(gb)
                dgate = dmb * (gb * sgb)
                pend[:, _cols(C_ZU, j)] = (dgate * mixed) * gu
                pend[:, _cols(C_GB, j)] = (dmb * (zu * mixed)) * (sgb * (1.0 + gb * (1.0 - sgb)))
                dmixed = dgate * zu
                accb[:, _cols(0, j)] += dmixed
                dms.append(jnp.concatenate([jnp.where(h_a, dmixed, 0.0).astype(BF16),
                                            jnp.where(h_b, dmixed, 0.0).astype(BF16)], axis=0))
            dzv = []
            for j in range(8):
                dzv.append(jnp.dot(wtril_t_scr[j], dms[j], preferred_element_type=F32))
                dw_ab = lax.dot_general(dms[j], sgu[j][1], NT_DIMS, preferred_element_type=F32)
                dws_ref[2 * j] += dw_ab[0:BLK]
                dws_ref[2 * j + 1] += dw_ab[BLK:2 * BLK]
            dq_w = jnp.zeros((BLK, BLK), F32)
            for kt in range(2):
                for tt in range(4):
                    j = 4 * kt + tt
                    qyhat, qr, dsilu = tiles[kt][tt]
                    dqn = _unstack_heads(dqn_all[kt], tt, h_a)
                    pend[:, _cols(C_GA, j)] = _unstack_heads(o[kt], tt, h_a) * dsilu
                    pend[:, _cols(C_Q, j)] = _half_rms_bwd(dqn, qyhat, qr, qw_v, ones)
                    dq_w = dq_w + dqn * qyhat
            dk_w = jnp.zeros((BLK, BLK), F32)
            for kt in range(2):
                kyhat, kr = bands[kt][0], bands[kt][1]
                dk = _half_rms_bwd(dkn[kt], kyhat, kr, kw_v, ones)
                dkw_part = dkn[kt] * kyhat
                dk_w = dk_w + (dkw_part[0:BLK] + dkw_part[BLK:2 * BLK])
                dv = dvb[kt]
                dp_ref[:, _cols(C_K, kt)] = (pend[:, _cols(C_K, kt)] + dk[0:BLK]).astype(BF16)
                dp_ref[:, _cols(C_V, kt)] = (pend[:, _cols(C_V, kt)] + dv[0:BLK]).astype(BF16)
                pend[:, _cols(C_K, kt)] = dk[BLK:2 * BLK]
                pend[:, _cols(C_V, kt)] = dv[BLK:2 * BLK]
            accq[...] += dq_w
            acck[...] += dk_w
            accs[...] += dsink
            for j in range(8):
                pend[:, _cols(C_ZV, j)] = dzv[j] * sgu[j][3]

        @pl.when(n == nb)
        def _():
            tril = _tril_mask()
            emit_pending()
            dp_ref[:, C_K:C_GA] = pend[:, C_K:C_GA].astype(BF16)
            aq = accq[...]
            ak = acck[...]
            dqw_ref[...] = jnp.sum(aq + pltpu.roll(aq, HALF, 1), axis=0, keepdims=True)
            dkw_ref[...] = jnp.sum(ak + pltpu.roll(ak, HALF, 1), axis=0, keepdims=True)
            dsk_ref[...] = jnp.sum(accs[...], axis=0, keepdims=True)
            for hd in range(N_HEADS):
                dws_ref[hd] = jnp.where(tril, dws_ref[hd], 0.0)
            hrow = lax.broadcasted_iota(jnp.int32, (N_HEADS, D_ATTN), 0)
            hcol = lax.broadcasted_iota(jnp.int32, (N_HEADS, D_ATTN), 1)
            sel = jnp.where((hcol >= hrow * HALF) & (hcol < (hrow + 1) * HALF), 1.0, 0.0).astype(BF16)
            rem = accb[...]
            tot = jnp.zeros((N_HEADS, BLK), F32)
            for _ in range(3):
                part = rem.astype(BF16)
                tot = tot + lax.dot_general(sel, part, NT_DIMS, preferred_element_type=F32)
                rem = rem - part.astype(F32)
            dbs_ref[...] = tot

    smem = pl.BlockSpec(memory_space=pltpu.SMEM)
    last = nb - 1
    tile_f32 = pltpu.VMEM((BLK, BLK), F32)
    return _call(
        "mixer_bwd", compute, (nb + 1,),
        [smem, smem,
         pl.BlockSpec((BLK, D_IN), lambda n: (jnp.minimum(n, last), 0)),
         pl.BlockSpec((BLK, 512), lambda n: (jnp.maximum(jnp.minimum(n, last) - 1, 0), 2)),
         pl.BlockSpec((BLK, D_MODEL), lambda n: (jnp.minimum(n, last), 0)),
         pl.BlockSpec((1, BLK), lambda n: (0, 0)),
         pl.BlockSpec((1, BLK), lambda n: (0, 0)),
         pl.BlockSpec((N_HEADS, BLK, BLK), lambda n: (0, 0, 0)),
         pl.BlockSpec((BLK, D_ATTN), lambda n: (0, 0))],
        [pl.BlockSpec((BLK, D_IN), lambda n: (jnp.maximum(n - 1, 0), 0)),
         pl.BlockSpec((1, BLK), lambda n: (0, 0)),
         pl.BlockSpec((1, BLK), lambda n: (0, 0)),
         pl.BlockSpec((1, BLK), lambda n: (0, 0)),
         pl.BlockSpec((N_HEADS, BLK, BLK), lambda n: (0, 0, 0)),
         pl.BlockSpec((N_HEADS, BLK), lambda n: (0, 0))],
        [jax.ShapeDtypeStruct((s, D_IN), BF16),
         jax.ShapeDtypeStruct((1, BLK), F32),
         jax.ShapeDtypeStruct((1, BLK), F32),
         jax.ShapeDtypeStruct((1, BLK), F32),
         jax.ShapeDtypeStruct((N_HEADS, BLK, BLK), F32),
         jax.ShapeDtypeStruct((N_HEADS, BLK), F32)],
        [pltpu.VMEM((BLK, D_IN), F32), tile_f32, tile_f32, tile_f32, pltpu.VMEM((BLK, D_ATTN), F32),
         pltpu.VMEM((N_HEADS * BLK, BLK), BF16), pltpu.VMEM((N_HEADS // 2, BLK, 2 * BLK), BF16),
         pltpu.VMEM((N_HEADS * BLK, 2 * BLK), F32)],
        [sinks, slopes, proj, proj, dmix, qw, kw, w_s, bmap], push)


def _dh_norm_bwd(dproj, w_t, x, dx_out, g, bf16_copy, push=None):
    s = x.shape[0]
    tm = _tile(TM_RESIDENT, s)

    def compute(dp_ref, w_ref, x_ref, dxo_ref, g_ref, dx_ref, dg_ref, *dxb_ref):
        @pl.when(pl.program_id(0) == 0)
        def _():
            dg_ref[...] = jnp.zeros_like(dg_ref)

        dh = jnp.dot(dp_ref[...], w_ref[...], preferred_element_type=F32)
        xf = x_ref[...]
        r = lax.rsqrt(jnp.mean(xf * xf, axis=-1, keepdims=True) + RMS_EPS)
        yhat = xf * r
        dyh = dh * g_ref[...]
        c = jnp.mean(dyh * yhat, axis=-1, keepdims=True)
        dx = dxo_ref[...] + r * (dyh - yhat * c)
        dx_ref[...] = dx
        if bf16_copy:
            dxb_ref[0][...] = dx.astype(BF16)
        dg_ref[...] += jnp.sum(dh * yhat, axis=0, keepdims=True)

    tok = pl.BlockSpec((tm, D_MODEL), lambda i: (i, 0))
    return _call(
        "dh_norm_bwd", compute, (s // tm,),
        [pl.BlockSpec((tm, D_IN), lambda i: (i, 0)), _resident((D_IN, D_MODEL)), tok, tok,
         pl.BlockSpec((1, D_MODEL), lambda i: (0, 0))],
        [tok, pl.BlockSpec((1, D_MODEL), lambda i: (0, 0))] + [tok] * bf16_copy,
        [jax.ShapeDtypeStruct((s, D_MODEL), F32), jax.ShapeDtypeStruct((1, D_MODEL), F32)]
        + [jax.ShapeDtypeStruct((s, D_MODEL), BF16)] * bf16_copy,
        [], [dproj, w_t, x, dx_out, g], push)


def _dw_in(dproj, h, push=None):
    s = h.shape[0]
    tk = _tile(DW_IN_TOK, s)
    nk = s // tk

    def compute(dp_ref, h_ref, o_ref, acc):
        k = pl.program_id(1)

        @pl.when(k == 0)
        def _():
            acc[...] = jnp.zeros_like(acc)

        acc[...] += lax.dot_general(dp_ref[...], h_ref[...], TN_DIMS, preferred_element_type=F32)

        @pl.when(k == nk - 1)
        def _():
            o_ref[...] = acc[...].astype(BF16)

    return _call(
        "dw_in", compute, (D_IN // DW_IN_ROWS, nk),
        [pl.BlockSpec((tk, DW_IN_ROWS), lambda j, k: (k, j)), pl.BlockSpec((tk, D_MODEL), lambda j, k: (k, 0))],
        [pl.BlockSpec((DW_IN_ROWS, D_MODEL), lambda j, k: (j, 0))],
        [jax.ShapeDtypeStruct((D_IN, D_MODEL), BF16)],
        [pltpu.VMEM((DW_IN_ROWS, D_MODEL), F32)], [dproj, h], push)


def _adamw_math(w, g, m, v):
    m_new = ADAM_B1 * m + (1.0 - ADAM_B1) * g
    v_new = ADAM_B2 * v + (1.0 - ADAM_B2) * jnp.square(g)
    m_hat = m_new / (1.0 - ADAM_B1 ** ADAM_STEP)
    v_hat = v_new / (1.0 - ADAM_B2 ** ADAM_STEP)
    return -ADAM_LR * (m_hat / (jnp.sqrt(v_hat) + ADAM_EPS) + ADAM_WD * w), m_new, v_new


def _adamw(name, w, g, m, v, tr=None):
    shape = w.shape
    c = shape[-1]
    flat = [a.reshape(-1, c) for a in (w, g, m, v)]
    r = flat[0].shape[0]
    tr = r if tr is None else tr

    def compute(w_ref, g_ref, m_ref, v_ref, d_ref, mo_ref, vo_ref):
        d_ref[...], mo_ref[...], vo_ref[...] = _adamw_math(w_ref[...], g_ref[...], m_ref[...], v_ref[...])

    spec = pl.BlockSpec((tr, c), lambda i: (i, 0))
    outs = _call(name, compute, (r // tr,), [spec] * 4, [spec] * 3, [jax.ShapeDtypeStruct((r, c), F32)] * 3,
                 [], flat)
    return [o.reshape(shape) for o in outs]


def _adamw_from_slots(name, slots, w, m, v, tr):
    n_layers, r, c = w.shape
    nt = r // tr

    def compute(*refs):
        s_refs, (w_ref, m_ref, v_ref), (g_ref, d_ref, mo_ref, vo_ref) = refs[:n_layers], refs[n_layers:n_layers + 3], \
            refs[n_layers + 3:]
        for k in range(n_layers):
            @pl.when(pl.program_id(0) == k)
            def _(k=k):
                g = s_refs[k][0].astype(F32)
                for dev in range(1, N_DEV):
                    g = g + s_refs[k][dev].astype(F32)
                g_ref[0] = g
                d_ref[0], mo_ref[0], vo_ref[0] = _adamw_math(w_ref[0], g, m_ref[0], v_ref[0])

    def slots_spec(k):
        return pl.BlockSpec((N_DEV, tr, c),
                            lambda l, i: (0, jnp.where(l == k, i, jnp.where(l < k, 0, nt - 1)), 0))

    tile = pl.BlockSpec((1, tr, c), lambda l, i: (l, i, 0))
    return _call(name, compute, (n_layers, nt), [slots_spec(k) for k in range(n_layers)] + [tile] * 3, [tile] * 4,
                 [jax.ShapeDtypeStruct((n_layers, r, c), F32)] * 4, [], list(slots) + [w, m, v])


def _pack_rows(parts):
    rows = []
    for a in parts:
        flat = a.reshape(-1)
        n = -(-flat.shape[0] // (8 * BLK)) * 8
        rows.append(jnp.pad(flat, (0, n * BLK - flat.shape[0])).reshape(n, BLK))
    return jnp.concatenate(rows, axis=0)


def _unpack_rows(packed, like):
    out = []
    row = 0
    for a in like:
        n = -(-a.size // (8 * BLK)) * 8
        out.append(packed[row:row + n].reshape(-1)[:a.size].reshape(a.shape))
        row += n
    return out


def kernel(x, norm_g, w_in, q_norm, k_norm, sinks, w_s, b_s, w_out, loss_target, m_norm_g, m_w_in, m_q_norm, m_k_norm, m_sinks, m_w_s, m_b_s, m_w_out, v_norm_g, v_w_in, v_q_norm, v_k_norm, v_sinks, v_w_s, v_b_s, v_w_out):
    xs = x[0]
    tgt = loss_target[0]
    slopes = jnp.asarray(2.0 ** (-8.0 * np.arange(1, N_HEADS + 1) / N_HEADS), dtype=F32)
    wt_sh = jnp.swapaxes(w_in, 1, 2).astype(BF16)
    wo_sh = w_out.astype(BF16)

    layer_par = []
    for l in range(DEPTH):
        layer_par.append((jnp.tile(q_norm[l], 2)[None, :], jnp.tile(k_norm[l], 2)[None, :],
                          jnp.repeat(b_s[l].T, HALF, axis=1)))

    wt_full = _exchange("gather_w0", _gather_rows((W_IN_SHARD,), (0,)), [wt_sh], [W_FULL[W_IN_SHARD]])[0]
    saved = []
    cur = xs
    for l in range(DEPTH):
        qw, kw, bmap = layer_par[l]
        more = l + 1 < DEPTH
        if more:
            h, proj, wo_full, wt_next = _norm_proj(
                cur, norm_g[l][None, :], wt_full,
                (_gather_rows((W_OUT_SHARD, W_IN_SHARD), (l, l + 1)), [wo_sh, wt_sh],
                 [W_FULL[W_OUT_SHARD], W_FULL[W_IN_SHARD]]))
        else:
            h, proj, wo_full = _norm_proj(cur, norm_g[l][None, :], wt_full,
                                          (_gather_rows((W_OUT_SHARD,), (l,)), [wo_sh], [W_FULL[W_OUT_SHARD]]))
        mix = _mixer_fwd(proj, qw, kw, sinks[l], slopes, w_s[l], bmap)[0]
        saved.append((cur, h, proj, mix, wt_full, wo_full))
        if more:
            cur = _out_proj(mix, wo_full, cur)
            wt_full = wt_next
        else:
            dx, dx_b, sq = _out_proj_loss(mix, wo_full, cur, tgt)
    loss = lax.psum(0.5 * jnp.sum(sq) / D_MODEL, MESH_AXES)

    wt_slots, wo_slots, g_small, g_norm = ([None] * DEPTH for _ in range(4))
    wt_slots_shape = _slots_shape(W_IN_SHARD, D_MODEL, BF16)
    dwt_waiting = None
    for l in reversed(range(DEPTH)):
        x_l, h, proj, mix, wt_l, wo_l = saved[l]
        qw, kw, bmap = layer_par[l]
        dmix = _dmix(dx_b, wo_l)
        dwo_part = _dw_out(mix, dx_b)
        res = _mixer_bwd(proj, dmix, qw, kw, sinks[l], slopes, w_s[l], bmap,
                         None if dwt_waiting is None else
                         (_scatter_rows((W_IN_SHARD,)), [dwt_waiting], [wt_slots_shape]))
        dproj, dqw, dkw, dsk, dws, dbs = res[:6]
        if dwt_waiting is not None:
            wt_slots[l + 1] = res[6]
        small_like = [dqw[0, :HALF], dkw[0, :HALF], dsk[0, :N_HEADS], dws, dbs]
        packed = _pack_rows(small_like)
        dwt_waiting, small_slots, wo_slots[l] = _dw_in(
            dproj, h, (_gather_slots_and_scatter_rows(W_OUT_SHARD), [packed, dwo_part],
                       [_slots_shape(*packed.shape, F32), _slots_shape(W_OUT_SHARD, D_MODEL, BF16)]))
        if l > 0:
            dx, dng, dx_b = _dh_norm_bwd(dproj, wt_l, x_l, dx, norm_g[l][None, :], True)
        else:
            dx, dng, wt_slots[0] = _dh_norm_bwd(dproj, wt_l, x_l, dx, norm_g[l][None, :], False,
                                                (_scatter_rows((W_IN_SHARD,)), [dwt_waiting], [wt_slots_shape]))
        g_small[l] = _unpack_rows(_sum_slots(small_slots, packed.shape[0]), small_like)
        g_norm[l] = dng[0]

    dng_all = _pack_rows([jnp.stack(g_norm)])
    dng_slots = _exchange("gather_dnorm", _gather_slots(), [dng_all], [_slots_shape(*dng_all.shape, F32)])[0]
    gr_norm = _unpack_rows(_sum_slots(dng_slots, dng_all.shape[0]), [norm_g])[0]
    gr_qn, gr_kn, gr_sk, gr_ws, gr_bs = (jnp.stack([g_small[l][i] for l in range(DEPTH)]) for i in range(5))

    def t(a):
        return jnp.swapaxes(a, 1, 2)

    from_slots = {1: _adamw_from_slots("adamw_w_in", wt_slots, t(w_in), t(m_w_in), t(v_w_in), ADAMW_ROWS),
                  7: _adamw_from_slots("adamw_w_out", wo_slots, w_out, m_w_out, v_w_out, ADAMW_ROWS)}
    from_slots[1] = [t(a) for a in from_slots[1]]

    grads = [gr_norm, None, gr_qn, gr_kn, gr_sk, gr_ws, gr_bs, None]
    weights = [norm_g, w_in, q_norm, k_norm, sinks, w_s, b_s, w_out]
    moms = [m_norm_g, m_w_in, m_q_norm, m_k_norm, m_sinks, m_w_s, m_b_s, m_w_out]
    vels = [v_norm_g, v_w_in, v_q_norm, v_k_norm, v_sinks, v_w_s, v_b_s, v_w_out]
    tiles = [None, None, None, None, None, 1024, None, None]
    names = ["norm_g", "w_in", "q_norm", "k_norm", "sinks", "w_s", "b_s", "w_out"]
    deltas, new_m, new_v = [], [], []
    for i, (nm, w, g, m, v, tr) in enumerate(zip(names, weights, grads, moms, vels, tiles)):
        if i in from_slots:
            grads[i], d, mo, vo = from_slots[i]
        else:
            d, mo, vo = _adamw("adamw_" + nm, w, g, m, v, tr)
        deltas.append(d)
        new_m.append(mo)
        new_v.append(vo)

    return (loss, dx[None], *grads, *deltas, *new_m, *new_v)
```

```python
import numpy as np
import jax
import jax.numpy as jnp
from jax import lax
from jax.experimental import pallas as pl
from jax.experimental.pallas import tpu as pltpu

F32 = jnp.float32
BF16 = jnp.bfloat16

D_MODEL = 2048
D_ATTN = 1024
D_IN = 5632
N_HEADS = 16
DEPTH = 4
BLK = 128
HALF = 64
RMS_EPS = 1e-6
C_Q, C_K, C_V, C_GA, C_ZU, C_ZV, C_GB = 0, 1024, 1280, 1536, 2560, 3584, 4608
NEG = -1e30
N_DEV = 8
W_IN_SHARD = D_IN // N_DEV
W_OUT_SHARD = D_MODEL // N_DEV
INV_SQRT2 = 0.7071067811865476
INV_SQRT_2PI = 0.3989422804014327

TM_RESIDENT = 256
TM_STREAM = 512
TM_WIDE = 1024
TN_PROJ = 512
DW_IN_ROWS = D_IN // 4
DW_IN_TOK = 2048
ADAMW_ROWS = 64
MID_STEP_16THS = 13

ADAM_LR = 0.001
ADAM_B1 = 0.9
ADAM_B2 = 0.999
ADAM_EPS = 1e-08
ADAM_WD = 0.01
ADAM_STEP = 10

NT_DIMS = (((1,), (1,)), ((), ()))
TN_DIMS = (((0,), (0,)), ((), ()))
MESH_AXES = ("x", "y", "c")


def _sigmoid(v):
    return 1.0 / (1.0 + jnp.exp(-v))


def _gelu_cdf(z):
    return 0.5 * (1.0 + lax.erf(z * INV_SQRT2))


def _gelu_grad(z, cdf):
    return cdf + z * (jnp.exp(-0.5 * z * z) * INV_SQRT_2PI)


def _lane_halves(rows):
    lane = lax.broadcasted_iota(jnp.int32, (rows, BLK), 1)
    return lane < HALF, lane >= HALF


def _half_masks():
    return {BLK: _lane_halves(BLK), 2 * BLK: _lane_halves(2 * BLK)}


def _half_sum(v, ones):
    h_a, h_b = ones[v.shape[0]]
    s_a = jnp.sum(jnp.where(h_a, v, 0.0), axis=-1, keepdims=True)
    s_b = jnp.sum(jnp.where(h_b, v, 0.0), axis=-1, keepdims=True)
    return jnp.where(h_a, s_a, s_b)


def _half_rms(v, w, ones):
    r = lax.rsqrt(_half_sum(v * v, ones) * (1.0 / HALF) + RMS_EPS)
    yhat = v * r
    return yhat, r, yhat * w


def _half_rms_bwd(dy, yhat, r, w, ones):
    dyh = dy * w
    c = _half_sum(dyh * yhat, ones) * (1.0 / HALF)
    return r * (dyh - yhat * c)


def _band_mask(n):
    t = lax.broadcasted_iota(jnp.int32, (BLK, 2 * BLK), 0)
    kk = lax.broadcasted_iota(jnp.int32, (BLK, 2 * BLK), 1)
    dist = t + BLK - kk
    first_key = jnp.where(n > 0, 0, BLK)
    ok = (dist >= 0) & (dist < BLK) & (kk >= first_key)
    return ok, dist.astype(F32)


def _alibi_table(n, slopes_ref, bias_scr):
    ok, distf = _band_mask(n)
    for hd in range(N_HEADS):
        bias_scr[BLK * hd:BLK * (hd + 1), :] = jnp.where(ok, -(slopes_ref[hd] * distf), NEG)


def _sink_col(sinks_ref, kt):
    return jnp.concatenate([jnp.full((BLK, 1), sinks_ref[8 * kt + i], F32) for i in range(8)], axis=0)


def _softmax_sink(s_scaled, bias, sink):
    s = s_scaled + bias
    m = jnp.maximum(jnp.max(s, axis=-1, keepdims=True), sink)
    p = jnp.exp(s - m)
    es = jnp.exp(sink - m)
    inv = 1.0 / (jnp.sum(p, axis=-1, keepdims=True) + es)
    return p * inv, es * inv


def _rows(i):
    return slice(BLK * i, BLK * (i + 1))


def _cols(base, j):
    return slice(base + BLK * j, base + BLK * (j + 1))


def _to_half(v, have, want):
    return v if have == want else pltpu.roll(v, HALF, 1)


def _tril_mask():
    row = lax.broadcasted_iota(jnp.int32, (BLK, BLK), 0)
    col = lax.broadcasted_iota(jnp.int32, (BLK, BLK), 1)
    return row >= col


def _tile(limit, s):
    t = min(limit, s)
    assert s % t == 0, (s, t)
    return t


def _mesh_place():
    x, y, c = lax.axis_index("x"), lax.axis_index("y"), lax.axis_index("c")
    return x, y, c, 4 * x + 2 * y + c


def _peer(x, y, c, k):
    px = 1 - x if k & 4 else x
    py = 1 - y if k & 2 else y
    pc = 1 - c if k & 1 else c
    return (px, py, pc), 4 * px + 2 * py + pc


class _Pushes:
    def __init__(self, n_arrays, src_view, dst_view):
        self.na = n_arrays
        self.src_view = src_view
        self.dst_view = dst_view

    def scratch(self):
        n = self.na * (N_DEV - 1)
        return [pltpu.SemaphoreType.DMA((n,)), pltpu.SemaphoreType.DMA((n,)), pltpu.SemaphoreType.DMA((self.na,))]

    def copies(self, src_refs, dst_refs, send_sems, recv_sems, local_sems):
        x, y, c, me = _mesh_place()
        cps = []
        for a in range(self.na):
            cps.append(pltpu.make_async_copy(self.src_view(a, src_refs[a], me), self.dst_view(a, dst_refs[a], me),
                                             local_sems.at[a]))
        for k in range(1, N_DEV):
            peer, pidx = _peer(x, y, c, k)
            for a in range(self.na):
                sem = a * (N_DEV - 1) + k - 1
                cps.append(pltpu.make_async_remote_copy(
                    src_ref=self.src_view(a, src_refs[a], pidx), dst_ref=self.dst_view(a, dst_refs[a], me),
                    send_sem=send_sems.at[sem], recv_sem=recv_sems.at[sem],
                    device_id=peer, device_id_type=pl.DeviceIdType.MESH))
        return cps

    def plan(self, src_refs, dst_refs, send_sems, recv_sems, local_sems):
        cps = self.copies(src_refs, dst_refs, send_sems, recv_sems, local_sems)
        return cps, [], [], [cp.wait for cp in cps]


class _TwoLevelGather:
    def __init__(self, shard_rows, layers):
        self.na = len(shard_rows)
        self.shard_rows = shard_rows
        self.layers = layers

    def scratch(self):
        n = self.na * (N_DEV - 1)
        return [pltpu.SemaphoreType.DMA((n,)), pltpu.SemaphoreType.DMA((n,)), pltpu.SemaphoreType.DMA((self.na,))]

    def plan(self, src_refs, dst_refs, send_sems, recv_sems, local_sems):
        x, y, c, _ = _mesh_place()
        sibling = (x, y, 1 - c)
        chips = [(1 - x, y), (x, 1 - y), (1 - x, 1 - y)]
        start, mid_wait, mid_start, final = [], [], [], []
        for a in range(self.na):
            r = self.shard_rows[a]
            src = src_refs[a].at[self.layers[a]]
            dst = dst_refs[a]

            def rows(px, py, pc, r=r, dst=dst):
                return dst.at[pl.ds(pl.multiple_of((4 * px + 2 * py + pc) * r, 64), r), :]

            def remote(k, s_ref, block, to, a=a, rows=rows):
                return pltpu.make_async_remote_copy(
                    src_ref=s_ref, dst_ref=rows(*block),
                    send_sem=send_sems.at[a * (N_DEV - 1) + k], recv_sem=recv_sems.at[a * (N_DEV - 1) + k],
                    device_id=to, device_id_type=pl.DeviceIdType.MESH)

            mine = pltpu.make_async_copy(src, rows(x, y, c), local_sems.at[a])
            own = [remote(0, src, (x, y, c), sibling)]
            own += [remote(1 + j, src, (x, y, c), (*chip, c)) for j, chip in enumerate(chips)]
            passed = [remote(4 + j, rows(*chip, c), (*chip, c), sibling) for j, chip in enumerate(chips)]
            start += [mine] + own
            mid_wait += own[1:]
            mid_start += passed
            final += [own[0].wait_recv] + [cp.wait_recv for cp in passed]
            final += [cp.wait_send for cp in own + passed] + [mine.wait]
        return start, mid_wait, mid_start, final


def _call(name, compute, grid, in_specs, out_specs, out_shape, scratch, args, push=None):
    sem = pltpu.CompilerParams(dimension_semantics=("arbitrary",) * len(grid))
    if push is None:
        return pl.pallas_call(compute, name=name, grid=grid, in_specs=in_specs, out_specs=out_specs,
                              out_shape=out_shape, scratch_shapes=scratch, compiler_params=sem)(*args)
    pushes, srcs, xshapes = push
    n_in, n_out, n_scr, na = len(args), len(out_shape), len(scratch), pushes.na
    hbm = pl.BlockSpec(memory_space=pltpu.HBM)

    def body(*refs):
        ins, refs = refs[:n_in], refs[n_in:]
        xin, refs = refs[:na], refs[na:]
        outs, refs = refs[:n_out], refs[n_out:]
        xout, refs = refs[:na], refs[na:]
        scr, sems = refs[:n_scr], refs[n_scr:]
        start, mid_wait, mid_start, final = pushes.plan(xin, xout, *sems)
        first = pl.program_id(0) == 0
        middle = pl.program_id(0) == (grid[0] * MID_STEP_16THS) // 16
        last = pl.program_id(0) == grid[0] - 1
        for d in range(1, len(grid)):
            first = first & (pl.program_id(d) == 0)
            middle = middle & (pl.program_id(d) == 0)
            last = last & (pl.program_id(d) == grid[d] - 1)

        @pl.when(first)
        def _():
            for cp in start:
                cp.start()

        if mid_start:
            @pl.when(middle)
            def _():
                for cp in mid_wait:
                    cp.wait_recv()
                for cp in mid_start:
                    cp.start()

        compute(*ins, *outs, *scr)

        @pl.when(last)
        def _():
            for wait in final:
                wait()

    return pl.pallas_call(
        body, name=name, grid=grid,
        in_specs=list(in_specs) + [hbm] * na, out_specs=list(out_specs) + [hbm] * na,
        out_shape=list(out_shape) + list(xshapes),
        scratch_shapes=list(scratch) + pushes.scratch(), compiler_params=sem)(*args, *srcs)


def _exchange(name, pushes, srcs, out_shapes):
    na = pushes.na
    hbm = pl.BlockSpec(memory_space=pltpu.HBM)

    def body(*refs):
        start, mid_wait, mid_start, final = pushes.plan(refs[:na], refs[na:2 * na], *refs[2 * na:])
        for cp in start:
            cp.start()
        for cp in mid_wait:
            cp.wait_recv()
        for cp in mid_start:
            cp.start()
        for wait in final:
            wait()

    return pl.pallas_call(body, name=name, in_specs=[hbm] * na, out_specs=[hbm] * na, out_shape=out_shapes,
                          scratch_shapes=pushes.scratch())(*srcs)


def _gather_rows(shard_rows, layers):
    return _TwoLevelGather(shard_rows, layers)


def _scatter_rows(shard_rows):
    def src_view(a, ref, idx):
        r = shard_rows[a]
        return ref.at[pl.ds(pl.multiple_of(idx * r, 64), r), :]

    def dst_view(a, ref, idx):
        return ref.at[idx]

    return _Pushes(len(shard_rows), src_view, dst_view)


def _gather_slots():
    return _Pushes(1, lambda a, ref, idx: ref, lambda a, ref, idx: ref.at[idx])


def _gather_slots_and_scatter_rows(rows):
    def src_view(a, ref, idx):
        return ref if a == 0 else ref.at[pl.ds(pl.multiple_of(idx * rows, 64), rows), :]

    return _Pushes(2, src_view, lambda a, ref, idx: ref.at[idx])


W_FULL = {W_IN_SHARD: jax.ShapeDtypeStruct((D_IN, D_MODEL), BF16),
          W_OUT_SHARD: jax.ShapeDtypeStruct((D_MODEL, D_MODEL), BF16)}


def _slots_shape(rows, cols, dtype):
    return jax.ShapeDtypeStruct((N_DEV, rows, cols), dtype)


def _sum_slots(slots, tr):
    _, r, c = slots.shape

    def compute(s_ref, o_ref):
        tot = s_ref[0].astype(F32)
        for d in range(1, N_DEV):
            tot = tot + s_ref[d].astype(F32)
        o_ref[...] = tot

    return _call("sum_slots", compute, (r // tr,),
                 [pl.BlockSpec((N_DEV, tr, c), lambda i: (0, i, 0))], [pl.BlockSpec((tr, c), lambda i: (i, 0))],
                 [jax.ShapeDtypeStruct((r, c), F32)], [], [slots])[0]


def _resident(shape):
    return pl.BlockSpec(shape, lambda *_: (0,) * len(shape), pipeline_mode=pl.Buffered(1))


def _norm_proj(x, g, w_t, push=None):
    s = x.shape[0]
    tm = _tile(TM_RESIDENT, s)

    def compute(x_ref, g_ref, w_ref, h_ref, p_ref):
        xf = x_ref[...]
        r = lax.rsqrt(jnp.mean(xf * xf, axis=-1, keepdims=True) + RMS_EPS)
        h = ((xf * r) * g_ref[...]).astype(BF16)
        h_ref[...] = h
        for j in range(D_IN // TN_PROJ):
            cols = slice(j * TN_PROJ, (j + 1) * TN_PROJ)
            p_ref[:, cols] = lax.dot_general(h, w_ref[cols, :], NT_DIMS, preferred_element_type=F32)

    return _call(
        "norm_proj", compute, (s // tm,),
        [pl.BlockSpec((tm, D_MODEL), lambda i: (i, 0)), pl.BlockSpec((1, D_MODEL), lambda i: (0, 0)),
         _resident((D_IN, D_MODEL))],
        [pl.BlockSpec((tm, D_MODEL), lambda i: (i, 0)), pl.BlockSpec((tm, D_IN), lambda i: (i, 0))],
        [jax.ShapeDtypeStruct((s, D_MODEL), BF16), jax.ShapeDtypeStruct((s, D_IN), F32)],
        [], [x, g, w_t], push)


def _sgu_weights(ws_ref, wtril_scr, wtril_t_scr=None):
    tril = _tril_mask()
    for hd in range(N_HEADS):
        w = jnp.where(tril, ws_ref[hd], 0.0)
        wtril_scr[BLK * hd:BLK * (hd + 1), :] = w.astype(BF16)
        if wtril_t_scr is not None:
            wtril_t_scr[hd // 2, :, BLK * (hd % 2):BLK * (hd % 2 + 1)] = w.T.astype(BF16)


def _kv_band(kt, p_ref, kvp_ref, kw_v, ones):
    kband = jnp.concatenate([kvp_ref[:, _cols(0, kt)], p_ref[:, _cols(C_K, kt)]], axis=0)
    kyhat, kr, kn = _half_rms(kband, kw_v, ones)
    vband = jnp.concatenate([kvp_ref[:, _cols(256, kt)], p_ref[:, _cols(C_V, kt)]], axis=0)
    return kyhat, kr, (kn * 0.125).astype(BF16), vband.astype(BF16)


def _stack_heads(tiles, halves):
    parts = []
    for tt, tile in enumerate(tiles):
        for qh in range(2):
            parts.append(_to_half(jnp.where(halves[qh], tile, 0.0), qh, tt // 2).astype(BF16))
    return jnp.concatenate(parts, axis=0)


def _unstack_heads(stacked, tt, h_a):
    return jnp.where(h_a, _to_half(stacked[_rows(2 * tt)], tt // 2, 0), _to_half(stacked[_rows(2 * tt + 1)], tt // 2, 1))


def _mixer_fwd(proj, qw, kw, sinks, slopes, w_s, bmap, push=None):
    s = proj.shape[0]
    nb = s // BLK

    def compute(sinks_ref, slopes_ref, p_ref, kvp_ref, qw_ref, kw_ref, ws_ref, bmap_ref, mix_ref,
                wtril_scr, bias_scr):
        _mixer_tables(pl.program_id(0), slopes_ref, ws_ref, wtril_scr, bias_scr)
        _mixer_fwd_block(sinks_ref, p_ref, kvp_ref, qw_ref, kw_ref, bmap_ref, mix_ref, wtril_scr, bias_scr)

    smem = pl.BlockSpec(memory_space=pltpu.SMEM)
    return _call(
        "mixer_fwd", compute, (nb,),
        [smem, smem,
         pl.BlockSpec((BLK, D_IN), lambda n: (n, 0)),
         pl.BlockSpec((BLK, 512), lambda n: (jnp.maximum(n - 1, 0), 2)),
         pl.BlockSpec((1, BLK), lambda n: (0, 0)),
         pl.BlockSpec((1, BLK), lambda n: (0, 0)),
         pl.BlockSpec((N_HEADS, BLK, BLK), lambda n: (0, 0, 0)),
         pl.BlockSpec((BLK, D_ATTN), lambda n: (0, 0))],
        [pl.BlockSpec((BLK, D_MODEL), lambda n: (n, 0))],
        [jax.ShapeDtypeStruct((s, D_MODEL), BF16)],
        [pltpu.VMEM((N_HEADS * BLK, BLK), BF16), pltpu.VMEM((N_HEADS * BLK, 2 * BLK), F32)],
        [sinks, slopes, proj, proj, qw, kw, w_s, bmap], push)


def _mixer_tables(n, slopes_ref, ws_ref, wtril_scr, bias_scr, wtril_t_scr=None):
    @pl.when(n == 0)
    def _():
        _sgu_weights(ws_ref, wtril_scr, wtril_t_scr)

    @pl.when(n <= 1)
    def _():
        _alibi_table(n, slopes_ref, bias_scr)


def _mixer_fwd_block(sinks_ref, p_ref, kvp_ref, qw_ref, kw_ref, bmap_ref, mix_ref, wtril_scr, bias_scr):
    h_a, h_b = _lane_halves(BLK)
    ones = _half_masks()
    halves = (h_a, h_b)
    qw_v = qw_ref[...]
    kw_v = kw_ref[...]
    bands = [_kv_band(kt, p_ref, kvp_ref, kw_v, ones) for kt in range(2)]
    sc = []
    for kt in range(2):
        qn = [_half_rms(p_ref[:, _cols(C_Q, 4 * kt + tt)], qw_v, ones)[2] for tt in range(4)]
        sc.append(lax.dot_general(_stack_heads(qn, halves), bands[kt][2], NT_DIMS, preferred_element_type=F32))
    zu, mixed = [], []
    for j in range(8):
        zu_pre = p_ref[:, _cols(C_ZU, j)]
        zv_pre = p_ref[:, _cols(C_ZV, j)]
        zu.append(zu_pre * _gelu_cdf(zu_pre))
        zvb = (zv_pre * _gelu_cdf(zv_pre)).astype(BF16)
        mixed.append(jnp.dot(wtril_scr[2 * BLK * j:2 * BLK * (j + 1), :], zvb, preferred_element_type=F32))
    o = []
    for kt in range(2):
        p, _ = _softmax_sink(sc[kt], bias_scr[8 * BLK * kt:8 * BLK * (kt + 1), :], _sink_col(sinks_ref, kt))
        o.append(jnp.dot(p.astype(BF16), bands[kt][3], preferred_element_type=F32))
    for j in range(8):
        gb = p_ref[:, _cols(C_GB, j)]
        mx = jnp.where(h_a, mixed[j][0:BLK], mixed[j][BLK:2 * BLK]) + bmap_ref[:, _cols(0, j)]
        mix_ref[:, _cols(D_ATTN, j)] = ((zu[j] * mx) * (gb * _sigmoid(gb))).astype(BF16)
    for kt in range(2):
        for tt in range(4):
            j = 4 * kt + tt
            ga = p_ref[:, _cols(C_GA, j)]
            mix_ref[:, _cols(0, j)] = (_unstack_heads(o[kt], tt, h_a) * (ga * _sigmoid(ga))).astype(BF16)


def _out_proj(mix, w_o, x):
    s = x.shape[0]
    tm = _tile(TM_WIDE, s)

    def compute(m_ref, w_ref, x_ref, o_ref):
        o_ref[...] = x_ref[...] + jnp.dot(m_ref[...], w_ref[...], preferred_element_type=F32)

    return _call(
        "out_proj", compute, (s // tm,),
        [pl.BlockSpec((tm, D_MODEL), lambda i: (i, 0)), _resident((D_MODEL, D_MODEL)),
         pl.BlockSpec((tm, D_MODEL), lambda i: (i, 0))],
        [pl.BlockSpec((tm, D_MODEL), lambda i: (i, 0))],
        [jax.ShapeDtypeStruct((s, D_MODEL), F32)], [], [mix, w_o, x])[0]


def _out_proj_loss(mix, w_o, x, tgt):
    s = x.shape[0]
    tm = _tile(TM_STREAM, s)

    def compute(m_ref, w_ref, x_ref, t_ref, dy_ref, dyb_ref, sq_ref, dmix_ref):
        @pl.when(pl.program_id(0) == 0)
        def _():
            sq_ref[...] = jnp.zeros_like(sq_ref)

        y = x_ref[...] + jnp.dot(m_ref[...], w_ref[...], preferred_element_type=F32)
        e = y - t_ref[...]
        dy = e * (1.0 / D_MODEL)
        dyb = dy.astype(BF16)
        dy_ref[...] = dy
        dyb_ref[...] = dyb
        sq_ref[...] += jnp.sum(e * e, axis=0, keepdims=True)
        dmix_ref[...] = lax.dot_general(dyb, w_ref[...], NT_DIMS, preferred_element_type=F32)

    tok = pl.BlockSpec((tm, D_MODEL), lambda i: (i, 0))
    return _call(
        "out_proj_loss", compute, (s // tm,),
        [tok, _resident((D_MODEL, D_MODEL)), tok, tok],
        [tok, tok, pl.BlockSpec((1, D_MODEL), lambda i: (0, 0)), tok],
        [jax.ShapeDtypeStruct((s, D_MODEL), F32), jax.ShapeDtypeStruct((s, D_MODEL), BF16),
         jax.ShapeDtypeStruct((1, D_MODEL), F32), jax.ShapeDtypeStruct((s, D_MODEL), F32)], [], [mix, w_o, x, tgt])


def _dmix(dx, w_o):
    s = dx.shape[0]
    tm = _tile(TM_WIDE, s)

    def compute(d_ref, w_ref, o_ref):
        o_ref[...] = lax.dot_general(d_ref[...], w_ref[...], NT_DIMS, preferred_element_type=F32)

    return _call(
        "dmix", compute, (s // tm,),
        [pl.BlockSpec((tm, D_MODEL), lambda i: (i, 0)), _resident((D_MODEL, D_MODEL))],
        [pl.BlockSpec((tm, D_MODEL), lambda i: (i, 0))],
        [jax.ShapeDtypeStruct((s, D_MODEL), F32)], [], [dx, w_o])[0]


def _dw_out(mix, dx):
    s = dx.shape[0]
    tk = _tile(TM_WIDE, s)
    nk = s // tk

    def compute(m_ref, d_ref, o_ref, acc):
        k = pl.program_id(0)

        @pl.when(k == 0)
        def _():
            acc[...] = jnp.zeros_like(acc)

        acc[...] += lax.dot_general(m_ref[...], d_ref[...], TN_DIMS, preferred_element_type=F32)

        @pl.when(k == nk - 1)
        def _():
            o_ref[...] = acc[...].astype(BF16)

    return _call(
        "dw_out", compute, (nk,),
        [pl.BlockSpec((tk, D_MODEL), lambda k: (k, 0)), pl.BlockSpec((tk, D_MODEL), lambda k: (k, 0))],
        [pl.BlockSpec((D_MODEL, D_MODEL), lambda k: (0, 0))],
        [jax.ShapeDtypeStruct((D_MODEL, D_MODEL), BF16)],
        [pltpu.VMEM((D_MODEL, D_MODEL), F32)], [mix, dx])[0]


def _mixer_bwd(proj, dmix, qw, kw, sinks, slopes, w_s, bmap, push=None):
    s = proj.shape[0]
    nb = s // BLK

    def compute(sinks_ref, slopes_ref, p_ref, kvp_ref, dm_ref, qw_ref, kw_ref, ws_ref, bmap_ref,
                dp_ref, dqw_ref, dkw_ref, dsk_ref, dws_ref, dbs_ref,
                pend, accq, acck, accs, accb, wtril_scr, wtril_t_scr, bias_scr):
        n = pl.program_id(0)
        h_a, h_b = _lane_halves(BLK)
        ones = _half_masks()
        halves = (h_a, h_b)
        lane = lax.broadcasted_iota(jnp.int32, (BLK, BLK), 1)

        @pl.when(n == 0)
        def _():
            accq[...] = jnp.zeros_like(accq)
            acck[...] = jnp.zeros_like(acck)
            accs[...] = jnp.zeros_like(accs)
            accb[...] = jnp.zeros_like(accb)
            dws_ref[...] = jnp.zeros_like(dws_ref)
            pend[...] = jnp.zeros_like(pend)
            _sgu_weights(ws_ref, wtril_scr, wtril_t_scr)

        @pl.when(n <= 1)
        def _():
            _alibi_table(n, slopes_ref, bias_scr)

        def emit_pending():
            dp_ref[:, 0:C_K] = pend[:, 0:C_K].astype(BF16)
            dp_ref[:, C_GA:D_IN] = pend[:, C_GA:D_IN].astype(BF16)

        @pl.when(n < nb)
        def _():
            emit_pending()
            qw_v = qw_ref[...]
            kw_v = kw_ref[...]
            bands = [_kv_band(kt, p_ref, kvp_ref, kw_v, ones) for kt in range(2)]
            tiles, qst, dost, sc, dpm = [], [], [], [], []
            for kt in range(2):
                qn, d_o, tl = [], [], []
                for tt in range(4):
                    j = 4 * kt + tt
                    qyhat, qr, qn_t = _half_rms(p_ref[:, _cols(C_Q, j)], qw_v, ones)
                    ga = p_ref[:, _cols(C_GA, j)]
                    sg = _sigmoid(ga)
                    dma = dm_ref[:, _cols(0, j)]
                    qn.append(qn_t)
                    d_o.append(dma * (ga * sg))
                    tl.append((qyhat, qr, dma * (sg * (1.0 + ga * (1.0 - sg)))))
                tiles.append(tl)
                qst.append(_stack_heads(qn, halves))
                dost.append(_stack_heads(d_o, halves))
                sc.append(lax.dot_general(qst[kt], bands[kt][2], NT_DIMS, preferred_element_type=F32))
                dpm.append(lax.dot_general(dost[kt], bands[kt][3], NT_DIMS, preferred_element_type=F32))
            sgu = []
            for j in range(8):
                zu_pre = p_ref[:, _cols(C_ZU, j)]
                zv_pre = p_ref[:, _cols(C_ZV, j)]
                cu = _gelu_cdf(zu_pre)
                cv = _gelu_cdf(zv_pre)
                zvb = (zv_pre * cv).astype(BF16)
                sgu.append((zu_pre * cu, zvb, _gelu_grad(zu_pre, cu), _gelu_grad(zv_pre, cv),
                            jnp.dot(wtril_scr[2 * BLK * j:2 * BLK * (j + 1), :], zvb, preferred_element_type=F32)))
            dsink = jnp.zeros((BLK, BLK), F32)
            pst, dqkst = [], []
            for kt in range(2):
                p, p_sink = _softmax_sink(sc[kt], bias_scr[8 * BLK * kt:8 * BLK * (kt + 1), :],
                                          _sink_col(sinks_ref, kt))
                dsum = jnp.sum(p * dpm[kt], axis=-1, keepdims=True)
                dsink_col = -(p_sink * dsum)
                for i in range(8):
                    dsink = dsink + jnp.where(lane == 8 * kt + i, dsink_col[_rows(i)], 0.0)
                pst.append(p.astype(BF16))
                dqkst.append((p * (dpm[kt] - dsum)).astype(BF16))
            o, dqn_all, dvb, dkn = [], [], [], []
            for kt in range(2):
                o.append(jnp.dot(pst[kt], bands[kt][3], preferred_element_type=F32))
                dqn_all.append(jnp.dot(dqkst[kt], bands[kt][2], preferred_element_type=F32))
                dvb.append(lax.dot_general(pst[kt], dost[kt], TN_DIMS, preferred_element_type=F32))
                dkn.append(0.125 * lax.dot_general(dqkst[kt], qst[kt], TN_DIMS, preferred_element_type=F32))
            dms = []
            for j in range(8):
                zu, zvb, gu, gv, m_ab = sgu[j]
                gb = p_ref[:, _cols(C_GB, j)]
                dmb = dm_ref[:, _cols(D_ATTN, j)]
                mixed = jnp.where(h_a, m_ab[0:BLK], m_ab[BLK:2 * BLK]) + bmap_ref[:, _cols(0, j)]
                sgb = _sigmoid(gb)
                dgate = dmb * (gb * sgb)
                pend[:, _cols(C_ZU, j)] = (dgate * mixed) * gu
                pend[:, _cols(C_GB, j)] = (dmb * (zu * mixed)) * (sgb * (1.0 + gb * (1.0 - sgb)))
                dmixed = dgate * zu
                accb[:, _cols(0, j)] += dmixed
                dms.append(jnp.concatenate([jnp.where(h_a, dmixed, 0.0).astype(BF16),
                                            jnp.where(h_b, dmixed, 0.0).astype(BF16)], axis=0))
            dzv = []
            for j in range(8):
                dzv.append(jnp.dot(wtril_t_scr[j], dms[j], preferred_element_type=F32))
                dw_ab = lax.dot_general(dms[j], sgu[j][1], NT_DIMS, preferred_element_type=F32)
                dws_ref[2 * j] += dw_ab[0:BLK]
                dws_ref[2 * j + 1] += dw_ab[BLK:2 * BLK]
            dq_w = jnp.zeros((BLK, BLK), F32)
            for kt in range(2):
                for tt in range(4):
                    j = 4 * kt + tt
                    qyhat, qr, dsilu = tiles[kt][tt]
                    dqn = _unstack_heads(dqn_all[kt], tt, h_a)
                    pend[:, _cols(C_GA, j)] = _unstack_heads(o[kt], tt, h_a) * dsilu
                    pend[:, _cols(C_Q, j)] = _half_rms_bwd(dqn, qyhat, qr, qw_v, ones)
                    dq_w = dq_w + dqn * qyhat
            dk_w = jnp.zeros((BLK, BLK), F32)
            for kt in range(2):
                kyhat, kr = bands[kt][0], bands[kt][1]
                dk = _half_rms_bwd(dkn[kt], kyhat, kr, kw_v, ones)
                dkw_part = dkn[kt] * kyhat
                dk_w = dk_w + (dkw_part[0:BLK] + dkw_part[BLK:2 * BLK])
                dv = dvb[kt]
                dp_ref[:, _cols(C_K, kt)] = (pend[:, _cols(C_K, kt)] + dk[0:BLK]).astype(BF16)
                dp_ref[:, _cols(C_V, kt)] = (pend[:, _cols(C_V, kt)] + dv[0:BLK]).astype(BF16)
                pend[:, _cols(C_K, kt)] = dk[BLK:2 * BLK]
                pend[:, _cols(C_V, kt)] = dv[BLK:2 * BLK]
            accq[...] += dq_w
            acck[...] += dk_w
            accs[...] += dsink
            for j in range(8):
                pend[:, _cols(C_ZV, j)] = dzv[j] * sgu[j][3]

        @pl.when(n == nb)
        def _():
            tril = _tril_mask()
            emit_pending()
            dp_ref[:, C_K:C_GA] = pend[:, C_K:C_GA].astype(BF16)
            aq = accq[...]
            ak = acck[...]
            dqw_ref[...] = jnp.sum(aq + pltpu.roll(aq, HALF, 1), axis=0, keepdims=True)
            dkw_ref[...] = jnp.sum(ak + pltpu.roll(ak, HALF, 1), axis=0, keepdims=True)
            dsk_ref[...] = jnp.sum(accs[...], axis=0, keepdims=True)
            for hd in range(N_HEADS):
                dws_ref[hd] = jnp.where(tril, dws_ref[hd], 0.0)
            hrow = lax.broadcasted_iota(jnp.int32, (N_HEADS, D_ATTN), 0)
            hcol = lax.broadcasted_iota(jnp.int32, (N_HEADS, D_ATTN), 1)
            sel = jnp.where((hcol >= hrow * HALF) & (hcol < (hrow + 1) * HALF), 1.0, 0.0).astype(BF16)
            rem = accb[...]
            tot = jnp.zeros((N_HEADS, BLK), F32)
            for _ in range(3):
                part = rem.astype(BF16)
                tot = tot + lax.dot_general(sel, part, NT_DIMS, preferred_element_type=F32)
                rem = rem - part.astype(F32)
            dbs_ref[...] = tot

    smem = pl.BlockSpec(memory_space=pltpu.SMEM)
    last = nb - 1
    tile_f32 = pltpu.VMEM((BLK, BLK), F32)
    return _call(
        "mixer_bwd", compute, (nb + 1,),
        [smem, smem,
         pl.BlockSpec((BLK, D_IN), lambda n: (jnp.minimum(n, last), 0)),
         pl.BlockSpec((BLK, 512), lambda n: (jnp.maximum(jnp.minimum(n, last) - 1, 0), 2)),
         pl.BlockSpec((BLK, D_MODEL), lambda n: (jnp.minimum(n, last), 0)),
         pl.BlockSpec((1, BLK), lambda n: (0, 0)),
         pl.BlockSpec((1, BLK), lambda n: (0, 0)),
         pl.BlockSpec((N_HEADS, BLK, BLK), lambda n: (0, 0, 0)),
         pl.BlockSpec((BLK, D_ATTN), lambda n: (0, 0))],
        [pl.BlockSpec((BLK, D_IN), lambda n: (jnp.maximum(n - 1, 0), 0)),
         pl.BlockSpec((1, BLK), lambda n: (0, 0)),
         pl.BlockSpec((1, BLK), lambda n: (0, 0)),
         pl.BlockSpec((1, BLK), lambda n: (0, 0)),
         pl.BlockSpec((N_HEADS, BLK, BLK), lambda n: (0, 0, 0)),
         pl.BlockSpec((N_HEADS, BLK), lambda n: (0, 0))],
        [jax.ShapeDtypeStruct((s, D_IN), BF16),
         jax.ShapeDtypeStruct((1, BLK), F32),
         jax.ShapeDtypeStruct((1, BLK), F32),
         jax.ShapeDtypeStruct((1, BLK), F32),
         jax.ShapeDtypeStruct((N_HEADS, BLK, BLK), F32),
         jax.ShapeDtypeStruct((N_HEADS, BLK), F32)],
        [pltpu.VMEM((BLK, D_IN), F32), tile_f32, tile_f32, tile_f32, pltpu.VMEM((BLK, D_ATTN), F32),
         pltpu.VMEM((N_HEADS * BLK, BLK), BF16), pltpu.VMEM((N_HEADS // 2, BLK, 2 * BLK), BF16),
         pltpu.VMEM((N_HEADS * BLK, 2 * BLK), F32)],
        [sinks, slopes, proj, proj, dmix, qw, kw, w_s, bmap], push)


def _dh_norm_bwd(dproj, w_t, x, dx_out, g, bf16_copy, push=None):
    s = x.shape[0]
    tm = _tile(TM_RESIDENT, s)

    def compute(dp_ref, w_ref, x_ref, dxo_ref, g_ref, dx_ref, dg_ref, *dxb_ref):
        @pl.when(pl.program_id(0) == 0)
        def _():
            dg_ref[...] = jnp.zeros_like(dg_ref)

        dh = jnp.dot(dp_ref[...], w_ref[...], preferred_element_type=F32)
        xf = x_ref[...]
        r = lax.rsqrt(jnp.mean(xf * xf, axis=-1, keepdims=True) + RMS_EPS)
        yhat = xf * r
        dyh = dh * g_ref[...]
        c = jnp.mean(dyh * yhat, axis=-1, keepdims=True)
        dx = dxo_ref[...] + r * (dyh - yhat * c)
        dx_ref[...] = dx
        if bf16_copy:
            dxb_ref[0][...] = dx.astype(BF16)
        dg_ref[...] += jnp.sum(dh * yhat, axis=0, keepdims=True)

    tok = pl.BlockSpec((tm, D_MODEL), lambda i: (i, 0))
    return _call(
        "dh_norm_bwd", compute, (s // tm,),
        [pl.BlockSpec((tm, D_IN), lambda i: (i, 0)), _resident((D_IN, D_MODEL)), tok, tok,
         pl.BlockSpec((1, D_MODEL), lambda i: (0, 0))],
        [tok, pl.BlockSpec((1, D_MODEL), lambda i: (0, 0))] + [tok] * bf16_copy,
        [jax.ShapeDtypeStruct((s, D_MODEL), F32), jax.ShapeDtypeStruct((1, D_MODEL), F32)]
        + [jax.ShapeDtypeStruct((s, D_MODEL), BF16)] * bf16_copy,
        [], [dproj, w_t, x, dx_out, g], push)


def _dw_in(dproj, h, push=None):
    s = h.shape[0]
    tk = _tile(DW_IN_TOK, s)
    nk = s // tk

    def compute(dp_ref, h_ref, o_ref, acc):
        k = pl.program_id(1)

        @pl.when(k == 0)
        def _():
            acc[...] = jnp.zeros_like(acc)

        acc[...] += lax.dot_general(dp_ref[...], h_ref[...], TN_DIMS, preferred_element_type=F32)

        @pl.when(k == nk - 1)
        def _():
            o_ref[...] = acc[...].astype(BF16)

    return _call(
        "dw_in", compute, (D_IN // DW_IN_ROWS, nk),
        [pl.BlockSpec((tk, DW_IN_ROWS), lambda j, k: (k, j)), pl.BlockSpec((tk, D_MODEL), lambda j, k: (k, 0))],
        [pl.BlockSpec((DW_IN_ROWS, D_MODEL), lambda j, k: (j, 0))],
        [jax.ShapeDtypeStruct((D_IN, D_MODEL), BF16)],
        [pltpu.VMEM((DW_IN_ROWS, D_MODEL), F32)], [dproj, h], push)


def _adamw_math(w, g, m, v):
    m_new = ADAM_B1 * m + (1.0 - ADAM_B1) * g
    v_new = ADAM_B2 * v + (1.0 - ADAM_B2) * jnp.square(g)
    m_hat = m_new / (1.0 - ADAM_B1 ** ADAM_STEP)
    v_hat = v_new / (1.0 - ADAM_B2 ** ADAM_STEP)
    return -ADAM_LR * (m_hat / (jnp.sqrt(v_hat) + ADAM_EPS) + ADAM_WD * w), m_new, v_new


def _adamw(name, w, g, m, v, tr=None):
    shape = w.shape
    c = shape[-1]
    flat = [a.reshape(-1, c) for a in (w, g, m, v)]
    r = flat[0].shape[0]
    tr = r if tr is None else tr

    def compute(w_ref, g_ref, m_ref, v_ref, d_ref, mo_ref, vo_ref):
        d_ref[...], mo_ref[...], vo_ref[...] = _adamw_math(w_ref[...], g_ref[...], m_ref[...], v_ref[...])

    spec = pl.BlockSpec((tr, c), lambda i: (i, 0))
    outs = _call(name, compute, (r // tr,), [spec] * 4, [spec] * 3, [jax.ShapeDtypeStruct((r, c), F32)] * 3,
                 [], flat)
    return [o.reshape(shape) for o in outs]


def _adamw_from_slots(name, slots, w, m, v, tr):
    n_layers, r, c = w.shape
    nt = r // tr

    def compute(*refs):
        s_refs, (w_ref, m_ref, v_ref), (g_ref, d_ref, mo_ref, vo_ref) = refs[:n_layers], refs[n_layers:n_layers + 3], \
            refs[n_layers + 3:]
        for k in range(n_layers):
            @pl.when(pl.program_id(0) == k)
            def _(k=k):
                g = s_refs[k][0].astype(F32)
                for dev in range(1, N_DEV):
                    g = g + s_refs[k][dev].astype(F32)
                g_ref[0] = g
                d_ref[0], mo_ref[0], vo_ref[0] = _adamw_math(w_ref[0], g, m_ref[0], v_ref[0])

    def slots_spec(k):
        return pl.BlockSpec((N_DEV, tr, c),
                            lambda l, i: (0, jnp.where(l == k, i, jnp.where(l < k, 0, nt - 1)), 0))

    tile = pl.BlockSpec((1, tr, c), lambda l, i: (l, i, 0))
    return _call(name, compute, (n_layers, nt), [slots_spec(k) for k in range(n_layers)] + [tile] * 3, [tile] * 4,
                 [jax.ShapeDtypeStruct((n_layers, r, c), F32)] * 4, [], list(slots) + [w, m, v])


def _pack_rows(parts):
    rows = []
    for a in parts:
        flat = a.reshape(-1)
        n = -(-flat.shape[0] // (8 * BLK)) * 8
        rows.append(jnp.pad(flat, (0, n * BLK - flat.shape[0])).reshape(n, BLK))
    return jnp.concatenate(rows, axis=0)


def _unpack_rows(packed, like):
    out = []
    row = 0
    for a in like:
        n = -(-a.size // (8 * BLK)) * 8
        out.append(packed[row:row + n].reshape(-1)[:a.size].reshape(a.shape))
        row += n
    return out


def kernel(x, norm_g, w_in, q_norm, k_norm, sinks, w_s, b_s, w_out, loss_target, m_norm_g, m_w_in, m_q_norm, m_k_norm, m_sinks, m_w_s, m_b_s, m_w_out, v_norm_g, v_w_in, v_q_norm, v_k_norm, v_sinks, v_w_s, v_b_s, v_w_out):
    xs = x[0]
    tgt = loss_target[0]
    slopes = jnp.asarray(2.0 ** (-8.0 * np.arange(1, N_HEADS + 1) / N_HEADS), dtype=F32)
    wt_sh = jnp.swapaxes(w_in, 1, 2).astype(BF16)
    wo_sh = w_out.astype(BF16)

    layer_par = []
    for l in range(DEPTH):
        layer_par.append((jnp.tile(q_norm[l], 2)[None, :], jnp.tile(k_norm[l], 2)[None, :],
                          jnp.repeat(b_s[l].T, HALF, axis=1)))

    wt_full = _exchange("gather_w0", _gather_rows((W_IN_SHARD,), (0,)), [wt_sh], [W_FULL[W_IN_SHARD]])[0]
    saved = []
    cur = xs
    for l in range(DEPTH):
        qw, kw, bmap = layer_par[l]
        more = l + 1 < DEPTH
        if more:
            h, proj, wo_full, wt_next = _norm_proj(
                cur, norm_g[l][None, :], wt_full,
                (_gather_rows((W_OUT_SHARD, W_IN_SHARD), (l, l + 1)), [wo_sh, wt_sh],
                 [W_FULL[W_OUT_SHARD], W_FULL[W_IN_SHARD]]))
        else:
            h, proj, wo_full = _norm_proj(cur, norm_g[l][None, :], wt_full,
                                          (_gather_rows((W_OUT_SHARD,), (l,)), [wo_sh], [W_FULL[W_OUT_SHARD]]))
        mix = _mixer_fwd(proj, qw, kw, sinks[l], slopes, w_s[l], bmap)[0]
        saved.append((cur, h, proj, mix, wt_full, wo_full))
        if more:
            cur = _out_proj(mix, wo_full, cur)
            wt_full = wt_next
        else:
            dx, dx_b, sq, dmix_top = _out_proj_loss(mix, wo_full, cur, tgt)
    loss = lax.psum(0.5 * jnp.sum(sq) / D_MODEL, MESH_AXES)

    wt_slots, wo_slots, g_small, g_norm = ([None] * DEPTH for _ in range(4))
    wt_slots_shape = _slots_shape(W_IN_SHARD, D_MODEL, BF16)
    dwt_waiting = None
    for l in reversed(range(DEPTH)):
        x_l, h, proj, mix, wt_l, wo_l = saved[l]
        qw, kw, bmap = layer_par[l]
        dmix = dmix_top if l == DEPTH - 1 else _dmix(dx_b, wo_l)
        dwo_part = _dw_out(mix, dx_b)
        res = _mixer_bwd(proj, dmix, qw, kw, sinks[l], slopes, w_s[l], bmap,
                         None if dwt_waiting is None else
                         (_scatter_rows((W_IN_SHARD,)), [dwt_waiting], [wt_slots_shape]))
        dproj, dqw, dkw, dsk, dws, dbs = res[:6]
        if dwt_waiting is not None:
            wt_slots[l + 1] = res[6]
        small_like = [dqw[0, :HALF], dkw[0, :HALF], dsk[0, :N_HEADS], dws, dbs]
        packed = _pack_rows(small_like)
        dwt_waiting, small_slots, wo_slots[l] = _dw_in(
            dproj, h, (_gather_slots_and_scatter_rows(W_OUT_SHARD), [packed, dwo_part],
                       [_slots_shape(*packed.shape, F32), _slots_shape(W_OUT_SHARD, D_MODEL, BF16)]))
        if l > 0:
            dx, dng, dx_b = _dh_norm_bwd(dproj, wt_l, x_l, dx, norm_g[l][None, :], True)
        else:
            dx, dng, wt_slots[0] = _dh_norm_bwd(dproj, wt_l, x_l, dx, norm_g[l][None, :], False,
                                                (_scatter_rows((W_IN_SHARD,)), [dwt_waiting], [wt_slots_shape]))
        g_small[l] = _unpack_rows(_sum_slots(small_slots, packed.shape[0]), small_like)
        g_norm[l] = dng[0]

    dng_all = _pack_rows([jnp.stack(g_norm)])
    dng_slots = _exchange("gather_dnorm", _gather_slots(), [dng_all], [_slots_shape(*dng_all.shape, F32)])[0]
    gr_norm = _unpack_rows(_sum_slots(dng_slots, dng_all.shape[0]), [norm_g])[0]
    gr_qn, gr_kn, gr_sk, gr_ws, gr_bs = (jnp.stack([g_small[l][i] for l in range(DEPTH)]) for i in range(5))

    def t(a):
        return jnp.swapaxes(a, 1, 2)

    from_slots = {1: _adamw_from_slots("adamw_w_in", wt_slots, t(w_in), t(m_w_in), t(v_w_in), ADAMW_ROWS),
                  7: _adamw_from_slots("adamw_w_out", wo_slots, w_out, m_w_out, v_w_out, ADAMW_ROWS)}
    from_slots[1] = [t(a) for a in from_slots[1]]

    grads = [gr_norm, None, gr_qn, gr_kn, gr_sk, gr_ws, gr_bs, None]
    weights = [norm_g, w_in, q_norm, k_norm, sinks, w_s, b_s, w_out]
    moms = [m_norm_g, m_w_in, m_q_norm, m_k_norm, m_sinks, m_w_s, m_b_s, m_w_out]
    vels = [v_norm_g, v_w_in, v_q_norm, v_k_norm, v_sinks, v_w_s, v_b_s, v_w_out]
    tiles = [None, None, None, None, None, 1024, None, None]
    names = ["norm_g", "w_in", "q_norm", "k_norm", "sinks", "w_s", "b_s", "w_out"]
    deltas, new_m, new_v = [], [], []
    for i, (nm, w, g, m, v, tr) in enumerate(zip(names, weights, grads, moms, vels, tiles)):
        if i in from_slots:
            grads[i], d, mo, vo = from_slots[i]
        else:
            d, mo, vo = _adamw("adamw_" + nm, w, g, m, v, tr)
        deltas.append(d)
        new_m.append(mo)
        new_v.append(vo)

    return (loss, dx[None], *grads, *deltas, *new_m, *new_v)
```

```python
import numpy as np
import jax
import jax.numpy as jnp
from jax import lax
from jax.experimental import pallas as pl
from jax.experimental.pallas import tpu as pltpu

F32 = jnp.float32
BF16 = jnp.bfloat16

D_MODEL = 2048
D_ATTN = 1024
D_IN = 5632
N_HEADS = 16
DEPTH = 4
BLK = 128
HALF = 64
RMS_EPS = 1e-6
C_Q, C_K, C_V, C_GA, C_ZU, C_ZV, C_GB = 0, 1024, 1280, 1536, 2560, 3584, 4608
NEG = -1e30
N_DEV = 8
W_IN_SHARD = D_IN // N_DEV
W_OUT_SHARD = D_MODEL // N_DEV
INV_SQRT2 = 0.7071067811865476
INV_SQRT_2PI = 0.3989422804014327

TM_RESIDENT = 256
TM_STREAM = 512
TM_WIDE = 1024
TN_PROJ = 512
DW_IN_ROWS = D_IN // 4
DW_IN_TOK = 2048
ADAMW_ROWS = 64
MID_STEP_16THS = 13

ADAM_LR = 0.001
ADAM_B1 = 0.9
ADAM_B2 = 0.999
ADAM_EPS = 1e-08
ADAM_WD = 0.01
ADAM_STEP = 10

NT_DIMS = (((1,), (1,)), ((), ()))
TN_DIMS = (((0,), (0,)), ((), ()))
MESH_AXES = ("x", "y", "c")


def _sigmoid(v):
    return 1.0 / (1.0 + jnp.exp(-v))


def _gelu_cdf(z):
    return 0.5 * (1.0 + lax.erf(z * INV_SQRT2))


def _gelu_grad(z, cdf):
    return cdf + z * (jnp.exp(-0.5 * z * z) * INV_SQRT_2PI)


def _lane_halves(rows):
    lane = lax.broadcasted_iota(jnp.int32, (rows, BLK), 1)
    return lane < HALF, lane >= HALF


def _half_masks():
    return {BLK: _lane_halves(BLK), 2 * BLK: _lane_halves(2 * BLK)}


def _half_sum(v, ones):
    h_a, h_b = ones[v.shape[0]]
    s_a = jnp.sum(jnp.where(h_a, v, 0.0), axis=-1, keepdims=True)
    s_b = jnp.sum(jnp.where(h_b, v, 0.0), axis=-1, keepdims=True)
    return jnp.where(h_a, s_a, s_b)


def _half_rms(v, w, ones):
    r = lax.rsqrt(_half_sum(v * v, ones) * (1.0 / HALF) + RMS_EPS)
    yhat = v * r
    return yhat, r, yhat * w


def _half_rms_bwd(dy, yhat, r, w, ones):
    dyh = dy * w
    c = _half_sum(dyh * yhat, ones) * (1.0 / HALF)
    return r * (dyh - yhat * c)


def _band_mask(n):
    t = lax.broadcasted_iota(jnp.int32, (BLK, 2 * BLK), 0)
    kk = lax.broadcasted_iota(jnp.int32, (BLK, 2 * BLK), 1)
    dist = t + BLK - kk
    first_key = jnp.where(n > 0, 0, BLK)
    ok = (dist >= 0) & (dist < BLK) & (kk >= first_key)
    return ok, dist.astype(F32)


def _alibi_table(n, slopes_ref, bias_scr):
    ok, distf = _band_mask(n)
    for hd in range(N_HEADS):
        bias_scr[BLK * hd:BLK * (hd + 1), :] = jnp.where(ok, -(slopes_ref[hd] * distf), NEG)


def _sink_col(sinks_ref, kt):
    return jnp.concatenate([jnp.full((BLK, 1), sinks_ref[8 * kt + i], F32) for i in range(8)], axis=0)


def _softmax_sink(s_scaled, bias, sink):
    s = s_scaled + bias
    m = jnp.maximum(jnp.max(s, axis=-1, keepdims=True), sink)
    p = jnp.exp(s - m)
    es = jnp.exp(sink - m)
    inv = 1.0 / (jnp.sum(p, axis=-1, keepdims=True) + es)
    return p * inv, es * inv


def _rows(i):
    return slice(BLK * i, BLK * (i + 1))


def _cols(base, j):
    return slice(base + BLK * j, base + BLK * (j + 1))


def _to_half(v, have, want):
    return v if have == want else pltpu.roll(v, HALF, 1)


def _tril_mask():
    row = lax.broadcasted_iota(jnp.int32, (BLK, BLK), 0)
    col = lax.broadcasted_iota(jnp.int32, (BLK, BLK), 1)
    return row >= col


def _tile(limit, s):
    t = min(limit, s)
    assert s % t == 0, (s, t)
    return t


def _mesh_place():
    x, y, c = lax.axis_index("x"), lax.axis_index("y"), lax.axis_index("c")
    return x, y, c, 4 * x + 2 * y + c


def _peer(x, y, c, k):
    px = 1 - x if k & 4 else x
    py = 1 - y if k & 2 else y
    pc = 1 - c if k & 1 else c
    return (px, py, pc), 4 * px + 2 * py + pc


class _Pushes:
    def __init__(self, n_arrays, src_view, dst_view):
        self.na = n_arrays
        self.src_view = src_view
        self.dst_view = dst_view

    def scratch(self):
        n = self.na * (N_DEV - 1)
        return [pltpu.SemaphoreType.DMA((n,)), pltpu.SemaphoreType.DMA((n,)), pltpu.SemaphoreType.DMA((self.na,))]

    def copies(self, src_refs, dst_refs, send_sems, recv_sems, local_sems):
        x, y, c, me = _mesh_place()
        cps = []
        for a in range(self.na):
            cps.append(pltpu.make_async_copy(self.src_view(a, src_refs[a], me), self.dst_view(a, dst_refs[a], me),
                                             local_sems.at[a]))
        for k in range(1, N_DEV):
            peer, pidx = _peer(x, y, c, k)
            for a in range(self.na):
                sem = a * (N_DEV - 1) + k - 1
                cps.append(pltpu.make_async_remote_copy(
                    src_ref=self.src_view(a, src_refs[a], pidx), dst_ref=self.dst_view(a, dst_refs[a], me),
                    send_sem=send_sems.at[sem], recv_sem=recv_sems.at[sem],
                    device_id=peer, device_id_type=pl.DeviceIdType.MESH))
        return cps

    def plan(self, src_refs, dst_refs, send_sems, recv_sems, local_sems):
        cps = self.copies(src_refs, dst_refs, send_sems, recv_sems, local_sems)
        return cps, [], [], [cp.wait for cp in cps]


class _TwoLevelGather:
    def __init__(self, shard_rows, layers):
        self.na = len(shard_rows)
        self.shard_rows = shard_rows
        self.layers = layers

    def scratch(self):
        n = self.na * (N_DEV - 1)
        return [pltpu.SemaphoreType.DMA((n,)), pltpu.SemaphoreType.DMA((n,)), pltpu.SemaphoreType.DMA((self.na,))]

    def plan(self, src_refs, dst_refs, send_sems, recv_sems, local_sems):
        x, y, c, _ = _mesh_place()
        sibling = (x, y, 1 - c)
        chips = [(1 - x, y), (x, 1 - y), (1 - x, 1 - y)]
        start, mid_wait, mid_start, final = [], [], [], []
        for a in range(self.na):
            r = self.shard_rows[a]
            src = src_refs[a].at[self.layers[a]]
            dst = dst_refs[a]

            def rows(px, py, pc, r=r, dst=dst):
                return dst.at[pl.ds(pl.multiple_of((4 * px + 2 * py + pc) * r, 64), r), :]

            def remote(k, s_ref, block, to, a=a, rows=rows):
                return pltpu.make_async_remote_copy(
                    src_ref=s_ref, dst_ref=rows(*block),
                    send_sem=send_sems.at[a * (N_DEV - 1) + k], recv_sem=recv_sems.at[a * (N_DEV - 1) + k],
                    device_id=to, device_id_type=pl.DeviceIdType.MESH)

            mine = pltpu.make_async_copy(src, rows(x, y, c), local_sems.at[a])
            own = [remote(0, src, (x, y, c), sibling)]
            own += [remote(1 + j, src, (x, y, c), (*chip, c)) for j, chip in enumerate(chips)]
            passed = [remote(4 + j, rows(*chip, c), (*chip, c), sibling) for j, chip in enumerate(chips)]
            start += [mine] + own
            mid_wait += own[1:]
            mid_start += passed
            final += [own[0].wait_recv] + [cp.wait_recv for cp in passed]
            final += [cp.wait_send for cp in own + passed] + [mine.wait]
        return start, mid_wait, mid_start, final


def _call(name, compute, grid, in_specs, out_specs, out_shape, scratch, args, push=None):
    sem = pltpu.CompilerParams(dimension_semantics=("arbitrary",) * len(grid))
    if push is None:
        return pl.pallas_call(compute, name=name, grid=grid, in_specs=in_specs, out_specs=out_specs,
                              out_shape=out_shape, scratch_shapes=scratch, compiler_params=sem)(*args)
    pushes, srcs, xshapes = push
    n_in, n_out, n_scr, na = len(args), len(out_shape), len(scratch), pushes.na
    hbm = pl.BlockSpec(memory_space=pltpu.HBM)

    def body(*refs):
        ins, refs = refs[:n_in], refs[n_in:]
        xin, refs = refs[:na], refs[na:]
        outs, refs = refs[:n_out], refs[n_out:]
        xout, refs = refs[:na], refs[na:]
        scr, sems = refs[:n_scr], refs[n_scr:]
        start, mid_wait, mid_start, final = pushes.plan(xin, xout, *sems)
        first = pl.program_id(0) == 0
        middle = pl.program_id(0) == (grid[0] * MID_STEP_16THS) // 16
        last = pl.program_id(0) == grid[0] - 1
        for d in range(1, len(grid)):
            first = first & (pl.program_id(d) == 0)
            middle = middle & (pl.program_id(d) == 0)
            last = last & (pl.program_id(d) == grid[d] - 1)

        @pl.when(first)
        def _():
            for cp in start:
                cp.start()

        if mid_start:
            @pl.when(middle)
            def _():
                for cp in mid_wait:
                    cp.wait_recv()
                for cp in mid_start:
                    cp.start()

        compute(*ins, *outs, *scr)

        @pl.when(last)
        def _():
            for wait in final:
                wait()

    return pl.pallas_call(
        body, name=name, grid=grid,
        in_specs=list(in_specs) + [hbm] * na, out_specs=list(out_specs) + [hbm] * na,
        out_shape=list(out_shape) + list(xshapes),
        scratch_shapes=list(scratch) + pushes.scratch(), compiler_params=sem)(*args, *srcs)


def _exchange(name, pushes, srcs, out_shapes):
    na = pushes.na
    hbm = pl.BlockSpec(memory_space=pltpu.HBM)

    def body(*refs):
        start, mid_wait, mid_start, final = pushes.plan(refs[:na], refs[na:2 * na], *refs[2 * na:])
        for cp in start:
            cp.start()
        for cp in mid_wait:
            cp.wait_recv()
        for cp in mid_start:
            cp.start()
        for wait in final:
            wait()

    return pl.pallas_call(body, name=name, in_specs=[hbm] * na, out_specs=[hbm] * na, out_shape=out_shapes,
                          scratch_shapes=pushes.scratch())(*srcs)


def _gather_rows(shard_rows, layers):
    return _TwoLevelGather(shard_rows, layers)


def _scatter_rows(shard_rows):
    def src_view(a, ref, idx):
        r = shard_rows[a]
        return ref.at[pl.ds(pl.multiple_of(idx * r, 64), r), :]

    def dst_view(a, ref, idx):
        return ref.at[idx]

    return _Pushes(len(shard_rows), src_view, dst_view)


def _gather_slots():
    return _Pushes(1, lambda a, ref, idx: ref, lambda a, ref, idx: ref.at[idx])


def _gather_slots_and_scatter_rows(rows):
    def src_view(a, ref, idx):
        return ref if a == 0 else ref.at[pl.ds(pl.multiple_of(idx * rows, 64), rows), :]

    return _Pushes(2, src_view, lambda a, ref, idx: ref.at[idx])


W_FULL = {W_IN_SHARD: jax.ShapeDtypeStruct((D_IN, D_MODEL), BF16),
          W_OUT_SHARD: jax.ShapeDtypeStruct((D_MODEL, D_MODEL), BF16)}


def _slots_shape(rows, cols, dtype):
    return jax.ShapeDtypeStruct((N_DEV, rows, cols), dtype)


def _sum_slots(slots, tr):
    _, r, c = slots.shape

    def compute(s_ref, o_ref):
        tot = s_ref[0].astype(F32)
        for d in range(1, N_DEV):
            tot = tot + s_ref[d].astype(F32)
        o_ref[...] = tot

    return _call("sum_slots", compute, (r // tr,),
                 [pl.BlockSpec((N_DEV, tr, c), lambda i: (0, i, 0))], [pl.BlockSpec((tr, c), lambda i: (i, 0))],
                 [jax.ShapeDtypeStruct((r, c), F32)], [], [slots])[0]


def _resident(shape):
    return pl.BlockSpec(shape, lambda *_: (0,) * len(shape), pipeline_mode=pl.Buffered(1))


def _norm_proj(x, g, w_t, push=None):
    s = x.shape[0]
    tm = _tile(TM_RESIDENT, s)

    def compute(x_ref, g_ref, w_ref, h_ref, p_ref, act_ref):
        xf = x_ref[...]
        r = lax.rsqrt(jnp.mean(xf * xf, axis=-1, keepdims=True) + RMS_EPS)
        h = ((xf * r) * g_ref[...]).astype(BF16)
        h_ref[...] = h
        for j in range(D_IN // TN_PROJ):
            cols = slice(j * TN_PROJ, (j + 1) * TN_PROJ)
            pj = lax.dot_general(h, w_ref[cols, :], NT_DIMS, preferred_element_type=F32)
            p_ref[:, cols] = pj
            if j * TN_PROJ >= C_GA:
                gelu = C_ZU <= j * TN_PROJ < C_GB
                a = pj * (_gelu_cdf(pj) if gelu else _sigmoid(pj))
                act_ref[:, j * TN_PROJ - C_GA:(j + 1) * TN_PROJ - C_GA] = a.astype(BF16)

    return _call(
        "norm_proj", compute, (s // tm,),
        [pl.BlockSpec((tm, D_MODEL), lambda i: (i, 0)), pl.BlockSpec((1, D_MODEL), lambda i: (0, 0)),
         _resident((D_IN, D_MODEL))],
        [pl.BlockSpec((tm, D_MODEL), lambda i: (i, 0)), pl.BlockSpec((tm, D_IN), lambda i: (i, 0)),
         pl.BlockSpec((tm, D_IN - C_GA), lambda i: (i, 0))],
        [jax.ShapeDtypeStruct((s, D_MODEL), BF16), jax.ShapeDtypeStruct((s, D_IN), F32),
         jax.ShapeDtypeStruct((s, D_IN - C_GA), BF16)],
        [], [x, g, w_t], push)


def _sgu_weights(ws_ref, wtril_scr, wtril_t_scr=None):
    tril = _tril_mask()
    for hd in range(N_HEADS):
        w = jnp.where(tril, ws_ref[hd], 0.0)
        wtril_scr[BLK * hd:BLK * (hd + 1), :] = w.astype(BF16)
        if wtril_t_scr is not None:
            wtril_t_scr[hd // 2, :, BLK * (hd % 2):BLK * (hd % 2 + 1)] = w.T.astype(BF16)


def _kv_band(kt, p_ref, kvp_ref, kw_v, ones):
    kband = jnp.concatenate([kvp_ref[:, _cols(0, kt)], p_ref[:, _cols(C_K, kt)]], axis=0)
    kyhat, kr, kn = _half_rms(kband, kw_v, ones)
    vband = jnp.concatenate([kvp_ref[:, _cols(256, kt)], p_ref[:, _cols(C_V, kt)]], axis=0)
    return kyhat, kr, (kn * 0.125).astype(BF16), vband.astype(BF16)


def _stack_heads(tiles, halves):
    parts = []
    for tt, tile in enumerate(tiles):
        for qh in range(2):
            parts.append(_to_half(jnp.where(halves[qh], tile, 0.0), qh, tt // 2).astype(BF16))
    return jnp.concatenate(parts, axis=0)


def _unstack_heads(stacked, tt, h_a):
    return jnp.where(h_a, _to_half(stacked[_rows(2 * tt)], tt // 2, 0), _to_half(stacked[_rows(2 * tt + 1)], tt // 2, 1))


def _mixer_fwd(proj, act, qw, kw, sinks, slopes, w_s, bmap, push=None):
    s = proj.shape[0]
    nb = s // BLK

    def compute(sinks_ref, slopes_ref, p_ref, kvp_ref, act_ref, qw_ref, kw_ref, ws_ref, bmap_ref, mix_ref,
                wtril_scr, bias_scr):
        _mixer_tables(pl.program_id(0), slopes_ref, ws_ref, wtril_scr, bias_scr)
        _mixer_fwd_block(sinks_ref, p_ref, kvp_ref, act_ref, qw_ref, kw_ref, bmap_ref, mix_ref, wtril_scr, bias_scr)

    smem = pl.BlockSpec(memory_space=pltpu.SMEM)
    return _call(
        "mixer_fwd", compute, (nb,),
        [smem, smem,
         pl.BlockSpec((BLK, C_GA), lambda n: (n, 0)),
         pl.BlockSpec((BLK, 512), lambda n: (jnp.maximum(n - 1, 0), 2)),
         pl.BlockSpec((BLK, D_IN - C_GA), lambda n: (n, 0)),
         pl.BlockSpec((1, BLK), lambda n: (0, 0)),
         pl.BlockSpec((1, BLK), lambda n: (0, 0)),
         pl.BlockSpec((N_HEADS, BLK, BLK), lambda n: (0, 0, 0)),
         pl.BlockSpec((BLK, D_ATTN), lambda n: (0, 0))],
        [pl.BlockSpec((BLK, D_MODEL), lambda n: (n, 0))],
        [jax.ShapeDtypeStruct((s, D_MODEL), BF16)],
        [pltpu.VMEM((N_HEADS * BLK, BLK), BF16), pltpu.VMEM((N_HEADS * BLK, 2 * BLK), F32)],
        [sinks, slopes, proj, proj, act, qw, kw, w_s, bmap], push)


def _mixer_tables(n, slopes_ref, ws_ref, wtril_scr, bias_scr, wtril_t_scr=None):
    @pl.when(n == 0)
    def _():
        _sgu_weights(ws_ref, wtril_scr, wtril_t_scr)

    @pl.when(n <= 1)
    def _():
        _alibi_table(n, slopes_ref, bias_scr)


def _mixer_fwd_block(sinks_ref, p_ref, kvp_ref, act_ref, qw_ref, kw_ref, bmap_ref, mix_ref, wtril_scr, bias_scr):
    a_ga, a_zu, a_zv, a_gb = (c - C_GA for c in (C_GA, C_ZU, C_ZV, C_GB))
    h_a, h_b = _lane_halves(BLK)
    ones = _half_masks()
    halves = (h_a, h_b)
    qw_v = qw_ref[...]
    kw_v = kw_ref[...]
    bands = [_kv_band(kt, p_ref, kvp_ref, kw_v, ones) for kt in range(2)]
    sc = []
    for kt in range(2):
        qn = [_half_rms(p_ref[:, _cols(C_Q, 4 * kt + tt)], qw_v, ones)[2] for tt in range(4)]
        sc.append(lax.dot_general(_stack_heads(qn, halves), bands[kt][2], NT_DIMS, preferred_element_type=F32))
    mixed = []
    for j in range(8):
        mixed.append(jnp.dot(wtril_scr[2 * BLK * j:2 * BLK * (j + 1), :], act_ref[:, _cols(a_zv, j)],
                             preferred_element_type=F32))
    o = []
    for kt in range(2):
        p, _ = _softmax_sink(sc[kt], bias_scr[8 * BLK * kt:8 * BLK * (kt + 1), :], _sink_col(sinks_ref, kt))
        o.append(jnp.dot(p.astype(BF16), bands[kt][3], preferred_element_type=F32))
    for j in range(8):
        mx = jnp.where(h_a, mixed[j][0:BLK], mixed[j][BLK:2 * BLK]) + bmap_ref[:, _cols(0, j)]
        mix_ref[:, _cols(D_ATTN, j)] = ((act_ref[:, _cols(a_zu, j)].astype(F32) * mx)
                                        * act_ref[:, _cols(a_gb, j)].astype(F32)).astype(BF16)
    for kt in range(2):
        for tt in range(4):
            j = 4 * kt + tt
            mix_ref[:, _cols(0, j)] = (_unstack_heads(o[kt], tt, h_a)
                                       * act_ref[:, _cols(a_ga, j)].astype(F32)).astype(BF16)


def _out_proj(mix, w_o, x):
    s = x.shape[0]
    tm = _tile(TM_WIDE, s)

    def compute(m_ref, w_ref, x_ref, o_ref):
        o_ref[...] = x_ref[...] + jnp.dot(m_ref[...], w_ref[...], preferred_element_type=F32)

    return _call(
        "out_proj", compute, (s // tm,),
        [pl.BlockSpec((tm, D_MODEL), lambda i: (i, 0)), _resident((D_MODEL, D_MODEL)),
         pl.BlockSpec((tm, D_MODEL), lambda i: (i, 0))],
        [pl.BlockSpec((tm, D_MODEL), lambda i: (i, 0))],
        [jax.ShapeDtypeStruct((s, D_MODEL), F32)], [], [mix, w_o, x])[0]


def _out_proj_loss(mix, w_o, x, tgt):
    s = x.shape[0]
    tm = _tile(TM_STREAM, s)

    def compute(m_ref, w_ref, x_ref, t_ref, dy_ref, dyb_ref, sq_ref, dmix_ref):
        @pl.when(pl.program_id(0) == 0)
        def _():
            sq_ref[...] = jnp.zeros_like(sq_ref)

        y = x_ref[...] + jnp.dot(m_ref[...], w_ref[...], preferred_element_type=F32)
        e = y - t_ref[...]
        dy = e * (1.0 / D_MODEL)
        dyb = dy.astype(BF16)
        dy_ref[...] = dy
        dyb_ref[...] = dyb
        sq_ref[...] += jnp.sum(e * e, axis=0, keepdims=True)
        dmix_ref[...] = lax.dot_general(dyb, w_ref[...], NT_DIMS, preferred_element_type=F32)

    tok = pl.BlockSpec((tm, D_MODEL), lambda i: (i, 0))
    return _call(
        "out_proj_loss", compute, (s // tm,),
        [tok, _resident((D_MODEL, D_MODEL)), tok, tok],
        [tok, tok, pl.BlockSpec((1, D_MODEL), lambda i: (0, 0)), tok],
        [jax.ShapeDtypeStruct((s, D_MODEL), F32), jax.ShapeDtypeStruct((s, D_MODEL), BF16),
         jax.ShapeDtypeStruct((1, D_MODEL), F32), jax.ShapeDtypeStruct((s, D_MODEL), F32)], [], [mix, w_o, x, tgt])


def _dmix(dx, w_o):
    s = dx.shape[0]
    tm = _tile(TM_WIDE, s)

    def compute(d_ref, w_ref, o_ref):
        o_ref[...] = lax.dot_general(d_ref[...], w_ref[...], NT_DIMS, preferred_element_type=F32)

    return _call(
        "dmix", compute, (s // tm,),
        [pl.BlockSpec((tm, D_MODEL), lambda i: (i, 0)), _resident((D_MODEL, D_MODEL))],
        [pl.BlockSpec((tm, D_MODEL), lambda i: (i, 0))],
        [jax.ShapeDtypeStruct((s, D_MODEL), F32)], [], [dx, w_o])[0]


def _dw_out(mix, dx):
    s = dx.shape[0]
    tk = _tile(TM_WIDE, s)
    nk = s // tk

    def compute(m_ref, d_ref, o_ref, acc):
        k = pl.program_id(0)

        @pl.when(k == 0)
        def _():
            acc[...] = jnp.zeros_like(acc)

        acc[...] += lax.dot_general(m_ref[...], d_ref[...], TN_DIMS, preferred_element_type=F32)

        @pl.when(k == nk - 1)
        def _():
            o_ref[...] = acc[...].astype(BF16)

    return _call(
        "dw_out", compute, (nk,),
        [pl.BlockSpec((tk, D_MODEL), lambda k: (k, 0)), pl.BlockSpec((tk, D_MODEL), lambda k: (k, 0))],
        [pl.BlockSpec((D_MODEL, D_MODEL), lambda k: (0, 0))],
        [jax.ShapeDtypeStruct((D_MODEL, D_MODEL), BF16)],
        [pltpu.VMEM((D_MODEL, D_MODEL), F32)], [mix, dx])[0]


def _mixer_bwd(proj, dmix, qw, kw, sinks, slopes, w_s, bmap, push=None):
    s = proj.shape[0]
    nb = s // BLK

    def compute(sinks_ref, slopes_ref, p_ref, kvp_ref, dm_ref, qw_ref, kw_ref, ws_ref, bmap_ref,
                dp_ref, dqw_ref, dkw_ref, dsk_ref, dws_ref, dbs_ref,
                pend, accq, acck, accs, accb, wtril_scr, wtril_t_scr, bias_scr):
        n = pl.program_id(0)
        h_a, h_b = _lane_halves(BLK)
        ones = _half_masks()
        halves = (h_a, h_b)
        lane = lax.broadcasted_iota(jnp.int32, (BLK, BLK), 1)

        @pl.when(n == 0)
        def _():
            accq[...] = jnp.zeros_like(accq)
            acck[...] = jnp.zeros_like(acck)
            accs[...] = jnp.zeros_like(accs)
            accb[...] = jnp.zeros_like(accb)
            dws_ref[...] = jnp.zeros_like(dws_ref)
            pend[...] = jnp.zeros_like(pend)
            _sgu_weights(ws_ref, wtril_scr, wtril_t_scr)

        @pl.when(n <= 1)
        def _():
            _alibi_table(n, slopes_ref, bias_scr)

        def emit_pending():
            dp_ref[:, 0:C_K] = pend[:, 0:C_K].astype(BF16)
            dp_ref[:, C_GA:D_IN] = pend[:, C_GA:D_IN].astype(BF16)

        @pl.when(n < nb)
        def _():
            emit_pending()
            qw_v = qw_ref[...]
            kw_v = kw_ref[...]
            bands = [_kv_band(kt, p_ref, kvp_ref, kw_v, ones) for kt in range(2)]
            tiles, qst, dost, sc, dpm = [], [], [], [], []
            for kt in range(2):
                qn, d_o, tl = [], [], []
                for tt in range(4):
                    j = 4 * kt + tt
                    qyhat, qr, qn_t = _half_rms(p_ref[:, _cols(C_Q, j)], qw_v, ones)
                    ga = p_ref[:, _cols(C_GA, j)]
                    sg = _sigmoid(ga)
                    dma = dm_ref[:, _cols(0, j)]
                    qn.append(qn_t)
                    d_o.append(dma * (ga * sg))
                    tl.append((qyhat, qr, dma * (sg * (1.0 + ga * (1.0 - sg)))))
                tiles.append(tl)
                qst.append(_stack_heads(qn, halves))
                dost.append(_stack_heads(d_o, halves))
                sc.append(lax.dot_general(qst[kt], bands[kt][2], NT_DIMS, preferred_element_type=F32))
                dpm.append(lax.dot_general(dost[kt], bands[kt][3], NT_DIMS, preferred_element_type=F32))
            sgu = []
            for j in range(8):
                zu_pre = p_ref[:, _cols(C_ZU, j)]
                zv_pre = p_ref[:, _cols(C_ZV, j)]
                cu = _gelu_cdf(zu_pre)
                cv = _gelu_cdf(zv_pre)
                zvb = (zv_pre * cv).astype(BF16)
                sgu.append((zu_pre * cu, zvb, _gelu_grad(zu_pre, cu), _gelu_grad(zv_pre, cv),
                            jnp.dot(wtril_scr[2 * BLK * j:2 * BLK * (j + 1), :], zvb, preferred_element_type=F32)))
            dsink = jnp.zeros((BLK, BLK), F32)
            pst, dqkst = [], []
            for kt in range(2):
                p, p_sink = _softmax_sink(sc[kt], bias_scr[8 * BLK * kt:8 * BLK * (kt + 1), :],
                                          _sink_col(sinks_ref, kt))
                dsum = jnp.sum(p * dpm[kt], axis=-1, keepdims=True)
                dsink_col = -(p_sink * dsum)
                for i in range(8):
                    dsink = dsink + jnp.where(lane == 8 * kt + i, dsink_col[_rows(i)], 0.0)
                pst.append(p.astype(BF16))
                dqkst.append((p * (dpm[kt] - dsum)).astype(BF16))
            o, dqn_all, dvb, dkn = [], [], [], []
            for kt in range(2):
                o.append(jnp.dot(pst[kt], bands[kt][3], preferred_element_type=F32))
                dqn_all.append(jnp.dot(dqkst[kt], bands[kt][2], preferred_element_type=F32))
                dvb.append(lax.dot_general(pst[kt], dost[kt], TN_DIMS, preferred_element_type=F32))
                dkn.append(0.125 * lax.dot_general(dqkst[kt], qst[kt], TN_DIMS, preferred_element_type=F32))
            dms = []
            for j in range(8):
                zu, zvb, gu, gv, m_ab = sgu[j]
                gb = p_ref[:, _cols(C_GB, j)]
                dmb = dm_ref[:, _cols(D_ATTN, j)]
                mixed = jnp.where(h_a, m_ab[0:BLK], m_ab[BLK:2 * BLK]) + bmap_ref[:, _cols(0, j)]
                sgb = _sigmoid(gb)
                dgate = dmb * (gb * sgb)
                pend[:, _cols(C_ZU, j)] = (dgate * mixed) * gu
                pend[:, _cols(C_GB, j)] = (dmb * (zu * mixed)) * (sgb * (1.0 + gb * (1.0 - sgb)))
                dmixed = dgate * zu
                accb[:, _cols(0, j)] += dmixed
                dms.append(jnp.concatenate([jnp.where(h_a, dmixed, 0.0).astype(BF16),
                                            jnp.where(h_b, dmixed, 0.0).astype(BF16)], axis=0))
            dzv = []
            for j in range(8):
                dzv.append(jnp.dot(wtril_t_scr[j], dms[j], preferred_element_type=F32))
                dw_ab = lax.dot_general(dms[j], sgu[j][1], NT_DIMS, preferred_element_type=F32)
                dws_ref[2 * j] += dw_ab[0:BLK]
                dws_ref[2 * j + 1] += dw_ab[BLK:2 * BLK]
            dq_w = jnp.zeros((BLK, BLK), F32)
            for kt in range(2):
                for tt in range(4):
                    j = 4 * kt + tt
                    qyhat, qr, dsilu = tiles[kt][tt]
                    dqn = _unstack_heads(dqn_all[kt], tt, h_a)
                    pend[:, _cols(C_GA, j)] = _unstack_heads(o[kt], tt, h_a) * dsilu
                    pend[:, _cols(C_Q, j)] = _half_rms_bwd(dqn, qyhat, qr, qw_v, ones)
                    dq_w = dq_w + dqn * qyhat
            dk_w = jnp.zeros((BLK, BLK), F32)
            for kt in range(2):
                kyhat, kr = bands[kt][0], bands[kt][1]
                dk = _half_rms_bwd(dkn[kt], kyhat, kr, kw_v, ones)
                dkw_part = dkn[kt] * kyhat
                dk_w = dk_w + (dkw_part[0:BLK] + dkw_part[BLK:2 * BLK])
                dv = dvb[kt]
                dp_ref[:, _cols(C_K, kt)] = (pend[:, _cols(C_K, kt)] + dk[0:BLK]).astype(BF16)
                dp_ref[:, _cols(C_V, kt)] = (pend[:, _cols(C_V, kt)] + dv[0:BLK]).astype(BF16)
                pend[:, _cols(C_K, kt)] = dk[BLK:2 * BLK]
                pend[:, _cols(C_V, kt)] = dv[BLK:2 * BLK]
            accq[...] += dq_w
            acck[...] += dk_w
            accs[...] += dsink
            for j in range(8):
                pend[:, _cols(C_ZV, j)] = dzv[j] * sgu[j][3]

        @pl.when(n == nb)
        def _():
            tril = _tril_mask()
            emit_pending()
            dp_ref[:, C_K:C_GA] = pend[:, C_K:C_GA].astype(BF16)
            aq = accq[...]
            ak = acck[...]
            dqw_ref[...] = jnp.sum(aq + pltpu.roll(aq, HALF, 1), axis=0, keepdims=True)
            dkw_ref[...] = jnp.sum(ak + pltpu.roll(ak, HALF, 1), axis=0, keepdims=True)
            dsk_ref[...] = jnp.sum(accs[...], axis=0, keepdims=True)
            for hd in range(N_HEADS):
                dws_ref[hd] = jnp.where(tril, dws_ref[hd], 0.0)
            hrow = lax.broadcasted_iota(jnp.int32, (N_HEADS, D_ATTN), 0)
            hcol = lax.broadcasted_iota(jnp.int32, (N_HEADS, D_ATTN), 1)
            sel = jnp.where((hcol >= hrow * HALF) & (hcol < (hrow + 1) * HALF), 1.0, 0.0).astype(BF16)
            rem = accb[...]
            tot = jnp.zeros((N_HEADS, BLK), F32)
            for _ in range(3):
                part = rem.astype(BF16)
                tot = tot + lax.dot_general(sel, part, NT_DIMS, preferred_element_type=F32)
                rem = rem - part.astype(F32)
            dbs_ref[...] = tot

    smem = pl.BlockSpec(memory_space=pltpu.SMEM)
    last = nb - 1
    tile_f32 = pltpu.VMEM((BLK, BLK), F32)
    return _call(
        "mixer_bwd", compute, (nb + 1,),
        [smem, smem,
         pl.BlockSpec((BLK, D_IN), lambda n: (jnp.minimum(n, last), 0)),
         pl.BlockSpec((BLK, 512), lambda n: (jnp.maximum(jnp.minimum(n, last) - 1, 0), 2)),
         pl.BlockSpec((BLK, D_MODEL), lambda n: (jnp.minimum(n, last), 0)),
         pl.BlockSpec((1, BLK), lambda n: (0, 0)),
         pl.BlockSpec((1, BLK), lambda n: (0, 0)),
         pl.BlockSpec((N_HEADS, BLK, BLK), lambda n: (0, 0, 0)),
         pl.BlockSpec((BLK, D_ATTN), lambda n: (0, 0))],
        [pl.BlockSpec((BLK, D_IN), lambda n: (jnp.maximum(n - 1, 0), 0)),
         pl.BlockSpec((1, BLK), lambda n: (0, 0)),
         pl.BlockSpec((1, BLK), lambda n: (0, 0)),
         pl.BlockSpec((1, BLK), lambda n: (0, 0)),
         pl.BlockSpec((N_HEADS, BLK, BLK), lambda n: (0, 0, 0)),
         pl.BlockSpec((N_HEADS, BLK), lambda n: (0, 0))],
        [jax.ShapeDtypeStruct((s, D_IN), BF16),
         jax.ShapeDtypeStruct((1, BLK), F32),
         jax.ShapeDtypeStruct((1, BLK), F32),
         jax.ShapeDtypeStruct((1, BLK), F32),
         jax.ShapeDtypeStruct((N_HEADS, BLK, BLK), F32),
         jax.ShapeDtypeStruct((N_HEADS, BLK), F32)],
        [pltpu.VMEM((BLK, D_IN), F32), tile_f32, tile_f32, tile_f32, pltpu.VMEM((BLK, D_ATTN), F32),
         pltpu.VMEM((N_HEADS * BLK, BLK), BF16), pltpu.VMEM((N_HEADS // 2, BLK, 2 * BLK), BF16),
         pltpu.VMEM((N_HEADS * BLK, 2 * BLK), F32)],
        [sinks, slopes, proj, proj, dmix, qw, kw, w_s, bmap], push)


def _dh_norm_bwd(dproj, w_t, x, dx_out, g, bf16_copy, push=None):
    s = x.shape[0]
    tm = _tile(TM_RESIDENT, s)

    def compute(dp_ref, w_ref, x_ref, dxo_ref, g_ref, dx_ref, dg_ref, *dxb_ref):
        @pl.when(pl.program_id(0) == 0)
        def _():
            dg_ref[...] = jnp.zeros_like(dg_ref)

        dh = jnp.dot(dp_ref[...], w_ref[...], preferred_element_type=F32)
        xf = x_ref[...]
        r = lax.rsqrt(jnp.mean(xf * xf, axis=-1, keepdims=True) + RMS_EPS)
        yhat = xf * r
        dyh = dh * g_ref[...]
        c = jnp.mean(dyh * yhat, axis=-1, keepdims=True)
        dx = dxo_ref[...] + r * (dyh - yhat * c)
        dx_ref[...] = dx
        if bf16_copy:
            dxb_ref[0][...] = dx.astype(BF16)
        dg_ref[...] += jnp.sum(dh * yhat, axis=0, keepdims=True)

    tok = pl.BlockSpec((tm, D_MODEL), lambda i: (i, 0))
    return _call(
        "dh_norm_bwd", compute, (s // tm,),
        [pl.BlockSpec((tm, D_IN), lambda i: (i, 0)), _resident((D_IN, D_MODEL)), tok, tok,
         pl.BlockSpec((1, D_MODEL), lambda i: (0, 0))],
        [tok, pl.BlockSpec((1, D_MODEL), lambda i: (0, 0))] + [tok] * bf16_copy,
        [jax.ShapeDtypeStruct((s, D_MODEL), F32), jax.ShapeDtypeStruct((1, D_MODEL), F32)]
        + [jax.ShapeDtypeStruct((s, D_MODEL), BF16)] * bf16_copy,
        [], [dproj, w_t, x, dx_out, g], push)


def _dw_in(dproj, h, push=None):
    s = h.shape[0]
    tk = _tile(DW_IN_TOK, s)
    nk = s // tk

    def compute(dp_ref, h_ref, o_ref, acc):
        k = pl.program_id(1)

        @pl.when(k == 0)
        def _():
            acc[...] = jnp.zeros_like(acc)

        acc[...] += lax.dot_general(dp_ref[...], h_ref[...], TN_DIMS, preferred_element_type=F32)

        @pl.when(k == nk - 1)
        def _():
            o_ref[...] = acc[...].astype(BF16)

    return _call(
        "dw_in", compute, (D_IN // DW_IN_ROWS, nk),
        [pl.BlockSpec((tk, DW_IN_ROWS), lambda j, k: (k, j)), pl.BlockSpec((tk, D_MODEL), lambda j, k: (k, 0))],
        [pl.BlockSpec((DW_IN_ROWS, D_MODEL), lambda j, k: (j, 0))],
        [jax.ShapeDtypeStruct((D_IN, D_MODEL), BF16)],
        [pltpu.VMEM((DW_IN_ROWS, D_MODEL), F32)], [dproj, h], push)


def _adamw_math(w, g, m, v):
    m_new = ADAM_B1 * m + (1.0 - ADAM_B1) * g
    v_new = ADAM_B2 * v + (1.0 - ADAM_B2) * jnp.square(g)
    m_hat = m_new / (1.0 - ADAM_B1 ** ADAM_STEP)
    v_hat = v_new / (1.0 - ADAM_B2 ** ADAM_STEP)
    return -ADAM_LR * (m_hat / (jnp.sqrt(v_hat) + ADAM_EPS) + ADAM_WD * w), m_new, v_new


def _adamw(name, w, g, m, v, tr=None):
    shape = w.shape
    c = shape[-1]
    flat = [a.reshape(-1, c) for a in (w, g, m, v)]
    r = flat[0].shape[0]
    tr = r if tr is None else tr

    def compute(w_ref, g_ref, m_ref, v_ref, d_ref, mo_ref, vo_ref):
        d_ref[...], mo_ref[...], vo_ref[...] = _adamw_math(w_ref[...], g_ref[...], m_ref[...], v_ref[...])

    spec = pl.BlockSpec((tr, c), lambda i: (i, 0))
    outs = _call(name, compute, (r // tr,), [spec] * 4, [spec] * 3, [jax.ShapeDtypeStruct((r, c), F32)] * 3,
                 [], flat)
    return [o.reshape(shape) for o in outs]


def _adamw_from_slots(name, slots, w, m, v, tr):
    n_layers, r, c = w.shape
    nt = r // tr

    def compute(*refs):
        s_refs, (w_ref, m_ref, v_ref), (g_ref, d_ref, mo_ref, vo_ref) = refs[:n_layers], refs[n_layers:n_layers + 3], \
            refs[n_layers + 3:]
        for k in range(n_layers):
            @pl.when(pl.program_id(0) == k)
            def _(k=k):
                g = s_refs[k][0].astype(F32)
                for dev in range(1, N_DEV):
                    g = g + s_refs[k][dev].astype(F32)
                g_ref[0] = g
                d_ref[0], mo_ref[0], vo_ref[0] = _adamw_math(w_ref[0], g, m_ref[0], v_ref[0])

    def slots_spec(k):
        return pl.BlockSpec((N_DEV, tr, c),
                            lambda l, i: (0, jnp.where(l == k, i, jnp.where(l < k, 0, nt - 1)), 0))

    tile = pl.BlockSpec((1, tr, c), lambda l, i: (l, i, 0))
    return _call(name, compute, (n_layers, nt), [slots_spec(k) for k in range(n_layers)] + [tile] * 3, [tile] * 4,
                 [jax.ShapeDtypeStruct((n_layers, r, c), F32)] * 4, [], list(slots) + [w, m, v])


def _pack_rows(parts):
    rows = []
    for a in parts:
        flat = a.reshape(-1)
        n = -(-flat.shape[0] // (8 * BLK)) * 8
        rows.append(jnp.pad(flat, (0, n * BLK - flat.shape[0])).reshape(n, BLK))
    return jnp.concatenate(rows, axis=0)


def _unpack_rows(packed, like):
    out = []
    row = 0
    for a in like:
        n = -(-a.size // (8 * BLK)) * 8
        out.append(packed[row:row + n].reshape(-1)[:a.size].reshape(a.shape))
        row += n
    return out


def kernel(x, norm_g, w_in, q_norm, k_norm, sinks, w_s, b_s, w_out, loss_target, m_norm_g, m_w_in, m_q_norm, m_k_norm, m_sinks, m_w_s, m_b_s, m_w_out, v_norm_g, v_w_in, v_q_norm, v_k_norm, v_sinks, v_w_s, v_b_s, v_w_out):
    xs = x[0]
    tgt = loss_target[0]
    slopes = jnp.asarray(2.0 ** (-8.0 * np.arange(1, N_HEADS + 1) / N_HEADS), dtype=F32)
    wt_sh = jnp.swapaxes(w_in, 1, 2).astype(BF16)
    wo_sh = w_out.astype(BF16)

    layer_par = []
    for l in range(DEPTH):
        layer_par.append((jnp.tile(q_norm[l], 2)[None, :], jnp.tile(k_norm[l], 2)[None, :],
                          jnp.repeat(b_s[l].T, HALF, axis=1)))

    wt_full = _exchange("gather_w0", _gather_rows((W_IN_SHARD,), (0,)), [wt_sh], [W_FULL[W_IN_SHARD]])[0]
    saved = []
    cur = xs
    for l in range(DEPTH):
        qw, kw, bmap = layer_par[l]
        more = l + 1 < DEPTH
        if more:
            h, proj, act, wo_full, wt_next = _norm_proj(
                cur, norm_g[l][None, :], wt_full,
                (_gather_rows((W_OUT_SHARD, W_IN_SHARD), (l, l + 1)), [wo_sh, wt_sh],
                 [W_FULL[W_OUT_SHARD], W_FULL[W_IN_SHARD]]))
        else:
            h, proj, act, wo_full = _norm_proj(cur, norm_g[l][None, :], wt_full,
                                          (_gather_rows((W_OUT_SHARD,), (l,)), [wo_sh], [W_FULL[W_OUT_SHARD]]))
        mix = _mixer_fwd(proj, act, qw, kw, sinks[l], slopes, w_s[l], bmap)[0]
        saved.append((cur, h, proj, mix, wt_full, wo_full))
        if more:
            cur = _out_proj(mix, wo_full, cur)
            wt_full = wt_next
        else:
            dx, dx_b, sq, dmix_top = _out_proj_loss(mix, wo_full, cur, tgt)
    loss = lax.psum(0.5 * jnp.sum(sq) / D_MODEL, MESH_AXES)

    wt_slots, wo_slots, g_small, g_norm = ([None] * DEPTH for _ in range(4))
    wt_slots_shape = _slots_shape(W_IN_SHARD, D_MODEL, BF16)
    dwt_waiting = None
    for l in reversed(range(DEPTH)):
        x_l, h, proj, mix, wt_l, wo_l = saved[l]
        qw, kw, bmap = layer_par[l]
        dmix = dmix_top if l == DEPTH - 1 else _dmix(dx_b, wo_l)
        dwo_part = _dw_out(mix, dx_b)
        res = _mixer_bwd(proj, dmix, qw, kw, sinks[l], slopes, w_s[l], bmap,
                         None if dwt_waiting is None else
                         (_scatter_rows((W_IN_SHARD,)), [dwt_waiting], [wt_slots_shape]))
        dproj, dqw, dkw, dsk, dws, dbs = res[:6]
        if dwt_waiting is not None:
            wt_slots[l + 1] = res[6]
        small_like = [dqw[0, :HALF], dkw[0, :HALF], dsk[0, :N_HEADS], dws, dbs]
        packed = _pack_rows(small_like)
        dwt_waiting, small_slots, wo_slots[l] = _dw_in(
            dproj, h, (_gather_slots_and_scatter_rows(W_OUT_SHARD), [packed, dwo_part],
                       [_slots_shape(*packed.shape, F32), _slots_shape(W_OUT_SHARD, D_MODEL, BF16)]))
        if l > 0:
            dx, dng, dx_b = _dh_norm_bwd(dproj, wt_l, x_l, dx, norm_g[l][None, :], True)
        else:
            dx, dng, wt_slots[0] = _dh_norm_bwd(dproj, wt_l, x_l, dx, norm_g[l][None, :], False,
                                                (_scatter_rows((W_IN_SHARD,)), [dwt_waiting], [wt_slots_shape]))
        g_small[l] = _unpack_rows(_sum_slots(small_slots, packed.shape[0]), small_like)
        g_norm[l] = dng[0]

    dng_all = _pack_rows([jnp.stack(g_norm)])
    dng_slots = _exchange("gather_dnorm", _gather_slots(), [dng_all], [_slots_shape(*dng_all.shape, F32)])[0]
    gr_norm = _unpack_rows(_sum_slots(dng_slots, dng_all.shape[0]), [norm_g])[0]
    gr_qn, gr_kn, gr_sk, gr_ws, gr_bs = (jnp.stack([g_small[l][i] for l in range(DEPTH)]) for i in range(5))

    def t(a):
        return jnp.swapaxes(a, 1, 2)

    from_slots = {1: _adamw_from_slots("adamw_w_in", wt_slots, t(w_in), t(m_w_in), t(v_w_in), ADAMW_ROWS),
                  7: _adamw_from_slots("adamw_w_out", wo_slots, w_out, m_w_out, v_w_out, ADAMW_ROWS)}
    from_slots[1] = [t(a) for a in from_slots[1]]

    grads = [gr_norm, None, gr_qn, gr_kn, gr_sk, gr_ws, gr_bs, None]
    weights = [norm_g, w_in, q_norm, k_norm, sinks, w_s, b_s, w_out]
    moms = [m_norm_g, m_w_in, m_q_norm, m_k_norm, m_sinks, m_w_s, m_b_s, m_w_out]
    vels = [v_norm_g, v_w_in, v_q_norm, v_k_norm, v_sinks, v_w_s, v_b_s, v_w_out]
    tiles = [None, None, None, None, None, 1024, None, None]
    names = ["norm_g", "w_in", "q_norm", "k_norm", "sinks", "w_s", "b_s", "w_out"]
    deltas, new_m, new_v = [], [], []
    for i, (nm, w, g, m, v, tr) in enumerate(zip(names, weights, grads, moms, vels, tiles)):
        if i in from_slots:
            grads[i], d, mo, vo = from_slots[i]
        else:
            d, mo, vo = _adamw("adamw_" + nm, w, g, m, v, tr)
        deltas.append(d)
        new_m.append(mo)
        new_v.append(vo)

    return (loss, dx[None], *grads, *deltas, *new_m, *new_v)
```

```python
import numpy as np
import jax
import jax.numpy as jnp
from jax import lax
from jax.experimental import pallas as pl
from jax.experimental.pallas import tpu as pltpu

F32 = jnp.float32
BF16 = jnp.bfloat16

D_MODEL = 2048
D_ATTN = 1024
D_IN = 5632
N_HEADS = 16
DEPTH = 4
BLK = 128
HALF = 64
RMS_EPS = 1e-6
C_Q, C_K, C_V, C_GA, C_ZU, C_ZV, C_GB = 0, 1024, 1280, 1536, 2560, 3584, 4608
NEG = -1e30
N_DEV = 8
W_IN_SHARD = D_IN // N_DEV
W_OUT_SHARD = D_MODEL // N_DEV
INV_SQRT2 = 0.7071067811865476
INV_SQRT_2PI = 0.3989422804014327

TM_RESIDENT = 256
TM_STREAM = 512
TM_WIDE = 1024
TN_PROJ = 512
DW_IN_ROWS = D_IN // 4
DW_IN_TOK = 2048
ADAMW_ROWS = 64
MID_STEP_16THS = 13

ADAM_LR = 0.001
ADAM_B1 = 0.9
ADAM_B2 = 0.999
ADAM_EPS = 1e-08
ADAM_WD = 0.01
ADAM_STEP = 10

NT_DIMS = (((1,), (1,)), ((), ()))
TN_DIMS = (((0,), (0,)), ((), ()))
MESH_AXES = ("x", "y", "c")


def _sigmoid(v):
    return 1.0 / (1.0 + jnp.exp(-v))


def _gelu_cdf(z):
    return 0.5 * (1.0 + lax.erf(z * INV_SQRT2))


def _gelu_grad(z, cdf):
    return cdf + z * (jnp.exp(-0.5 * z * z) * INV_SQRT_2PI)


def _lane_halves(rows):
    lane = lax.broadcasted_iota(jnp.int32, (rows, BLK), 1)
    return lane < HALF, lane >= HALF


def _half_masks():
    return {BLK: _lane_halves(BLK), 2 * BLK: _lane_halves(2 * BLK)}


def _half_sum(v, ones):
    h_a, h_b = ones[v.shape[0]]
    s_a = jnp.sum(jnp.where(h_a, v, 0.0), axis=-1, keepdims=True)
    s_b = jnp.sum(jnp.where(h_b, v, 0.0), axis=-1, keepdims=True)
    return jnp.where(h_a, s_a, s_b)


def _half_rms(v, w, ones):
    r = lax.rsqrt(_half_sum(v * v, ones) * (1.0 / HALF) + RMS_EPS)
    yhat = v * r
    return yhat, r, yhat * w


def _half_rms_bwd(dy, yhat, r, w, ones):
    dyh = dy * w
    c = _half_sum(dyh * yhat, ones) * (1.0 / HALF)
    return r * (dyh - yhat * c)


def _band_mask(n):
    t = lax.broadcasted_iota(jnp.int32, (BLK, 2 * BLK), 0)
    kk = lax.broadcasted_iota(jnp.int32, (BLK, 2 * BLK), 1)
    dist = t + BLK - kk
    first_key = jnp.where(n > 0, 0, BLK)
    ok = (dist >= 0) & (dist < BLK) & (kk >= first_key)
    return ok, dist.astype(F32)


def _alibi_table(n, slopes_ref, bias_scr):
    ok, distf = _band_mask(n)
    for hd in range(N_HEADS):
        bias_scr[BLK * hd:BLK * (hd + 1), :] = jnp.where(ok, -(slopes_ref[hd] * distf), NEG)


def _sink_col(sinks_ref, kt):
    return jnp.concatenate([jnp.full((BLK, 1), sinks_ref[8 * kt + i], F32) for i in range(8)], axis=0)


def _softmax_sink(s_scaled, bias, sink):
    s = s_scaled + bias
    m = jnp.maximum(jnp.max(s, axis=-1, keepdims=True), sink)
    p = jnp.exp(s - m)
    es = jnp.exp(sink - m)
    inv = 1.0 / (jnp.sum(p, axis=-1, keepdims=True) + es)
    return p * inv, es * inv


def _rows(i):
    return slice(BLK * i, BLK * (i + 1))


def _cols(base, j):
    return slice(base + BLK * j, base + BLK * (j + 1))


def _to_half(v, have, want):
    return v if have == want else pltpu.roll(v, HALF, 1)


def _tril_mask():
    row = lax.broadcasted_iota(jnp.int32, (BLK, BLK), 0)
    col = lax.broadcasted_iota(jnp.int32, (BLK, BLK), 1)
    return row >= col


def _tile(limit, s):
    t = min(limit, s)
    assert s % t == 0, (s, t)
    return t


def _mesh_place():
    x, y, c = lax.axis_index("x"), lax.axis_index("y"), lax.axis_index("c")
    return x, y, c, 4 * x + 2 * y + c


def _peer(x, y, c, k):
    px = 1 - x if k & 4 else x
    py = 1 - y if k & 2 else y
    pc = 1 - c if k & 1 else c
    return (px, py, pc), 4 * px + 2 * py + pc


class _Pushes:
    def __init__(self, n_arrays, src_view, dst_view):
        self.na = n_arrays
        self.src_view = src_view
        self.dst_view = dst_view

    def scratch(self):
        n = self.na * (N_DEV - 1)
        return [pltpu.SemaphoreType.DMA((n,)), pltpu.SemaphoreType.DMA((n,)), pltpu.SemaphoreType.DMA((self.na,))]

    def copies(self, src_refs, dst_refs, send_sems, recv_sems, local_sems):
        x, y, c, me = _mesh_place()
        cps = []
        for a in range(self.na):
            cps.append(pltpu.make_async_copy(self.src_view(a, src_refs[a], me), self.dst_view(a, dst_refs[a], me),
                                             local_sems.at[a]))
        for k in range(1, N_DEV):
            peer, pidx = _peer(x, y, c, k)
            for a in range(self.na):
                sem = a * (N_DEV - 1) + k - 1
                cps.append(pltpu.make_async_remote_copy(
                    src_ref=self.src_view(a, src_refs[a], pidx), dst_ref=self.dst_view(a, dst_refs[a], me),
                    send_sem=send_sems.at[sem], recv_sem=recv_sems.at[sem],
                    device_id=peer, device_id_type=pl.DeviceIdType.MESH))
        return cps

    def plan(self, src_refs, dst_refs, send_sems, recv_sems, local_sems):
        cps = self.copies(src_refs, dst_refs, send_sems, recv_sems, local_sems)
        return cps, [], [], [cp.wait for cp in cps]


class _TwoLevelGather:
    def __init__(self, shard_rows, layers):
        self.na = len(shard_rows)
        self.shard_rows = shard_rows
        self.layers = layers

    def scratch(self):
        n = self.na * (N_DEV - 1)
        return [pltpu.SemaphoreType.DMA((n,)), pltpu.SemaphoreType.DMA((n,)), pltpu.SemaphoreType.DMA((self.na,))]

    def plan(self, src_refs, dst_refs, send_sems, recv_sems, local_sems):
        x, y, c, _ = _mesh_place()
        sibling = (x, y, 1 - c)
        chips = [(1 - x, y), (x, 1 - y), (1 - x, 1 - y)]
        start, mid_wait, mid_start, final = [], [], [], []
        for a in range(self.na):
            r = self.shard_rows[a]
            src = src_refs[a].at[self.layers[a]]
            dst = dst_refs[a]

            def rows(px, py, pc, r=r, dst=dst):
                return dst.at[pl.ds(pl.multiple_of((4 * px + 2 * py + pc) * r, 64), r), :]

            def remote(k, s_ref, block, to, a=a, rows=rows):
                return pltpu.make_async_remote_copy(
                    src_ref=s_ref, dst_ref=rows(*block),
                    send_sem=send_sems.at[a * (N_DEV - 1) + k], recv_sem=recv_sems.at[a * (N_DEV - 1) + k],
                    device_id=to, device_id_type=pl.DeviceIdType.MESH)

            mine = pltpu.make_async_copy(src, rows(x, y, c), local_sems.at[a])
            own = [remote(0, src, (x, y, c), sibling)]
            own += [remote(1 + j, src, (x, y, c), (*chip, c)) for j, chip in enumerate(chips)]
            passed = [remote(4 + j, rows(*chip, c), (*chip, c), sibling) for j, chip in enumerate(chips)]
            start += [mine] + own
            mid_wait += own[1:]
            mid_start += passed
            final += [own[0].wait_recv] + [cp.wait_recv for cp in passed]
            final += [cp.wait_send for cp in own + passed] + [mine.wait]
        return start, mid_wait, mid_start, final


def _call(name, compute, grid, in_specs, out_specs, out_shape, scratch, args, push=None):
    sem = pltpu.CompilerParams(dimension_semantics=("arbitrary",) * len(grid))
    if push is None:
        return pl.pallas_call(compute, name=name, grid=grid, in_specs=in_specs, out_specs=out_specs,
                              out_shape=out_shape, scratch_shapes=scratch, compiler_params=sem)(*args)
    pushes, srcs, xshapes = push
    n_in, n_out, n_scr, na = len(args), len(out_shape), len(scratch), pushes.na
    hbm = pl.BlockSpec(memory_space=pltpu.HBM)

    def body(*refs):
        ins, refs = refs[:n_in], refs[n_in:]
        xin, refs = refs[:na], refs[na:]
        outs, refs = refs[:n_out], refs[n_out:]
        xout, refs = refs[:na], refs[na:]
        scr, sems = refs[:n_scr], refs[n_scr:]
        start, mid_wait, mid_start, final = pushes.plan(xin, xout, *sems)
        first = pl.program_id(0) == 0
        middle = pl.program_id(0) == (grid[0] * MID_STEP_16THS) // 16
        last = pl.program_id(0) == grid[0] - 1
        for d in range(1, len(grid)):
            first = first & (pl.program_id(d) == 0)
            middle = middle & (pl.program_id(d) == 0)
            last = last & (pl.program_id(d) == grid[d] - 1)

        @pl.when(first)
        def _():
            for cp in start:
                cp.start()

        if mid_start:
            @pl.when(middle)
            def _():
                for cp in mid_wait:
                    cp.wait_recv()
                for cp in mid_start:
                    cp.start()

        compute(*ins, *outs, *scr)

        @pl.when(last)
        def _():
            for wait in final:
                wait()

    return pl.pallas_call(
        body, name=name, grid=grid,
        in_specs=list(in_specs) + [hbm] * na, out_specs=list(out_specs) + [hbm] * na,
        out_shape=list(out_shape) + list(xshapes),
        scratch_shapes=list(scratch) + pushes.scratch(), compiler_params=sem)(*args, *srcs)


def _exchange(name, pushes, srcs, out_shapes):
    na = pushes.na
    hbm = pl.BlockSpec(memory_space=pltpu.HBM)

    def body(*refs):
        start, mid_wait, mid_start, final = pushes.plan(refs[:na], refs[na:2 * na], *refs[2 * na:])
        for cp in start:
            cp.start()
        for cp in mid_wait:
            cp.wait_recv()
        for cp in mid_start:
            cp.start()
        for wait in final:
            wait()

    return pl.pallas_call(body, name=name, in_specs=[hbm] * na, out_specs=[hbm] * na, out_shape=out_shapes,
                          scratch_shapes=pushes.scratch())(*srcs)


def _gather_rows(shard_rows, layers):
    return _TwoLevelGather(shard_rows, layers)


def _scatter_rows(shard_rows):
    def src_view(a, ref, idx):
        r = shard_rows[a]
        return ref.at[pl.ds(pl.multiple_of(idx * r, 64), r), :]

    def dst_view(a, ref, idx):
        return ref.at[idx]

    return _Pushes(len(shard_rows), src_view, dst_view)


def _gather_slots():
    return _Pushes(1, lambda a, ref, idx: ref, lambda a, ref, idx: ref.at[idx])


def _gather_slots_and_scatter_rows(rows):
    def src_view(a, ref, idx):
        return ref if a == 0 else ref.at[pl.ds(pl.multiple_of(idx * rows, 64), rows), :]

    return _Pushes(2, src_view, lambda a, ref, idx: ref.at[idx])


W_FULL = {W_IN_SHARD: jax.ShapeDtypeStruct((D_IN, D_MODEL), BF16),
          W_OUT_SHARD: jax.ShapeDtypeStruct((D_MODEL, D_MODEL), BF16)}


def _slots_shape(rows, cols, dtype):
    return jax.ShapeDtypeStruct((N_DEV, rows, cols), dtype)


def _sum_slots(slots, tr):
    _, r, c = slots.shape

    def compute(s_ref, o_ref):
        tot = s_ref[0].astype(F32)
        for d in range(1, N_DEV):
            tot = tot + s_ref[d].astype(F32)
        o_ref[...] = tot

    return _call("sum_slots", compute, (r // tr,),
                 [pl.BlockSpec((N_DEV, tr, c), lambda i: (0, i, 0))], [pl.BlockSpec((tr, c), lambda i: (i, 0))],
                 [jax.ShapeDtypeStruct((r, c), F32)], [], [slots])[0]


def _resident(shape):
    return pl.BlockSpec(shape, lambda *_: (0,) * len(shape), pipeline_mode=pl.Buffered(1))


def _norm_proj(x, g, w_t, push=None):
    s = x.shape[0]
    tm = _tile(TM_RESIDENT, s)

    def compute(x_ref, g_ref, w_ref, h_ref, p_ref):
        xf = x_ref[...]
        r = lax.rsqrt(jnp.mean(xf * xf, axis=-1, keepdims=True) + RMS_EPS)
        h = ((xf * r) * g_ref[...]).astype(BF16)
        h_ref[...] = h
        for j in range(D_IN // TN_PROJ):
            cols = slice(j * TN_PROJ, (j + 1) * TN_PROJ)
            p_ref[:, cols] = lax.dot_general(h, w_ref[cols, :], NT_DIMS, preferred_element_type=F32)

    return _call(
        "norm_proj", compute, (s // tm,),
        [pl.BlockSpec((tm, D_MODEL), lambda i: (i, 0)), pl.BlockSpec((1, D_MODEL), lambda i: (0, 0)),
         _resident((D_IN, D_MODEL))],
        [pl.BlockSpec((tm, D_MODEL), lambda i: (i, 0)), pl.BlockSpec((tm, D_IN), lambda i: (i, 0))],
        [jax.ShapeDtypeStruct((s, D_MODEL), BF16), jax.ShapeDtypeStruct((s, D_IN), F32)],
        [], [x, g, w_t], push)


def _sgu_weights(ws_ref, wtril_scr, wtril_t_scr=None):
    tril = _tril_mask()
    for hd in range(N_HEADS):
        w = jnp.where(tril, ws_ref[hd], 0.0)
        wtril_scr[BLK * hd:BLK * (hd + 1), :] = w.astype(BF16)
        if wtril_t_scr is not None:
            wtril_t_scr[hd // 2, :, BLK * (hd % 2):BLK * (hd % 2 + 1)] = w.T.astype(BF16)


def _kv_band(kt, p_ref, kvp_ref, kw_v, ones):
    kband = jnp.concatenate([kvp_ref[:, _cols(0, kt)], p_ref[:, _cols(C_K, kt)]], axis=0)
    kyhat, kr, kn = _half_rms(kband, kw_v, ones)
    vband = jnp.concatenate([kvp_ref[:, _cols(256, kt)], p_ref[:, _cols(C_V, kt)]], axis=0)
    return kyhat, kr, (kn * 0.125).astype(BF16), vband.astype(BF16)


def _stack_heads(tiles, halves):
    parts = []
    for tt, tile in enumerate(tiles):
        for qh in range(2):
            parts.append(_to_half(jnp.where(halves[qh], tile, 0.0), qh, tt // 2).astype(BF16))
    return jnp.concatenate(parts, axis=0)


def _unstack_heads(stacked, tt, h_a):
    return jnp.where(h_a, _to_half(stacked[_rows(2 * tt)], tt // 2, 0), _to_half(stacked[_rows(2 * tt + 1)], tt // 2, 1))


def _mixer_fwd(proj, qw, kw, sinks, slopes, w_s, bmap, push=None):
    s = proj.shape[0]
    assert s % (2 * BLK) == 0, s
    steps = s // (2 * BLK)

    def compute(sinks_ref, slopes_ref, p_ref, kvp_ref, qw_ref, kw_ref, ws_ref, bmap_ref, mix_ref,
                wtril_scr, bias_scr):
        n = pl.program_id(0)

        @pl.when(n == 0)
        def _():
            _sgu_weights(ws_ref, wtril_scr)
            _alibi_table(0, slopes_ref, bias_scr.at[0])
            _alibi_table(1, slopes_ref, bias_scr.at[1])

        first_table = jnp.where(n == 0, 0, 1)
        lo, hi = pl.ds(0, BLK), pl.ds(BLK, BLK)
        _mixer_fwd_block(sinks_ref, p_ref.at[lo], kvp_ref, qw_ref, kw_ref, bmap_ref, mix_ref.at[lo], wtril_scr,
                         lambda rows: bias_scr[first_table, rows, :])
        _mixer_fwd_block(sinks_ref, p_ref.at[hi], p_ref.at[lo, pl.ds(C_K, 512)], qw_ref, kw_ref, bmap_ref,
                         mix_ref.at[hi], wtril_scr, lambda rows: bias_scr[1, rows, :])

    smem = pl.BlockSpec(memory_space=pltpu.SMEM)
    return _call(
        "mixer_fwd", compute, (steps,),
        [smem, smem,
         pl.BlockSpec((2 * BLK, D_IN), lambda n: (n, 0)),
         pl.BlockSpec((BLK, 512), lambda n: (jnp.maximum(2 * n - 1, 0), 2)),
         pl.BlockSpec((1, BLK), lambda n: (0, 0)),
         pl.BlockSpec((1, BLK), lambda n: (0, 0)),
         pl.BlockSpec((N_HEADS, BLK, BLK), lambda n: (0, 0, 0)),
         pl.BlockSpec((BLK, D_ATTN), lambda n: (0, 0))],
        [pl.BlockSpec((2 * BLK, D_MODEL), lambda n: (n, 0))],
        [jax.ShapeDtypeStruct((s, D_MODEL), BF16)],
        [pltpu.VMEM((N_HEADS * BLK, BLK), BF16), pltpu.VMEM((2, N_HEADS * BLK, 2 * BLK), F32)],
        [sinks, slopes, proj, proj, qw, kw, w_s, bmap], push)


def _mixer_fwd_block(sinks_ref, p_ref, kvp_ref, qw_ref, kw_ref, bmap_ref, mix_ref, wtril_scr, bias_rows):
    h_a, h_b = _lane_halves(BLK)
    ones = _half_masks()
    halves = (h_a, h_b)
    qw_v = qw_ref[...]
    kw_v = kw_ref[...]
    bands = [_kv_band(kt, p_ref, kvp_ref, kw_v, ones) for kt in range(2)]
    sc = []
    for kt in range(2):
        qn = [_half_rms(p_ref[:, _cols(C_Q, 4 * kt + tt)], qw_v, ones)[2] for tt in range(4)]
        sc.append(lax.dot_general(_stack_heads(qn, halves), bands[kt][2], NT_DIMS, preferred_element_type=F32))
    zu, mixed = [], []
    for j in range(8):
        zu_pre = p_ref[:, _cols(C_ZU, j)]
        zv_pre = p_ref[:, _cols(C_ZV, j)]
        zu.append(zu_pre * _gelu_cdf(zu_pre))
        zvb = (zv_pre * _gelu_cdf(zv_pre)).astype(BF16)
        mixed.append(jnp.dot(wtril_scr[2 * BLK * j:2 * BLK * (j + 1), :], zvb, preferred_element_type=F32))
    o = []
    for kt in range(2):
        p, _ = _softmax_sink(sc[kt], bias_rows(pl.ds(8 * BLK * kt, 8 * BLK)), _sink_col(sinks_ref, kt))
        o.append(jnp.dot(p.astype(BF16), bands[kt][3], preferred_element_type=F32))
    for j in range(8):
        gb = p_ref[:, _cols(C_GB, j)]
        mx = jnp.where(h_a, mixed[j][0:BLK], mixed[j][BLK:2 * BLK]) + bmap_ref[:, _cols(0, j)]
        mix_ref[:, _cols(D_ATTN, j)] = ((zu[j] * mx) * (gb * _sigmoid(gb))).astype(BF16)
    for kt in range(2):
        for tt in range(4):
            j = 4 * kt + tt
            ga = p_ref[:, _cols(C_GA, j)]
            mix_ref[:, _cols(0, j)] = (_unstack_heads(o[kt], tt, h_a) * (ga * _sigmoid(ga))).astype(BF16)


def _out_proj(mix, w_o, x):
    s = x.shape[0]
    tm = _tile(TM_WIDE, s)

    def compute(m_ref, w_ref, x_ref, o_ref):
        o_ref[...] = x_ref[...] + jnp.dot(m_ref[...], w_ref[...], preferred_element_type=F32)

    return _call(
        "out_proj", compute, (s // tm,),
        [pl.BlockSpec((tm, D_MODEL), lambda i: (i, 0)), _resident((D_MODEL, D_MODEL)),
         pl.BlockSpec((tm, D_MODEL), lambda i: (i, 0))],
        [pl.BlockSpec((tm, D_MODEL), lambda i: (i, 0))],
        [jax.ShapeDtypeStruct((s, D_MODEL), F32)], [], [mix, w_o, x])[0]


def _out_proj_loss(mix, w_o, x, tgt):
    s = x.shape[0]
    tm = _tile(TM_STREAM, s)

    def compute(m_ref, w_ref, x_ref, t_ref, dy_ref, dyb_ref, sq_ref, dmix_ref):
        @pl.when(pl.program_id(0) == 0)
        def _():
            sq_ref[...] = jnp.zeros_like(sq_ref)

        y = x_ref[...] + jnp.dot(m_ref[...], w_ref[...], preferred_element_type=F32)
        e = y - t_ref[...]
        dy = e * (1.0 / D_MODEL)
        dyb = dy.astype(BF16)
        dy_ref[...] = dy
        dyb_ref[...] = dyb
        sq_ref[...] += jnp.sum(e * e, axis=0, keepdims=True)
        dmix_ref[...] = lax.dot_general(dyb, w_ref[...], NT_DIMS, preferred_element_type=F32)

    tok = pl.BlockSpec((tm, D_MODEL), lambda i: (i, 0))
    return _call(
        "out_proj_loss", compute, (s // tm,),
        [tok, _resident((D_MODEL, D_MODEL)), tok, tok],
        [tok, tok, pl.BlockSpec((1, D_MODEL), lambda i: (0, 0)), tok],
        [jax.ShapeDtypeStruct((s, D_MODEL), F32), jax.ShapeDtypeStruct((s, D_MODEL), BF16),
         jax.ShapeDtypeStruct((1, D_MODEL), F32), jax.ShapeDtypeStruct((s, D_MODEL), F32)], [], [mix, w_o, x, tgt])


def _dmix(dx, w_o):
    s = dx.shape[0]
    tm = _tile(TM_WIDE, s)

    def compute(d_ref, w_ref, o_ref):
        o_ref[...] = lax.dot_general(d_ref[...], w_ref[...], NT_DIMS, preferred_element_type=F32)

    return _call(
        "dmix", compute, (s // tm,),
        [pl.BlockSpec((tm, D_MODEL), lambda i: (i, 0)), _resident((D_MODEL, D_MODEL))],
        [pl.BlockSpec((tm, D_MODEL), lambda i: (i, 0))],
        [jax.ShapeDtypeStruct((s, D_MODEL), F32)], [], [dx, w_o])[0]


def _dw_out(mix, dx):
    s = dx.shape[0]
    tk = _tile(TM_WIDE, s)
    nk = s // tk

    def compute(m_ref, d_ref, o_ref, acc):
        k = pl.program_id(0)

        @pl.when(k == 0)
        def _():
            acc[...] = jnp.zeros_like(acc)

        acc[...] += lax.dot_general(m_ref[...], d_ref[...], TN_DIMS, preferred_element_type=F32)

        @pl.when(k == nk - 1)
        def _():
            o_ref[...] = acc[...].astype(BF16)

    return _call(
        "dw_out", compute, (nk,),
        [pl.BlockSpec((tk, D_MODEL), lambda k: (k, 0)), pl.BlockSpec((tk, D_MODEL), lambda k: (k, 0))],
        [pl.BlockSpec((D_MODEL, D_MODEL), lambda k: (0, 0))],
        [jax.ShapeDtypeStruct((D_MODEL, D_MODEL), BF16)],
        [pltpu.VMEM((D_MODEL, D_MODEL), F32)], [mix, dx])[0]


def _mixer_bwd(proj, dmix, qw, kw, sinks, slopes, w_s, bmap, push=None):
    s = proj.shape[0]
    nb = s // BLK

    def compute(sinks_ref, slopes_ref, p_ref, kvp_ref, dm_ref, qw_ref, kw_ref, ws_ref, bmap_ref,
                dp_ref, dqw_ref, dkw_ref, dsk_ref, dws_ref, dbs_ref,
                pend, accq, acck, accs, accb, wtril_scr, wtril_t_scr, bias_scr):
        n = pl.program_id(0)
        h_a, h_b = _lane_halves(BLK)
        ones = _half_masks()
        halves = (h_a, h_b)
        lane = lax.broadcasted_iota(jnp.int32, (BLK, BLK), 1)

        @pl.when(n == 0)
        def _():
            accq[...] = jnp.zeros_like(accq)
            acck[...] = jnp.zeros_like(acck)
            accs[...] = jnp.zeros_like(accs)
            accb[...] = jnp.zeros_like(accb)
            dws_ref[...] = jnp.zeros_like(dws_ref)
            pend[...] = jnp.zeros_like(pend)
            _sgu_weights(ws_ref, wtril_scr, wtril_t_scr)

        @pl.when(n <= 1)
        def _():
            _alibi_table(n, slopes_ref, bias_scr)

        def emit_pending():
            dp_ref[:, 0:C_K] = pend[:, 0:C_K].astype(BF16)
            dp_ref[:, C_GA:D_IN] = pend[:, C_GA:D_IN].astype(BF16)

        @pl.when(n < nb)
        def _():
            emit_pending()
            qw_v = qw_ref[...]
            kw_v = kw_ref[...]
            bands = [_kv_band(kt, p_ref, kvp_ref, kw_v, ones) for kt in range(2)]
            tiles, qst, dost, sc, dpm = [], [], [], [], []
            for kt in range(2):
                qn, d_o, tl = [], [], []
                for tt in range(4):
                    j = 4 * kt + tt
                    qyhat, qr, qn_t = _half_rms(p_ref[:, _cols(C_Q, j)], qw_v, ones)
                    ga = p_ref[:, _cols(C_GA, j)]
                    sg = _sigmoid(ga)
                    dma = dm_ref[:, _cols(0, j)]
                    qn.append(qn_t)
                    d_o.append(dma * (ga * sg))
                    tl.append((qyhat, qr, dma * (sg * (1.0 + ga * (1.0 - sg)))))
                tiles.append(tl)
                qst.append(_stack_heads(qn, halves))
                dost.append(_stack_heads(d_o, halves))
                sc.append(lax.dot_general(qst[kt], bands[kt][2], NT_DIMS, preferred_element_type=F32))
                dpm.append(lax.dot_general(dost[kt], bands[kt][3], NT_DIMS, preferred_element_type=F32))
            sgu = []
            for j in range(8):
                zu_pre = p_ref[:, _cols(C_ZU, j)]
                zv_pre = p_ref[:, _cols(C_ZV, j)]
                cu = _gelu_cdf(zu_pre)
                cv = _gelu_cdf(zv_pre)
                zvb = (zv_pre * cv).astype(BF16)
                sgu.append((zu_pre * cu, zvb, _gelu_grad(zu_pre, cu), _gelu_grad(zv_pre, cv),
                            jnp.dot(wtril_scr[2 * BLK * j:2 * BLK * (j + 1), :], zvb, preferred_element_type=F32)))
            dsink = jnp.zeros((BLK, BLK), F32)
            pst, dqkst = [], []
            for kt in range(2):
                p, p_sink = _softmax_sink(sc[kt], bias_scr[8 * BLK * kt:8 * BLK * (kt + 1), :],
                                          _sink_col(sinks_ref, kt))
                dsum = jnp.sum(p * dpm[kt], axis=-1, keepdims=True)
                dsink_col = -(p_sink * dsum)
                for i in range(8):
                    dsink = dsink + jnp.where(lane == 8 * kt + i, dsink_col[_rows(i)], 0.0)
                pst.append(p.astype(BF16))
                dqkst.append((p * (dpm[kt] - dsum)).astype(BF16))
            o, dqn_all, dvb, dkn = [], [], [], []
            for kt in range(2):
                o.append(jnp.dot(pst[kt], bands[kt][3], preferred_element_type=F32))
                dqn_all.append(jnp.dot(dqkst[kt], bands[kt][2], preferred_element_type=F32))
                dvb.append(lax.dot_general(pst[kt], dost[kt], TN_DIMS, preferred_element_type=F32))
                dkn.append(0.125 * lax.dot_general(dqkst[kt], qst[kt], TN_DIMS, preferred_element_type=F32))
            dms = []
            for j in range(8):
                zu, zvb, gu, gv, m_ab = sgu[j]
                gb = p_ref[:, _cols(C_GB, j)]
                dmb = dm_ref[:, _cols(D_ATTN, j)]
                mixed = jnp.where(h_a, m_ab[0:BLK], m_ab[BLK:2 * BLK]) + bmap_ref[:, _cols(0, j)]
                sgb = _sigmoid(gb)
                dgate = dmb * (gb * sgb)
                pend[:, _cols(C_ZU, j)] = (dgate * mixed) * gu
                pend[:, _cols(C_GB, j)] = (dmb * (zu * mixed)) * (sgb * (1.0 + gb * (1.0 - sgb)))
                dmixed = dgate * zu
                accb[:, _cols(0, j)] += dmixed
                dms.append(jnp.concatenate([jnp.where(h_a, dmixed, 0.0).astype(BF16),
                                            jnp.where(h_b, dmixed, 0.0).astype(BF16)], axis=0))
            dzv = []
            for j in range(8):
                dzv.append(jnp.dot(wtril_t_scr[j], dms[j], preferred_element_type=F32))
                dw_ab = lax.dot_general(dms[j], sgu[j][1], NT_DIMS, preferred_element_type=F32)
                dws_ref[2 * j] += dw_ab[0:BLK]
                dws_ref[2 * j + 1] += dw_ab[BLK:2 * BLK]
            dq_w = jnp.zeros((BLK, BLK), F32)
            for kt in range(2):
                for tt in range(4):
                    j = 4 * kt + tt
                    qyhat, qr, dsilu = tiles[kt][tt]
                    dqn = _unstack_heads(dqn_all[kt], tt, h_a)
                    pend[:, _cols(C_GA, j)] = _unstack_heads(o[kt], tt, h_a) * dsilu
                    pend[:, _cols(C_Q, j)] = _half_rms_bwd(dqn, qyhat, qr, qw_v, ones)
                    dq_w = dq_w + dqn * qyhat
            dk_w = jnp.zeros((BLK, BLK), F32)
            for kt in range(2):
                kyhat, kr = bands[kt][0], bands[kt][1]
                dk = _half_rms_bwd(dkn[kt], kyhat, kr, kw_v, ones)
                dkw_part = dkn[kt] * kyhat
                dk_w = dk_w + (dkw_part[0:BLK] + dkw_part[BLK:2 * BLK])
                dv = dvb[kt]
                dp_ref[:, _cols(C_K, kt)] = (pend[:, _cols(C_K, kt)] + dk[0:BLK]).astype(BF16)
                dp_ref[:, _cols(C_V, kt)] = (pend[:, _cols(C_V, kt)] + dv[0:BLK]).astype(BF16)
                pend[:, _cols(C_K, kt)] = dk[BLK:2 * BLK]
                pend[:, _cols(C_V, kt)] = dv[BLK:2 * BLK]
            accq[...] += dq_w
            acck[...] += dk_w
            accs[...] += dsink
            for j in range(8):
                pend[:, _cols(C_ZV, j)] = dzv[j] * sgu[j][3]

        @pl.when(n == nb)
        def _():
            tril = _tril_mask()
            emit_pending()
            dp_ref[:, C_K:C_GA] = pend[:, C_K:C_GA].astype(BF16)
            aq = accq[...]
            ak = acck[...]
            dqw_ref[...] = jnp.sum(aq + pltpu.roll(aq, HALF, 1), axis=0, keepdims=True)
            dkw_ref[...] = jnp.sum(ak + pltpu.roll(ak, HALF, 1), axis=0, keepdims=True)
            dsk_ref[...] = jnp.sum(accs[...], axis=0, keepdims=True)
            for hd in range(N_HEADS):
                dws_ref[hd] = jnp.where(tril, dws_ref[hd], 0.0)
            hrow = lax.broadcasted_iota(jnp.int32, (N_HEADS, D_ATTN), 0)
            hcol = lax.broadcasted_iota(jnp.int32, (N_HEADS, D_ATTN), 1)
            sel = jnp.where((hcol >= hrow * HALF) & (hcol < (hrow + 1) * HALF), 1.0, 0.0).astype(BF16)
            rem = accb[...]
            tot = jnp.zeros((N_HEADS, BLK), F32)
            for _ in range(3):
                part = rem.astype(BF16)
                tot = tot + lax.dot_general(sel, part, NT_DIMS, preferred_element_type=F32)
                rem = rem - part.astype(F32)
            dbs_ref[...] = tot

    smem = pl.BlockSpec(memory_space=pltpu.SMEM)
    last = nb - 1
    tile_f32 = pltpu.VMEM((BLK, BLK), F32)
    return _call(
        "mixer_bwd", compute, (nb + 1,),
        [smem, smem,
         pl.BlockSpec((BLK, D_IN), lambda n: (jnp.minimum(n, last), 0)),
         pl.BlockSpec((BLK, 512), lambda n: (jnp.maximum(jnp.minimum(n, last) - 1, 0), 2)),
         pl.BlockSpec((BLK, D_MODEL), lambda n: (jnp.minimum(n, last), 0)),
         pl.BlockSpec((1, BLK), lambda n: (0, 0)),
         pl.BlockSpec((1, BLK), lambda n: (0, 0)),
         pl.BlockSpec((N_HEADS, BLK, BLK), lambda n: (0, 0, 0)),
         pl.BlockSpec((BLK, D_ATTN), lambda n: (0, 0))],
        [pl.BlockSpec((BLK, D_IN), lambda n: (jnp.maximum(n - 1, 0), 0)),
         pl.BlockSpec((1, BLK), lambda n: (0, 0)),
         pl.BlockSpec((1, BLK), lambda n: (0, 0)),
         pl.BlockSpec((1, BLK), lambda n: (0, 0)),
         pl.BlockSpec((N_HEADS, BLK, BLK), lambda n: (0, 0, 0)),
         pl.BlockSpec((N_HEADS, BLK), lambda n: (0, 0))],
        [jax.ShapeDtypeStruct((s, D_IN), BF16),
         jax.ShapeDtypeStruct((1, BLK), F32),
         jax.ShapeDtypeStruct((1, BLK), F32),
         jax.ShapeDtypeStruct((1, BLK), F32),
         jax.ShapeDtypeStruct((N_HEADS, BLK, BLK), F32),
         jax.ShapeDtypeStruct((N_HEADS, BLK), F32)],
        [pltpu.VMEM((BLK, D_IN), F32), tile_f32, tile_f32, tile_f32, pltpu.VMEM((BLK, D_ATTN), F32),
         pltpu.VMEM((N_HEADS * BLK, BLK), BF16), pltpu.VMEM((N_HEADS // 2, BLK, 2 * BLK), BF16),
         pltpu.VMEM((N_HEADS * BLK, 2 * BLK), F32)],
        [sinks, slopes, proj, proj, dmix, qw, kw, w_s, bmap], push)


def _dh_norm_bwd(dproj, w_t, x, dx_out, g, bf16_copy, push=None):
    s = x.shape[0]
    tm = _tile(TM_RESIDENT, s)

    def compute(dp_ref, w_ref, x_ref, dxo_ref, g_ref, dx_ref, dg_ref, *dxb_ref):
        @pl.when(pl.program_id(0) == 0)
        def _():
            dg_ref[...] = jnp.zeros_like(dg_ref)

        dh = jnp.dot(dp_ref[...], w_ref[...], preferred_element_type=F32)
        xf = x_ref[...]
        r = lax.rsqrt(jnp.mean(xf * xf, axis=-1, keepdims=True) + RMS_EPS)
        yhat = xf * r
        dyh = dh * g_ref[...]
        c = jnp.mean(dyh * yhat, axis=-1, keepdims=True)
        dx = dxo_ref[...] + r * (dyh - yhat * c)
        dx_ref[...] = dx
        if bf16_copy:
            dxb_ref[0][...] = dx.astype(BF16)
        dg_ref[...] += jnp.sum(dh * yhat, axis=0, keepdims=True)

    tok = pl.BlockSpec((tm, D_MODEL), lambda i: (i, 0))
    return _call(
        "dh_norm_bwd", compute, (s // tm,),
        [pl.BlockSpec((tm, D_IN), lambda i: (i, 0)), _resident((D_IN, D_MODEL)), tok, tok,
         pl.BlockSpec((1, D_MODEL), lambda i: (0, 0))],
        [tok, pl.BlockSpec((1, D_MODEL), lambda i: (0, 0))] + [tok] * bf16_copy,
        [jax.ShapeDtypeStruct((s, D_MODEL), F32), jax.ShapeDtypeStruct((1, D_MODEL), F32)]
        + [jax.ShapeDtypeStruct((s, D_MODEL), BF16)] * bf16_copy,
        [], [dproj, w_t, x, dx_out, g], push)


def _dw_in(dproj, h, push=None):
    s = h.shape[0]
    tk = _tile(DW_IN_TOK, s)
    nk = s // tk

    def compute(dp_ref, h_ref, o_ref, acc):
        k = pl.program_id(1)

        @pl.when(k == 0)
        def _():
            acc[...] = jnp.zeros_like(acc)

        acc[...] += lax.dot_general(dp_ref[...], h_ref[...], TN_DIMS, preferred_element_type=F32)

        @pl.when(k == nk - 1)
        def _():
            o_ref[...] = acc[...].astype(BF16)

    return _call(
        "dw_in", compute, (D_IN // DW_IN_ROWS, nk),
        [pl.BlockSpec((tk, DW_IN_ROWS), lambda j, k: (k, j)), pl.BlockSpec((tk, D_MODEL), lambda j, k: (k, 0))],
        [pl.BlockSpec((DW_IN_ROWS, D_MODEL), lambda j, k: (j, 0))],
        [jax.ShapeDtypeStruct((D_IN, D_MODEL), BF16)],
        [pltpu.VMEM((DW_IN_ROWS, D_MODEL), F32)], [dproj, h], push)


def _adamw_math(w, g, m, v):
    m_new = ADAM_B1 * m + (1.0 - ADAM_B1) * g
    v_new = ADAM_B2 * v + (1.0 - ADAM_B2) * jnp.square(g)
    m_hat = m_new / (1.0 - ADAM_B1 ** ADAM_STEP)
    v_hat = v_new / (1.0 - ADAM_B2 ** ADAM_STEP)
    return -ADAM_LR * (m_hat / (jnp.sqrt(v_hat) + ADAM_EPS) + ADAM_WD * w), m_new, v_new


def _adamw(name, w, g, m, v, tr=None):
    shape = w.shape
    c = shape[-1]
    flat = [a.reshape(-1, c) for a in (w, g, m, v)]
    r = flat[0].shape[0]
    tr = r if tr is None else tr

    def compute(w_ref, g_ref, m_ref, v_ref, d_ref, mo_ref, vo_ref):
        d_ref[...], mo_ref[...], vo_ref[...] = _adamw_math(w_ref[...], g_ref[...], m_ref[...], v_ref[...])

    spec = pl.BlockSpec((tr, c), lambda i: (i, 0))
    outs = _call(name, compute, (r // tr,), [spec] * 4, [spec] * 3, [jax.ShapeDtypeStruct((r, c), F32)] * 3,
                 [], flat)
    return [o.reshape(shape) for o in outs]


def _adamw_from_slots(name, slots, w, m, v, tr):
    n_layers, r, c = w.shape
    nt = r // tr

    def compute(*refs):
        s_refs, (w_ref, m_ref, v_ref), (g_ref, d_ref, mo_ref, vo_ref) = refs[:n_layers], refs[n_layers:n_layers + 3], \
            refs[n_layers + 3:]
        for k in range(n_layers):
            @pl.when(pl.program_id(0) == k)
            def _(k=k):
                g = s_refs[k][0].astype(F32)
                for dev in range(1, N_DEV):
                    g = g + s_refs[k][dev].astype(F32)
                g_ref[0] = g
                d_ref[0], mo_ref[0], vo_ref[0] = _adamw_math(w_ref[0], g, m_ref[0], v_ref[0])

    def slots_spec(k):
        return pl.BlockSpec((N_DEV, tr, c),
                            lambda l, i: (0, jnp.where(l == k, i, jnp.where(l < k, 0, nt - 1)), 0))

    tile = pl.BlockSpec((1, tr, c), lambda l, i: (l, i, 0))
    return _call(name, compute, (n_layers, nt), [slots_spec(k) for k in range(n_layers)] + [tile] * 3, [tile] * 4,
                 [jax.ShapeDtypeStruct((n_layers, r, c), F32)] * 4, [], list(slots) + [w, m, v])


def _pack_rows(parts):
    rows = []
    for a in parts:
        flat = a.reshape(-1)
        n = -(-flat.shape[0] // (8 * BLK)) * 8
        rows.append(jnp.pad(flat, (0, n * BLK - flat.shape[0])).reshape(n, BLK))
    return jnp.concatenate(rows, axis=0)


def _unpack_rows(packed, like):
    out = []
    row = 0
    for a in like:
        n = -(-a.size // (8 * BLK)) * 8
        out.append(packed[row:row + n].reshape(-1)[:a.size].reshape(a.shape))
        row += n
    return out


def kernel(x, norm_g, w_in, q_norm, k_norm, sinks, w_s, b_s, w_out, loss_target, m_norm_g, m_w_in, m_q_norm, m_k_norm, m_sinks, m_w_s, m_b_s, m_w_out, v_norm_g, v_w_in, v_q_norm, v_k_norm, v_sinks, v_w_s, v_b_s, v_w_out):
    xs = x[0]
    tgt = loss_target[0]
    slopes = jnp.asarray(2.0 ** (-8.0 * np.arange(1, N_HEADS + 1) / N_HEADS), dtype=F32)
    wt_sh = jnp.swapaxes(w_in, 1, 2).astype(BF16)
    wo_sh = w_out.astype(BF16)

    layer_par = []
    for l in range(DEPTH):
        layer_par.append((jnp.tile(q_norm[l], 2)[None, :], jnp.tile(k_norm[l], 2)[None, :],
                          jnp.repeat(b_s[l].T, HALF, axis=1)))

    wt_full = _exchange("gather_w0", _gather_rows((W_IN_SHARD,), (0,)), [wt_sh], [W_FULL[W_IN_SHARD]])[0]
    saved = []
    cur = xs
    for l in range(DEPTH):
        qw, kw, bmap = layer_par[l]
        more = l + 1 < DEPTH
        if more:
            h, proj, wo_full, wt_next = _norm_proj(
                cur, norm_g[l][None, :], wt_full,
                (_gather_rows((W_OUT_SHARD, W_IN_SHARD), (l, l + 1)), [wo_sh, wt_sh],
                 [W_FULL[W_OUT_SHARD], W_FULL[W_IN_SHARD]]))
        else:
            h, proj, wo_full = _norm_proj(cur, norm_g[l][None, :], wt_full,
                                          (_gather_rows((W_OUT_SHARD,), (l,)), [wo_sh], [W_FULL[W_OUT_SHARD]]))
        mix = _mixer_fwd(proj, qw, kw, sinks[l], slopes, w_s[l], bmap)[0]
        saved.append((cur, h, proj, mix, wt_full, wo_full))
        if more:
            cur = _out_proj(mix, wo_full, cur)
            wt_full = wt_next
        else:
            dx, dx_b, sq, dmix_top = _out_proj_loss(mix, wo_full, cur, tgt)
    loss = lax.psum(0.5 * jnp.sum(sq) / D_MODEL, MESH_AXES)

    wt_slots, wo_slots, g_small, g_norm = ([None] * DEPTH for _ in range(4))
    wt_slots_shape = _slots_shape(W_IN_SHARD, D_MODEL, BF16)
    dwt_waiting = None
    for l in reversed(range(DEPTH)):
        x_l, h, proj, mix, wt_l, wo_l = saved[l]
        qw, kw, bmap = layer_par[l]
        dmix = dmix_top if l == DEPTH - 1 else _dmix(dx_b, wo_l)
        dwo_part = _dw_out(mix, dx_b)
        res = _mixer_bwd(proj, dmix, qw, kw, sinks[l], slopes, w_s[l], bmap,
                         None if dwt_waiting is None else
                         (_scatter_rows((W_IN_SHARD,)), [dwt_waiting], [wt_slots_shape]))
        dproj, dqw, dkw, dsk, dws, dbs = res[:6]
        if dwt_waiting is not None:
            wt_slots[l + 1] = res[6]
        small_like = [dqw[0, :HALF], dkw[0, :HALF], dsk[0, :N_HEADS], dws, dbs]
        packed = _pack_rows(small_like)
        dwt_waiting, small_slots, wo_slots[l] = _dw_in(
            dproj, h, (_gather_slots_and_scatter_rows(W_OUT_SHARD), [packed, dwo_part],
                       [_slots_shape(*packed.shape, F32), _slots_shape(W_OUT_SHARD, D_MODEL, BF16)]))
        if l > 0:
            dx, dng, dx_b = _dh_norm_bwd(dproj, wt_l, x_l, dx, norm_g[l][None, :], True)
        else:
            dx, dng, wt_slots[0] = _dh_norm_bwd(dproj, wt_l, x_l, dx, norm_g[l][None, :], False,
                                                (_scatter_rows((W_IN_SHARD,)), [dwt_waiting], [wt_slots_shape]))
        g_small[l] = _unpack_rows(_sum_slots(small_slots, packed.shape[0]), small_like)
        g_norm[l] = dng[0]

    dng_all = _pack_rows([jnp.stack(g_norm)])
    dng_slots = _exchange("gather_dnorm", _gather_slots(), [dng_all], [_slots_shape(*dng_all.shape, F32)])[0]
    gr_norm = _unpack_rows(_sum_slots(dng_slots, dng_all.shape[0]), [norm_g])[0]
    gr_qn, gr_kn, gr_sk, gr_ws, gr_bs = (jnp.stack([g_small[l][i] for l in range(DEPTH)]) for i in range(5))

    def t(a):
        return jnp.swapaxes(a, 1, 2)

    from_slots = {1: _adamw_from_slots("adamw_w_in", wt_slots, t(w_in), t(m_w_in), t(v_w_in), ADAMW_ROWS),
                  7: _adamw_from_slots("adamw_w_out", wo_slots, w_out, m_w_out, v_w_out, ADAMW_ROWS)}
    from_slots[1] = [t(a) for a in from_slots[1]]

    grads = [gr_norm, None, gr_qn, gr_kn, gr_sk, gr_ws, gr_bs, None]
    weights = [norm_g, w_in, q_norm, k_norm, sinks, w_s, b_s, w_out]
    moms = [m_norm_g, m_w_in, m_q_norm, m_k_norm, m_sinks, m_w_s, m_b_s, m_w_out]
    vels = [v_norm_g, v_w_in, v_q_norm, v_k_norm, v_sinks, v_w_s, v_b_s, v_w_out]
    tiles = [None, None, None, None, None, 1024, None, None]
    names = ["norm_g", "w_in", "q_norm", "k_norm", "sinks", "w_s", "b_s", "w_out"]
    deltas, new_m, new_v = [], [], []
    for i, (nm, w, g, m, v, tr) in enumerate(zip(names, weights, grads, moms, vels, tiles)):
        if i in from_slots:
            grads[i], d, mo, vo = from_slots[i]
        else:
            d, mo, vo = _adamw("adamw_" + nm, w, g, m, v, tr)
        deltas.append(d)
        new_m.append(mo)
        new_v.append(vo)

    return (loss, dx[None], *grads, *deltas, *new_m, *new_v)
```

```python
import numpy as np
import jax
import jax.numpy as jnp
from jax import lax
from jax.experimental import pallas as pl
from jax.experimental.pallas import tpu as pltpu

F32 = jnp.float32
BF16 = jnp.bfloat16

D_MODEL = 2048
D_ATTN = 1024
D_IN = 5632
N_HEADS = 16
DEPTH = 4
BLK = 128
HALF = 64
RMS_EPS = 1e-6
C_Q, C_K, C_V, C_GA, C_ZU, C_ZV, C_GB = 0, 1024, 1280, 1536, 2560, 3584, 4608
NEG = -1e30
N_DEV = 8
W_IN_SHARD = D_IN // N_DEV
W_OUT_SHARD = D_MODEL // N_DEV
INV_SQRT2 = 0.7071067811865476
INV_SQRT_2PI = 0.3989422804014327

TM_RESIDENT = 256
TM_STREAM = 512
TM_WIDE = 1024
TN_PROJ = 512
DW_IN_ROWS = D_IN // 4
DW_IN_TOK = 2048
ADAMW_ROWS = 64
MID_STEP_16THS = 13

ADAM_LR = 0.001
ADAM_B1 = 0.9
ADAM_B2 = 0.999
ADAM_EPS = 1e-08
ADAM_WD = 0.01
ADAM_STEP = 10

NT_DIMS = (((1,), (1,)), ((), ()))
TN_DIMS = (((0,), (0,)), ((), ()))
MESH_AXES = ("x", "y", "c")


def _sigmoid(v):
    return 1.0 / (1.0 + jnp.exp(-v))


def _gelu_cdf(z):
    return 0.5 * (1.0 + lax.erf(z * INV_SQRT2))


def _gelu_grad(z, cdf):
    return cdf + z * (jnp.exp(-0.5 * z * z) * INV_SQRT_2PI)


def _lane_halves(rows):
    lane = lax.broadcasted_iota(jnp.int32, (rows, BLK), 1)
    return lane < HALF, lane >= HALF


def _half_masks():
    return {BLK: _lane_halves(BLK), 2 * BLK: _lane_halves(2 * BLK)}


def _half_sum(v, ones):
    h_a, h_b = ones[v.shape[0]]
    s_a = jnp.sum(jnp.where(h_a, v, 0.0), axis=-1, keepdims=True)
    s_b = jnp.sum(jnp.where(h_b, v, 0.0), axis=-1, keepdims=True)
    return jnp.where(h_a, s_a, s_b)


def _half_rms(v, w, ones):
    r = lax.rsqrt(_half_sum(v * v, ones) * (1.0 / HALF) + RMS_EPS)
    yhat = v * r
    return yhat, r, yhat * w


def _half_rms_bwd(dy, yhat, r, w, ones):
    dyh = dy * w
    c = _half_sum(dyh * yhat, ones) * (1.0 / HALF)
    return r * (dyh - yhat * c)


def _band_mask(n):
    t = lax.broadcasted_iota(jnp.int32, (BLK, 2 * BLK), 0)
    kk = lax.broadcasted_iota(jnp.int32, (BLK, 2 * BLK), 1)
    dist = t + BLK - kk
    first_key = jnp.where(n > 0, 0, BLK)
    ok = (dist >= 0) & (dist < BLK) & (kk >= first_key)
    return ok, dist.astype(F32)


def _alibi_table(n, slopes_ref, bias_scr):
    ok, distf = _band_mask(n)
    for hd in range(N_HEADS):
        bias_scr[BLK * hd:BLK * (hd + 1), :] = jnp.where(ok, -(slopes_ref[hd] * distf), NEG)


def _sink_col(sinks_ref, kt):
    return jnp.concatenate([jnp.full((BLK, 1), sinks_ref[8 * kt + i], F32) for i in range(8)], axis=0)


def _softmax_sink(s_scaled, bias, sink):
    s = s_scaled + bias
    m = jnp.maximum(jnp.max(s, axis=-1, keepdims=True), sink)
    p = jnp.exp(s - m)
    es = jnp.exp(sink - m)
    inv = 1.0 / (jnp.sum(p, axis=-1, keepdims=True) + es)
    return p * inv, es * inv


def _rows(i):
    return slice(BLK * i, BLK * (i + 1))


def _cols(base, j):
    return slice(base + BLK * j, base + BLK * (j + 1))


def _to_half(v, have, want):
    return v if have == want else pltpu.roll(v, HALF, 1)


def _tril_mask():
    row = lax.broadcasted_iota(jnp.int32, (BLK, BLK), 0)
    col = lax.broadcasted_iota(jnp.int32, (BLK, BLK), 1)
    return row >= col


def _tile(limit, s):
    t = min(limit, s)
    assert s % t == 0, (s, t)
    return t


def _mesh_place():
    x, y, c = lax.axis_index("x"), lax.axis_index("y"), lax.axis_index("c")
    return x, y, c, 4 * x + 2 * y + c


def _peer(x, y, c, k):
    px = 1 - x if k & 4 else x
    py = 1 - y if k & 2 else y
    pc = 1 - c if k & 1 else c
    return (px, py, pc), 4 * px + 2 * py + pc


class _Pushes:
    def __init__(self, n_arrays, src_view, dst_view):
        self.na = n_arrays
        self.src_view = src_view
        self.dst_view = dst_view

    def scratch(self):
        n = self.na * (N_DEV - 1)
        return [pltpu.SemaphoreType.DMA((n,)), pltpu.SemaphoreType.DMA((n,)), pltpu.SemaphoreType.DMA((self.na,))]

    def copies(self, src_refs, dst_refs, send_sems, recv_sems, local_sems):
        x, y, c, me = _mesh_place()
        cps = []
        for a in range(self.na):
            cps.append(pltpu.make_async_copy(self.src_view(a, src_refs[a], me), self.dst_view(a, dst_refs[a], me),
                                             local_sems.at[a]))
        for k in range(1, N_DEV):
            peer, pidx = _peer(x, y, c, k)
            for a in range(self.na):
                sem = a * (N_DEV - 1) + k - 1
                cps.append(pltpu.make_async_remote_copy(
                    src_ref=self.src_view(a, src_refs[a], pidx), dst_ref=self.dst_view(a, dst_refs[a], me),
                    send_sem=send_sems.at[sem], recv_sem=recv_sems.at[sem],
                    device_id=peer, device_id_type=pl.DeviceIdType.MESH))
        return cps

    def plan(self, src_refs, dst_refs, send_sems, recv_sems, local_sems):
        cps = self.copies(src_refs, dst_refs, send_sems, recv_sems, local_sems)
        return cps, [], [], [cp.wait for cp in cps]


class _TwoLevelGather:
    def __init__(self, shard_rows, layers):
        self.na = len(shard_rows)
        self.shard_rows = shard_rows
        self.layers = layers

    def scratch(self):
        n = self.na * (N_DEV - 1)
        return [pltpu.SemaphoreType.DMA((n,)), pltpu.SemaphoreType.DMA((n,)), pltpu.SemaphoreType.DMA((self.na,))]

    def plan(self, src_refs, dst_refs, send_sems, recv_sems, local_sems):
        x, y, c, _ = _mesh_place()
        sibling = (x, y, 1 - c)
        chips = [(1 - x, y), (x, 1 - y), (1 - x, 1 - y)]
        start, mid_wait, mid_start, final = [], [], [], []
        for a in range(self.na):
            r = self.shard_rows[a]
            src = src_refs[a].at[self.layers[a]]
            dst = dst_refs[a]

            def rows(px, py, pc, r=r, dst=dst):
                return dst.at[pl.ds(pl.multiple_of((4 * px + 2 * py + pc) * r, 64), r), :]

            def remote(k, s_ref, block, to, a=a, rows=rows):
                return pltpu.make_async_remote_copy(
                    src_ref=s_ref, dst_ref=rows(*block),
                    send_sem=send_sems.at[a * (N_DEV - 1) + k], recv_sem=recv_sems.at[a * (N_DEV - 1) + k],
                    device_id=to, device_id_type=pl.DeviceIdType.MESH)

            mine = pltpu.make_async_copy(src, rows(x, y, c), local_sems.at[a])
            own = [remote(0, src, (x, y, c), sibling)]
            own += [remote(1 + j, src, (x, y, c), (*chip, c)) for j, chip in enumerate(chips)]
            passed = [remote(4 + j, rows(*chip, c), (*chip, c), sibling) for j, chip in enumerate(chips)]
            start += [mine] + own
            mid_wait += own[1:]
            mid_start += passed
            final += [own[0].wait_recv] + [cp.wait_recv for cp in passed]
            final += [cp.wait_send for cp in own + passed] + [mine.wait]
        return start, mid_wait, mid_start, final


def _call(name, compute, grid, in_specs, out_specs, out_shape, scratch, args, push=None):
    sem = pltpu.CompilerParams(dimension_semantics=("arbitrary",) * len(grid))
    if push is None:
        return pl.pallas_call(compute, name=name, grid=grid, in_specs=in_specs, out_specs=out_specs,
                              out_shape=out_shape, scratch_shapes=scratch, compiler_params=sem)(*args)
    pushes, srcs, xshapes = push
    n_in, n_out, n_scr, na = len(args), len(out_shape), len(scratch), pushes.na
    hbm = pl.BlockSpec(memory_space=pltpu.HBM)

    def body(*refs):
        ins, refs = refs[:n_in], refs[n_in:]
        xin, refs = refs[:na], refs[na:]
        outs, refs = refs[:n_out], refs[n_out:]
        xout, refs = refs[:na], refs[na:]
        scr, sems = refs[:n_scr], refs[n_scr:]
        start, mid_wait, mid_start, final = pushes.plan(xin, xout, *sems)
        first = pl.program_id(0) == 0
        middle = pl.program_id(0) == (grid[0] * MID_STEP_16THS) // 16
        last = pl.program_id(0) == grid[0] - 1
        for d in range(1, len(grid)):
            first = first & (pl.program_id(d) == 0)
            middle = middle & (pl.program_id(d) == 0)
            last = last & (pl.program_id(d) == grid[d] - 1)

        @pl.when(first)
        def _():
            for cp in start:
                cp.start()

        if mid_start:
            @pl.when(middle)
            def _():
                for cp in mid_wait:
                    cp.wait_recv()
                for cp in mid_start:
                    cp.start()

        compute(*ins, *outs, *scr)

        @pl.when(last)
        def _():
            for wait in final:
                wait()

    return pl.pallas_call(
        body, name=name, grid=grid,
        in_specs=list(in_specs) + [hbm] * na, out_specs=list(out_specs) + [hbm] * na,
        out_shape=list(out_shape) + list(xshapes),
        scratch_shapes=list(scratch) + pushes.scratch(), compiler_params=sem)(*args, *srcs)


def _exchange(name, pushes, srcs, out_shapes):
    na = pushes.na
    hbm = pl.BlockSpec(memory_space=pltpu.HBM)

    def body(*refs):
        start, mid_wait, mid_start, final = pushes.plan(refs[:na], refs[na:2 * na], *refs[2 * na:])
        for cp in start:
            cp.start()
        for cp in mid_wait:
            cp.wait_recv()
        for cp in mid_start:
            cp.start()
        for wait in final:
            wait()

    return pl.pallas_call(body, name=name, in_specs=[hbm] * na, out_specs=[hbm] * na, out_shape=out_shapes,
                          scratch_shapes=pushes.scratch())(*srcs)


def _gather_rows(shard_rows, layers):
    return _TwoLevelGather(shard_rows, layers)


def _scatter_rows(shard_rows):
    def src_view(a, ref, idx):
        r = shard_rows[a]
        return ref.at[pl.ds(pl.multiple_of(idx * r, 64), r), :]

    def dst_view(a, ref, idx):
        return ref.at[idx]

    return _Pushes(len(shard_rows), src_view, dst_view)


def _gather_slots():
    return _Pushes(1, lambda a, ref, idx: ref, lambda a, ref, idx: ref.at[idx])


def _gather_slots_and_scatter_rows(rows):
    def src_view(a, ref, idx):
        return ref if a == 0 else ref.at[pl.ds(pl.multiple_of(idx * rows, 64), rows), :]

    return _Pushes(2, src_view, lambda a, ref, idx: ref.at[idx])


W_FULL = {W_IN_SHARD: jax.ShapeDtypeStruct((D_IN, D_MODEL), BF16),
          W_OUT_SHARD: jax.ShapeDtypeStruct((D_MODEL, D_MODEL), BF16)}


def _slots_shape(rows, cols, dtype):
    return jax.ShapeDtypeStruct((N_DEV, rows, cols), dtype)


def _sum_slots(slots, tr):
    _, r, c = slots.shape

    def compute(s_ref, o_ref):
        tot = s_ref[0].astype(F32)
        for d in range(1, N_DEV):
            tot = tot + s_ref[d].astype(F32)
        o_ref[...] = tot

    return _call("sum_slots", compute, (r // tr,),
                 [pl.BlockSpec((N_DEV, tr, c), lambda i: (0, i, 0))], [pl.BlockSpec((tr, c), lambda i: (i, 0))],
                 [jax.ShapeDtypeStruct((r, c), F32)], [], [slots])[0]


def _resident(shape):
    return pl.BlockSpec(shape, lambda *_: (0,) * len(shape), pipeline_mode=pl.Buffered(1))


def _norm_proj(x, g, w_t, push=None):
    s = x.shape[0]
    tm = _tile(TM_RESIDENT, s)

    def compute(x_ref, g_ref, w_ref, h_ref, p_ref):
        xf = x_ref[...]
        r = lax.rsqrt(jnp.mean(xf * xf, axis=-1, keepdims=True) + RMS_EPS)
        h = ((xf * r) * g_ref[...]).astype(BF16)
        h_ref[...] = h
        for j in range(D_IN // TN_PROJ):
            cols = slice(j * TN_PROJ, (j + 1) * TN_PROJ)
            p_ref[:, cols] = lax.dot_general(h, w_ref[cols, :], NT_DIMS, preferred_element_type=F32)

    return _call(
        "norm_proj", compute, (s // tm,),
        [pl.BlockSpec((tm, D_MODEL), lambda i: (i, 0)), pl.BlockSpec((1, D_MODEL), lambda i: (0, 0)),
         _resident((D_IN, D_MODEL))],
        [pl.BlockSpec((tm, D_MODEL), lambda i: (i, 0)), pl.BlockSpec((tm, D_IN), lambda i: (i, 0))],
        [jax.ShapeDtypeStruct((s, D_MODEL), BF16), jax.ShapeDtypeStruct((s, D_IN), F32)],
        [], [x, g, w_t], push)


def _sgu_weights(ws_ref, wtril_scr, wtril_t_scr=None):
    tril = _tril_mask()
    for hd in range(N_HEADS):
        w = jnp.where(tril, ws_ref[hd], 0.0)
        wtril_scr[BLK * hd:BLK * (hd + 1), :] = w.astype(BF16)
        if wtril_t_scr is not None:
            wtril_t_scr[hd // 2, :, BLK * (hd % 2):BLK * (hd % 2 + 1)] = w.T.astype(BF16)


def _kv_band(kt, p_ref, kvp_ref, kw_v, ones):
    kband = jnp.concatenate([kvp_ref[:, _cols(0, kt)], p_ref[:, _cols(C_K, kt)]], axis=0)
    kyhat, kr, kn = _half_rms(kband, kw_v, ones)
    vband = jnp.concatenate([kvp_ref[:, _cols(256, kt)], p_ref[:, _cols(C_V, kt)]], axis=0)
    return kyhat, kr, (kn * 0.125).astype(BF16), vband.astype(BF16)


def _stack_heads(tiles, halves):
    parts = []
    for tt, tile in enumerate(tiles):
        for qh in range(2):
            parts.append(_to_half(jnp.where(halves[qh], tile, 0.0), qh, tt // 2).astype(BF16))
    return jnp.concatenate(parts, axis=0)


def _unstack_heads(stacked, tt, h_a):
    return jnp.where(h_a, _to_half(stacked[_rows(2 * tt)], tt // 2, 0), _to_half(stacked[_rows(2 * tt + 1)], tt // 2, 1))


def _mixer_fwd(proj, qw, kw, sinks, slopes, w_s, bmap, push=None):
    s = proj.shape[0]
    assert s % (2 * BLK) == 0, s
    steps = s // (2 * BLK)

    def compute(sinks_ref, slopes_ref, p_ref, kvp_ref, qw_ref, kw_ref, ws_ref, bmap_ref, mix_ref,
                wtril_scr, bias_scr):
        n = pl.program_id(0)

        @pl.when(n == 0)
        def _():
            _sgu_weights(ws_ref, wtril_scr)
            _alibi_table(0, slopes_ref, bias_scr.at[0])
            _alibi_table(1, slopes_ref, bias_scr.at[1])

        first_table = jnp.where(n == 0, 0, 1)
        lo, hi = pl.ds(0, BLK), pl.ds(BLK, BLK)
        _mixer_fwd_block(sinks_ref, p_ref.at[lo], kvp_ref, qw_ref, kw_ref, bmap_ref, mix_ref.at[lo], wtril_scr,
                         lambda rows: bias_scr[first_table, rows, :])
        _mixer_fwd_block(sinks_ref, p_ref.at[hi], p_ref.at[lo, pl.ds(C_K, 512)], qw_ref, kw_ref, bmap_ref,
                         mix_ref.at[hi], wtril_scr, lambda rows: bias_scr[1, rows, :])

    smem = pl.BlockSpec(memory_space=pltpu.SMEM)
    return _call(
        "mixer_fwd", compute, (steps,),
        [smem, smem,
         pl.BlockSpec((2 * BLK, D_IN), lambda n: (n, 0)),
         pl.BlockSpec((BLK, 512), lambda n: (jnp.maximum(2 * n - 1, 0), 2)),
         pl.BlockSpec((1, BLK), lambda n: (0, 0)),
         pl.BlockSpec((1, BLK), lambda n: (0, 0)),
         pl.BlockSpec((N_HEADS, BLK, BLK), lambda n: (0, 0, 0)),
         pl.BlockSpec((BLK, D_ATTN), lambda n: (0, 0))],
        [pl.BlockSpec((2 * BLK, D_MODEL), lambda n: (n, 0))],
        [jax.ShapeDtypeStruct((s, D_MODEL), BF16)],
        [pltpu.VMEM((N_HEADS * BLK, BLK), BF16), pltpu.VMEM((2, N_HEADS * BLK, 2 * BLK), F32)],
        [sinks, slopes, proj, proj, qw, kw, w_s, bmap], push)


def _mixer_fwd_block(sinks_ref, p_ref, kvp_ref, qw_ref, kw_ref, bmap_ref, mix_ref, wtril_scr, bias_rows):
    h_a, h_b = _lane_halves(BLK)
    ones = _half_masks()
    halves = (h_a, h_b)
    qw_v = qw_ref[...]
    kw_v = kw_ref[...]
    bands = [_kv_band(kt, p_ref, kvp_ref, kw_v, ones) for kt in range(2)]
    sc = []
    for kt in range(2):
        qn = [_half_rms(p_ref[:, _cols(C_Q, 4 * kt + tt)], qw_v, ones)[2] for tt in range(4)]
        sc.append(lax.dot_general(_stack_heads(qn, halves), bands[kt][2], NT_DIMS, preferred_element_type=F32))
    zu, mixed = [], []
    for j in range(8):
        zu_pre = p_ref[:, _cols(C_ZU, j)]
        zv_pre = p_ref[:, _cols(C_ZV, j)]
        zu.append(zu_pre * _gelu_cdf(zu_pre))
        zvb = (zv_pre * _gelu_cdf(zv_pre)).astype(BF16)
        mixed.append(jnp.dot(wtril_scr[2 * BLK * j:2 * BLK * (j + 1), :], zvb, preferred_element_type=F32))
    o = []
    for kt in range(2):
        p, _ = _softmax_sink(sc[kt], bias_rows(pl.ds(8 * BLK * kt, 8 * BLK)), _sink_col(sinks_ref, kt))
        o.append(jnp.dot(p.astype(BF16), bands[kt][3], preferred_element_type=F32))
    for j in range(8):
        gb = p_ref[:, _cols(C_GB, j)]
        mx = jnp.where(h_a, mixed[j][0:BLK], mixed[j][BLK:2 * BLK]) + bmap_ref[:, _cols(0, j)]
        mix_ref[:, _cols(D_ATTN, j)] = ((zu[j] * mx) * (gb * _sigmoid(gb))).astype(BF16)
    for kt in range(2):
        for tt in range(4):
            j = 4 * kt + tt
            ga = p_ref[:, _cols(C_GA, j)]
            mix_ref[:, _cols(0, j)] = (_unstack_heads(o[kt], tt, h_a) * (ga * _sigmoid(ga))).astype(BF16)


def _out_proj(mix, w_o, x):
    s = x.shape[0]
    tm = _tile(TM_WIDE, s)

    def compute(m_ref, w_ref, x_ref, o_ref):
        o_ref[...] = x_ref[...] + jnp.dot(m_ref[...], w_ref[...], preferred_element_type=F32)

    return _call(
        "out_proj", compute, (s // tm,),
        [pl.BlockSpec((tm, D_MODEL), lambda i: (i, 0)), _resident((D_MODEL, D_MODEL)),
         pl.BlockSpec((tm, D_MODEL), lambda i: (i, 0))],
        [pl.BlockSpec((tm, D_MODEL), lambda i: (i, 0))],
        [jax.ShapeDtypeStruct((s, D_MODEL), F32)], [], [mix, w_o, x])[0]


def _out_proj_loss(mix, w_o, x, tgt):
    s = x.shape[0]
    tm = _tile(TM_STREAM, s)

    def compute(m_ref, w_ref, x_ref, t_ref, dy_ref, dyb_ref, sq_ref, dmix_ref):
        @pl.when(pl.program_id(0) == 0)
        def _():
            sq_ref[...] = jnp.zeros_like(sq_ref)

        y = x_ref[...] + jnp.dot(m_ref[...], w_ref[...], preferred_element_type=F32)
        e = y - t_ref[...]
        dy = e * (1.0 / D_MODEL)
        dyb = dy.astype(BF16)
        dy_ref[...] = dy
        dyb_ref[...] = dyb
        sq_ref[...] += jnp.sum(e * e, axis=0, keepdims=True)
        dmix_ref[...] = lax.dot_general(dyb, w_ref[...], NT_DIMS, preferred_element_type=F32)

    tok = pl.BlockSpec((tm, D_MODEL), lambda i: (i, 0))
    return _call(
        "out_proj_loss", compute, (s // tm,),
        [tok, _resident((D_MODEL, D_MODEL)), tok, tok],
        [tok, tok, pl.BlockSpec((1, D_MODEL), lambda i: (0, 0)), tok],
        [jax.ShapeDtypeStruct((s, D_MODEL), F32), jax.ShapeDtypeStruct((s, D_MODEL), BF16),
         jax.ShapeDtypeStruct((1, D_MODEL), F32), jax.ShapeDtypeStruct((s, D_MODEL), F32)], [], [mix, w_o, x, tgt])


def _dmix(dx, w_o):
    s = dx.shape[0]
    tm = _tile(TM_WIDE, s)

    def compute(d_ref, w_ref, o_ref):
        o_ref[...] = lax.dot_general(d_ref[...], w_ref[...], NT_DIMS, preferred_element_type=F32)

    return _call(
        "dmix", compute, (s // tm,),
        [pl.BlockSpec((tm, D_MODEL), lambda i: (i, 0)), _resident((D_MODEL, D_MODEL))],
        [pl.BlockSpec((tm, D_MODEL), lambda i: (i, 0))],
        [jax.ShapeDtypeStruct((s, D_MODEL), F32)], [], [dx, w_o])[0]


def _dw_out(mix, dx):
    s = dx.shape[0]
    tk = _tile(TM_WIDE, s)
    nk = s // tk

    def compute(m_ref, d_ref, o_ref, acc):
        k = pl.program_id(0)

        @pl.when(k == 0)
        def _():
            acc[...] = jnp.zeros_like(acc)

        acc[...] += lax.dot_general(m_ref[...], d_ref[...], TN_DIMS, preferred_element_type=F32)

        @pl.when(k == nk - 1)
        def _():
            o_ref[...] = acc[...].astype(BF16)

    return _call(
        "dw_out", compute, (nk,),
        [pl.BlockSpec((tk, D_MODEL), lambda k: (k, 0)), pl.BlockSpec((tk, D_MODEL), lambda k: (k, 0))],
        [pl.BlockSpec((D_MODEL, D_MODEL), lambda k: (0, 0))],
        [jax.ShapeDtypeStruct((D_MODEL, D_MODEL), BF16)],
        [pltpu.VMEM((D_MODEL, D_MODEL), F32)], [mix, dx])[0]


def _mixer_bwd(proj, dmix, qw, kw, sinks, slopes, w_s, bmap, push=None):
    s = proj.shape[0]
    nb = s // BLK

    def compute(sinks_ref, slopes_ref, p_ref, kvp_ref, dm_ref, qw_ref, kw_ref, ws_ref, bmap_ref,
                dp_ref, dqw_ref, dkw_ref, dsk_ref, dws_ref, dbs_ref,
                pend, accq, acck, accs, accb, wtril_scr, wtril_t_scr, bias_scr):
        n = pl.program_id(0)
        h_a, h_b = _lane_halves(BLK)
        ones = _half_masks()
        halves = (h_a, h_b)
        lane = lax.broadcasted_iota(jnp.int32, (BLK, BLK), 1)

        @pl.when(n == 0)
        def _():
            accq[...] = jnp.zeros_like(accq)
            acck[...] = jnp.zeros_like(acck)
            accs[...] = jnp.zeros_like(accs)
            accb[...] = jnp.zeros_like(accb)
            dws_ref[...] = jnp.zeros_like(dws_ref)
            pend[...] = jnp.zeros_like(pend)
            _sgu_weights(ws_ref, wtril_scr, wtril_t_scr)

        @pl.when(n <= 1)
        def _():
            _alibi_table(n, slopes_ref, bias_scr)

        def emit_pending():
            dp_ref[:, 0:C_K] = pend[:, 0:C_K].astype(BF16)
            dp_ref[:, C_GA:D_IN] = pend[:, C_GA:D_IN].astype(BF16)

        @pl.when(n < nb)
        def _():
            emit_pending()
            qw_v = qw_ref[...]
            kw_v = kw_ref[...]
            bands = [_kv_band(kt, p_ref, kvp_ref, kw_v, ones) for kt in range(2)]
            tiles, qst, dost, sc, dpm = [], [], [], [], []
            for kt in range(2):
                qn, d_o, tl = [], [], []
                for tt in range(4):
                    j = 4 * kt + tt
                    qyhat, qr, qn_t = _half_rms(p_ref[:, _cols(C_Q, j)], qw_v, ones)
                    ga = p_ref[:, _cols(C_GA, j)]
                    sg = _sigmoid(ga)
                    dma = dm_ref[:, _cols(0, j)]
                    qn.append(qn_t)
                    d_o.append(dma * (ga * sg))
                    tl.append((qyhat, qr, dma * (sg * (1.0 + ga * (1.0 - sg)))))
                tiles.append(tl)
                qst.append(_stack_heads(qn, halves))
                dost.append(_stack_heads(d_o, halves))
                sc.append(lax.dot_general(qst[kt], bands[kt][2], NT_DIMS, preferred_element_type=F32))
                dpm.append(lax.dot_general(dost[kt], bands[kt][3], NT_DIMS, preferred_element_type=F32))
            sgu = []
            for j in range(8):
                zu_pre = p_ref[:, _cols(C_ZU, j)]
                zv_pre = p_ref[:, _cols(C_ZV, j)]
                cu = _gelu_cdf(zu_pre)
                cv = _gelu_cdf(zv_pre)
                zvb = (zv_pre * cv).astype(BF16)
                sgu.append((zu_pre * cu, zvb, _gelu_grad(zu_pre, cu), _gelu_grad(zv_pre, cv),
                            jnp.dot(wtril_scr[2 * BLK * j:2 * BLK * (j + 1), :], zvb, preferred_element_type=F32)))
            dsink = jnp.zeros((BLK, BLK), F32)
            pst, dqkst = [], []
            for kt in range(2):
                p, p_sink = _softmax_sink(sc[kt], bias_scr[8 * BLK * kt:8 * BLK * (kt + 1), :],
                                          _sink_col(sinks_ref, kt))
                dsum = jnp.sum(p * dpm[kt], axis=-1, keepdims=True)
                dsink_col = -(p_sink * dsum)
                for i in range(8):
                    dsink = dsink + jnp.where(lane == 8 * kt + i, dsink_col[_rows(i)], 0.0)
                pst.append(p.astype(BF16))
                dqkst.append((p * (dpm[kt] - dsum)).astype(BF16))
            o, dqn_all, dvb, dkn = [], [], [], []
            for kt in range(2):
                o.append(jnp.dot(pst[kt], bands[kt][3], preferred_element_type=F32))
                dqn_all.append(jnp.dot(dqkst[kt], bands[kt][2], preferred_element_type=F32))
                dvb.append(lax.dot_general(pst[kt], dost[kt], TN_DIMS, preferred_element_type=F32))
                dkn.append(0.125 * lax.dot_general(dqkst[kt], qst[kt], TN_DIMS, preferred_element_type=F32))
            dms = []
            for j in range(8):
                zu, zvb, gu, gv, m_ab = sgu[j]
                gb = p_ref[:, _cols(C_GB, j)]
                dmb = dm_ref[:, _cols(D_ATTN, j)]
                mixed = jnp.where(h_a, m_ab[0:BLK], m_ab[BLK:2 * BLK]) + bmap_ref[:, _cols(0, j)]
                sgb = _sigmoid(gb)
                dgate = dmb * (gb * sgb)
                pend[:, _cols(C_ZU, j)] = (dgate * mixed) * gu
                pend[:, _cols(C_GB, j)] = (dmb * (zu * mixed)) * (sgb * (1.0 + gb * (1.0 - sgb)))
                dmixed = dgate * zu
                accb[:, _cols(0, j)] += dmixed
                dms.append(jnp.concatenate([jnp.where(h_a, dmixed, 0.0).astype(BF16),
                                            jnp.where(h_b, dmixed, 0.0).astype(BF16)], axis=0))
            dzv = []
            for j in range(8):
                dzv.append(jnp.dot(wtril_t_scr[j], dms[j], preferred_element_type=F32))
                dw_ab = lax.dot_general(dms[j], sgu[j][1], NT_DIMS, preferred_element_type=F32)
                dws_ref[2 * j] += dw_ab[0:BLK]
                dws_ref[2 * j + 1] += dw_ab[BLK:2 * BLK]
            dq_w = jnp.zeros((BLK, BLK), F32)
            for kt in range(2):
                for tt in range(4):
                    j = 4 * kt + tt
                    qyhat, qr, dsilu = tiles[kt][tt]
                    dqn = _unstack_heads(dqn_all[kt], tt, h_a)
                    pend[:, _cols(C_GA, j)] = _unstack_heads(o[kt], tt, h_a) * dsilu
                    pend[:, _cols(C_Q, j)] = _half_rms_bwd(dqn, qyhat, qr, qw_v, ones)
                    dq_w = dq_w + dqn * qyhat
            dk_w = jnp.zeros((BLK, BLK), F32)
            for kt in range(2):
                kyhat, kr = bands[kt][0], bands[kt][1]
                dk = _half_rms_bwd(dkn[kt], kyhat, kr, kw_v, ones)
                dkw_part = dkn[kt] * kyhat
                dk_w = dk_w + (dkw_part[0:BLK] + dkw_part[BLK:2 * BLK])
                dv = dvb[kt]
                dp_ref[:, _cols(C_K, kt)] = (pend[:, _cols(C_K, kt)] + dk[0:BLK]).astype(BF16)
                dp_ref[:, _cols(C_V, kt)] = (pend[:, _cols(C_V, kt)] + dv[0:BLK]).astype(BF16)
                pend[:, _cols(C_K, kt)] = dk[BLK:2 * BLK]
                pend[:, _cols(C_V, kt)] = dv[BLK:2 * BLK]
            accq[...] += dq_w
            acck[...] += dk_w
            accs[...] += dsink
            for j in range(8):
                pend[:, _cols(C_ZV, j)] = dzv[j] * sgu[j][3]

        @pl.when(n == nb)
        def _():
            tril = _tril_mask()
            emit_pending()
            dp_ref[:, C_K:C_GA] = pend[:, C_K:C_GA].astype(BF16)
            aq = accq[...]
            ak = acck[...]
            dqw_ref[...] = jnp.sum(aq + pltpu.roll(aq, HALF, 1), axis=0, keepdims=True)
            dkw_ref[...] = jnp.sum(ak + pltpu.roll(ak, HALF, 1), axis=0, keepdims=True)
            dsk_ref[...] = jnp.sum(accs[...], axis=0, keepdims=True)
            for hd in range(N_HEADS):
                dws_ref[hd] = jnp.where(tril, dws_ref[hd], 0.0)
            hrow = lax.broadcasted_iota(jnp.int32, (N_HEADS, D_ATTN), 0)
            hcol = lax.broadcasted_iota(jnp.int32, (N_HEADS, D_ATTN), 1)
            sel = jnp.where((hcol >= hrow * HALF) & (hcol < (hrow + 1) * HALF), 1.0, 0.0).astype(BF16)
            rem = accb[...]
            tot = jnp.zeros((N_HEADS, BLK), F32)
            for _ in range(3):
                part = rem.astype(BF16)
                tot = tot + lax.dot_general(sel, part, NT_DIMS, preferred_element_type=F32)
                rem = rem - part.astype(F32)
            dbs_ref[...] = tot

    smem = pl.BlockSpec(memory_space=pltpu.SMEM)
    last = nb - 1
    tile_f32 = pltpu.VMEM((BLK, BLK), F32)
    return _call(
        "mixer_bwd", compute, (nb + 1,),
        [smem, smem,
         pl.BlockSpec((BLK, D_IN), lambda n: (jnp.minimum(n, last), 0)),
         pl.BlockSpec((BLK, 512), lambda n: (jnp.maximum(jnp.minimum(n, last) - 1, 0), 2)),
         pl.BlockSpec((BLK, D_MODEL), lambda n: (jnp.minimum(n, last), 0)),
         pl.BlockSpec((1, BLK), lambda n: (0, 0)),
         pl.BlockSpec((1, BLK), lambda n: (0, 0)),
         pl.BlockSpec((N_HEADS, BLK, BLK), lambda n: (0, 0, 0)),
         pl.BlockSpec((BLK, D_ATTN), lambda n: (0, 0))],
        [pl.BlockSpec((BLK, D_IN), lambda n: (jnp.maximum(n - 1, 0), 0)),
         pl.BlockSpec((1, BLK), lambda n: (0, 0)),
         pl.BlockSpec((1, BLK), lambda n: (0, 0)),
         pl.BlockSpec((1, BLK), lambda n: (0, 0)),
         pl.BlockSpec((N_HEADS, BLK, BLK), lambda n: (0, 0, 0)),
         pl.BlockSpec((N_HEADS, BLK), lambda n: (0, 0))],
        [jax.ShapeDtypeStruct((s, D_IN), BF16),
         jax.ShapeDtypeStruct((1, BLK), F32),
         jax.ShapeDtypeStruct((1, BLK), F32),
         jax.ShapeDtypeStruct((1, BLK), F32),
         jax.ShapeDtypeStruct((N_HEADS, BLK, BLK), F32),
         jax.ShapeDtypeStruct((N_HEADS, BLK), F32)],
        [pltpu.VMEM((BLK, D_IN), F32), tile_f32, tile_f32, tile_f32, pltpu.VMEM((BLK, D_ATTN), F32),
         pltpu.VMEM((N_HEADS * BLK, BLK), BF16), pltpu.VMEM((N_HEADS // 2, BLK, 2 * BLK), BF16),
         pltpu.VMEM((N_HEADS * BLK, 2 * BLK), F32)],
        [sinks, slopes, proj, proj, dmix, qw, kw, w_s, bmap], push)


def _dh_norm_bwd(dproj, w_t, x, dx_out, g, bf16_copy, push=None):
    s = x.shape[0]
    tm = _tile(TM_RESIDENT, s)

    def compute(dp_ref, w_ref, x_ref, dxo_ref, g_ref, dx_ref, dg_ref, *dxb_ref):
        @pl.when(pl.program_id(0) == 0)
        def _():
            dg_ref[...] = jnp.zeros_like(dg_ref)

        dh = jnp.dot(dp_ref[...], w_ref[...], preferred_element_type=F32)
        xf = x_ref[...]
        r = lax.rsqrt(jnp.mean(xf * xf, axis=-1, keepdims=True) + RMS_EPS)
        yhat = xf * r
        dyh = dh * g_ref[...]
        c = jnp.mean(dyh * yhat, axis=-1, keepdims=True)
        dx = dxo_ref[...] + r * (dyh - yhat * c)
        dx_ref[...] = dx
        if bf16_copy:
            dxb_ref[0][...] = dx.astype(BF16)
        dg_ref[...] += jnp.sum(dh * yhat, axis=0, keepdims=True)

    tok = pl.BlockSpec((tm, D_MODEL), lambda i: (i, 0))
    return _call(
        "dh_norm_bwd", compute, (s // tm,),
        [pl.BlockSpec((tm, D_IN), lambda i: (i, 0)), _resident((D_IN, D_MODEL)), tok, tok,
         pl.BlockSpec((1, D_MODEL), lambda i: (0, 0))],
        [tok, pl.BlockSpec((1, D_MODEL), lambda i: (0, 0))] + [tok] * bf16_copy,
        [jax.ShapeDtypeStruct((s, D_MODEL), F32), jax.ShapeDtypeStruct((1, D_MODEL), F32)]
        + [jax.ShapeDtypeStruct((s, D_MODEL), BF16)] * bf16_copy,
        [], [dproj, w_t, x, dx_out, g], push)


def _dw_in(dproj, h, push=None):
    s = h.shape[0]
    tk = _tile(DW_IN_TOK, s)
    nk = s // tk

    def compute(dp_ref, h_ref, o_ref, acc):
        k = pl.program_id(1)

        @pl.when(k == 0)
        def _():
            acc[...] = jnp.zeros_like(acc)

        acc[...] += lax.dot_general(dp_ref[...], h_ref[...], TN_DIMS, preferred_element_type=F32)

        @pl.when(k == nk - 1)
        def _():
            o_ref[...] = acc[...].astype(BF16)

    return _call(
        "dw_in", compute, (D_IN // DW_IN_ROWS, nk),
        [pl.BlockSpec((tk, DW_IN_ROWS), lambda j, k: (k, j)), pl.BlockSpec((tk, D_MODEL), lambda j, k: (k, 0))],
        [pl.BlockSpec((DW_IN_ROWS, D_MODEL), lambda j, k: (j, 0))],
        [jax.ShapeDtypeStruct((D_IN, D_MODEL), BF16)],
        [pltpu.VMEM((DW_IN_ROWS, D_MODEL), F32)], [dproj, h], push)


def _adamw_math(w, g, m, v):
    m_new = ADAM_B1 * m + (1.0 - ADAM_B1) * g
    v_new = ADAM_B2 * v + (1.0 - ADAM_B2) * jnp.square(g)
    m_hat = m_new / (1.0 - ADAM_B1 ** ADAM_STEP)
    v_hat = v_new / (1.0 - ADAM_B2 ** ADAM_STEP)
    return -ADAM_LR * (m_hat / (jnp.sqrt(v_hat) + ADAM_EPS) + ADAM_WD * w), m_new, v_new


def _adamw(name, w, g, m, v, tr=None):
    shape = w.shape
    c = shape[-1]
    flat = [a.reshape(-1, c) for a in (w, g, m, v)]
    r = flat[0].shape[0]
    tr = r if tr is None else tr

    def compute(w_ref, g_ref, m_ref, v_ref, d_ref, mo_ref, vo_ref):
        d_ref[...], mo_ref[...], vo_ref[...] = _adamw_math(w_ref[...], g_ref[...], m_ref[...], v_ref[...])

    spec = pl.BlockSpec((tr, c), lambda i: (i, 0))
    outs = _call(name, compute, (r // tr,), [spec] * 4, [spec] * 3, [jax.ShapeDtypeStruct((r, c), F32)] * 3,
                 [], flat)
    return [o.reshape(shape) for o in outs]


def _adamw_from_slots(name, slots, w, m, v, tr):
    n_layers, r, c = w.shape
    nt = r // tr

    def compute(*refs):
        s_refs, (w_ref, m_ref, v_ref), (g_ref, d_ref, mo_ref, vo_ref) = refs[:n_layers], refs[n_layers:n_layers + 3], \
            refs[n_layers + 3:]
        for k in range(n_layers):
            @pl.when(pl.program_id(0) == k)
            def _(k=k):
                g = s_refs[k][0].astype(F32)
                for dev in range(1, N_DEV):
                    g = g + s_refs[k][dev].astype(F32)
                g_ref[0] = g
                d_ref[0], mo_ref[0], vo_ref[0] = _adamw_math(w_ref[0], g, m_ref[0], v_ref[0])

    def slots_spec(k):
        return pl.BlockSpec((N_DEV, tr, c),
                            lambda l, i: (0, jnp.where(l == k, i, jnp.where(l < k, 0, nt - 1)), 0))

    tile = pl.BlockSpec((1, tr, c), lambda l, i: (l, i, 0))
    return _call(name, compute, (n_layers, nt), [slots_spec(k) for k in range(n_layers)] + [tile] * 3, [tile] * 4,
                 [jax.ShapeDtypeStruct((n_layers, r, c), F32)] * 4, [], list(slots) + [w, m, v])


def _pack_rows(parts):
    rows = []
    for a in parts:
        flat = a.reshape(-1)
        n = -(-flat.shape[0] // (8 * BLK)) * 8
        rows.append(jnp.pad(flat, (0, n * BLK - flat.shape[0])).reshape(n, BLK))
    return jnp.concatenate(rows, axis=0)


def _unpack_rows(packed, like):
    out = []
    row = 0
    for a in like:
        n = -(-a.size // (8 * BLK)) * 8
        out.append(packed[row:row + n].reshape(-1)[:a.size].reshape(a.shape))
        row += n
    return out


def kernel(x, norm_g, w_in, q_norm, k_norm, sinks, w_s, b_s, w_out, loss_target, m_norm_g, m_w_in, m_q_norm, m_k_norm, m_sinks, m_w_s, m_b_s, m_w_out, v_norm_g, v_w_in, v_q_norm, v_k_norm, v_sinks, v_w_s, v_b_s, v_w_out):
    xs = x[0]
    tgt = loss_target[0]
    slopes = jnp.asarray(2.0 ** (-8.0 * np.arange(1, N_HEADS + 1) / N_HEADS), dtype=F32)
    wt_sh = jnp.swapaxes(w_in, 1, 2).astype(BF16)
    wo_sh = w_out.astype(BF16)

    layer_par = []
    for l in range(DEPTH):
        layer_par.append((jnp.tile(q_norm[l], 2)[None, :], jnp.tile(k_norm[l], 2)[None, :],
                          jnp.repeat(b_s[l].T, HALF, axis=1)))

    wt_full = _exchange("gather_w0", _gather_rows((W_IN_SHARD,), (0,)), [wt_sh], [W_FULL[W_IN_SHARD]])[0]
    saved = []
    cur = xs
    for l in range(DEPTH):
        qw, kw, bmap = layer_par[l]
        more = l + 1 < DEPTH
        if more:
            h, proj, wo_full, wt_next = _norm_proj(
                cur, norm_g[l][None, :], wt_full,
                (_gather_rows((W_OUT_SHARD, W_IN_SHARD), (l, l + 1)), [wo_sh, wt_sh],
                 [W_FULL[W_OUT_SHARD], W_FULL[W_IN_SHARD]]))
        else:
            h, proj, wo_full = _norm_proj(cur, norm_g[l][None, :], wt_full,
                                          (_gather_rows((W_OUT_SHARD,), (l,)), [wo_sh], [W_FULL[W_OUT_SHARD]]))
        mix = _mixer_fwd(proj, qw, kw, sinks[l], slopes, w_s[l], bmap)[0]
        saved.append((cur, h, proj, mix, wt_full, wo_full))
        if more:
            cur = _out_proj(mix, wo_full, cur)
            wt_full = wt_next
        else:
            dx, dx_b, sq, dmix_top = _out_proj_loss(mix, wo_full, cur, tgt)
    loss_part = jnp.reshape(0.5 * jnp.sum(sq) / D_MODEL, (1,))

    wt_slots, wo_slots, g_small, g_norm = ([None] * DEPTH for _ in range(4))
    wt_slots_shape = _slots_shape(W_IN_SHARD, D_MODEL, BF16)
    dwt_waiting = None
    for l in reversed(range(DEPTH)):
        x_l, h, proj, mix, wt_l, wo_l = saved[l]
        qw, kw, bmap = layer_par[l]
        dmix = dmix_top if l == DEPTH - 1 else _dmix(dx_b, wo_l)
        dwo_part = _dw_out(mix, dx_b)
        res = _mixer_bwd(proj, dmix, qw, kw, sinks[l], slopes, w_s[l], bmap,
                         None if dwt_waiting is None else
                         (_scatter_rows((W_IN_SHARD,)), [dwt_waiting], [wt_slots_shape]))
        dproj, dqw, dkw, dsk, dws, dbs = res[:6]
        if dwt_waiting is not None:
            wt_slots[l + 1] = res[6]
        small_like = [dqw[0, :HALF], dkw[0, :HALF], dsk[0, :N_HEADS], dws, dbs]
        packed = _pack_rows(small_like)
        dwt_waiting, small_slots, wo_slots[l] = _dw_in(
            dproj, h, (_gather_slots_and_scatter_rows(W_OUT_SHARD), [packed, dwo_part],
                       [_slots_shape(*packed.shape, F32), _slots_shape(W_OUT_SHARD, D_MODEL, BF16)]))
        if l > 0:
            dx, dng, dx_b = _dh_norm_bwd(dproj, wt_l, x_l, dx, norm_g[l][None, :], True)
        else:
            dx, dng, wt_slots[0] = _dh_norm_bwd(dproj, wt_l, x_l, dx, norm_g[l][None, :], False,
                                                (_scatter_rows((W_IN_SHARD,)), [dwt_waiting], [wt_slots_shape]))
        g_small[l] = _unpack_rows(_sum_slots(small_slots, packed.shape[0]), small_like)
        g_norm[l] = dng[0]

    dng_all = _pack_rows([jnp.stack(g_norm), loss_part])
    dng_slots = _exchange("gather_dnorm", _gather_slots(), [dng_all], [_slots_shape(*dng_all.shape, F32)])[0]
    gr_norm, loss = _unpack_rows(_sum_slots(dng_slots, dng_all.shape[0]), [norm_g, loss_part])
    loss = loss[0]
    gr_qn, gr_kn, gr_sk, gr_ws, gr_bs = (jnp.stack([g_small[l][i] for l in range(DEPTH)]) for i in range(5))

    def t(a):
        return jnp.swapaxes(a, 1, 2)

    from_slots = {1: _adamw_from_slots("adamw_w_in", wt_slots, t(w_in), t(m_w_in), t(v_w_in), ADAMW_ROWS),
                  7: _adamw_from_slots("adamw_w_out", wo_slots, w_out, m_w_out, v_w_out, ADAMW_ROWS)}
    from_slots[1] = [t(a) for a in from_slots[1]]

    grads = [gr_norm, None, gr_qn, gr_kn, gr_sk, gr_ws, gr_bs, None]
    weights = [norm_g, w_in, q_norm, k_norm, sinks, w_s, b_s, w_out]
    moms = [m_norm_g, m_w_in, m_q_norm, m_k_norm, m_sinks, m_w_s, m_b_s, m_w_out]
    vels = [v_norm_g, v_w_in, v_q_norm, v_k_norm, v_sinks, v_w_s, v_b_s, v_w_out]
    tiles = [None, None, None, None, None, 1024, None, None]
    names = ["norm_g", "w_in", "q_norm", "k_norm", "sinks", "w_s", "b_s", "w_out"]
    deltas, new_m, new_v = [], [], []
    for i, (nm, w, g, m, v, tr) in enumerate(zip(names, weights, grads, moms, vels, tiles)):
        if i in from_slots:
            grads[i], d, mo, vo = from_slots[i]
        else:
            d, mo, vo = _adamw("adamw_" + nm, w, g, m, v, tr)
        deltas.append(d)
        new_m.append(mo)
        new_v.append(vo)

    return (loss, dx[None], *grads, *deltas, *new_m, *new_v)
```
